```python
import math
import jax, jax.numpy as jnp
from jax import lax
import numpy as np

D_MODEL = 1024
BATCH = 8
SEQ = 2048
DEPTH = 2
DEC_BATCH = 128
DEC_SEQ = 4
PAST_LEN = 16384
PAGE_SIZE = 128

GDN_HEADS = 4
GDN_DK = 128
GDN_DV = 128
MLSTM_HEADS = 4
MLSTM_DQK = 64
MLSTM_DV = 64
SSD_HEADS = 4
SSD_HEAD_DIM = 64
SSD_GROUPS = 2
SSD_STATE = 128
CONV_WIDTH = 4
FFN_CONV_WIDTH = 3
D_FF = 2816
CHUNK = 64
EPS = 1e-6

GDN_QK = GDN_HEADS * GDN_DK
GDN_V = GDN_HEADS * GDN_DV
MLSTM_QK = MLSTM_HEADS * MLSTM_DQK
MLSTM_V = MLSTM_HEADS * MLSTM_DV
SSD_INNER = SSD_HEADS * SSD_HEAD_DIM
SSD_BC = SSD_GROUPS * SSD_STATE
D_MIX = GDN_V + MLSTM_V + SSD_INNER
CONV_DIM = 2 * GDN_QK + GDN_V + SSD_INNER + 2 * SSD_BC
IN_COLS = CONV_DIM + GDN_V + 2 * GDN_HEADS + 2 * MLSTM_QK + 2 * MLSTM_V + 2 * MLSTM_HEADS + SSD_INNER + SSD_HEADS
CONV_SIZES = (GDN_QK, GDN_QK, GDN_V, SSD_INNER, SSD_BC, SSD_BC)
REST_SIZES = (GDN_V, GDN_HEADS, GDN_HEADS, MLSTM_QK, MLSTM_QK, MLSTM_V, MLSTM_V,
              MLSTM_HEADS, MLSTM_HEADS, SSD_INNER, SSD_HEADS)

kernel_name = "hymba_gdn_mlstm_ssd_convffn_step"


def rms_norm(x, w):
    xf = x.astype(jnp.float32)
    y = xf * lax.rsqrt(jnp.mean(xf * xf, axis=-1, keepdims=True) + EPS)
    return (y * w.astype(jnp.float32)).astype(x.dtype)


def l2norm(x):
    return x * lax.rsqrt(jnp.sum(x * x, axis=-1, keepdims=True) + EPS)


def _split(t, sizes):
    return jnp.split(t, np.cumsum(sizes)[:-1].tolist(), axis=-1)


def _to_chunks(t, n_chunks):
    b, l = t.shape[:2]
    return jnp.swapaxes(t.reshape((b, n_chunks, l // n_chunks) + t.shape[2:]), 0, 1)


def _from_chunks(t):
    t = jnp.swapaxes(t, 0, 1)
    return t.reshape((t.shape[0], t.shape[1] * t.shape[2]) + t.shape[3:])


def _masks(c):
    tril = jnp.tril(jnp.ones((c, c), dtype=bool))
    strict = jnp.tril(jnp.ones((c, c), dtype=bool), -1)
    return tril, strict


def causal_dwconv(x, buf, w, b):
    width = w.shape[0]
    L = x.shape[1]
    xp = jnp.concatenate([buf.astype(x.dtype), x], axis=1)
    y = b
    for j in range(width):
        y = y + w[j] * xp[:, j:j + L]
    return y, xp[:, L:]


def gdn_chunked(q, k, v, g, beta, s0):
    L = q.shape[1]
    c = math.gcd(L, CHUNK)
    n = L // c
    tril, strict = _masks(c)
    eye = jnp.eye(c, dtype=jnp.float32)

    def step(s, inp):
        qi, ki, vi, gi, bi = inp
        gam = jnp.cumsum(jnp.swapaxes(gi, 1, 2), axis=-1)
        bh = jnp.swapaxes(bi, 1, 2)
        decay = jnp.exp(jnp.where(tril, gam[..., :, None] - gam[..., None, :], -jnp.inf))
        kk = jnp.einsum('bihd,bjhd->bhij', ki, ki)
        m = jnp.where(strict, kk * decay * bh[..., :, None], 0.0)
        t = lax.linalg.triangular_solve(m + eye, jnp.broadcast_to(eye, m.shape), left_side=True,
                                        lower=True, unit_diagonal=True)
        u = jnp.einsum('bhij,bjhv,bhj->bhiv', t, vi, bh)
        wk = jnp.einsum('bhij,bjhd,bhj->bhid', t, ki, bh * jnp.exp(gam))
        v_new = u - jnp.einsum('bhid,bhdv->bhiv', wk, s)
        qk = jnp.einsum('bihd,bjhd->bhij', qi, ki) * decay
        o = (jnp.einsum('bihd,bhi,bhdv->bhiv', qi, jnp.exp(gam), s)
             + jnp.einsum('bhij,bhjv->bhiv', qk, v_new))
        g_last = gam[..., -1]
        s_new = (jnp.exp(g_last)[..., None, None] * s
                 + jnp.einsum('bjhd,bhj,bhjv->bhdv', ki, jnp.exp(g_last[..., None] - gam), v_new))
        return s_new, jnp.swapaxes(o, 1, 2)

    xs = tuple(_to_chunks(t, n) for t in (q, k, v, g, beta))
    s_fin, o = lax.scan(step, s0, xs)
    return _from_chunks(o), s_fin


def mlstm_chunked(q, k, v, log_i, log_f, c0, n0, m0):
    L = q.shape[1]
    c = math.gcd(L, CHUNK)
    n = L // c
    tril, _ = _masks(c)

    def step(carry, inp):
        cm, nv, m = carry
        qi, ki, vi, ii, fi = inp
        b = jnp.cumsum(jnp.swapaxes(fi, 1, 2), axis=-1)
        ih = jnp.swapaxes(ii, 1, 2)
        d = jnp.where(tril, b[..., :, None] - b[..., None, :] + ih[..., None, :], -jnp.inf)
        inter = b + m[..., None]
        m_t = jnp.maximum(inter, jnp.max(d, axis=-1))
        w_intra = jnp.exp(d - m_t[..., None])
        w_inter = jnp.exp(inter - m_t)
        s = jnp.einsum('bihd,bjhd->bhij', qi, ki) * w_intra
        num = (jnp.einsum('bihd,bhdv->bhiv', qi, cm) * w_inter[..., None]
               + jnp.einsum('bhij,bjhv->bhiv', s, vi))
        den = jnp.einsum('bihd,bhd->bhi', qi, nv) * w_inter + jnp.sum(s, axis=-1)
        den = jnp.maximum(jnp.abs(den), jnp.exp(-m_t))
        h = jnp.swapaxes(num / den[..., None], 1, 2)
        m_new = m_t[..., -1]
        b_last = b[..., -1]
        w_c = jnp.exp(b_last + m - m_new)
        w_k = jnp.exp(b_last[..., None] - b + ih - m_new[..., None])
        c_new = w_c[..., None, None] * cm + jnp.einsum('bjhd,bhj,bjhv->bhdv', ki, w_k, vi)
        n_new = w_c[..., None] * nv + jnp.einsum('bjhd,bhj->bhd', ki, w_k)
        return (c_new, n_new, m_new), h

    xs = tuple(_to_chunks(t, n) for t in (q, k, v, log_i, log_f))
    (c_fin, n_fin, m_fin), h = lax.scan(step, (c0, n0, m0), xs)
    return _from_chunks(h), c_fin, n_fin, m_fin


def ssd_chunked(x, dt, a, bm, cm, h0):
    L = x.shape[1]
    c = math.gcd(L, CHUNK)
    n = L // c
    tril, _ = _masks(c)
    rep = SSD_HEADS // SSD_GROUPS
    bh_ = jnp.repeat(bm, rep, axis=2)
    ch_ = jnp.repeat(cm, rep, axis=2)

    def step(h, inp):
        xi, dti, bi, ci = inp
        dth = jnp.swapaxes(dti, 1, 2)
        gam = jnp.cumsum(dth * a[:, None], axis=-1)
        decay = jnp.exp(jnp.where(tril, gam[..., :, None] - gam[..., None, :], -jnp.inf))
        cb = jnp.einsum('bihn,bjhn->bhij', ci, bi) * decay * dth[..., None, :]
        y = (jnp.einsum('bhij,bjhp->bihp', cb, xi)
             + jnp.einsum('bihn,bhi,bhpn->bihp', ci, jnp.exp(gam), h))
        g_last = gam[..., -1]
        h_new = (jnp.exp(g_last)[..., None, None] * h
                 + jnp.einsum('bjhp,bhj,bjhn->bhpn', xi, jnp.exp(g_last[..., None] - gam) * dth, bi))
        return h_new, y

    xs = tuple(_to_chunks(t, n) for t in (x, dt, bh_, ch_))
    h_fin, y = lax.scan(step, h0, xs)
    return _from_chunks(y), h_fin


def hybrid_mixer(h, conv_state, gdn_state, ml_c, ml_n, ml_m, ssd_state, w_in, conv_w, conv_b,
                 gdn_a_log, gdn_dt_bias, gdn_norm, mlstm_i_bias, mlstm_f_bias, mlstm_norm,
                 ssd_a_log, ssd_dt_bias, ssd_d, ssd_norm, w_out):
    B, L, _ = h.shape
    f32 = lambda t: t.astype(jnp.float32)
    proj = jnp.einsum('bld,de->ble', h, w_in)
    conv_in, rest = proj[..., :CONV_DIM], proj[..., CONV_DIM:]
    conv_out, new_conv = causal_dwconv(conv_in, conv_state, conv_w, conv_b)
    conv_out = jax.nn.silu(conv_out)
    gq, gk, gv, sx, sb, sc = _split(conv_out, CONV_SIZES)
    gg, ga, gb, mq, mk, mv, mo, mi, mf, sz, sdt = _split(rest, REST_SIZES)

    q = l2norm(f32(gq).reshape(B, L, GDN_HEADS, GDN_DK)) * (GDN_DK ** -0.5)
    k = l2norm(f32(gk).reshape(B, L, GDN_HEADS, GDN_DK))
    v = f32(gv).reshape(B, L, GDN_HEADS, GDN_DV)
    g = -jnp.exp(f32(gdn_a_log)) * jax.nn.softplus(f32(ga) + f32(gdn_dt_bias))
    beta = jax.nn.sigmoid(f32(gb))
    o_gdn, gdn_new = gdn_chunked(q, k, v, g, beta, f32(gdn_state))
    o_gdn = rms_norm(o_gdn, gdn_norm) * jax.nn.silu(f32(gg).reshape(B, L, GDN_HEADS, GDN_DV))
    o_gdn = o_gdn.reshape(B, L, GDN_V)

    mq_ = f32(mq).reshape(B, L, MLSTM_HEADS, MLSTM_DQK)
    mk_ = f32(mk).reshape(B, L, MLSTM_HEADS, MLSTM_DQK) * (MLSTM_DQK ** -0.5)
    mv_ = f32(mv).reshape(B, L, MLSTM_HEADS, MLSTM_DV)
    log_i = f32(mi) + f32(mlstm_i_bias)
    log_f = jax.nn.log_sigmoid(f32(mf) + f32(mlstm_f_bias))
    h_ml, c_new, n_new, m_new = mlstm_chunked(mq_, mk_, mv_, log_i, log_f, f32(ml_c), f32(ml_n), f32(ml_m))
    h_ml = jax.nn.sigmoid(f32(mo)).reshape(B, L, MLSTM_HEADS, MLSTM_DV) * h_ml
    h_ml = rms_norm(h_ml, mlstm_norm).reshape(B, L, MLSTM_V)

    xs = f32(sx).reshape(B, L, SSD_HEADS, SSD_HEAD_DIM)
    bm = f32(sb).reshape(B, L, SSD_GROUPS, SSD_STATE)
    cm = f32(sc).reshape(B, L, SSD_GROUPS, SSD_STATE)
    dt = jax.nn.softplus(f32(sdt) + f32(ssd_dt_bias))
    a = -jnp.exp(f32(ssd_a_log))
    y_ssd, ssd_new = ssd_chunked(xs, dt, a, bm, cm, f32(ssd_state))
    y_ssd = (y_ssd + f32(ssd_d)[:, None] * xs).reshape(B, L, SSD_INNER)
    y_ssd = rms_norm(y_ssd * jax.nn.silu(f32(sz)), ssd_norm)

    mix = jnp.concatenate([o_gdn, h_ml, y_ssd], axis=-1).astype(h.dtype)
    out = jnp.einsum('ble,ed->bld', mix, w_out)
    return (out, new_conv.astype(conv_state.dtype), gdn_new.astype(gdn_state.dtype),
            c_new.astype(ml_c.dtype), n_new.astype(ml_n.dtype), m_new.astype(ml_m.dtype),
            ssd_new.astype(ssd_state.dtype))


def conv_ffn(h, ffn_state, w_up, conv_w, conv_b, w_down):
    up = jnp.einsum('bld,df->blf', h, w_up)
    gate, val = up[..., :D_FF], up[..., D_FF:]
    gate, new_state = causal_dwconv(gate, ffn_state, conv_w, conv_b)
    out = jnp.einsum('blf,fd->bld', jax.nn.gelu(gate, approximate=True) * val, w_down)
    return out, new_state.astype(ffn_state.dtype)


def run_trunk(x, conv_s, gdn_s, mc_s, mn_s, mm_s, ssd_s, ffn_s, weights):
    (norm_mix_pre, norm_mix_post, norm_ffn_pre, norm_ffn_post, w_in, conv_w, conv_b,
     gdn_a_log, gdn_dt_bias, gdn_norm, mlstm_i_bias, mlstm_f_bias, mlstm_norm,
     ssd_a_log, ssd_dt_bias, ssd_d, ssd_norm, w_out,
     ffn_w_up, ffn_conv_w, ffn_conv_b, ffn_w_down) = weights
    per_layer = []
    for l in range(DEPTH):
        hn = rms_norm(x, norm_mix_pre[l])
        mix, nc, ng, nmc, nmn, nmm, nss = hybrid_mixer(
            hn, conv_s[l], gdn_s[l], mc_s[l], mn_s[l], mm_s[l], ssd_s[l], w_in[l], conv_w[l], conv_b[l],
            gdn_a_log[l], gdn_dt_bias[l], gdn_norm[l], mlstm_i_bias[l], mlstm_f_bias[l], mlstm_norm[l],
            ssd_a_log[l], ssd_dt_bias[l], ssd_d[l], ssd_norm[l], w_out[l])
        x = x + rms_norm(mix, norm_mix_post[l])
        hn = rms_norm(x, norm_ffn_pre[l])
        f, nf = conv_ffn(hn, ffn_s[l], ffn_w_up[l], ffn_conv_w[l], ffn_conv_b[l], ffn_w_down[l])
        x = x + rms_norm(f, norm_ffn_post[l])
        per_layer.append((nc, ng, nmc, nmn, nmm, nss, nf))
    new_states = [jnp.stack(s, axis=0) for s in zip(*per_layer)]
    return x, new_states


def _dt_bias(key, shape):
    dt = jnp.exp(jax.random.uniform(key, shape, minval=math.log(1e-3), maxval=math.log(1e-1)))
    return dt + jnp.log(-jnp.expm1(-dt))


def setup_inputs(seed: int = 0) -> dict:
    key = jax.random.key(seed)
    ks = iter(jax.random.split(key, 48))
    nrm = lambda shape, scale: scale * jax.random.normal(next(ks), shape, dtype=jnp.float32)
    gain = lambda shape: 1.0 + nrm(shape, 0.05)
    Dp = DEPTH
    inp = {}
    inp["x_prompt"] = nrm((BATCH, SEQ, D_MODEL), 1.0)
    inp["x_sample"] = nrm((DEC_BATCH, DEC_SEQ, D_MODEL), 1.0)
    inp["state_conv"] = nrm((Dp, DEC_BATCH, CONV_WIDTH - 1, CONV_DIM), 1.0)
    inp["state_gdn"] = nrm((Dp, DEC_BATCH, GDN_HEADS, GDN_DK, GDN_DV), 0.3)
    inp["state_mlstm_c"] = nrm((Dp, DEC_BATCH, MLSTM_HEADS, MLSTM_DQK, MLSTM_DV), 0.3)
    inp["state_mlstm_n"] = nrm((Dp, DEC_BATCH, MLSTM_HEADS, MLSTM_DQK), 0.3)
    inp["state_mlstm_m"] = nrm((Dp, DEC_BATCH, MLSTM_HEADS), 1.0)
    inp["state_ssd"] = nrm((Dp, DEC_BATCH, SSD_HEADS, SSD_HEAD_DIM, SSD_STATE), 0.3)
    inp["state_ffn_conv"] = nrm((Dp, DEC_BATCH, FFN_CONV_WIDTH - 1, D_FF), 1.0)
    inp["norm_mix_pre"] = gain((Dp, D_MODEL))
    inp["norm_mix_post"] = gain((Dp, D_MODEL))
    inp["norm_ffn_pre"] = gain((Dp, D_MODEL))
    inp["norm_ffn_post"] = gain((Dp, D_MODEL))
    inp["w_in"] = nrm((Dp, D_MODEL, IN_COLS), D_MODEL ** -0.5)
    inp["conv_w"] = nrm((Dp, CONV_WIDTH, CONV_DIM), CONV_WIDTH ** -0.5)
    inp["conv_b"] = nrm((Dp, CONV_DIM), 0.01)
    inp["gdn_a_log"] = jnp.log(jax.random.uniform(next(ks), (Dp, GDN_HEADS), minval=1.0, maxval=16.0))
    inp["gdn_dt_bias"] = _dt_bias(next(ks), (Dp, GDN_HEADS))
    inp["gdn_norm"] = gain((Dp, GDN_DV))
    inp["mlstm_i_bias"] = nrm((Dp, MLSTM_HEADS), 0.1)
    inp["mlstm_f_bias"] = jnp.linspace(3.0, 6.0, MLSTM_HEADS, dtype=jnp.float32)[None, :] + nrm((Dp, MLSTM_HEADS), 0.1)
    inp["mlstm_norm"] = gain((Dp, MLSTM_DV))
    inp["ssd_a_log"] = jnp.log(jax.random.uniform(next(ks), (Dp, SSD_HEADS), minval=1.0, maxval=16.0))
    inp["ssd_dt_bias"] = _dt_bias(next(ks), (Dp, SSD_HEADS))
    inp["ssd_d"] = gain((Dp, SSD_HEADS))
    inp["ssd_norm"] = gain((Dp, SSD_INNER))
    inp["w_out"] = nrm((Dp, D_MIX, D_MODEL), D_MIX ** -0.5)
    inp["ffn_w_up"] = nrm((Dp, D_MODEL, 2 * D_FF), D_MODEL ** -0.5)
    inp["ffn_conv_w"] = nrm((Dp, FFN_CONV_WIDTH, D_FF), FFN_CONV_WIDTH ** -0.5)
    inp["ffn_conv_b"] = nrm((Dp, D_FF), 0.01)
    inp["ffn_w_down"] = nrm((Dp, D_FF, D_MODEL), D_FF ** -0.5)
    return inp


def reference(x_prompt, x_sample, state_conv, state_gdn, state_mlstm_c, state_mlstm_n, state_mlstm_m,
              state_ssd, state_ffn_conv, norm_mix_pre, norm_mix_post, norm_ffn_pre, norm_ffn_post,
              w_in, conv_w, conv_b, gdn_a_log, gdn_dt_bias, gdn_norm, mlstm_i_bias, mlstm_f_bias,
              mlstm_norm, ssd_a_log, ssd_dt_bias, ssd_d, ssd_norm, w_out,
              ffn_w_up, ffn_conv_w, ffn_conv_b, ffn_w_down):
    weights = (norm_mix_pre, norm_mix_post, norm_ffn_pre, norm_ffn_post, w_in, conv_w, conv_b,
               gdn_a_log, gdn_dt_bias, gdn_norm, mlstm_i_bias, mlstm_f_bias, mlstm_norm,
               ssd_a_log, ssd_dt_bias, ssd_d, ssd_norm, w_out,
               ffn_w_up, ffn_conv_w, ffn_conv_b, ffn_w_down)
    bp = x_prompt.shape[0]
    dtp = x_prompt.dtype
    z_conv = jnp.zeros((DEPTH, bp, CONV_WIDTH - 1, CONV_DIM), dtp)
    z_gdn = jnp.zeros((DEPTH, bp, GDN_HEADS, GDN_DK, GDN_DV), dtp)
    z_mc = jnp.zeros((DEPTH, bp, MLSTM_HEADS, MLSTM_DQK, MLSTM_DV), dtp)
    z_mn = jnp.zeros((DEPTH, bp, MLSTM_HEADS, MLSTM_DQK), dtp)
    z_mm = jnp.zeros((DEPTH, bp, MLSTM_HEADS), dtp)
    z_ssd = jnp.zeros((DEPTH, bp, SSD_HEADS, SSD_HEAD_DIM, SSD_STATE), dtp)
    z_ffn = jnp.zeros((DEPTH, bp, FFN_CONV_WIDTH - 1, D_FF), dtp)
    y_prompt, (p_conv, p_gdn, p_mc, p_mn, p_mm, p_ssd, p_ffn) = run_trunk(
        x_prompt, z_conv, z_gdn, z_mc, z_mn, z_mm, z_ssd, z_ffn, weights)
    y_sample, (s_conv, s_gdn, s_mc, s_mn, s_mm, s_ssd, s_ffn) = run_trunk(
        x_sample, state_conv, state_gdn, state_mlstm_c, state_mlstm_n, state_mlstm_m, state_ssd,
        state_ffn_conv, weights)
    return (y_prompt, y_sample, p_conv, p_gdn, p_mc, p_mn, p_mm, p_ssd, p_ffn,
            s_conv, s_gdn, s_mc, s_mn, s_mm, s_ssd, s_ffn)
```

```python
import functools

import jax
import jax.numpy as jnp
from jax import lax
from jax.experimental import pallas as pl
from jax.experimental.pallas import tpu as pltpu

F32 = jnp.float32
BF16 = jnp.bfloat16

D_MODEL = 1024
DEPTH = 2
GDN_HEADS, GDN_DK, GDN_DV = 4, 128, 128
ML_HEADS, ML_DQK, ML_DV = 4, 64, 64
SSD_HEADS, SSD_P, SSD_GROUPS, SSD_N = 4, 64, 2, 128
D_FF = 2816
EPS = 1e-6

CONV_DIM = 2304
C_GQ, C_GK, C_GV, C_SX, C_SB, C_SC = 0, 512, 1024, 1536, 1792, 2048
C_GG, C_MQ, C_MK, C_MV, C_MO, C_SZ, C_GATE = 2304, 2816, 3072, 3328, 3584, 3840, 4096
N_IN = 4224
IN_NBLK = 1408
L_GA, L_GB, L_MI, L_MF, L_DT = 0, 4, 8, 12, 16

SUBLANES = 8
SLOT = 8
SLOT_FIRST, SLOT_LAST = 3, 6
FF_BLK = 256
VMEM_LIMIT = 56 * 1024 * 1024

PROMPT_TM, PROMPT_CHUNK = 512, 64
SAMPLE_TM = 128


def _dot(a, b):
    return jnp.dot(a, b, preferred_element_type=F32)


def _dot_nt(a, b):
    return lax.dot_general(a, b, (((1,), (1,)), ((), ())), preferred_element_type=F32)


def _dot_tn(a, b):
    return lax.dot_general(a, b, (((0,), (0,)), ((), ())), preferred_element_type=F32)


def _sigmoid(x):
    return 1.0 / (1.0 + jnp.exp(-x))


def _silu(x):
    return x * _sigmoid(x)


def _rms(x, w):
    return x * lax.rsqrt(jnp.mean(x * x, axis=-1, keepdims=True) + EPS) * w


def _l2n(x):
    return x * lax.rsqrt(jnp.sum(x * x, axis=-1, keepdims=True) + EPS)


def _gelu_tanh(x):
    return 0.5 * x * (1.0 + jnp.tanh(0.7978845608028654 * (x + 0.044715 * (x * x * x))))


def _cumsum_rows(tril_b, x):
    hi = x.astype(BF16)
    r1 = x - hi.astype(F32)
    mid = r1.astype(BF16)
    lo = (r1 - mid.astype(F32)).astype(BF16)
    return _dot(tril_b, hi) + _dot(tril_b, mid) + _dot(tril_b, lo)


def _conv_silu(proj_scr, r0, c, col0, width, cw_ref, cb_ref):
    win = proj_scr[pl.ds(r0, c + SUBLANES), col0:col0 + width]
    acc = cb_ref[0:1, col0:col0 + width] + cw_ref[3:4, col0:col0 + width] * win[SUBLANES:, :]
    for j in range(3):
        shifted = pltpu.roll(win, 3 - j, 0)[SUBLANES:, :]
        acc = acc + cw_ref[j:j + 1, col0:col0 + width] * shifted
    return _silu(acc)


def _mixer_chunk(i, *, c, slot, proj_scr, mix_scr, cw_ref, cb_ref, gprm_ref, gdnn_ref, mln_ref,
                 ssdn_ref, gdn_o, mc_o, mn_o, mm_o, ssd_o):
    r0 = pl.multiple_of(i * c, c)
    seq = i if slot else 0
    rows = pl.ds(r0, c)

    ii = lax.broadcasted_iota(jnp.int32, (c, c), 0)
    jj = lax.broadcasted_iota(jnp.int32, (c, c), 1)
    tril = ii >= jj
    strict = ii > jj
    eye = (ii == jj).astype(F32)
    tril_b = tril.astype(BF16)

    graw = proj_scr[pl.ds(r0 + SUBLANES, c), C_GATE:C_GATE + 128]
    lane = lax.broadcasted_iota(jnp.int32, (c, 128), 1)
    z = graw + gprm_ref[0:1, :]
    soft = jnp.log1p(jnp.exp(-jnp.abs(z)))
    sp = jnp.maximum(z, 0.0) + soft
    log_sig = -(jnp.maximum(-z, 0.0) + soft)
    a_neg = -jnp.exp(gprm_ref[1:2, :])
    is_ga = lane < L_GB
    is_gb = (lane >= L_GB) & (lane < L_MI)
    is_mi = (lane >= L_MI) & (lane < L_MF)
    is_mf = (lane >= L_MF) & (lane < L_DT)
    is_dt = (lane >= L_DT) & (lane < L_DT + SSD_HEADS)
    cum_src = jnp.where(is_ga | is_dt, a_neg * sp, jnp.where(is_mf, log_sig, 0.0))
    elem = jnp.where(is_gb, _sigmoid(graw), jnp.where(is_mi, z, jnp.where(is_dt, sp, 0.0)))
    if slot:
        rr = lax.broadcasted_iota(jnp.int32, (c, 128), 0)
        valid = (rr >= SLOT_FIRST) & (rr <= SLOT_LAST)
        cum_src = jnp.where(valid, cum_src, 0.0)
        elem = jnp.where(valid, elem, jnp.where(is_mi, -jnp.inf, 0.0))
    cum = _cumsum_rows(tril_b, cum_src)
    parts = [cum, elem]
    if 2 * c < 128:
        parts.append(jnp.zeros((128 - 2 * c, 128), F32))
    zt = jnp.concatenate(parts, axis=0).T

    def col(a, l):
        return a[:, l:l + 1]

    def row_cum(l):
        return zt[l:l + 1, 0:c]

    def row_elem(l):
        return zt[l:l + 1, c:2 * c]

    nsq = c.bit_length() - 2
    for h in range(GDN_HEADS):
        q = _l2n(_conv_silu(proj_scr, r0, c, C_GQ + h * GDN_DK, GDN_DK, cw_ref, cb_ref)) * (GDN_DK ** -0.5)
        k = _l2n(_conv_silu(proj_scr, r0, c, C_GK + h * GDN_DK, GDN_DK, cw_ref, cb_ref))
        v = _conv_silu(proj_scr, r0, c, C_GV + h * GDN_DV, GDN_DV, cw_ref, cb_ref)
        gam_c = col(cum, L_GA + h)
        beta_c = col(elem, L_GB + h)
        dmat = jnp.exp(jnp.where(tril, gam_c - row_cum(L_GA + h), -jnp.inf))
        kk = _dot_nt(k, k)
        p = -jnp.where(strict, kk * dmat * beta_c, 0.0)
        t_inv = eye + p
        pk = p
        for _ in range(nsq):
            pk = _dot(pk, pk)
            t_inv = t_inv + _dot(t_inv, pk)
        eg = jnp.exp(gam_c)
        uw = _dot(t_inv, jnp.concatenate([v * beta_c, k * (beta_c * eg)], axis=1))
        s_old = gdn_o[seq, h]
        v_new = uw[:, :GDN_DV] - _dot(uw[:, GDN_DV:], s_old)
        qk = _dot_nt(q, k) * dmat
        o = _dot(q * eg, s_old) + _dot(qk, v_new)
        g_last = gam_c[c - 1:c, :]
        gdn_o[seq, h] = jnp.exp(g_last) * s_old + _dot_tn(k * jnp.exp(g_last - gam_c), v_new)
        gg = proj_scr[pl.ds(r0 + SUBLANES, c), C_GG + h * GDN_DV:C_GG + (h + 1) * GDN_DV]
        mix_scr[rows, h * GDN_DV:(h + 1) * GDN_DV] = _rms(o, gdnn_ref[...]) * _silu(gg)

    m_row = mm_o[seq, 0:1, :]
    m_row_new = m_row
    lane_row = lax.broadcasted_iota(jnp.int32, (1, 128), 1)
    h_parts = []
    for h in range(ML_HEADS):
        def piece(base):
            return proj_scr[pl.ds(r0 + SUBLANES, c), base + h * ML_DQK:base + (h + 1) * ML_DQK]
        q = piece(C_MQ)
        k = piece(C_MK) * (ML_DQK ** -0.5)
        v = piece(C_MV)
        b_c = col(cum, L_MF + h)
        i_c = col(elem, L_MI + h)
        m0 = m_row[:, h:h + 1]
        d = jnp.where(tril, b_c - row_cum(L_MF + h) + row_elem(L_MI + h), -jnp.inf)
        inter = b_c + m0
        m_t = jnp.maximum(inter, jnp.max(d, axis=-1, keepdims=True))
        w_intra = jnp.exp(d - m_t)
        w_inter = jnp.exp(inter - m_t)
        s = _dot_nt(q, k) * w_intra
        c_old = mc_o[seq, h]
        n_old = mn_o[seq, h:h + 1, :]
        num = _dot(q, c_old) * w_inter + _dot(s, v)
        den = jnp.sum(q * n_old, axis=-1, keepdims=True) * w_inter + jnp.sum(s, axis=-1, keepdims=True)
        den = jnp.maximum(jnp.abs(den), jnp.exp(-m_t))
        hh = num / den
        m_new = m_t[c - 1:c, :]
        b_last = b_c[c - 1:c, :]
        w_c = jnp.exp(b_last + m0 - m_new)
        kw = k * jnp.exp(b_last - b_c + i_c - m_new)
        mc_o[seq, h] = w_c * c_old + _dot_tn(kw, v)
        mn_o[seq, h:h + 1, :] = w_c * n_old + jnp.sum(kw, axis=0, keepdims=True)
        m_row_new = jnp.where(lane_row == h, m_new, m_row_new)
        h_parts.append(_rms(_sigmoid(piece(C_MO)) * hh, mln_ref[...]))
    mm_o[seq, 0:1, :] = m_row_new
    mix_scr[rows, GDN_HEADS * GDN_DV:GDN_HEADS * GDN_DV + ML_HEADS * ML_DV] = jnp.concatenate(h_parts, axis=1)

    rep = SSD_HEADS // SSD_GROUPS
    y_parts = []
    for g in range(SSD_GROUPS):
        bg = _conv_silu(proj_scr, r0, c, C_SB + g * SSD_N, SSD_N, cw_ref, cb_ref)
        cg = _conv_silu(proj_scr, r0, c, C_SC + g * SSD_N, SSD_N, cw_ref, cb_ref)
        xg = _conv_silu(proj_scr, r0, c, C_SX + g * rep * SSD_P, rep * SSD_P, cw_ref, cb_ref)
        cb_raw = _dot_nt(cg, bg)
        for hl in range(rep):
            h = g * rep + hl
            x = xg[:, hl * SSD_P:(hl + 1) * SSD_P]
            gam_c = col(cum, L_DT + h)
            dt_c = col(elem, L_DT + h)
            dmat = jnp.exp(jnp.where(tril, gam_c - row_cum(L_DT + h), -jnp.inf))
            cb = cb_raw * dmat * row_elem(L_DT + h)
            h_old = ssd_o[seq, h]
            y = _dot(cb, x) + _dot_nt(cg * jnp.exp(gam_c), h_old)
            g_last = gam_c[c - 1:c, :]
            ssd_o[seq, h] = (jnp.exp(g_last) * h_old
                             + _dot_tn(x * (jnp.exp(g_last - gam_c) * dt_c), bg))
            y_parts.append(y + gprm_ref[2:3, L_DT + h:L_DT + h + 1] * x)
    sz = proj_scr[pl.ds(r0 + SUBLANES, c), C_SZ:C_SZ + SSD_HEADS * SSD_P]
    y_all = jnp.concatenate(y_parts, axis=1) * _silu(sz)
    mix_scr[rows, GDN_HEADS * GDN_DV + ML_HEADS * ML_DV:D_MODEL] = _rms(y_all, ssdn_ref[...])


def _mixer_kernel(*refs, tm, c, slot, nt):
    if slot:
        (x_ref, tail_ref, win_ref, wout_ref, cw_ref, cb_ref, gprm_ref, npre_ref, npost_ref, gdnn_ref,
         mln_ref, ssdn_ref, gdn_i, mc_i, mn_i, mm_i, ssd_i,
         o_ref, conv_o, gdn_o, mc_o, mn_o, mm_o, ssd_o, proj_scr, mix_scr) = refs
    else:
        (x_ref, win_ref, wout_ref, cw_ref, cb_ref, gprm_ref, npre_ref, npost_ref, gdnn_ref,
         mln_ref, ssdn_ref,
         o_ref, conv_o, gdn_o, mc_o, mn_o, mm_o, ssd_o, proj_scr, mix_scr) = refs
    t = pl.program_id(1)

    if slot:
        proj_scr[0:SUBLANES, :] = jnp.zeros((SUBLANES, N_IN), F32)
        gdn_o[...] = gdn_i[...]
        mc_o[...] = mc_i[...]
        mn_o[...] = mn_i[...]
        mm_o[...] = mm_i[...]
        ssd_o[...] = ssd_i[...]
    else:
        @pl.when(t == 0)
        def _():
            proj_scr[0:SUBLANES, :] = jnp.zeros((SUBLANES, N_IN), F32)
            gdn_o[...] = jnp.zeros(gdn_o.shape, F32)
            mc_o[...] = jnp.zeros(mc_o.shape, F32)
            mn_o[...] = jnp.zeros(mn_o.shape, F32)
            mm_o[...] = jnp.zeros(mm_o.shape, F32)
            ssd_o[...] = jnp.zeros(ssd_o.shape, F32)

    x = x_ref[...]
    hn = _rms(x, npre_ref[...]).astype(BF16)
    for nb in range(N_IN // IN_NBLK):
        cols = slice(nb * IN_NBLK, (nb + 1) * IN_NBLK)
        proj_scr[SUBLANES:SUBLANES + tm, cols] = _dot(hn, win_ref[:, cols])

    if slot:
        rr = lax.broadcasted_iota(jnp.int32, (tm, CONV_DIM), 0)
        is_tail = (rr % SLOT) < SLOT_FIRST
        pre = jnp.where(is_tail, tail_ref[...], proj_scr[SUBLANES:SUBLANES + tm, 0:CONV_DIM])
        proj_scr[SUBLANES:SUBLANES + tm, 0:CONV_DIM] = pre
        conv_o[...] = pre

    chunk = functools.partial(
        _mixer_chunk, c=c, slot=slot, proj_scr=proj_scr, mix_scr=mix_scr, cw_ref=cw_ref, cb_ref=cb_ref,
        gprm_ref=gprm_ref, gdnn_ref=gdnn_ref, mln_ref=mln_ref, ssdn_ref=ssdn_ref,
        gdn_o=gdn_o, mc_o=mc_o, mn_o=mn_o, mm_o=mm_o, ssd_o=ssd_o)

    def body(i, carry):
        chunk(i)
        return carry

    lax.fori_loop(0, tm // c, body, 0)

    if not slot:
        last_rows = proj_scr[tm:tm + SUBLANES, 0:CONV_DIM]
        proj_scr[0:SUBLANES, 0:CONV_DIM] = last_rows

        @pl.when(t == nt - 1)
        def _():
            conv_o[0] = last_rows

    out = _dot(mix_scr[...].astype(BF16), wout_ref[...])
    o_ref[...] = x + _rms(out, npost_ref[...])


def _ffn_kernel(*refs, tm, slot, nt):
    if slot:
        (x_ref, ftail_ref, wup_ref, wdn_ref, fw_ref, fb_ref, npre_ref, npost_ref,
         o_ref, gate_o, tails_scr) = refs
    else:
        (x_ref, wup_ref, wdn_ref, fw_ref, fb_ref, npre_ref, npost_ref,
         o_ref, gate_o, tails_scr) = refs
    t = pl.program_id(1)

    if slot:
        tails_scr[...] = jnp.zeros(tails_scr.shape, F32)
    else:
        @pl.when(t == 0)
        def _():
            tails_scr[...] = jnp.zeros(tails_scr.shape, F32)

    x = x_ref[...]
    hn = _rms(x, npre_ref[...]).astype(BF16)
    acc = jnp.zeros((tm, D_MODEL), F32)
    for blk in range(D_FF // FF_BLK):
        cols = slice(blk * FF_BLK, (blk + 1) * FF_BLK)
        gate = _dot(hn, wup_ref[:, cols])
        val = _dot(hn, wup_ref[:, D_FF + blk * FF_BLK:D_FF + (blk + 1) * FF_BLK])
        if slot:
            rr = lax.broadcasted_iota(jnp.int32, (tm, FF_BLK), 0) % SLOT
            gate = jnp.where((rr >= SLOT_FIRST - 2) & (rr < SLOT_FIRST), ftail_ref[:, cols], gate)
            gate_o[:, cols] = gate
        full = jnp.concatenate([tails_scr[:, cols], gate], axis=0)
        conv = fb_ref[0:1, cols] + fw_ref[2:3, cols] * gate
        for j in range(2):
            conv = conv + fw_ref[j:j + 1, cols] * pltpu.roll(full, 2 - j, 0)[SUBLANES:, :]
        last_rows = full[tm:tm + SUBLANES, :]
        tails_scr[:, cols] = last_rows
        if not slot:
            @pl.when(t == nt - 1)
            def _():
                gate_o[0, :, cols] = last_rows
        act = (_gelu_tanh(conv) * val).astype(BF16)
        acc = acc + _dot(act, wdn_ref[cols, :])
    o_ref[...] = x + _rms(acc, npost_ref[...])


def _const_spec(shape, layer):
    nd = len(shape)
    return pl.BlockSpec((None,) + tuple(shape), lambda b, t: (layer,) + (0,) * nd,
                        pipeline_mode=pl.Buffered(1))


def _mixer_call(x, layer, prm, *, slot, states=None, tail=None):
    rows = x.shape[0]
    if slot:
        tm, c, nt = SAMPLE_TM, SLOT, 1
        nseq = tm // SLOT
    else:
        tm, c = PROMPT_TM, PROMPT_CHUNK
        nt = 2048 // tm
        nseq = 1
    ngrp = rows // (tm * nt)
    nb = ngrp * nseq
    row_spec = lambda w: pl.BlockSpec((tm, w), lambda b, t: (b * nt + t, 0))
    st_out = lambda *dims: pl.BlockSpec((nseq,) + dims, lambda b, t: (b,) + (0,) * len(dims))
    st_in = lambda *dims: pl.BlockSpec((None, nseq) + dims, lambda b, t: (layer, b) + (0,) * len(dims))

    in_specs = [row_spec(D_MODEL)]
    args = [x]
    if slot:
        in_specs.append(row_spec(CONV_DIM))
        args.append(tail)
    in_specs += [
        _const_spec((D_MODEL, N_IN), layer), _const_spec((D_MODEL, D_MODEL), layer),
        _const_spec((4, CONV_DIM), layer), _const_spec((1, CONV_DIM), layer),
        _const_spec((8, 128), layer), _const_spec((1, D_MODEL), layer), _const_spec((1, D_MODEL), layer),
        _const_spec((1, GDN_DV), layer), _const_spec((1, ML_DV), layer),
        _const_spec((1, SSD_HEADS * SSD_P), layer)]
    args += [prm["w_in"], prm["w_out"], prm["conv_w"], prm["conv_b"], prm["gprm"], prm["norm_mix_pre"],
             prm["norm_mix_post"], prm["gdn_norm"], prm["mlstm_norm"], prm["ssd_norm"]]
    if slot:
        in_specs += [st_in(GDN_HEADS, GDN_DK, GDN_DV), st_in(ML_HEADS, ML_DQK, ML_DV),
                     st_in(ML_HEADS, ML_DQK), st_in(SUBLANES, 128), st_in(SSD_HEADS, SSD_P, SSD_N)]
        args += list(states)

    if slot:
        conv_spec, conv_shape = row_spec(CONV_DIM), (rows, CONV_DIM)
    else:
        conv_spec, conv_shape = st_out(SUBLANES, CONV_DIM), (nb, SUBLANES, CONV_DIM)
    out_specs = [row_spec(D_MODEL), conv_spec,
                 st_out(GDN_HEADS, GDN_DK, GDN_DV), st_out(ML_HEADS, ML_DQK, ML_DV),
                 st_out(ML_HEADS, ML_DQK), st_out(SUBLANES, 128), st_out(SSD_HEADS, SSD_P, SSD_N)]
    out_shape = [jax.ShapeDtypeStruct((rows, D_MODEL), F32), jax.ShapeDtypeStruct(conv_shape, F32),
                 jax.ShapeDtypeStruct((nb, GDN_HEADS, GDN_DK, GDN_DV), F32),
                 jax.ShapeDtypeStruct((nb, ML_HEADS, ML_DQK, ML_DV), F32),
                 jax.ShapeDtypeStruct((nb, ML_HEADS, ML_DQK), F32),
                 jax.ShapeDtypeStruct((nb, SUBLANES, 128), F32),
                 jax.ShapeDtypeStruct((nb, SSD_HEADS, SSD_P, SSD_N), F32)]
    return pl.pallas_call(
        functools.partial(_mixer_kernel, tm=tm, c=c, slot=slot, nt=nt),
        grid=(ngrp, nt), in_specs=in_specs, out_specs=out_specs, out_shape=out_shape,
        scratch_shapes=[pltpu.VMEM((tm + SUBLANES, N_IN), F32), pltpu.VMEM((tm, D_MODEL), F32)],
        compiler_params=pltpu.CompilerParams(dimension_semantics=("arbitrary", "arbitrary"),
                                             vmem_limit_bytes=VMEM_LIMIT),
        name=("mixer_sample" if slot else "mixer_prompt"),
    )(*args)


def _ffn_call(x, layer, prm, *, slot, tail=None):
    rows = x.shape[0]
    tm = 512
    nt = 1 if slot else 2048 // tm
    ngrp = rows // (tm * nt)
    row_spec = lambda w: pl.BlockSpec((tm, w), lambda b, t: (b * nt + t, 0))
    in_specs = [row_spec(D_MODEL)]
    args = [x]
    if slot:
        in_specs.append(row_spec(D_FF))
        args.append(tail)
    in_specs += [_const_spec((D_MODEL, 2 * D_FF), layer), _const_spec((D_FF, D_MODEL), layer),
                 _const_spec((3, D_FF), layer), _const_spec((1, D_FF), layer),
                 _const_spec((1, D_MODEL), layer), _const_spec((1, D_MODEL), layer)]
    args += [prm["ffn_w_up"], prm["ffn_w_down"], prm["ffn_conv_w"], prm["ffn_conv_b"],
             prm["norm_ffn_pre"], prm["norm_ffn_post"]]
    if slot:
        gate_spec, gate_shape = row_spec(D_FF), (rows, D_FF)
    else:
        gate_spec = pl.BlockSpec((1, SUBLANES, D_FF), lambda b, t: (b, 0, 0))
        gate_shape = (ngrp, SUBLANES, D_FF)
    return pl.pallas_call(
        functools.partial(_ffn_kernel, tm=tm, slot=slot, nt=nt),
        grid=(ngrp, nt), in_specs=in_specs, out_specs=[row_spec(D_MODEL), gate_spec],
        out_shape=[jax.ShapeDtypeStruct((rows, D_MODEL), F32), jax.ShapeDtypeStruct(gate_shape, F32)],
        scratch_shapes=[pltpu.VMEM((SUBLANES, D_FF), F32)],
        compiler_params=pltpu.CompilerParams(dimension_semantics=("arbitrary", "arbitrary"),
                                             vmem_limit_bytes=VMEM_LIMIT),
        name=("ffn_sample" if slot else "ffn_prompt"),
    )(*args)


def _prepare_params(norm_mix_pre, norm_mix_post, norm_ffn_pre, norm_ffn_post, w_in, conv_w, conv_b,
                    gdn_a_log, gdn_dt_bias, gdn_norm, mlstm_i_bias, mlstm_f_bias, mlstm_norm,
                    ssd_a_log, ssd_dt_bias, ssd_d, ssd_norm, w_out, ffn_w_up, ffn_conv_w, ffn_conv_b,
                    ffn_w_down):
    w_in_p = jnp.concatenate(
        [w_in[..., :2816], w_in[..., 2824:3848], w_in[..., 3856:4112], w_in[..., 2816:2824],
         w_in[..., 3848:3856], w_in[..., 4112:4116],
         jnp.zeros((DEPTH, D_MODEL, N_IN - 4116), w_in.dtype)], axis=-1).astype(BF16)
    z4 = jnp.zeros((DEPTH, 4), F32)
    pad = jnp.zeros((DEPTH, 128 - 20), F32)
    gprm = jnp.stack(
        [jnp.concatenate([gdn_dt_bias, z4, mlstm_i_bias, mlstm_f_bias, ssd_dt_bias, pad], axis=-1),
         jnp.concatenate([gdn_a_log, z4, z4, z4, ssd_a_log, pad], axis=-1),
         jnp.concatenate([z4, z4, z4, z4, ssd_d, pad], axis=-1)]
        + [jnp.zeros((DEPTH, 128), F32)] * 5, axis=1)
    row = lambda a: a[:, None, :]
    return dict(
        w_in=w_in_p, w_out=w_out.astype(BF16), conv_w=conv_w, conv_b=row(conv_b), gprm=gprm,
        norm_mix_pre=row(norm_mix_pre), norm_mix_post=row(norm_mix_post),
        norm_ffn_pre=row(norm_ffn_pre), norm_ffn_post=row(norm_ffn_post),
        gdn_norm=row(gdn_norm), mlstm_norm=row(mlstm_norm), ssd_norm=row(ssd_norm),
        ffn_w_up=ffn_w_up.astype(BF16), ffn_w_down=ffn_w_down.astype(BF16),
        ffn_conv_w=ffn_conv_w, ffn_conv_b=row(ffn_conv_b))


def kernel(x_prompt, x_sample, state_conv, state_gdn, state_mlstm_c, state_mlstm_n, state_mlstm_m, state_ssd, state_ffn_conv, norm_mix_pre, norm_mix_post, norm_ffn_pre, norm_ffn_post, w_in, conv_w, conv_b, gdn_a_log, gdn_dt_bias, gdn_norm, mlstm_i_bias, mlstm_f_bias, mlstm_norm, ssd_a_log, ssd_dt_bias, ssd_d, ssd_norm, w_out, ffn_w_up, ffn_conv_w, ffn_conv_b, ffn_w_down):
    prm = _prepare_params(norm_mix_pre, norm_mix_post, norm_ffn_pre, norm_ffn_post, w_in, conv_w, conv_b,
                          gdn_a_log, gdn_dt_bias, gdn_norm, mlstm_i_bias, mlstm_f_bias, mlstm_norm,
                          ssd_a_log, ssd_dt_bias, ssd_d, ssd_norm, w_out, ffn_w_up, ffn_conv_w,
                          ffn_conv_b, ffn_w_down)
    bp, lp, _ = x_prompt.shape
    bs, ls, _ = x_sample.shape

    x = x_prompt.reshape(bp * lp, D_MODEL)
    p_states = []
    for layer in range(DEPTH):
        x, conv8, gdn, mc, mn, mm, ssd = _mixer_call(x, layer, prm, slot=False)
        x, gate8 = _ffn_call(x, layer, prm, slot=False)
        p_states.append((conv8[:, SUBLANES - 3:], gdn, mc, mn, mm[:, 0, :ML_HEADS], ssd,
                         gate8[:, SUBLANES - 2:]))
    y_prompt = x.reshape(bp, lp, D_MODEL)

    x = jnp.pad(x_sample, ((0, 0), (SLOT_FIRST, SLOT - SLOT_FIRST - ls), (0, 0))).reshape(bs * SLOT, D_MODEL)
    mm_in = jnp.pad(state_mlstm_m[:, :, None, :], ((0, 0), (0, 0), (0, SUBLANES - 1), (0, 128 - ML_HEADS)))
    conv_tail = jnp.pad(state_conv, ((0, 0), (0, 0), (0, SLOT - 3), (0, 0))).reshape(DEPTH, bs * SLOT, CONV_DIM)
    ffn_tail = jnp.pad(state_ffn_conv, ((0, 0), (0, 0), (SLOT_FIRST - 2, SLOT - SLOT_FIRST), (0, 0))
                       ).reshape(DEPTH, bs * SLOT, D_FF)
    s_states = []
    for layer in range(DEPTH):
        x, conv_all, gdn, mc, mn, mm, ssd = _mixer_call(
            x, layer, prm, slot=True, tail=conv_tail[layer],
            states=(state_gdn, state_mlstm_c, state_mlstm_n, mm_in, state_ssd))
        x, gate_all = _ffn_call(x, layer, prm, slot=True, tail=ffn_tail[layer])
        s_states.append((conv_all.reshape(bs, SLOT, CONV_DIM)[:, SLOT_LAST - 2:SLOT_LAST + 1], gdn, mc, mn,
                         mm[:, 0, :ML_HEADS], ssd,
                         gate_all.reshape(bs, SLOT, D_FF)[:, SLOT_LAST - 1:SLOT_LAST + 1]))
    y_sample = x.reshape(bs, SLOT, D_MODEL)[:, SLOT_FIRST:SLOT_LAST + 1]

    p_out = [jnp.stack(s, axis=0) for s in zip(*p_states)]
    s_out = [jnp.stack(s, axis=0) for s in zip(*s_states)]
    return (y_prompt, y_sample, *p_out, *s_out)
```

```python
import functools

import jax
import jax.numpy as jnp
from jax import lax
from jax.experimental import pallas as pl
from jax.experimental.pallas import tpu as pltpu

F32 = jnp.float32
BF16 = jnp.bfloat16

D_MODEL = 1024
DEPTH = 2
GDN_HEADS, GDN_DK, GDN_DV = 4, 128, 128
ML_HEADS, ML_DQK, ML_DV = 4, 64, 64
SSD_HEADS, SSD_P, SSD_GROUPS, SSD_N = 4, 64, 2, 128
D_FF = 2816
EPS = 1e-6

CONV_DIM = 2304
C_GQ, C_GK, C_GV, C_SX, C_SB, C_SC = 0, 512, 1024, 1536, 1792, 2048
C_GG, C_MQ, C_MK, C_MV, C_MO, C_SZ, C_GATE = 2304, 2816, 3072, 3328, 3584, 3840, 4096
N_IN = 4224
IN_NBLK = 1408
L_GA, L_GB, L_MI, L_MF, L_DT = 0, 4, 8, 12, 16
L_DMAX, L_RSUM = 0, 4

SUBLANES = 8
SLOT = 8
SLOT_FIRST, SLOT_LAST = 3, 6
FF_BLK = 256
VMEM_LIMIT = 56 * 1024 * 1024

PROMPT_TM, PROMPT_CHUNK, PROMPT_NA = 512, 64, 2
SAMPLE_TM, SAMPLE_NA = 64, 4


def _dot(a, b):
    return jnp.dot(a, b, preferred_element_type=F32)


def _dot_nt(a, b):
    return lax.dot_general(a, b, (((1,), (1,)), ((), ())), preferred_element_type=F32)


def _dot_tn(a, b):
    return lax.dot_general(a, b, (((0,), (0,)), ((), ())), preferred_element_type=F32)


def _sigmoid(x):
    return 1.0 / (1.0 + jnp.exp(-x))


def _silu(x):
    return x * _sigmoid(x)


def _rms(x, w):
    return x * lax.rsqrt(jnp.mean(x * x, axis=-1, keepdims=True) + EPS) * w


def _l2n(x):
    return x * lax.rsqrt(jnp.sum(x * x, axis=-1, keepdims=True) + EPS)


def _gelu_tanh(x):
    return 0.5 * x * (1.0 + jnp.tanh(0.7978845608028654 * (x + 0.044715 * (x * x * x))))


def _col(a, l):
    return a[:, l:l + 1]


def _cumsum_rows(tril_b, x):
    hi = x.astype(BF16)
    r1 = x - hi.astype(F32)
    mid = r1.astype(BF16)
    lo = (r1 - mid.astype(F32)).astype(BF16)
    return _dot(tril_b, hi) + _dot(tril_b, mid) + _dot(tril_b, lo)


def _conv_silu(proj_scr, r0, c, col0, width, cw_ref, cb_ref):
    win = proj_scr[pl.ds(r0, c + SUBLANES), col0:col0 + width]
    acc = cb_ref[0:1, col0:col0 + width] + cw_ref[3:4, col0:col0 + width] * win[SUBLANES:, :]
    for j in range(3):
        shifted = pltpu.roll(win, 3 - j, 0)[SUBLANES:, :]
        acc = acc + cw_ref[j:j + 1, col0:col0 + width] * shifted
    return _silu(acc)


def _gates(proj_scr, r0, c, slot, gprm_ref, tril_b):
    graw = proj_scr[pl.ds(r0 + SUBLANES, c), C_GATE:C_GATE + 128]
    lane = lax.broadcasted_iota(jnp.int32, (c, 128), 1)
    z = graw + gprm_ref[0:1, :]
    soft = jnp.log1p(jnp.exp(-jnp.abs(z)))
    sp = jnp.maximum(z, 0.0) + soft
    log_sig = -(jnp.maximum(-z, 0.0) + soft)
    a_neg = -jnp.exp(gprm_ref[1:2, :])
    is_ga = lane < L_GB
    is_gb = (lane >= L_GB) & (lane < L_MI)
    is_mi = (lane >= L_MI) & (lane < L_MF)
    is_mf = (lane >= L_MF) & (lane < L_DT)
    is_dt = (lane >= L_DT) & (lane < L_DT + SSD_HEADS)
    cum_src = jnp.where(is_ga | is_dt, a_neg * sp, jnp.where(is_mf, log_sig, 0.0))
    elem = jnp.where(is_gb, _sigmoid(graw), jnp.where(is_mi, z, jnp.where(is_dt, sp, 0.0)))
    if slot:
        rr = lax.broadcasted_iota(jnp.int32, (c, 128), 0)
        valid = (rr >= SLOT_FIRST) & (rr <= SLOT_LAST)
        cum_src = jnp.where(valid, cum_src, 0.0)
        elem = jnp.where(valid, elem, jnp.where(is_mi, -jnp.inf, 0.0))
    cum = _cumsum_rows(tril_b, cum_src)
    parts = [cum, elem]
    if 2 * c < 128:
        parts.append(jnp.zeros((128 - 2 * c, 128), F32))
    zt = jnp.concatenate(parts, axis=0).T
    return cum, elem, zt


def _pass_a(j, *, c, slot, na, proj_scr, cw_ref, cb_ref, gprm_ref,
            cum_s, col_s, gqx_s, gob_s, ml_s, sy_s, sce_s, sdh_s):
    chunks = [j * na + a for a in range(na)]
    r0s = [pl.multiple_of(i * c, c) for i in chunks]
    conv = functools.partial(_conv_silu, proj_scr, c=c, cw_ref=cw_ref, cb_ref=cb_ref)

    ii = lax.broadcasted_iota(jnp.int32, (c, c), 0)
    jj = lax.broadcasted_iota(jnp.int32, (c, c), 1)
    tril = ii >= jj
    strict = ii > jj
    eye = (ii == jj).astype(F32)
    tril_b = tril.astype(BF16)
    lane = lax.broadcasted_iota(jnp.int32, (c, 128), 1)

    gates = [_gates(proj_scr, r0, c, slot, gprm_ref, tril_b) for r0 in r0s]
    cum = [g[0] for g in gates]
    elem = [g[1] for g in gates]
    zt = [g[2] for g in gates]
    for a in range(na):
        cum_s[chunks[a]] = cum[a]

    def row_cum(a, l):
        return zt[a][l:l + 1, 0:c]

    def row_elem(a, l):
        return zt[a][l:l + 1, c:2 * c]

    it = [(a, h) for a in range(na) for h in range(GDN_HEADS)]
    n = range(len(it))
    q = [_l2n(conv(r0=r0s[a], col0=C_GQ + h * GDN_DK, width=GDN_DK)) * (GDN_DK ** -0.5) for a, h in it]
    k = [_l2n(conv(r0=r0s[a], col0=C_GK + h * GDN_DK, width=GDN_DK)) for a, h in it]
    v = [conv(r0=r0s[a], col0=C_GV + h * GDN_DV, width=GDN_DV) for a, h in it]
    gam_c = [_col(cum[a], L_GA + h) for a, h in it]
    beta_c = [_col(elem[a], L_GB + h) for a, h in it]
    dmat = [jnp.exp(jnp.where(tril, gam_c[x] - row_cum(a, L_GA + h), -jnp.inf)) for x, (a, h) in enumerate(it)]
    pk = [-jnp.where(strict, _dot_nt(k[x], k[x]) * dmat[x] * beta_c[x], 0.0) for x in n]
    t_inv = [eye + pk[x] for x in n]
    for _ in range(c.bit_length() - 2):
        pk = [_dot(pk[x], pk[x]) for x in n]
        t_inv = [t_inv[x] + _dot(t_inv[x], pk[x]) for x in n]
    eg = [jnp.exp(gam_c[x]) for x in n]
    uw = [_dot(t_inv[x], jnp.concatenate([v[x] * beta_c[x], k[x] * (beta_c[x] * eg[x])], axis=1)) for x in n]
    qk = [_dot_nt(q[x], k[x]) * dmat[x] for x in n]
    kd = [k[x] * jnp.exp(gam_c[x][c - 1:c, :] - gam_c[x]) for x in n]
    quw = [_dot(qk[x], uw[x]) for x in n]
    kuw = [_dot_tn(kd[x], uw[x]) for x in n]
    for x, (a, h) in enumerate(it):
        idx = chunks[a] * GDN_HEADS + h
        gqx_s[idx, 0:c, :] = q[x] * eg[x] - quw[x][:, GDN_DV:]
        gqx_s[idx, c:c + GDN_DK, :] = kuw[x][:, GDN_DV:]
        gob_s[idx, 0:c, :] = quw[x][:, :GDN_DV]
        gob_s[idx, c:c + GDN_DK, :] = kuw[x][:, :GDN_DV]

    it = [(a, h) for a in range(na) for h in range(ML_HEADS)]
    n = range(len(it))

    def piece(a, base, h):
        return proj_scr[pl.ds(r0s[a] + SUBLANES, c), base + h * ML_DQK:base + (h + 1) * ML_DQK]

    mq = [piece(a, C_MQ, h) for a, h in it]
    mk = [piece(a, C_MK, h) * (ML_DQK ** -0.5) for a, h in it]
    mv = [piece(a, C_MV, h) for a, h in it]
    b_c = [_col(cum[a], L_MF + h) for a, h in it]
    i_c = [_col(elem[a], L_MI + h) for a, h in it]
    d = [jnp.where(tril, b_c[x] - row_cum(a, L_MF + h) + row_elem(a, L_MI + h), -jnp.inf)
         for x, (a, h) in enumerate(it)]
    dmax = [jnp.max(d[x], axis=-1, keepdims=True) for x in n]
    dsafe = [jnp.where(dmax[x] == -jnp.inf, 0.0, dmax[x]) for x in n]
    s0 = [_dot_nt(mq[x], mk[x]) * jnp.exp(d[x] - dsafe[x]) for x in n]
    num0 = [_dot(s0[x], mv[x]) for x in n]
    rsum0 = [jnp.sum(s0[x], axis=-1, keepdims=True) for x in n]
    kw0 = [mk[x] * jnp.exp(b_c[x][c - 1:c, :] - b_c[x] + i_c[x] - dsafe[x][c - 1:c, :]) for x in n]
    kv0 = [_dot_tn(kw0[x], mv[x]) for x in n]
    ks0 = [jnp.sum(kw0[x], axis=0, keepdims=True) for x in n]
    for x, (a, h) in enumerate(it):
        idx = chunks[a] * ML_HEADS + h
        ml_s[idx, 0:c, :] = num0[x]
        ml_s[idx, c:c + ML_DQK, :] = kv0[x]
        ml_s[idx, c + ML_DQK:c + ML_DQK + 1, :] = ks0[x]
    for a in range(na):
        cols = jnp.zeros((c, 128), F32)
        for h in range(ML_HEADS):
            x = a * ML_HEADS + h
            cols = jnp.where(lane == L_DMAX + h, dmax[x], jnp.where(lane == L_RSUM + h, rsum0[x], cols))
        col_s[chunks[a]] = cols

    rep = SSD_HEADS // SSD_GROUPS
    gi = [(a, g) for a in range(na) for g in range(SSD_GROUPS)]
    bg = [conv(r0=r0s[a], col0=C_SB + g * SSD_N, width=SSD_N) for a, g in gi]
    cg = [conv(r0=r0s[a], col0=C_SC + g * SSD_N, width=SSD_N) for a, g in gi]
    xg = [conv(r0=r0s[a], col0=C_SX + g * rep * SSD_P, width=rep * SSD_P) for a, g in gi]
    cb_raw = [_dot_nt(cg[y], bg[y]) for y in range(len(gi))]
    it = [(a, h) for a in range(na) for h in range(SSD_HEADS)]
    n = range(len(it))
    grp = [a * SSD_GROUPS + h // rep for a, h in it]
    xs = [xg[grp[x]][:, (h % rep) * SSD_P:(h % rep + 1) * SSD_P] for x, (a, h) in enumerate(it)]
    sg_c = [_col(cum[a], L_DT + h) for a, h in it]
    dt_c = [_col(elem[a], L_DT + h) for a, h in it]
    cb = [cb_raw[grp[x]] * jnp.exp(jnp.where(tril, sg_c[x] - row_cum(a, L_DT + h), -jnp.inf))
          * row_elem(a, L_DT + h) for x, (a, h) in enumerate(it)]
    y0 = [_dot(cb[x], xs[x]) + gprm_ref[2:3, L_DT + h:L_DT + h + 1] * xs[x] for x, (a, h) in enumerate(it)]
    dh = [_dot_tn(xs[x] * (jnp.exp(sg_c[x][c - 1:c, :] - sg_c[x]) * dt_c[x]), bg[grp[x]]) for x in n]
    for x, (a, h) in enumerate(it):
        idx = chunks[a] * SSD_HEADS + h
        sce_s[idx] = cg[grp[x]] * jnp.exp(sg_c[x])
        sdh_s[idx] = dh[x]
    for a in range(na):
        sy_s[chunks[a]] = jnp.concatenate(y0[a * SSD_HEADS:(a + 1) * SSD_HEADS], axis=1)


def _pass_b(i, *, c, slot, proj_scr, mix_scr, gdnn_ref, mln_ref, ssdn_ref,
            cum_s, col_s, gqx_s, gob_s, ml_s, sy_s, sce_s, sdh_s, gdn_o, mc_o, mn_o, mm_o, ssd_o):
    r0 = pl.multiple_of(i * c, c)
    seq = i if slot else 0
    rows = pl.ds(r0, c)
    prow = pl.ds(r0 + SUBLANES, c)
    cum = cum_s[i]
    cols = col_s[i]
    last = cum[c - 1:c, :]
    e_last = jnp.exp(last)

    hs = range(GDN_HEADS)
    s_old = [gdn_o[seq, h] for h in hs]
    r = [_dot(gqx_s[i * GDN_HEADS + h], s_old[h]) for h in hs]
    ob = [gob_s[i * GDN_HEADS + h] for h in hs]
    for h in hs:
        gdn_o[seq, h] = e_last[:, L_GA + h:L_GA + h + 1] * s_old[h] - r[h][c:, :] + ob[h][c:, :]
        gg = proj_scr[prow, C_GG + h * GDN_DV:C_GG + (h + 1) * GDN_DV]
        mix_scr[rows, h * GDN_DV:(h + 1) * GDN_DV] = (
            _rms(r[h][:c, :] + ob[h][:c, :], gdnn_ref[...]) * _silu(gg))

    hs = range(ML_HEADS)
    m_row = mm_o[seq, 0:1, :]
    lane_row = lax.broadcasted_iota(jnp.int32, (1, 128), 1)
    mq = [proj_scr[prow, C_MQ + h * ML_DQK:C_MQ + (h + 1) * ML_DQK] for h in hs]
    c_old = [mc_o[seq, h] for h in hs]
    n_old = [mn_o[seq, h:h + 1, :] for h in hs]
    qc = [_dot(mq[h], c_old[h]) for h in hs]
    qn = [jnp.sum(mq[h] * n_old[h], axis=-1, keepdims=True) for h in hs]
    m0 = [m_row[:, h:h + 1] for h in hs]
    dmax = [_col(cols, L_DMAX + h) for h in hs]
    inter = [_col(cum, L_MF + h) + m0[h] for h in hs]
    m_t = [jnp.maximum(inter[h], dmax[h]) for h in hs]
    w_inter = [jnp.exp(inter[h] - m_t[h]) for h in hs]
    w_intra = [jnp.exp(dmax[h] - m_t[h]) for h in hs]
    blk = [ml_s[i * ML_HEADS + h] for h in hs]
    num = [qc[h] * w_inter[h] + w_intra[h] * blk[h][0:c, :] for h in hs]
    den = [qn[h] * w_inter[h] + w_intra[h] * _col(cols, L_RSUM + h) for h in hs]
    den = [jnp.maximum(jnp.abs(den[h]), jnp.exp(-m_t[h])) for h in hs]
    hh = [num[h] / den[h] for h in hs]
    m_new = [m_t[h][c - 1:c, :] for h in hs]
    w_c = [jnp.exp(last[:, L_MF + h:L_MF + h + 1] + m0[h] - m_new[h]) for h in hs]
    w_l = [w_intra[h][c - 1:c, :] for h in hs]
    m_row_new = m_row
    h_parts = []
    for h in hs:
        mc_o[seq, h] = w_c[h] * c_old[h] + w_l[h] * blk[h][c:c + ML_DQK, :]
        mn_o[seq, h:h + 1, :] = w_c[h] * n_old[h] + w_l[h] * blk[h][c + ML_DQK:c + ML_DQK + 1, :]
        m_row_new = jnp.where(lane_row == h, m_new[h], m_row_new)
        mo = proj_scr[prow, C_MO + h * ML_DV:C_MO + (h + 1) * ML_DV]
        h_parts.append(_rms(_sigmoid(mo) * hh[h], mln_ref[...]))
    mm_o[seq, 0:1, :] = m_row_new
    mix_scr[rows, GDN_HEADS * GDN_DV:GDN_HEADS * GDN_DV + ML_HEADS * ML_DV] = jnp.concatenate(h_parts, axis=1)

    hs = range(SSD_HEADS)
    h_old = [ssd_o[seq, h] for h in hs]
    yh = [_dot_nt(sce_s[i * SSD_HEADS + h], h_old[h]) for h in hs]
    for h in hs:
        ssd_o[seq, h] = e_last[:, L_DT + h:L_DT + h + 1] * h_old[h] + sdh_s[i * SSD_HEADS + h]
    sz = proj_scr[prow, C_SZ:C_SZ + SSD_HEADS * SSD_P]
    y_all = (sy_s[i] + jnp.concatenate(yh, axis=1)) * _silu(sz)
    mix_scr[rows, GDN_HEADS * GDN_DV + ML_HEADS * ML_DV:D_MODEL] = _rms(y_all, ssdn_ref[...])


def _mixer_kernel(*refs, tm, c, slot, nt, na):
    if slot:
        (x_ref, tail_ref, win_ref, wout_ref, cw_ref, cb_ref, gprm_ref, npre_ref, npost_ref, gdnn_ref,
         mln_ref, ssdn_ref, gdn_i, mc_i, mn_i, mm_i, ssd_i,
         o_ref, conv_o, gdn_o, mc_o, mn_o, mm_o, ssd_o, proj_scr, mix_scr, *ab) = refs
    else:
        (x_ref, win_ref, wout_ref, cw_ref, cb_ref, gprm_ref, npre_ref, npost_ref, gdnn_ref,
         mln_ref, ssdn_ref,
         o_ref, conv_o, gdn_o, mc_o, mn_o, mm_o, ssd_o, proj_scr, mix_scr, *ab) = refs
    names = ("cum_s", "col_s", "gqx_s", "gob_s", "ml_s", "sy_s", "sce_s", "sdh_s")
    ab = dict(zip(names, ab))
    t = pl.program_id(1)

    if slot:
        proj_scr[0:SUBLANES, :] = jnp.zeros((SUBLANES, N_IN), F32)
        gdn_o[...] = gdn_i[...]
        mc_o[...] = mc_i[...]
        mn_o[...] = mn_i[...]
        mm_o[...] = mm_i[...]
        ssd_o[...] = ssd_i[...]
    else:
        @pl.when(t == 0)
        def _():
            proj_scr[0:SUBLANES, :] = jnp.zeros((SUBLANES, N_IN), F32)
            gdn_o[...] = jnp.zeros(gdn_o.shape, F32)
            mc_o[...] = jnp.zeros(mc_o.shape, F32)
            mn_o[...] = jnp.zeros(mn_o.shape, F32)
            mm_o[...] = jnp.zeros(mm_o.shape, F32)
            ssd_o[...] = jnp.zeros(ssd_o.shape, F32)

    x = x_ref[...]
    hn = _rms(x, npre_ref[...]).astype(BF16)
    for nb in range(N_IN // IN_NBLK):
        cols = slice(nb * IN_NBLK, (nb + 1) * IN_NBLK)
        proj_scr[SUBLANES:SUBLANES + tm, cols] = _dot(hn, win_ref[:, cols])

    if slot:
        rr = lax.broadcasted_iota(jnp.int32, (tm, CONV_DIM), 0)
        is_tail = (rr % SLOT) < SLOT_FIRST
        pre = jnp.where(is_tail, tail_ref[...], proj_scr[SUBLANES:SUBLANES + tm, 0:CONV_DIM])
        proj_scr[SUBLANES:SUBLANES + tm, 0:CONV_DIM] = pre
        conv_o[...] = pre

    pass_a = functools.partial(_pass_a, c=c, slot=slot, na=na, proj_scr=proj_scr, cw_ref=cw_ref,
                               cb_ref=cb_ref, gprm_ref=gprm_ref, **ab)
    pass_b = functools.partial(_pass_b, c=c, slot=slot, proj_scr=proj_scr, mix_scr=mix_scr,
                               gdnn_ref=gdnn_ref, mln_ref=mln_ref, ssdn_ref=ssdn_ref,
                               gdn_o=gdn_o, mc_o=mc_o, mn_o=mn_o, mm_o=mm_o, ssd_o=ssd_o, **ab)

    def body_a(j, carry):
        pass_a(j)
        return carry

    def body_b(i, carry):
        pass_b(i)
        return carry

    lax.fori_loop(0, tm // (c * na), body_a, 0)
    lax.fori_loop(0, tm // c, body_b, 0)

    if not slot:
        last_rows = proj_scr[tm:tm + SUBLANES, 0:CONV_DIM]
        proj_scr[0:SUBLANES, 0:CONV_DIM] = last_rows

        @pl.when(t == nt - 1)
        def _():
            conv_o[0] = last_rows

    out = _dot(mix_scr[...].astype(BF16), wout_ref[...])
    o_ref[...] = x + _rms(out, npost_ref[...])


def _ffn_kernel(*refs, tm, slot, nt):
    if slot:
        (x_ref, ftail_ref, wup_ref, wdn_ref, fw_ref, fb_ref, npre_ref, npost_ref,
         o_ref, gate_o, tails_scr) = refs
    else:
        (x_ref, wup_ref, wdn_ref, fw_ref, fb_ref, npre_ref, npost_ref,
         o_ref, gate_o, tails_scr) = refs
    t = pl.program_id(1)

    if slot:
        tails_scr[...] = jnp.zeros(tails_scr.shape, F32)
    else:
        @pl.when(t == 0)
        def _():
            tails_scr[...] = jnp.zeros(tails_scr.shape, F32)

    x = x_ref[...]
    hn = _rms(x, npre_ref[...]).astype(BF16)
    acc = jnp.zeros((tm, D_MODEL), F32)
    for blk in range(D_FF // FF_BLK):
        cols = slice(blk * FF_BLK, (blk + 1) * FF_BLK)
        gate = _dot(hn, wup_ref[:, cols])
        val = _dot(hn, wup_ref[:, D_FF + blk * FF_BLK:D_FF + (blk + 1) * FF_BLK])
        if slot:
            rr = lax.broadcasted_iota(jnp.int32, (tm, FF_BLK), 0) % SLOT
            gate = jnp.where((rr >= SLOT_FIRST - 2) & (rr < SLOT_FIRST), ftail_ref[:, cols], gate)
            gate_o[:, cols] = gate
        full = jnp.concatenate([tails_scr[:, cols], gate], axis=0)
        conv = fb_ref[0:1, cols] + fw_ref[2:3, cols] * gate
        for j in range(2):
            conv = conv + fw_ref[j:j + 1, cols] * pltpu.roll(full, 2 - j, 0)[SUBLANES:, :]
        last_rows = full[tm:tm + SUBLANES, :]
        tails_scr[:, cols] = last_rows
        if not slot:
            @pl.when(t == nt - 1)
            def _():
                gate_o[0, :, cols] = last_rows
        act = (_gelu_tanh(conv) * val).astype(BF16)
        acc = acc + _dot(act, wdn_ref[cols, :])
    o_ref[...] = x + _rms(acc, npost_ref[...])


def _const_spec(shape, layer):
    nd = len(shape)
    return pl.BlockSpec((None,) + tuple(shape), lambda b, t: (layer,) + (0,) * nd,
                        pipeline_mode=pl.Buffered(1))


def _mixer_call(x, layer, prm, *, slot, states=None, tail=None):
    rows = x.shape[0]
    if slot:
        tm, c, nt, na = SAMPLE_TM, SLOT, 1, SAMPLE_NA
        nseq = tm // SLOT
    else:
        tm, c, na = PROMPT_TM, PROMPT_CHUNK, PROMPT_NA
        nt = 2048 // tm
        nseq = 1
    nch = tm // c
    ngrp = rows // (tm * nt)
    nb = ngrp * nseq
    row_spec = lambda w: pl.BlockSpec((tm, w), lambda b, t: (b * nt + t, 0))
    st_out = lambda *dims: pl.BlockSpec((nseq,) + dims, lambda b, t: (b,) + (0,) * len(dims))
    st_in = lambda *dims: pl.BlockSpec((None, nseq) + dims, lambda b, t: (layer, b) + (0,) * len(dims))

    in_specs = [row_spec(D_MODEL)]
    args = [x]
    if slot:
        in_specs.append(row_spec(CONV_DIM))
        args.append(tail)
    in_specs += [
        _const_spec((D_MODEL, N_IN), layer), _const_spec((D_MODEL, D_MODEL), layer),
        _const_spec((4, CONV_DIM), layer), _const_spec((1, CONV_DIM), layer),
        _const_spec((8, 128), layer), _const_spec((1, D_MODEL), layer), _const_spec((1, D_MODEL), layer),
        _const_spec((1, GDN_DV), layer), _const_spec((1, ML_DV), layer),
        _const_spec((1, SSD_HEADS * SSD_P), layer)]
    args += [prm["w_in"], prm["w_out"], prm["conv_w"], prm["conv_b"], prm["gprm"], prm["norm_mix_pre"],
             prm["norm_mix_post"], prm["gdn_norm"], prm["mlstm_norm"], prm["ssd_norm"]]
    if slot:
        in_specs += [st_in(GDN_HEADS, GDN_DK, GDN_DV), st_in(ML_HEADS, ML_DQK, ML_DV),
                     st_in(ML_HEADS, ML_DQK), st_in(SUBLANES, 128), st_in(SSD_HEADS, SSD_P, SSD_N)]
        args += list(states)

    if slot:
        conv_spec, conv_shape = row_spec(CONV_DIM), (rows, CONV_DIM)
    else:
        conv_spec, conv_shape = st_out(SUBLANES, CONV_DIM), (nb, SUBLANES, CONV_DIM)
    out_specs = [row_spec(D_MODEL), conv_spec,
                 st_out(GDN_HEADS, GDN_DK, GDN_DV), st_out(ML_HEADS, ML_DQK, ML_DV),
                 st_out(ML_HEADS, ML_DQK), st_out(SUBLANES, 128), st_out(SSD_HEADS, SSD_P, SSD_N)]
    out_shape = [jax.ShapeDtypeStruct((rows, D_MODEL), F32), jax.ShapeDtypeStruct(conv_shape, F32),
                 jax.ShapeDtypeStruct((nb, GDN_HEADS, GDN_DK, GDN_DV), F32),
                 jax.ShapeDtypeStruct((nb, ML_HEADS, ML_DQK, ML_DV), F32),
                 jax.ShapeDtypeStruct((nb, ML_HEADS, ML_DQK), F32),
                 jax.ShapeDtypeStruct((nb, SUBLANES, 128), F32),
                 jax.ShapeDtypeStruct((nb, SSD_HEADS, SSD_P, SSD_N), F32)]
    scratch = [
        pltpu.VMEM((tm + SUBLANES, N_IN), F32),
        pltpu.VMEM((tm, D_MODEL), F32),
        pltpu.VMEM((nch, c, 128), F32),
        pltpu.VMEM((nch, c, 128), F32),
        pltpu.VMEM((nch * GDN_HEADS, c + GDN_DK, GDN_DV), F32),
        pltpu.VMEM((nch * GDN_HEADS, c + GDN_DK, GDN_DV), F32),
        pltpu.VMEM((nch * ML_HEADS, c + ML_DQK + SUBLANES, ML_DV), F32),
        pltpu.VMEM((nch, c, SSD_HEADS * SSD_P), F32),
        pltpu.VMEM((nch * SSD_HEADS, c, SSD_N), F32),
        pltpu.VMEM((nch * SSD_HEADS, SSD_P, SSD_N), F32),
    ]
    return pl.pallas_call(
        functools.partial(_mixer_kernel, tm=tm, c=c, slot=slot, nt=nt, na=na),
        grid=(ngrp, nt), in_specs=in_specs, out_specs=out_specs, out_shape=out_shape,
        scratch_shapes=scratch,
        compiler_params=pltpu.CompilerParams(dimension_semantics=("arbitrary", "arbitrary"),
                                             vmem_limit_bytes=VMEM_LIMIT),
        name=("mixer_sample" if slot else "mixer_prompt"),
    )(*args)


def _ffn_call(x, layer, prm, *, slot, tail=None):
    rows = x.shape[0]
    tm = 512
    nt = 1 if slot else 2048 // tm
    ngrp = rows // (tm * nt)
    row_spec = lambda w: pl.BlockSpec((tm, w), lambda b, t: (b * nt + t, 0))
    in_specs = [row_spec(D_MODEL)]
    args = [x]
    if slot:
        in_specs.append(row_spec(D_FF))
        args.append(tail)
    in_specs += [_const_spec((D_MODEL, 2 * D_FF), layer), _const_spec((D_FF, D_MODEL), layer),
                 _const_spec((3, D_FF), layer), _const_spec((1, D_FF), layer),
                 _const_spec((1, D_MODEL), layer), _const_spec((1, D_MODEL), layer)]
    args += [prm["ffn_w_up"], prm["ffn_w_down"], prm["ffn_conv_w"], prm["ffn_conv_b"],
             prm["norm_ffn_pre"], prm["norm_ffn_post"]]
    if slot:
        gate_spec, gate_shape = row_spec(D_FF), (rows, D_FF)
    else:
        gate_spec = pl.BlockSpec((1, SUBLANES, D_FF), lambda b, t: (b, 0, 0))
        gate_shape = (ngrp, SUBLANES, D_FF)
    return pl.pallas_call(
        functools.partial(_ffn_kernel, tm=tm, slot=slot, nt=nt),
        grid=(ngrp, nt), in_specs=in_specs, out_specs=[row_spec(D_MODEL), gate_spec],
        out_shape=[jax.ShapeDtypeStruct((rows, D_MODEL), F32), jax.ShapeDtypeStruct(gate_shape, F32)],
        scratch_shapes=[pltpu.VMEM((SUBLANES, D_FF), F32)],
        compiler_params=pltpu.CompilerParams(dimension_semantics=("arbitrary", "arbitrary"),
                                             vmem_limit_bytes=VMEM_LIMIT),
        name=("ffn_sample" if slot else "ffn_prompt"),
    )(*args)


def _prepare_params(norm_mix_pre, norm_mix_post, norm_ffn_pre, norm_ffn_post, w_in, conv_w, conv_b,
                    gdn_a_log, gdn_dt_bias, gdn_norm, mlstm_i_bias, mlstm_f_bias, mlstm_norm,
                    ssd_a_log, ssd_dt_bias, ssd_d, ssd_norm, w_out, ffn_w_up, ffn_conv_w, ffn_conv_b,
                    ffn_w_down):
    w_in_p = jnp.concatenate(
        [w_in[..., :2816], w_in[..., 2824:3848], w_in[..., 3856:4112], w_in[..., 2816:2824],
         w_in[..., 3848:3856], w_in[..., 4112:4116],
         jnp.zeros((DEPTH, D_MODEL, N_IN - 4116), w_in.dtype)], axis=-1).astype(BF16)
    z4 = jnp.zeros((DEPTH, 4), F32)
    pad = jnp.zeros((DEPTH, 128 - 20), F32)
    gprm = jnp.stack(
        [jnp.concatenate([gdn_dt_bias, z4, mlstm_i_bias, mlstm_f_bias, ssd_dt_bias, pad], axis=-1),
         jnp.concatenate([gdn_a_log, z4, z4, z4, ssd_a_log, pad], axis=-1),
         jnp.concatenate([z4, z4, z4, z4, ssd_d, pad], axis=-1)]
        + [jnp.zeros((DEPTH, 128), F32)] * 5, axis=1)
    row = lambda a: a[:, None, :]
    return dict(
        w_in=w_in_p, w_out=w_out.astype(BF16), conv_w=conv_w, conv_b=row(conv_b), gprm=gprm,
        norm_mix_pre=row(norm_mix_pre), norm_mix_post=row(norm_mix_post),
        norm_ffn_pre=row(norm_ffn_pre), norm_ffn_post=row(norm_ffn_post),
        gdn_norm=row(gdn_norm), mlstm_norm=row(mlstm_norm), ssd_norm=row(ssd_norm),
        ffn_w_up=ffn_w_up.astype(BF16), ffn_w_down=ffn_w_down.astype(BF16),
        ffn_conv_w=ffn_conv_w, ffn_conv_b=row(ffn_conv_b))


def kernel(x_prompt, x_sample, state_conv, state_gdn, state_mlstm_c, state_mlstm_n, state_mlstm_m, state_ssd, state_ffn_conv, norm_mix_pre, norm_mix_post, norm_ffn_pre, norm_ffn_post, w_in, conv_w, conv_b, gdn_a_log, gdn_dt_bias, gdn_norm, mlstm_i_bias, mlstm_f_bias, mlstm_norm, ssd_a_log, ssd_dt_bias, ssd_d, ssd_norm, w_out, ffn_w_up, ffn_conv_w, ffn_conv_b, ffn_w_down):
    prm = _prepare_params(norm_mix_pre, norm_mix_post, norm_ffn_pre, norm_ffn_post, w_in, conv_w, conv_b,
                          gdn_a_log, gdn_dt_bias, gdn_norm, mlstm_i_bias, mlstm_f_bias, mlstm_norm,
                          ssd_a_log, ssd_dt_bias, ssd_d, ssd_norm, w_out, ffn_w_up, ffn_conv_w,
                          ffn_conv_b, ffn_w_down)
    bp, lp, _ = x_prompt.shape
    bs, ls, _ = x_sample.shape

    x = x_prompt.reshape(bp * lp, D_MODEL)
    p_states = []
    for layer in range(DEPTH):
        x, conv8, gdn, mc, mn, mm, ssd = _mixer_call(x, layer, prm, slot=False)
        x, gate8 = _ffn_call(x, layer, prm, slot=False)
        p_states.append((conv8[:, SUBLANES - 3:], gdn, mc, mn, mm[:, 0, :ML_HEADS], ssd,
                         gate8[:, SUBLANES - 2:]))
    y_prompt = x.reshape(bp, lp, D_MODEL)

    x = jnp.pad(x_sample, ((0, 0), (SLOT_FIRST, SLOT - SLOT_FIRST - ls), (0, 0))).reshape(bs * SLOT, D_MODEL)
    mm_in = jnp.pad(state_mlstm_m[:, :, None, :], ((0, 0), (0, 0), (0, SUBLANES - 1), (0, 128 - ML_HEADS)))
    conv_tail = jnp.pad(state_conv, ((0, 0), (0, 0), (0, SLOT - 3), (0, 0))).reshape(DEPTH, bs * SLOT, CONV_DIM)
    ffn_tail = jnp.pad(state_ffn_conv, ((0, 0), (0, 0), (SLOT_FIRST - 2, SLOT - SLOT_FIRST), (0, 0))
                       ).reshape(DEPTH, bs * SLOT, D_FF)
    s_states = []
    for layer in range(DEPTH):
        x, conv_all, gdn, mc, mn, mm, ssd = _mixer_call(
            x, layer, prm, slot=True, tail=conv_tail[layer],
            states=(state_gdn, state_mlstm_c, state_mlstm_n, mm_in, state_ssd))
        x, gate_all = _ffn_call(x, layer, prm, slot=True, tail=ffn_tail[layer])
        s_states.append((conv_all.reshape(bs, SLOT, CONV_DIM)[:, SLOT_LAST - 2:SLOT_LAST + 1], gdn, mc, mn,
                         mm[:, 0, :ML_HEADS], ssd,
                         gate_all.reshape(bs, SLOT, D_FF)[:, SLOT_LAST - 1:SLOT_LAST + 1]))
    y_sample = x.reshape(bs, SLOT, D_MODEL)[:, SLOT_FIRST:SLOT_LAST + 1]

    p_out = [jnp.stack(s, axis=0) for s in zip(*p_states)]
    s_out = [jnp.stack(s, axis=0) for s in zip(*s_states)]
    return (y_prompt, y_sample, *p_out, *s_out)
```

```python
import functools

import jax
import jax.numpy as jnp
from jax import lax
from jax.experimental import pallas as pl
from jax.experimental.pallas import tpu as pltpu

F32 = jnp.float32
BF16 = jnp.bfloat16

D_MODEL = 1024
DEPTH = 2
GDN_HEADS, GDN_DK, GDN_DV = 4, 128, 128
ML_HEADS, ML_DQK, ML_DV = 4, 64, 64
SSD_HEADS, SSD_P, SSD_GROUPS, SSD_N = 4, 64, 2, 128
D_FF = 2816
EPS = 1e-6

CONV_DIM = 2304
C_GQ, C_GK, C_GV, C_SX, C_SB, C_SC = 0, 512, 1024, 1536, 1792, 2048
C_GG, C_MQ, C_MK, C_MV, C_MO, C_SZ, C_GATE = 2304, 2816, 3072, 3328, 3584, 3840, 4096
N_IN = 4224
IN_NBLK = 1408
L_GA, L_GB, L_MI, L_MF, L_DT = 0, 4, 8, 12, 16

SUBLANES = 8
SLOT = 8
SLOT_FIRST, SLOT_LAST = 3, 6
FF_BLK = 256
VMEM_LIMIT = 56 * 1024 * 1024

PROMPT_TM, PROMPT_CHUNK, PROMPT_NA = 512, 64, 2
SAMPLE_TM, SAMPLE_NA = 64, 4


def _dot(a, b):
    return jnp.dot(a, b, preferred_element_type=F32)


def _dot_nt(a, b):
    return lax.dot_general(a, b, (((1,), (1,)), ((), ())), preferred_element_type=F32)


def _dot_tn(a, b):
    return lax.dot_general(a, b, (((0,), (0,)), ((), ())), preferred_element_type=F32)


def _sigmoid(x):
    return 1.0 / (1.0 + jnp.exp(-x))


def _silu(x):
    return x * _sigmoid(x)


def _rms(x, w):
    return x * lax.rsqrt(jnp.mean(x * x, axis=-1, keepdims=True) + EPS) * w


def _l2n(x):
    return x * lax.rsqrt(jnp.sum(x * x, axis=-1, keepdims=True) + EPS)


def _gelu_tanh(x):
    return 0.5 * x * (1.0 + jnp.tanh(0.7978845608028654 * (x + 0.044715 * (x * x * x))))


def _col(a, l):
    return a[:, l:l + 1]


def _cumsum_rows(tril_b, x):
    hi = x.astype(BF16)
    r1 = x - hi.astype(F32)
    mid = r1.astype(BF16)
    lo = (r1 - mid.astype(F32)).astype(BF16)
    return _dot(tril_b, hi) + _dot(tril_b, mid) + _dot(tril_b, lo)


def _conv_silu(proj_scr, r0, c, col0, width, cw_ref, cb_ref):
    win = proj_scr[pl.ds(r0, c + SUBLANES), col0:col0 + width]
    acc = cb_ref[0:1, col0:col0 + width] + cw_ref[3:4, col0:col0 + width] * win[SUBLANES:, :]
    for j in range(3):
        shifted = pltpu.roll(win, 3 - j, 0)[SUBLANES:, :]
        acc = acc + cw_ref[j:j + 1, col0:col0 + width] * shifted
    return _silu(acc)


def _gates(proj_scr, r0, c, slot, gprm_ref, tril_b):
    graw = proj_scr[pl.ds(r0 + SUBLANES, c), C_GATE:C_GATE + 128]
    lane = lax.broadcasted_iota(jnp.int32, (c, 128), 1)
    z = graw + gprm_ref[0:1, :]
    soft = jnp.log(1.0 + jnp.exp(-jnp.abs(z)))
    sp = jnp.maximum(z, 0.0) + soft
    log_sig = -(jnp.maximum(-z, 0.0) + soft)
    a_neg = -jnp.exp(gprm_ref[1:2, :])
    is_ga = lane < L_GB
    is_gb = (lane >= L_GB) & (lane < L_MI)
    is_mi = (lane >= L_MI) & (lane < L_MF)
    is_mf = (lane >= L_MF) & (lane < L_DT)
    is_dt = (lane >= L_DT) & (lane < L_DT + SSD_HEADS)
    cum_src = jnp.where(is_ga | is_dt, a_neg * sp, jnp.where(is_mf, log_sig, 0.0))
    elem = jnp.where(is_gb, _sigmoid(graw), jnp.where(is_mi, z, jnp.where(is_dt, sp, 0.0)))
    if slot:
        rr = lax.broadcasted_iota(jnp.int32, (c, 128), 0)
        valid = (rr >= SLOT_FIRST) & (rr <= SLOT_LAST)
        cum_src = jnp.where(valid, cum_src, 0.0)
        elem = jnp.where(valid, elem, jnp.where(is_mi, -jnp.inf, 0.0))
    cum = _cumsum_rows(tril_b, cum_src)
    if c == 128:
        return cum, elem, cum.T, elem.T
    parts = [cum, elem]
    if 2 * c < 128:
        parts.append(jnp.zeros((128 - 2 * c, 128), F32))
    zt = jnp.concatenate(parts, axis=0).T
    return cum, elem, zt[:, 0:c], zt[:, c:2 * c]


def _pass_a(j, *, c, slot, na, proj_scr, cw_ref, cb_ref, gprm_ref,
            cum_s, col_s, gqx_s, gob_s, ml_s, sy_s, sce_s, sdh_s):
    chunks = [j * na + a for a in range(na)]
    r0s = [pl.multiple_of(i * c, c) for i in chunks]
    conv = functools.partial(_conv_silu, proj_scr, c=c, cw_ref=cw_ref, cb_ref=cb_ref)

    ii = lax.broadcasted_iota(jnp.int32, (c, c), 0)
    jj = lax.broadcasted_iota(jnp.int32, (c, c), 1)
    tril = ii >= jj
    strict = ii > jj
    eye = (ii == jj).astype(F32)
    tril_b = tril.astype(BF16)
    lane = lax.broadcasted_iota(jnp.int32, (c, 128), 1)

    gates = [_gates(proj_scr, r0, c, slot, gprm_ref, tril_b) for r0 in r0s]
    cum = [g[0] for g in gates]
    elem = [g[1] for g in gates]
    cum_t = [g[2] for g in gates]
    elem_t = [g[3] for g in gates]
    for a in range(na):
        cum_s[chunks[a]] = cum[a]

    def row_cum(a, l):
        return cum_t[a][l:l + 1, :]

    def row_elem(a, l):
        return elem_t[a][l:l + 1, :]

    it = [(a, h) for a in range(na) for h in range(GDN_HEADS)]
    n = range(len(it))
    gd, ml, sd = {}, {}, {}

    def g_k():
        k = [_l2n(conv(r0=r0s[a], col0=C_GK + h * GDN_DK, width=GDN_DK)) for a, h in it]
        gam_c = [_col(cum[a], L_GA + h) for a, h in it]
        beta_c = [_col(elem[a], L_GB + h) for a, h in it]
        dmat = [jnp.exp(jnp.where(tril, gam_c[x] - row_cum(a, L_GA + h), -jnp.inf))
                for x, (a, h) in enumerate(it)]
        pk = [-jnp.where(strict, _dot_nt(k[x], k[x]) * dmat[x] * beta_c[x], 0.0) for x in n]
        gd.update(k=k, gam_c=gam_c, beta_c=beta_c, dmat=dmat, pk=pk, t_inv=[eye + pk[x] for x in n])

    def g_neumann():
        pk = [_dot(gd["pk"][x], gd["pk"][x]) for x in n]
        gd.update(pk=pk, t_inv=[gd["t_inv"][x] + _dot(gd["t_inv"][x], pk[x]) for x in n])

    def g_q():
        gd["q"] = [_l2n(conv(r0=r0s[a], col0=C_GQ + h * GDN_DK, width=GDN_DK)) * (GDN_DK ** -0.5)
                   for a, h in it]

    def g_v():
        v = [conv(r0=r0s[a], col0=C_GV + h * GDN_DV, width=GDN_DV) for a, h in it]
        eg = [jnp.exp(gd["gam_c"][x]) for x in n]
        gd.update(eg=eg, rhs=[jnp.concatenate([v[x] * gd["beta_c"][x],
                                               gd["k"][x] * (gd["beta_c"][x] * eg[x])], axis=1) for x in n])

    def g_uw():
        gd["uw"] = [_dot(gd["t_inv"][x], gd["rhs"][x]) for x in n]

    def g_qk():
        gd["qk"] = [_dot_nt(gd["q"][x], gd["k"][x]) * gd["dmat"][x] for x in n]
        gd["kd"] = [gd["k"][x] * jnp.exp(gd["gam_c"][x][c - 1:c, :] - gd["gam_c"][x]) for x in n]

    def g_out():
        quw = [_dot(gd["qk"][x], gd["uw"][x]) for x in n]
        kuw = [_dot_tn(gd["kd"][x], gd["uw"][x]) for x in n]
        for x, (a, h) in enumerate(it):
            idx = chunks[a] * GDN_HEADS + h
            gqx_s[idx, 0:c, :] = gd["q"][x] * gd["eg"][x] - quw[x][:, GDN_DV:]
            gqx_s[idx, c:c + GDN_DK, :] = kuw[x][:, GDN_DV:]
            gob_s[idx, 0:c, :] = quw[x][:, :GDN_DV]
            gob_s[idx, c:c + GDN_DK, :] = kuw[x][:, :GDN_DV]

    def piece(a, base, h):
        return proj_scr[pl.ds(r0s[a] + SUBLANES, c), base + h * ML_DQK:base + (h + 1) * ML_DQK]

    def m_1():
        mq = [piece(a, C_MQ, h) for a, h in it]
        mk = [piece(a, C_MK, h) * (ML_DQK ** -0.5) for a, h in it]
        b_c = [_col(cum[a], L_MF + h) for a, h in it]
        d = [jnp.where(tril, b_c[x] - row_cum(a, L_MF + h) + row_elem(a, L_MI + h), -jnp.inf)
             for x, (a, h) in enumerate(it)]
        dmax = [jnp.max(d[x], axis=-1, keepdims=True) for x in n]
        dsafe = [jnp.where(dmax[x] == -jnp.inf, 0.0, dmax[x]) for x in n]
        s0 = [_dot_nt(mq[x], mk[x]) * jnp.exp(d[x] - dsafe[x]) for x in n]
        for a in range(na):
            cols = jnp.zeros((c, 128), F32)
            for h in range(ML_HEADS):
                cols = jnp.where(lane == L_MF + h, dmax[a * ML_HEADS + h], cols)
            col_s[chunks[a]] = cols
        ml.update(mk=mk, b_c=b_c, dsafe=dsafe, s0=s0)

    def m_2():
        mv = [piece(a, C_MV, h) for a, h in it]
        i_c = [_col(elem[a], L_MI + h) for a, h in it]
        ones_col = (lax.broadcasted_iota(jnp.int32, (c, 128 - ML_DV), 1) == 0).astype(F32)
        v_aug = [jnp.concatenate([mv[x], ones_col], axis=1) for x in n]
        num0 = [_dot(ml["s0"][x], v_aug[x]) for x in n]
        b_c, dsafe = ml["b_c"], ml["dsafe"]
        kw0 = [ml["mk"][x] * jnp.exp(b_c[x][c - 1:c, :] - b_c[x] + i_c[x] - dsafe[x][c - 1:c, :]) for x in n]
        kv0 = [_dot_tn(kw0[x], v_aug[x]) for x in n]
        for x, (a, h) in enumerate(it):
            idx = chunks[a] * ML_HEADS + h
            ml_s[idx, 0:c, :] = num0[x]
            ml_s[idx, c:c + ML_DQK, :] = kv0[x]

    rep = SSD_HEADS // SSD_GROUPS
    gi = [(a, g) for a in range(na) for g in range(SSD_GROUPS)]
    grp = [a * SSD_GROUPS + h // rep for a, h in it]

    def s_1():
        bg = [conv(r0=r0s[a], col0=C_SB + g * SSD_N, width=SSD_N) for a, g in gi]
        cg = [conv(r0=r0s[a], col0=C_SC + g * SSD_N, width=SSD_N) for a, g in gi]
        sd.update(bg=bg, cg=cg, cb_raw=[_dot_nt(cg[y], bg[y]) for y in range(len(gi))])

    def s_2():
        xg = [conv(r0=r0s[a], col0=C_SX + g * rep * SSD_P, width=rep * SSD_P) for a, g in gi]
        xs = [xg[grp[x]][:, (h % rep) * SSD_P:(h % rep + 1) * SSD_P] for x, (a, h) in enumerate(it)]
        sg_c = [_col(cum[a], L_DT + h) for a, h in it]
        dt_c = [_col(elem[a], L_DT + h) for a, h in it]
        cb = [sd["cb_raw"][grp[x]] * jnp.exp(jnp.where(tril, sg_c[x] - row_cum(a, L_DT + h), -jnp.inf))
              * row_elem(a, L_DT + h) for x, (a, h) in enumerate(it)]
        y0 = [_dot(cb[x], xs[x]) + gprm_ref[2:3, L_DT + h:L_DT + h + 1] * xs[x]
              for x, (a, h) in enumerate(it)]
        dh = [_dot_tn(xs[x] * (jnp.exp(sg_c[x][c - 1:c, :] - sg_c[x]) * dt_c[x]), sd["bg"][grp[x]]) for x in n]
        for x, (a, h) in enumerate(it):
            idx = chunks[a] * SSD_HEADS + h
            sce_s[idx] = sd["cg"][grp[x]] * jnp.exp(sg_c[x])
            sdh_s[idx] = dh[x]
        for a in range(na):
            sy_s[chunks[a]] = jnp.concatenate(y0[a * SSD_HEADS:(a + 1) * SSD_HEADS], axis=1)

    chain = [g_k] + [g_neumann] * (c.bit_length() - 2) + [g_uw]
    fill = [g_q, g_v, m_1, m_2, s_1, s_2]
    for pos, link in enumerate(chain):
        link()
        if pos < len(fill):
            fill[pos]()
    for rest in fill[len(chain):]:
        rest()
    g_qk()
    g_out()


def _pass_b(i, *, c, slot, proj_scr, mix_scr, gdnn_ref, mln_ref, ssdn_ref,
            cum_s, col_s, gqx_s, gob_s, ml_s, sy_s, sce_s, sdh_s, mlc_s, gdn_o, mm_o, ssd_o):
    r0 = pl.multiple_of(i * c, c)
    seq = i if slot else 0
    rows = pl.ds(r0, c)
    prow = pl.ds(r0 + SUBLANES, c)
    hs = range(GDN_HEADS)
    cum = cum_s[i]
    dmx = col_s[i]
    last = cum[c - 1:c, :]
    e_last = jnp.exp(last)
    lane_row = lax.broadcasted_iota(jnp.int32, (1, 128), 1)

    s_old = [gdn_o[seq, h] for h in hs]
    c_old = [mlc_s[seq * ML_HEADS + h] for h in hs]
    h_old = [ssd_o[seq, h] for h in hs]
    mq = [proj_scr[prow, C_MQ + h * ML_DQK:C_MQ + (h + 1) * ML_DQK] for h in hs]

    r = [_dot(gqx_s[i * GDN_HEADS + h], s_old[h]) for h in hs]
    full = [_dot(mq[h], c_old[h]) for h in hs]
    yh = [_dot_nt(sce_s[i * SSD_HEADS + h], h_old[h]) for h in hs]

    m_row = mm_o[seq, 0:1, :]
    inter = cum + m_row
    m_t = jnp.maximum(inter, dmx)
    w_inter = jnp.exp(inter - m_t)
    w_intra = jnp.exp(dmx - m_t)
    e_neg_m = jnp.exp(-m_t)
    is_m = (lane_row >= L_MF) & (lane_row < L_MF + ML_HEADS)
    m_new_row = jnp.where(is_m, m_t[c - 1:c, :], 0.0)
    w_c_row = jnp.exp(last + m_row - m_new_row)
    w_l_row = w_intra[c - 1:c, :]

    ob = [gob_s[i * GDN_HEADS + h] for h in hs]
    blk = [ml_s[i * ML_HEADS + h] for h in hs]
    for h in hs:
        gdn_o[seq, h] = _col(e_last, L_GA + h) * s_old[h] - r[h][c:, :] + ob[h][c:, :]
    for h in hs:
        mlc_s[seq * ML_HEADS + h] = (_col(w_c_row, L_MF + h) * c_old[h]
                                     + _col(w_l_row, L_MF + h) * blk[h][c:c + ML_DQK, :])
    mm_o[seq, 0:1, :] = m_new_row
    for h in hs:
        ssd_o[seq, h] = _col(e_last, L_DT + h) * h_old[h] + sdh_s[i * SSD_HEADS + h]

    for h in hs:
        gg = proj_scr[prow, C_GG + h * GDN_DV:C_GG + (h + 1) * GDN_DV]
        mix_scr[rows, h * GDN_DV:(h + 1) * GDN_DV] = (
            _rms(r[h][:c, :] + ob[h][:c, :], gdnn_ref[...]) * _silu(gg))
    h_parts = []
    for h in hs:
        fl = full[h] * _col(w_inter, L_MF + h) + _col(w_intra, L_MF + h) * blk[h][0:c, :]
        den = jnp.maximum(jnp.abs(fl[:, ML_DV:ML_DV + 1]), _col(e_neg_m, L_MF + h))
        mo = proj_scr[prow, C_MO + h * ML_DV:C_MO + (h + 1) * ML_DV]
        h_parts.append(_rms(_sigmoid(mo) * (fl[:, :ML_DV] / den), mln_ref[...]))
    mix_scr[rows, GDN_HEADS * GDN_DV:GDN_HEADS * GDN_DV + ML_HEADS * ML_DV] = jnp.concatenate(h_parts, axis=1)
    sz = proj_scr[prow, C_SZ:C_SZ + SSD_HEADS * SSD_P]
    y_all = (sy_s[i] + jnp.concatenate(yh, axis=1)) * _silu(sz)
    mix_scr[rows, GDN_HEADS * GDN_DV + ML_HEADS * ML_DV:D_MODEL] = _rms(y_all, ssdn_ref[...])


def _ml_state_in(c_mat, n_row):
    eye = (lax.broadcasted_iota(jnp.int32, (ML_DQK, ML_DQK), 0)
           == lax.broadcasted_iota(jnp.int32, (ML_DQK, ML_DQK), 1)).astype(F32)
    n_col = jnp.sum(eye * n_row, axis=-1, keepdims=True)
    first = lax.broadcasted_iota(jnp.int32, (ML_DQK, 128 - ML_DV), 1) == 0
    return jnp.concatenate([c_mat, jnp.where(first, n_col, 0.0)], axis=1)


def _ml_state_out(c_aug):
    eye = (lax.broadcasted_iota(jnp.int32, (ML_DQK, ML_DQK), 0)
           == lax.broadcasted_iota(jnp.int32, (ML_DQK, ML_DQK), 1)).astype(F32)
    n_row = jnp.sum(eye * c_aug[:, ML_DV:ML_DV + 1], axis=0, keepdims=True)
    return c_aug[:, :ML_DV], n_row


def _mixer_kernel(*refs, tm, c, slot, nt, na):
    if slot:
        (x_ref, tail_ref, win_ref, wout_ref, cw_ref, cb_ref, gprm_ref, npre_ref, npost_ref, gdnn_ref,
         mln_ref, ssdn_ref, gdn_i, mc_i, mn_i, mm_i, ssd_i,
         o_ref, conv_o, gdn_o, mc_o, mn_o, mm_o, ssd_o, proj_scr, mix_scr, *ab) = refs
    else:
        (x_ref, win_ref, wout_ref, cw_ref, cb_ref, gprm_ref, npre_ref, npost_ref, gdnn_ref,
         mln_ref, ssdn_ref,
         o_ref, conv_o, gdn_o, mc_o, mn_o, mm_o, ssd_o, proj_scr, mix_scr, *ab) = refs
    names = ("cum_s", "col_s", "gqx_s", "gob_s", "ml_s", "sy_s", "sce_s", "sdh_s", "mlc_s")
    ab = dict(zip(names, ab))
    mlc_s = ab["mlc_s"]
    nseq = gdn_o.shape[0]
    t = pl.program_id(1)

    if slot:
        proj_scr[0:SUBLANES, :] = jnp.zeros((SUBLANES, N_IN), F32)
        gdn_o[...] = gdn_i[...]
        mm_o[...] = mm_i[...]
        ssd_o[...] = ssd_i[...]

        def load_ml(s, carry):
            for h in range(ML_HEADS):
                mlc_s[s * ML_HEADS + h] = _ml_state_in(mc_i[s, h], mn_i[s, h:h + 1, :])
            return carry

        lax.fori_loop(0, nseq, load_ml, 0)
    else:
        @pl.when(t == 0)
        def _():
            proj_scr[0:SUBLANES, :] = jnp.zeros((SUBLANES, N_IN), F32)
            gdn_o[...] = jnp.zeros(gdn_o.shape, F32)
            mlc_s[...] = jnp.zeros(mlc_s.shape, F32)
            mm_o[...] = jnp.zeros(mm_o.shape, F32)
            ssd_o[...] = jnp.zeros(ssd_o.shape, F32)

    x = x_ref[...]
    hn = _rms(x, npre_ref[...]).astype(BF16)
    for nb in range(N_IN // IN_NBLK):
        cols = slice(nb * IN_NBLK, (nb + 1) * IN_NBLK)
        proj_scr[SUBLANES:SUBLANES + tm, cols] = _dot(hn, win_ref[:, cols])

    if slot:
        rr = lax.broadcasted_iota(jnp.int32, (tm, CONV_DIM), 0)
        is_tail = (rr % SLOT) < SLOT_FIRST
        pre = jnp.where(is_tail, tail_ref[...], proj_scr[SUBLANES:SUBLANES + tm, 0:CONV_DIM])
        proj_scr[SUBLANES:SUBLANES + tm, 0:CONV_DIM] = pre
        conv_o[...] = pre

    pass_a = functools.partial(_pass_a, c=c, slot=slot, na=na, proj_scr=proj_scr, cw_ref=cw_ref,
                               cb_ref=cb_ref, gprm_ref=gprm_ref,
                               **{k: v for k, v in ab.items() if k != "mlc_s"})
    pass_b = functools.partial(_pass_b, c=c, slot=slot, proj_scr=proj_scr, mix_scr=mix_scr,
                               gdnn_ref=gdnn_ref, mln_ref=mln_ref, ssdn_ref=ssdn_ref,
                               gdn_o=gdn_o, mm_o=mm_o, ssd_o=ssd_o, **ab)

    def body_a(j, carry):
        pass_a(j)
        return carry

    def body_b(i, carry):
        pass_b(i)
        return carry

    lax.fori_loop(0, tm // (c * na), body_a, 0)
    lax.fori_loop(0, tm // c, body_b, 0)

    def store_ml(s, carry):
        for h in range(ML_HEADS):
            c_mat, n_row = _ml_state_out(mlc_s[s * ML_HEADS + h])
            mc_o[s, h] = c_mat
            mn_o[s, h:h + 1, :] = n_row
        return carry

    if slot:
        lax.fori_loop(0, nseq, store_ml, 0)
    else:
        @pl.when(t == nt - 1)
        def _():
            store_ml(0, 0)

    if not slot:
        last_rows = proj_scr[tm:tm + SUBLANES, 0:CONV_DIM]
        proj_scr[0:SUBLANES, 0:CONV_DIM] = last_rows

        @pl.when(t == nt - 1)
        def _():
            conv_o[0] = last_rows

    out = _dot(mix_scr[...].astype(BF16), wout_ref[...])
    o_ref[...] = x + _rms(out, npost_ref[...])


def _ffn_kernel(*refs, tm, slot, nt):
    if slot:
        (x_ref, ftail_ref, wup_ref, wdn_ref, fw_ref, fb_ref, npre_ref, npost_ref,
         o_ref, gate_o, tails_scr) = refs
    else:
        (x_ref, wup_ref, wdn_ref, fw_ref, fb_ref, npre_ref, npost_ref,
         o_ref, gate_o, tails_scr) = refs
    t = pl.program_id(1)

    if slot:
        tails_scr[...] = jnp.zeros(tails_scr.shape, F32)
    else:
        @pl.when(t == 0)
        def _():
            tails_scr[...] = jnp.zeros(tails_scr.shape, F32)

    x = x_ref[...]
    hn = _rms(x, npre_ref[...]).astype(BF16)
    acc = jnp.zeros((tm, D_MODEL), F32)
    for blk in range(D_FF // FF_BLK):
        cols = slice(blk * FF_BLK, (blk + 1) * FF_BLK)
        gate = _dot(hn, wup_ref[:, cols])
        val = _dot(hn, wup_ref[:, D_FF + blk * FF_BLK:D_FF + (blk + 1) * FF_BLK])
        if slot:
            rr = lax.broadcasted_iota(jnp.int32, (tm, FF_BLK), 0) % SLOT
            gate = jnp.where((rr >= SLOT_FIRST - 2) & (rr < SLOT_FIRST), ftail_ref[:, cols], gate)
            gate_o[:, cols] = gate
        full = jnp.concatenate([tails_scr[:, cols], gate], axis=0)
        conv = fb_ref[0:1, cols] + fw_ref[2:3, cols] * gate
        for j in range(2):
            conv = conv + fw_ref[j:j + 1, cols] * pltpu.roll(full, 2 - j, 0)[SUBLANES:, :]
        last_rows = full[tm:tm + SUBLANES, :]
        tails_scr[:, cols] = last_rows
        if not slot:
            @pl.when(t == nt - 1)
            def _():
                gate_o[0, :, cols] = last_rows
        act = (_gelu_tanh(conv) * val).astype(BF16)
        acc = acc + _dot(act, wdn_ref[cols, :])
    o_ref[...] = x + _rms(acc, npost_ref[...])


def _const_spec(shape, layer):
    nd = len(shape)
    return pl.BlockSpec((None,) + tuple(shape), lambda b, t: (layer,) + (0,) * nd,
                        pipeline_mode=pl.Buffered(1))


def _mixer_call(x, layer, prm, *, slot, states=None, tail=None):
    rows = x.shape[0]
    if slot:
        tm, c, nt, na = SAMPLE_TM, SLOT, 1, SAMPLE_NA
        nseq = tm // SLOT
    else:
        tm, c, na = PROMPT_TM, PROMPT_CHUNK, PROMPT_NA
        nt = 2048 // tm
        nseq = 1
    nch = tm // c
    ngrp = rows // (tm * nt)
    nb = ngrp * nseq
    row_spec = lambda w: pl.BlockSpec((tm, w), lambda b, t: (b * nt + t, 0))
    st_out = lambda *dims: pl.BlockSpec((nseq,) + dims, lambda b, t: (b,) + (0,) * len(dims))
    st_in = lambda *dims: pl.BlockSpec((None, nseq) + dims, lambda b, t: (layer, b) + (0,) * len(dims))

    in_specs = [row_spec(D_MODEL)]
    args = [x]
    if slot:
        in_specs.append(row_spec(CONV_DIM))
        args.append(tail)
    in_specs += [
        _const_spec((D_MODEL, N_IN), layer), _const_spec((D_MODEL, D_MODEL), layer),
        _const_spec((4, CONV_DIM), layer), _const_spec((1, CONV_DIM), layer),
        _const_spec((8, 128), layer), _const_spec((1, D_MODEL), layer), _const_spec((1, D_MODEL), layer),
        _const_spec((1, GDN_DV), layer), _const_spec((1, ML_DV), layer),
        _const_spec((1, SSD_HEADS * SSD_P), layer)]
    args += [prm["w_in"], prm["w_out"], prm["conv_w"], prm["conv_b"], prm["gprm"], prm["norm_mix_pre"],
             prm["norm_mix_post"], prm["gdn_norm"], prm["mlstm_norm"], prm["ssd_norm"]]
    if slot:
        in_specs += [st_in(GDN_HEADS, GDN_DK, GDN_DV), st_in(ML_HEADS, ML_DQK, ML_DV),
                     st_in(ML_HEADS, ML_DQK), st_in(SUBLANES, 128), st_in(SSD_HEADS, SSD_P, SSD_N)]
        args += list(states)

    if slot:
        conv_spec, conv_shape = row_spec(CONV_DIM), (rows, CONV_DIM)
    else:
        conv_spec, conv_shape = st_out(SUBLANES, CONV_DIM), (nb, SUBLANES, CONV_DIM)
    out_specs = [row_spec(D_MODEL), conv_spec,
                 st_out(GDN_HEADS, GDN_DK, GDN_DV), st_out(ML_HEADS, ML_DQK, ML_DV),
                 st_out(ML_HEADS, ML_DQK), st_out(SUBLANES, 128), st_out(SSD_HEADS, SSD_P, SSD_N)]
    out_shape = [jax.ShapeDtypeStruct((rows, D_MODEL), F32), jax.ShapeDtypeStruct(conv_shape, F32),
                 jax.ShapeDtypeStruct((nb, GDN_HEADS, GDN_DK, GDN_DV), F32),
                 jax.ShapeDtypeStruct((nb, ML_HEADS, ML_DQK, ML_DV), F32),
                 jax.ShapeDtypeStruct((nb, ML_HEADS, ML_DQK), F32),
                 jax.ShapeDtypeStruct((nb, SUBLANES, 128), F32),
                 jax.ShapeDtypeStruct((nb, SSD_HEADS, SSD_P, SSD_N), F32)]
    scratch = [
        pltpu.VMEM((tm + SUBLANES, N_IN), F32),
        pltpu.VMEM((tm, D_MODEL), F32),
        pltpu.VMEM((nch, c, 128), F32),
        pltpu.VMEM((nch, c, 128), F32),
        pltpu.VMEM((nch * GDN_HEADS, c + GDN_DK, GDN_DV), F32),
        pltpu.VMEM((nch * GDN_HEADS, c + GDN_DK, GDN_DV), F32),
        pltpu.VMEM((nch * ML_HEADS, c + ML_DQK, 128), F32),
        pltpu.VMEM((nch, c, SSD_HEADS * SSD_P), F32),
        pltpu.VMEM((nch * SSD_HEADS, c, SSD_N), F32),
        pltpu.VMEM((nch * SSD_HEADS, SSD_P, SSD_N), F32),
        pltpu.VMEM((nseq * ML_HEADS, ML_DQK, 128), F32),
    ]
    return pl.pallas_call(
        functools.partial(_mixer_kernel, tm=tm, c=c, slot=slot, nt=nt, na=na),
        grid=(ngrp, nt), in_specs=in_specs, out_specs=out_specs, out_shape=out_shape,
        scratch_shapes=scratch,
        compiler_params=pltpu.CompilerParams(dimension_semantics=("arbitrary", "arbitrary"),
                                             vmem_limit_bytes=VMEM_LIMIT),
        name=("mixer_sample" if slot else "mixer_prompt"),
    )(*args)


def _ffn_call(x, layer, prm, *, slot, tail=None):
    rows = x.shape[0]
    tm = 512
    nt = 1 if slot else 2048 // tm
    ngrp = rows // (tm * nt)
    row_spec = lambda w: pl.BlockSpec((tm, w), lambda b, t: (b * nt + t, 0))
    in_specs = [row_spec(D_MODEL)]
    args = [x]
    if slot:
        in_specs.append(row_spec(D_FF))
        args.append(tail)
    in_specs += [_const_spec((D_MODEL, 2 * D_FF), layer), _const_spec((D_FF, D_MODEL), layer),
                 _const_spec((3, D_FF), layer), _const_spec((1, D_FF), layer),
                 _const_spec((1, D_MODEL), layer), _const_spec((1, D_MODEL), layer)]
    args += [prm["ffn_w_up"], prm["ffn_w_down"], prm["ffn_conv_w"], prm["ffn_conv_b"],
             prm["norm_ffn_pre"], prm["norm_ffn_post"]]
    if slot:
        gate_spec, gate_shape = row_spec(D_FF), (rows, D_FF)
    else:
        gate_spec = pl.BlockSpec((1, SUBLANES, D_FF), lambda b, t: (b, 0, 0))
        gate_shape = (ngrp, SUBLANES, D_FF)
    return pl.pallas_call(
        functools.partial(_ffn_kernel, tm=tm, slot=slot, nt=nt),
        grid=(ngrp, nt), in_specs=in_specs, out_specs=[row_spec(D_MODEL), gate_spec],
        out_shape=[jax.ShapeDtypeStruct((rows, D_MODEL), F32), jax.ShapeDtypeStruct(gate_shape, F32)],
        scratch_shapes=[pltpu.VMEM((SUBLANES, D_FF), F32)],
        compiler_params=pltpu.CompilerParams(dimension_semantics=("arbitrary", "arbitrary"),
                                             vmem_limit_bytes=VMEM_LIMIT),
        name=("ffn_sample" if slot else "ffn_prompt"),
    )(*args)


def _prepare_params(norm_mix_pre, norm_mix_post, norm_ffn_pre, norm_ffn_post, w_in, conv_w, conv_b,
                    gdn_a_log, gdn_dt_bias, gdn_norm, mlstm_i_bias, mlstm_f_bias, mlstm_norm,
                    ssd_a_log, ssd_dt_bias, ssd_d, ssd_norm, w_out, ffn_w_up, ffn_conv_w, ffn_conv_b,
                    ffn_w_down):
    w_in_p = jnp.concatenate(
        [w_in[..., :2816], w_in[..., 2824:3848], w_in[..., 3856:4112], w_in[..., 2816:2824],
         w_in[..., 3848:3856], w_in[..., 4112:4116],
         jnp.zeros((DEPTH, D_MODEL, N_IN - 4116), w_in.dtype)], axis=-1).astype(BF16)
    z4 = jnp.zeros((DEPTH, 4), F32)
    pad = jnp.zeros((DEPTH, 128 - 20), F32)
    gprm = jnp.stack(
        [jnp.concatenate([gdn_dt_bias, z4, mlstm_i_bias, mlstm_f_bias, ssd_dt_bias, pad], axis=-1),
         jnp.concatenate([gdn_a_log, z4, z4, z4, ssd_a_log, pad], axis=-1),
         jnp.concatenate([z4, z4, z4, z4, ssd_d, pad], axis=-1)]
        + [jnp.zeros((DEPTH, 128), F32)] * 5, axis=1)
    row = lambda a: a[:, None, :]
    return dict(
        w_in=w_in_p, w_out=w_out.astype(BF16), conv_w=conv_w, conv_b=row(conv_b), gprm=gprm,
        norm_mix_pre=row(norm_mix_pre), norm_mix_post=row(norm_mix_post),
        norm_ffn_pre=row(norm_ffn_pre), norm_ffn_post=row(norm_ffn_post),
        gdn_norm=row(gdn_norm), mlstm_norm=row(mlstm_norm), ssd_norm=row(ssd_norm),
        ffn_w_up=ffn_w_up.astype(BF16), ffn_w_down=ffn_w_down.astype(BF16),
        ffn_conv_w=ffn_conv_w, ffn_conv_b=row(ffn_conv_b))


def kernel(x_prompt, x_sample, state_conv, state_gdn, state_mlstm_c, state_mlstm_n, state_mlstm_m, state_ssd, state_ffn_conv, norm_mix_pre, norm_mix_post, norm_ffn_pre, norm_ffn_post, w_in, conv_w, conv_b, gdn_a_log, gdn_dt_bias, gdn_norm, mlstm_i_bias, mlstm_f_bias, mlstm_norm, ssd_a_log, ssd_dt_bias, ssd_d, ssd_norm, w_out, ffn_w_up, ffn_conv_w, ffn_conv_b, ffn_w_down):
    prm = _prepare_params(norm_mix_pre, norm_mix_post, norm_ffn_pre, norm_ffn_post, w_in, conv_w, conv_b,
                          gdn_a_log, gdn_dt_bias, gdn_norm, mlstm_i_bias, mlstm_f_bias, mlstm_norm,
                          ssd_a_log, ssd_dt_bias, ssd_d, ssd_norm, w_out, ffn_w_up, ffn_conv_w,
                          ffn_conv_b, ffn_w_down)
    bp, lp, _ = x_prompt.shape
    bs, ls, _ = x_sample.shape

    x = x_prompt.reshape(bp * lp, D_MODEL)
    p_states = []
    for layer in range(DEPTH):
        x, conv8, gdn, mc, mn, mm, ssd = _mixer_call(x, layer, prm, slot=False)
        x, gate8 = _ffn_call(x, layer, prm, slot=False)
        p_states.append((conv8[:, SUBLANES - 3:], gdn, mc, mn, mm[:, 0, L_MF:L_MF + ML_HEADS], ssd,
                         gate8[:, SUBLANES - 2:]))
    y_prompt = x.reshape(bp, lp, D_MODEL)

    x = jnp.pad(x_sample, ((0, 0), (SLOT_FIRST, SLOT - SLOT_FIRST - ls), (0, 0))).reshape(bs * SLOT, D_MODEL)
    mm_in = jnp.pad(state_mlstm_m[:, :, None, :],
                    ((0, 0), (0, 0), (0, SUBLANES - 1), (L_MF, 128 - L_MF - ML_HEADS)))
    conv_tail = jnp.pad(state_conv, ((0, 0), (0, 0), (0, SLOT - 3), (0, 0))).reshape(DEPTH, bs * SLOT, CONV_DIM)
    ffn_tail = jnp.pad(state_ffn_conv, ((0, 0), (0, 0), (SLOT_FIRST - 2, SLOT - SLOT_FIRST), (0, 0))
                       ).reshape(DEPTH, bs * SLOT, D_FF)
    s_states = []
    for layer in range(DEPTH):
        x, conv_all, gdn, mc, mn, mm, ssd = _mixer_call(
            x, layer, prm, slot=True, tail=conv_tail[layer],
            states=(state_gdn, state_mlstm_c, state_mlstm_n, mm_in, state_ssd))
        x, gate_all = _ffn_call(x, layer, prm, slot=True, tail=ffn_tail[layer])
        s_states.append((conv_all.reshape(bs, SLOT, CONV_DIM)[:, SLOT_LAST - 2:SLOT_LAST + 1], gdn, mc, mn,
                         mm[:, 0, L_MF:L_MF + ML_HEADS], ssd,
                         gate_all.reshape(bs, SLOT, D_FF)[:, SLOT_LAST - 1:SLOT_LAST + 1]))
    y_sample = x.reshape(bs, SLOT, D_MODEL)[:, SLOT_FIRST:SLOT_LAST + 1]

    p_out = [jnp.stack(s, axis=0) for s in zip(*p_states)]
    s_out = [jnp.stack(s, axis=0) for s in zip(*s_states)]
    return (y_prompt, y_sample, *p_out, *s_out)
```

```python
import functools

import jax
import jax.numpy as jnp
from jax import lax
from jax.experimental import pallas as pl
from jax.experimental.pallas import tpu as pltpu

F32 = jnp.float32
BF16 = jnp.bfloat16

D_MODEL = 1024
DEPTH = 2
GDN_HEADS, GDN_DK, GDN_DV = 4, 128, 128
ML_HEADS, ML_DQK, ML_DV = 4, 64, 64
SSD_HEADS, SSD_P, SSD_GROUPS, SSD_N = 4, 64, 2, 128
D_FF = 2816
EPS = 1e-6

CONV_DIM = 2304
C_GQ, C_GK, C_GV, C_SX, C_SB, C_SC = 0, 512, 1024, 1536, 1792, 2048
C_GG, C_MQ, C_MK, C_MV, C_MO, C_SZ, C_GATE = 2304, 2816, 3072, 3328, 3584, 3840, 4096
N_IN = 4224
IN_NBLK = 1408
L_GA, L_GB, L_MI, L_MF, L_DT = 0, 4, 8, 12, 16

SUBLANES = 8
SLOT = 8
SLOT_FIRST, SLOT_LAST = 3, 6
FF_BLK = 256
VMEM_LIMIT = 56 * 1024 * 1024

PROMPT_TM, PROMPT_CHUNK, PROMPT_NA = 512, 64, 4
SAMPLE_TM, SAMPLE_NA = 64, 4
FFN_TM = 1024


def _dot(a, b):
    return jnp.dot(a, b, preferred_element_type=F32)


def _dot_nt(a, b):
    return lax.dot_general(a, b, (((1,), (1,)), ((), ())), preferred_element_type=F32)


def _dot_tn(a, b):
    return lax.dot_general(a, b, (((0,), (0,)), ((), ())), preferred_element_type=F32)


def _sigmoid(x):
    return 1.0 / (1.0 + jnp.exp(-x))


def _silu(x):
    return x * _sigmoid(x)


def _rms(x, w):
    return x * lax.rsqrt(jnp.mean(x * x, axis=-1, keepdims=True) + EPS) * w


def _l2n(x):
    return x * lax.rsqrt(jnp.sum(x * x, axis=-1, keepdims=True) + EPS)


def _gelu_tanh(x):
    return 0.5 * x * (1.0 + jnp.tanh(0.7978845608028654 * (x + 0.044715 * (x * x * x))))


def _col(a, l):
    return a[:, l:l + 1]


def _cumsum_rows(tril_b, x):
    hi = x.astype(BF16)
    r1 = x - hi.astype(F32)
    mid = r1.astype(BF16)
    lo = (r1 - mid.astype(F32)).astype(BF16)
    return _dot(tril_b, hi) + _dot(tril_b, mid) + _dot(tril_b, lo)


def _conv_silu(proj_scr, r0, c, col0, width, cw_ref, cb_ref):
    win = proj_scr[pl.ds(r0, c + SUBLANES), col0:col0 + width]
    acc = cb_ref[0:1, col0:col0 + width] + cw_ref[3:4, col0:col0 + width] * win[SUBLANES:, :]
    for j in range(3):
        shifted = pltpu.roll(win, 3 - j, 0)[SUBLANES:, :]
        acc = acc + cw_ref[j:j + 1, col0:col0 + width] * shifted
    return _silu(acc)


def _gates(proj_scr, r0, c, slot, gprm_ref, tril_b):
    graw = proj_scr[pl.ds(r0 + SUBLANES, c), C_GATE:C_GATE + 128]
    lane = lax.broadcasted_iota(jnp.int32, (c, 128), 1)
    z = graw + gprm_ref[0:1, :]
    soft = jnp.log(1.0 + jnp.exp(-jnp.abs(z)))
    sp = jnp.maximum(z, 0.0) + soft
    log_sig = -(jnp.maximum(-z, 0.0) + soft)
    a_neg = -jnp.exp(gprm_ref[1:2, :])
    is_ga = lane < L_GB
    is_gb = (lane >= L_GB) & (lane < L_MI)
    is_mi = (lane >= L_MI) & (lane < L_MF)
    is_mf = (lane >= L_MF) & (lane < L_DT)
    is_dt = (lane >= L_DT) & (lane < L_DT + SSD_HEADS)
    cum_src = jnp.where(is_ga | is_dt, a_neg * sp, jnp.where(is_mf, log_sig, 0.0))
    elem = jnp.where(is_gb, _sigmoid(graw), jnp.where(is_mi, z, jnp.where(is_dt, sp, 0.0)))
    if slot:
        rr = lax.broadcasted_iota(jnp.int32, (c, 128), 0)
        valid = (rr >= SLOT_FIRST) & (rr <= SLOT_LAST)
        cum_src = jnp.where(valid, cum_src, 0.0)
        elem = jnp.where(valid, elem, jnp.where(is_mi, -jnp.inf, 0.0))
    cum = _cumsum_rows(tril_b, cum_src)
    if c == 128:
        return cum, elem, cum.T, elem.T
    parts = [cum, elem]
    if 2 * c < 128:
        parts.append(jnp.zeros((128 - 2 * c, 128), F32))
    zt = jnp.concatenate(parts, axis=0).T
    return cum, elem, zt[:, 0:c], zt[:, c:2 * c]


def _pass_a(j, *, c, slot, na, proj_scr, cw_ref, cb_ref, gprm_ref,
            cum_s, col_s, gqx_s, gob_s, ml_s, sy_s, sce_s, sdh_s):
    chunks = [j * na + a for a in range(na)]
    r0s = [pl.multiple_of(i * c, c) for i in chunks]
    conv = functools.partial(_conv_silu, proj_scr, c=c, cw_ref=cw_ref, cb_ref=cb_ref)

    ii = lax.broadcasted_iota(jnp.int32, (c, c), 0)
    jj = lax.broadcasted_iota(jnp.int32, (c, c), 1)
    tril = ii >= jj
    strict = ii > jj
    eye = (ii == jj).astype(F32)
    tril_b = tril.astype(BF16)
    lane = lax.broadcasted_iota(jnp.int32, (c, 128), 1)

    gates = [_gates(proj_scr, r0, c, slot, gprm_ref, tril_b) for r0 in r0s]
    cum = [g[0] for g in gates]
    elem = [g[1] for g in gates]
    cum_t = [g[2] for g in gates]
    elem_t = [g[3] for g in gates]
    for a in range(na):
        cum_s[chunks[a]] = cum[a]

    def row_cum(a, l):
        return cum_t[a][l:l + 1, :]

    def row_elem(a, l):
        return elem_t[a][l:l + 1, :]

    it = [(a, h) for a in range(na) for h in range(GDN_HEADS)]
    n = range(len(it))
    gd, ml, sd = {}, {}, {}

    def g_k():
        k = [_l2n(conv(r0=r0s[a], col0=C_GK + h * GDN_DK, width=GDN_DK)) for a, h in it]
        gam_c = [_col(cum[a], L_GA + h) for a, h in it]
        beta_c = [_col(elem[a], L_GB + h) for a, h in it]
        dmat = [jnp.exp(jnp.where(tril, gam_c[x] - row_cum(a, L_GA + h), -jnp.inf))
                for x, (a, h) in enumerate(it)]
        pk = [-jnp.where(strict, _dot_nt(k[x], k[x]) * dmat[x] * beta_c[x], 0.0) for x in n]
        gd.update(k=k, gam_c=gam_c, beta_c=beta_c, dmat=dmat, pk=pk, t_inv=[eye + pk[x] for x in n])

    def g_neumann():
        pk = [_dot(gd["pk"][x], gd["pk"][x]) for x in n]
        gd.update(pk=pk, t_inv=[gd["t_inv"][x] + _dot(gd["t_inv"][x], pk[x]) for x in n])

    def g_q():
        gd["q"] = [_l2n(conv(r0=r0s[a], col0=C_GQ + h * GDN_DK, width=GDN_DK)) * (GDN_DK ** -0.5)
                   for a, h in it]

    def g_v():
        v = [conv(r0=r0s[a], col0=C_GV + h * GDN_DV, width=GDN_DV) for a, h in it]
        eg = [jnp.exp(gd["gam_c"][x]) for x in n]
        gd.update(eg=eg, rhs=[jnp.concatenate([v[x] * gd["beta_c"][x],
                                               gd["k"][x] * (gd["beta_c"][x] * eg[x])], axis=1) for x in n])

    def g_uw():
        gd["uw"] = [_dot(gd["t_inv"][x], gd["rhs"][x]) for x in n]

    def g_qk():
        gd["qk"] = [_dot_nt(gd["q"][x], gd["k"][x]) * gd["dmat"][x] for x in n]
        gd["kd"] = [gd["k"][x] * jnp.exp(gd["gam_c"][x][c - 1:c, :] - gd["gam_c"][x]) for x in n]

    def g_out():
        quw = [_dot(gd["qk"][x], gd["uw"][x]) for x in n]
        kuw = [_dot_tn(gd["kd"][x], gd["uw"][x]) for x in n]
        for x, (a, h) in enumerate(it):
            idx = chunks[a] * GDN_HEADS + h
            gqx_s[idx, 0:c, :] = gd["q"][x] * gd["eg"][x] - quw[x][:, GDN_DV:]
            gqx_s[idx, c:c + GDN_DK, :] = kuw[x][:, GDN_DV:]
            gob_s[idx, 0:c, :] = quw[x][:, :GDN_DV]
            gob_s[idx, c:c + GDN_DK, :] = kuw[x][:, :GDN_DV]

    def piece(a, base, h):
        return proj_scr[pl.ds(r0s[a] + SUBLANES, c), base + h * ML_DQK:base + (h + 1) * ML_DQK]

    def m_1():
        mq = [piece(a, C_MQ, h) for a, h in it]
        mk = [piece(a, C_MK, h) * (ML_DQK ** -0.5) for a, h in it]
        b_c = [_col(cum[a], L_MF + h) for a, h in it]
        d = [jnp.where(tril, b_c[x] - row_cum(a, L_MF + h) + row_elem(a, L_MI + h), -jnp.inf)
             for x, (a, h) in enumerate(it)]
        dmax = [jnp.max(d[x], axis=-1, keepdims=True) for x in n]
        dsafe = [jnp.where(dmax[x] == -jnp.inf, 0.0, dmax[x]) for x in n]
        s0 = [_dot_nt(mq[x], mk[x]) * jnp.exp(d[x] - dsafe[x]) for x in n]
        for a in range(na):
            cols = jnp.zeros((c, 128), F32)
            for h in range(ML_HEADS):
                cols = jnp.where(lane == L_MF + h, dmax[a * ML_HEADS + h], cols)
            col_s[chunks[a]] = cols
        ml.update(mk=mk, b_c=b_c, dsafe=dsafe, s0=s0)

    def m_2():
        mv = [piece(a, C_MV, h) for a, h in it]
        i_c = [_col(elem[a], L_MI + h) for a, h in it]
        ones_col = (lax.broadcasted_iota(jnp.int32, (c, 128 - ML_DV), 1) == 0).astype(F32)
        v_aug = [jnp.concatenate([mv[x], ones_col], axis=1) for x in n]
        num0 = [_dot(ml["s0"][x], v_aug[x]) for x in n]
        b_c, dsafe = ml["b_c"], ml["dsafe"]
        kw0 = [ml["mk"][x] * jnp.exp(b_c[x][c - 1:c, :] - b_c[x] + i_c[x] - dsafe[x][c - 1:c, :]) for x in n]
        kv0 = [_dot_tn(kw0[x], v_aug[x]) for x in n]
        for x, (a, h) in enumerate(it):
            idx = chunks[a] * ML_HEADS + h
            ml_s[idx, 0:c, :] = num0[x]
            ml_s[idx, c:c + ML_DQK, :] = kv0[x]

    rep = SSD_HEADS // SSD_GROUPS
    gi = [(a, g) for a in range(na) for g in range(SSD_GROUPS)]
    grp = [a * SSD_GROUPS + h // rep for a, h in it]

    def s_1():
        bg = [conv(r0=r0s[a], col0=C_SB + g * SSD_N, width=SSD_N) for a, g in gi]
        cg = [conv(r0=r0s[a], col0=C_SC + g * SSD_N, width=SSD_N) for a, g in gi]
        sd.update(bg=bg, cg=cg, cb_raw=[_dot_nt(cg[y], bg[y]) for y in range(len(gi))])

    def s_2():
        xg = [conv(r0=r0s[a], col0=C_SX + g * rep * SSD_P, width=rep * SSD_P) for a, g in gi]
        xs = [xg[grp[x]][:, (h % rep) * SSD_P:(h % rep + 1) * SSD_P] for x, (a, h) in enumerate(it)]
        sg_c = [_col(cum[a], L_DT + h) for a, h in it]
        dt_c = [_col(elem[a], L_DT + h) for a, h in it]
        cb = [sd["cb_raw"][grp[x]] * jnp.exp(jnp.where(tril, sg_c[x] - row_cum(a, L_DT + h), -jnp.inf))
              * row_elem(a, L_DT + h) for x, (a, h) in enumerate(it)]
        y0 = [_dot(cb[x], xs[x]) + gprm_ref[2:3, L_DT + h:L_DT + h + 1] * xs[x]
              for x, (a, h) in enumerate(it)]
        dh = [_dot_tn(xs[x] * (jnp.exp(sg_c[x][c - 1:c, :] - sg_c[x]) * dt_c[x]), sd["bg"][grp[x]]) for x in n]
        for x, (a, h) in enumerate(it):
            idx = chunks[a] * SSD_HEADS + h
            sce_s[idx] = sd["cg"][grp[x]] * jnp.exp(sg_c[x])
            sdh_s[idx] = dh[x]
        for a in range(na):
            sy_s[chunks[a]] = jnp.concatenate(y0[a * SSD_HEADS:(a + 1) * SSD_HEADS], axis=1)

    chain = [g_k] + [g_neumann] * (c.bit_length() - 2) + [g_uw]
    fill = [g_q, g_v, m_1, m_2, s_1, s_2]
    for pos, link in enumerate(chain):
        link()
        if pos < len(fill):
            fill[pos]()
    for rest in fill[len(chain):]:
        rest()
    g_qk()
    g_out()


def _pass_b(i, *, c, slot, proj_scr, mix_scr, gdnn_ref, mln_ref, ssdn_ref,
            cum_s, col_s, gqx_s, gob_s, ml_s, sy_s, sce_s, sdh_s, mlc_s, gdn_o, mm_o, ssd_o):
    r0 = pl.multiple_of(i * c, c)
    seq = i if slot else 0
    rows = pl.ds(r0, c)
    prow = pl.ds(r0 + SUBLANES, c)
    hs = range(GDN_HEADS)
    cum = cum_s[i]
    dmx = col_s[i]
    last = cum[c - 1:c, :]
    e_last = jnp.exp(last)
    lane_row = lax.broadcasted_iota(jnp.int32, (1, 128), 1)

    s_old = [gdn_o[seq, h] for h in hs]
    c_old = [mlc_s[seq * ML_HEADS + h] for h in hs]
    h_old = [ssd_o[seq, h] for h in hs]
    mq = [proj_scr[prow, C_MQ + h * ML_DQK:C_MQ + (h + 1) * ML_DQK] for h in hs]

    r = [_dot(gqx_s[i * GDN_HEADS + h], s_old[h]) for h in hs]
    full = [_dot(mq[h], c_old[h]) for h in hs]
    yh = [_dot_nt(sce_s[i * SSD_HEADS + h], h_old[h]) for h in hs]

    m_row = mm_o[seq, 0:1, :]
    inter = cum + m_row
    m_t = jnp.maximum(inter, dmx)
    w_inter = jnp.exp(inter - m_t)
    w_intra = jnp.exp(dmx - m_t)
    e_neg_m = jnp.exp(-m_t)
    is_m = (lane_row >= L_MF) & (lane_row < L_MF + ML_HEADS)
    m_new_row = jnp.where(is_m, m_t[c - 1:c, :], 0.0)
    w_c_row = jnp.exp(last + m_row - m_new_row)
    w_l_row = w_intra[c - 1:c, :]

    ob = [gob_s[i * GDN_HEADS + h] for h in hs]
    blk = [ml_s[i * ML_HEADS + h] for h in hs]
    for h in hs:
        gdn_o[seq, h] = _col(e_last, L_GA + h) * s_old[h] - r[h][c:, :] + ob[h][c:, :]
    for h in hs:
        mlc_s[seq * ML_HEADS + h] = (_col(w_c_row, L_MF + h) * c_old[h]
                                     + _col(w_l_row, L_MF + h) * blk[h][c:c + ML_DQK, :])
    mm_o[seq, 0:1, :] = m_new_row
    for h in hs:
        ssd_o[seq, h] = _col(e_last, L_DT + h) * h_old[h] + sdh_s[i * SSD_HEADS + h]

    for h in hs:
        gg = proj_scr[prow, C_GG + h * GDN_DV:C_GG + (h + 1) * GDN_DV]
        mix_scr[rows, h * GDN_DV:(h + 1) * GDN_DV] = (
            _rms(r[h][:c, :] + ob[h][:c, :], gdnn_ref[...]) * _silu(gg))
    h_parts = []
    for h in hs:
        fl = full[h] * _col(w_inter, L_MF + h) + _col(w_intra, L_MF + h) * blk[h][0:c, :]
        den = jnp.maximum(jnp.abs(fl[:, ML_DV:ML_DV + 1]), _col(e_neg_m, L_MF + h))
        mo = proj_scr[prow, C_MO + h * ML_DV:C_MO + (h + 1) * ML_DV]
        h_parts.append(_rms(_sigmoid(mo) * (fl[:, :ML_DV] / den), mln_ref[...]))
    mix_scr[rows, GDN_HEADS * GDN_DV:GDN_HEADS * GDN_DV + ML_HEADS * ML_DV] = jnp.concatenate(h_parts, axis=1)
    sz = proj_scr[prow, C_SZ:C_SZ + SSD_HEADS * SSD_P]
    y_all = (sy_s[i] + jnp.concatenate(yh, axis=1)) * _silu(sz)
    mix_scr[rows, GDN_HEADS * GDN_DV + ML_HEADS * ML_DV:D_MODEL] = _rms(y_all, ssdn_ref[...])


def _ml_state_in(c_mat, n_row):
    eye = (lax.broadcasted_iota(jnp.int32, (ML_DQK, ML_DQK), 0)
           == lax.broadcasted_iota(jnp.int32, (ML_DQK, ML_DQK), 1)).astype(F32)
    n_col = jnp.sum(eye * n_row, axis=-1, keepdims=True)
    first = lax.broadcasted_iota(jnp.int32, (ML_DQK, 128 - ML_DV), 1) == 0
    return jnp.concatenate([c_mat, jnp.where(first, n_col, 0.0)], axis=1)


def _ml_state_out(c_aug):
    eye = (lax.broadcasted_iota(jnp.int32, (ML_DQK, ML_DQK), 0)
           == lax.broadcasted_iota(jnp.int32, (ML_DQK, ML_DQK), 1)).astype(F32)
    n_row = jnp.sum(eye * c_aug[:, ML_DV:ML_DV + 1], axis=0, keepdims=True)
    return c_aug[:, :ML_DV], n_row


def _mixer_kernel(*refs, tm, c, slot, nt, na):
    if slot:
        (x_ref, tail_ref, win_ref, wout_ref, cw_ref, cb_ref, gprm_ref, npre_ref, npost_ref, gdnn_ref,
         mln_ref, ssdn_ref, gdn_i, mc_i, mn_i, mm_i, ssd_i,
         o_ref, conv_o, gdn_o, mc_o, mn_o, mm_o, ssd_o, proj_scr, mix_scr, *ab) = refs
    else:
        (x_ref, win_ref, wout_ref, cw_ref, cb_ref, gprm_ref, npre_ref, npost_ref, gdnn_ref,
         mln_ref, ssdn_ref,
         o_ref, conv_o, gdn_o, mc_o, mn_o, mm_o, ssd_o, proj_scr, mix_scr, *ab) = refs
    names = ("cum_s", "col_s", "gqx_s", "gob_s", "ml_s", "sy_s", "sce_s", "sdh_s", "mlc_s")
    ab = dict(zip(names, ab))
    mlc_s = ab["mlc_s"]
    nseq = gdn_o.shape[0]
    t = pl.program_id(1)

    if slot:
        proj_scr[0:SUBLANES, :] = jnp.zeros((SUBLANES, N_IN), F32)
        gdn_o[...] = gdn_i[...]
        mm_o[...] = mm_i[...]
        ssd_o[...] = ssd_i[...]

        def load_ml(s, carry):
            for h in range(ML_HEADS):
                mlc_s[s * ML_HEADS + h] = _ml_state_in(mc_i[s, h], mn_i[s, h:h + 1, :])
            return carry

        lax.fori_loop(0, nseq, load_ml, 0)
    else:
        @pl.when(t == 0)
        def _():
            proj_scr[0:SUBLANES, :] = jnp.zeros((SUBLANES, N_IN), F32)
            gdn_o[...] = jnp.zeros(gdn_o.shape, F32)
            mlc_s[...] = jnp.zeros(mlc_s.shape, F32)
            mm_o[...] = jnp.zeros(mm_o.shape, F32)
            ssd_o[...] = jnp.zeros(ssd_o.shape, F32)

    x = x_ref[...]
    hn = _rms(x, npre_ref[...]).astype(BF16)
    for nb in range(N_IN // IN_NBLK):
        cols = slice(nb * IN_NBLK, (nb + 1) * IN_NBLK)
        proj_scr[SUBLANES:SUBLANES + tm, cols] = _dot(hn, win_ref[:, cols])

    if slot:
        rr = lax.broadcasted_iota(jnp.int32, (tm, CONV_DIM), 0)
        is_tail = (rr % SLOT) < SLOT_FIRST
        pre = jnp.where(is_tail, tail_ref[...], proj_scr[SUBLANES:SUBLANES + tm, 0:CONV_DIM])
        proj_scr[SUBLANES:SUBLANES + tm, 0:CONV_DIM] = pre
        conv_o[...] = pre

    pass_a = functools.partial(_pass_a, c=c, slot=slot, na=na, proj_scr=proj_scr, cw_ref=cw_ref,
                               cb_ref=cb_ref, gprm_ref=gprm_ref,
                               **{k: v for k, v in ab.items() if k != "mlc_s"})
    pass_b = functools.partial(_pass_b, c=c, slot=slot, proj_scr=proj_scr, mix_scr=mix_scr,
                               gdnn_ref=gdnn_ref, mln_ref=mln_ref, ssdn_ref=ssdn_ref,
                               gdn_o=gdn_o, mm_o=mm_o, ssd_o=ssd_o, **ab)

    def body_a(j, carry):
        pass_a(j)
        return carry

    def body_b(i, carry):
        pass_b(i)
        return carry

    lax.fori_loop(0, tm // (c * na), body_a, 0)
    lax.fori_loop(0, tm // c, body_b, 0)

    def store_ml(s, carry):
        for h in range(ML_HEADS):
            c_mat, n_row = _ml_state_out(mlc_s[s * ML_HEADS + h])
            mc_o[s, h] = c_mat
            mn_o[s, h:h + 1, :] = n_row
        return carry

    if slot:
        lax.fori_loop(0, nseq, store_ml, 0)
    else:
        @pl.when(t == nt - 1)
        def _():
            store_ml(0, 0)

    if not slot:
        last_rows = proj_scr[tm:tm + SUBLANES, 0:CONV_DIM]
        proj_scr[0:SUBLANES, 0:CONV_DIM] = last_rows

        @pl.when(t == nt - 1)
        def _():
            conv_o[0] = last_rows

    out = _dot(mix_scr[...].astype(BF16), wout_ref[...])
    o_ref[...] = x + _rms(out, npost_ref[...])


def _ffn_kernel(*refs, tm, slot, nt):
    if slot:
        (x_ref, ftail_ref, wup_ref, wdn_ref, fw_ref, fb_ref, npre_ref, npost_ref,
         o_ref, gate_o, tails_scr, act_scr) = refs
    else:
        (x_ref, wup_ref, wdn_ref, fw_ref, fb_ref, npre_ref, npost_ref,
         o_ref, gate_o, tails_scr, act_scr) = refs
    t = pl.program_id(1)

    if slot:
        tails_scr[...] = jnp.zeros(tails_scr.shape, F32)
    else:
        @pl.when(t == 0)
        def _():
            tails_scr[...] = jnp.zeros(tails_scr.shape, F32)

    x = x_ref[...]
    hn = _rms(x, npre_ref[...]).astype(BF16)
    nblk = D_FF // FF_BLK

    def up(blk):
        return (_dot(hn, wup_ref[:, blk * FF_BLK:(blk + 1) * FF_BLK]),
                _dot(hn, wup_ref[:, D_FF + blk * FF_BLK:D_FF + (blk + 1) * FF_BLK]))

    ahead = up(0)
    for blk in range(nblk):
        cols = slice(blk * FF_BLK, (blk + 1) * FF_BLK)
        gate, val = ahead
        if blk + 1 < nblk:
            ahead = up(blk + 1)
        if slot:
            rr = lax.broadcasted_iota(jnp.int32, (tm, FF_BLK), 0) % SLOT
            gate = jnp.where((rr >= SLOT_FIRST - 2) & (rr < SLOT_FIRST), ftail_ref[:, cols], gate)
            gate_o[:, cols] = gate
        full = jnp.concatenate([tails_scr[:, cols], gate], axis=0)
        conv = fb_ref[0:1, cols] + fw_ref[2:3, cols] * gate
        for j in range(2):
            conv = conv + fw_ref[j:j + 1, cols] * pltpu.roll(full, 2 - j, 0)[SUBLANES:, :]
        last_rows = full[tm:tm + SUBLANES, :]
        tails_scr[:, cols] = last_rows
        act_scr[:, cols] = (_gelu_tanh(conv) * val).astype(BF16)
    o_ref[...] = x + _rms(_dot(act_scr[...], wdn_ref[...]), npost_ref[...])
    if not slot:
        @pl.when(t == nt - 1)
        def _():
            gate_o[0] = tails_scr[...]


def _const_spec(shape, layer):
    nd = len(shape)
    return pl.BlockSpec((None,) + tuple(shape), lambda b, t: (layer,) + (0,) * nd,
                        pipeline_mode=pl.Buffered(1))


def _mixer_call(x, layer, prm, *, slot, states=None, tail=None):
    rows = x.shape[0]
    if slot:
        tm, c, nt, na = SAMPLE_TM, SLOT, 1, SAMPLE_NA
        nseq = tm // SLOT
    else:
        tm, c, na = PROMPT_TM, PROMPT_CHUNK, PROMPT_NA
        nt = 2048 // tm
        nseq = 1
    nch = tm // c
    ngrp = rows // (tm * nt)
    nb = ngrp * nseq
    row_spec = lambda w: pl.BlockSpec((tm, w), lambda b, t: (b * nt + t, 0))
    st_out = lambda *dims: pl.BlockSpec((nseq,) + dims, lambda b, t: (b,) + (0,) * len(dims))
    st_in = lambda *dims: pl.BlockSpec((None, nseq) + dims, lambda b, t: (layer, b) + (0,) * len(dims))

    in_specs = [row_spec(D_MODEL)]
    args = [x]
    if slot:
        in_specs.append(row_spec(CONV_DIM))
        args.append(tail)
    in_specs += [
        _const_spec((D_MODEL, N_IN), layer), _const_spec((D_MODEL, D_MODEL), layer),
        _const_spec((4, CONV_DIM), layer), _const_spec((1, CONV_DIM), layer),
        _const_spec((8, 128), layer), _const_spec((1, D_MODEL), layer), _const_spec((1, D_MODEL), layer),
        _const_spec((1, GDN_DV), layer), _const_spec((1, ML_DV), layer),
        _const_spec((1, SSD_HEADS * SSD_P), layer)]
    args += [prm["w_in"], prm["w_out"], prm["conv_w"], prm["conv_b"], prm["gprm"], prm["norm_mix_pre"],
             prm["norm_mix_post"], prm["gdn_norm"], prm["mlstm_norm"], prm["ssd_norm"]]
    if slot:
        in_specs += [st_in(GDN_HEADS, GDN_DK, GDN_DV), st_in(ML_HEADS, ML_DQK, ML_DV),
                     st_in(ML_HEADS, ML_DQK), st_in(SUBLANES, 128), st_in(SSD_HEADS, SSD_P, SSD_N)]
        args += list(states)

    if slot:
        conv_spec, conv_shape = row_spec(CONV_DIM), (rows, CONV_DIM)
    else:
        conv_spec, conv_shape = st_out(SUBLANES, CONV_DIM), (nb, SUBLANES, CONV_DIM)
    out_specs = [row_spec(D_MODEL), conv_spec,
                 st_out(GDN_HEADS, GDN_DK, GDN_DV), st_out(ML_HEADS, ML_DQK, ML_DV),
                 st_out(ML_HEADS, ML_DQK), st_out(SUBLANES, 128), st_out(SSD_HEADS, SSD_P, SSD_N)]
    out_shape = [jax.ShapeDtypeStruct((rows, D_MODEL), F32), jax.ShapeDtypeStruct(conv_shape, F32),
                 jax.ShapeDtypeStruct((nb, GDN_HEADS, GDN_DK, GDN_DV), F32),
                 jax.ShapeDtypeStruct((nb, ML_HEADS, ML_DQK, ML_DV), F32),
                 jax.ShapeDtypeStruct((nb, ML_HEADS, ML_DQK), F32),
                 jax.ShapeDtypeStruct((nb, SUBLANES, 128), F32),
                 jax.ShapeDtypeStruct((nb, SSD_HEADS, SSD_P, SSD_N), F32)]
    scratch = [
        pltpu.VMEM((tm + SUBLANES, N_IN), F32),
        pltpu.VMEM((tm, D_MODEL), F32),
        pltpu.VMEM((nch, c, 128), F32),
        pltpu.VMEM((nch, c, 128), F32),
        pltpu.VMEM((nch * GDN_HEADS, c + GDN_DK, GDN_DV), F32),
        pltpu.VMEM((nch * GDN_HEADS, c + GDN_DK, GDN_DV), F32),
        pltpu.VMEM((nch * ML_HEADS, c + ML_DQK, 128), F32),
        pltpu.VMEM((nch, c, SSD_HEADS * SSD_P), F32),
        pltpu.VMEM((nch * SSD_HEADS, c, SSD_N), F32),
        pltpu.VMEM((nch * SSD_HEADS, SSD_P, SSD_N), F32),
        pltpu.VMEM((nseq * ML_HEADS, ML_DQK, 128), F32),
    ]
    return pl.pallas_call(
        functools.partial(_mixer_kernel, tm=tm, c=c, slot=slot, nt=nt, na=na),
        grid=(ngrp, nt), in_specs=in_specs, out_specs=out_specs, out_shape=out_shape,
        scratch_shapes=scratch,
        compiler_params=pltpu.CompilerParams(dimension_semantics=("arbitrary", "arbitrary"),
                                             vmem_limit_bytes=VMEM_LIMIT),
        name=("mixer_sample" if slot else "mixer_prompt"),
    )(*args)


def _ffn_call(x, layer, prm, *, slot, tail=None):
    rows = x.shape[0]
    tm = FFN_TM
    nt = 1 if slot else 2048 // tm
    ngrp = rows // (tm * nt)
    row_spec = lambda w: pl.BlockSpec((tm, w), lambda b, t: (b * nt + t, 0))
    in_specs = [row_spec(D_MODEL)]
    args = [x]
    if slot:
        in_specs.append(row_spec(D_FF))
        args.append(tail)
    in_specs += [_const_spec((D_MODEL, 2 * D_FF), layer), _const_spec((D_FF, D_MODEL), layer),
                 _const_spec((3, D_FF), layer), _const_spec((1, D_FF), layer),
                 _const_spec((1, D_MODEL), layer), _const_spec((1, D_MODEL), layer)]
    args += [prm["ffn_w_up"], prm["ffn_w_down"], prm["ffn_conv_w"], prm["ffn_conv_b"],
             prm["norm_ffn_pre"], prm["norm_ffn_post"]]
    if slot:
        gate_spec, gate_shape = row_spec(D_FF), (rows, D_FF)
    else:
        gate_spec = pl.BlockSpec((1, SUBLANES, D_FF), lambda b, t: (b, 0, 0))
        gate_shape = (ngrp, SUBLANES, D_FF)
    return pl.pallas_call(
        functools.partial(_ffn_kernel, tm=tm, slot=slot, nt=nt),
        grid=(ngrp, nt), in_specs=in_specs, out_specs=[row_spec(D_MODEL), gate_spec],
        out_shape=[jax.ShapeDtypeStruct((rows, D_MODEL), F32), jax.ShapeDtypeStruct(gate_shape, F32)],
        scratch_shapes=[pltpu.VMEM((SUBLANES, D_FF), F32), pltpu.VMEM((tm, D_FF), BF16)],
        compiler_params=pltpu.CompilerParams(dimension_semantics=("arbitrary", "arbitrary"),
                                             vmem_limit_bytes=VMEM_LIMIT),
        name=("ffn_sample" if slot else "ffn_prompt"),
    )(*args)


def _prepare_params(norm_mix_pre, norm_mix_post, norm_ffn_pre, norm_ffn_post, w_in, conv_w, conv_b,
                    gdn_a_log, gdn_dt_bias, gdn_norm, mlstm_i_bias, mlstm_f_bias, mlstm_norm,
                    ssd_a_log, ssd_dt_bias, ssd_d, ssd_norm, w_out, ffn_w_up, ffn_conv_w, ffn_conv_b,
                    ffn_w_down):
    w_in_p = jnp.concatenate(
        [w_in[..., :2816], w_in[..., 2824:3848], w_in[..., 3856:4112], w_in[..., 2816:2824],
         w_in[..., 3848:3856], w_in[..., 4112:4116],
         jnp.zeros((DEPTH, D_MODEL, N_IN - 4116), w_in.dtype)], axis=-1).astype(BF16)
    z4 = jnp.zeros((DEPTH, 4), F32)
    pad = jnp.zeros((DEPTH, 128 - 20), F32)
    gprm = jnp.stack(
        [jnp.concatenate([gdn_dt_bias, z4, mlstm_i_bias, mlstm_f_bias, ssd_dt_bias, pad], axis=-1),
         jnp.concatenate([gdn_a_log, z4, z4, z4, ssd_a_log, pad], axis=-1),
         jnp.concatenate([z4, z4, z4, z4, ssd_d, pad], axis=-1)]
        + [jnp.zeros((DEPTH, 128), F32)] * 5, axis=1)
    row = lambda a: a[:, None, :]
    return dict(
        w_in=w_in_p, w_out=w_out.astype(BF16), conv_w=conv_w, conv_b=row(conv_b), gprm=gprm,
        norm_mix_pre=row(norm_mix_pre), norm_mix_post=row(norm_mix_post),
        norm_ffn_pre=row(norm_ffn_pre), norm_ffn_post=row(norm_ffn_post),
        gdn_norm=row(gdn_norm), mlstm_norm=row(mlstm_norm), ssd_norm=row(ssd_norm),
        ffn_w_up=ffn_w_up.astype(BF16), ffn_w_down=ffn_w_down.astype(BF16),
        ffn_conv_w=ffn_conv_w, ffn_conv_b=row(ffn_conv_b))


def kernel(x_prompt, x_sample, state_conv, state_gdn, state_mlstm_c, state_mlstm_n, state_mlstm_m, state_ssd, state_ffn_conv, norm_mix_pre, norm_mix_post, norm_ffn_pre, norm_ffn_post, w_in, conv_w, conv_b, gdn_a_log, gdn_dt_bias, gdn_norm, mlstm_i_bias, mlstm_f_bias, mlstm_norm, ssd_a_log, ssd_dt_bias, ssd_d, ssd_norm, w_out, ffn_w_up, ffn_conv_w, ffn_conv_b, ffn_w_down):
    prm = _prepare_params(norm_mix_pre, norm_mix_post, norm_ffn_pre, norm_ffn_post, w_in, conv_w, conv_b,
                          gdn_a_log, gdn_dt_bias, gdn_norm, mlstm_i_bias, mlstm_f_bias, mlstm_norm,
                          ssd_a_log, ssd_dt_bias, ssd_d, ssd_norm, w_out, ffn_w_up, ffn_conv_w,
                          ffn_conv_b, ffn_w_down)
    bp, lp, _ = x_prompt.shape
    bs, ls, _ = x_sample.shape

    x = x_prompt.reshape(bp * lp, D_MODEL)
    p_states = []
    for layer in range(DEPTH):
        x, conv8, gdn, mc, mn, mm, ssd = _mixer_call(x, layer, prm, slot=False)
        x, gate8 = _ffn_call(x, layer, prm, slot=False)
        p_states.append((conv8[:, SUBLANES - 3:], gdn, mc, mn, mm[:, 0, L_MF:L_MF + ML_HEADS], ssd,
                         gate8[:, SUBLANES - 2:]))
    y_prompt = x.reshape(bp, lp, D_MODEL)

    x = jnp.pad(x_sample, ((0, 0), (SLOT_FIRST, SLOT - SLOT_FIRST - ls), (0, 0))).reshape(bs * SLOT, D_MODEL)
    mm_in = jnp.pad(state_mlstm_m[:, :, None, :],
                    ((0, 0), (0, 0), (0, SUBLANES - 1), (L_MF, 128 - L_MF - ML_HEADS)))
    conv_tail = jnp.pad(state_conv, ((0, 0), (0, 0), (0, SLOT - 3), (0, 0))).reshape(DEPTH, bs * SLOT, CONV_DIM)
    ffn_tail = jnp.pad(state_ffn_conv, ((0, 0), (0, 0), (SLOT_FIRST - 2, SLOT - SLOT_FIRST), (0, 0))
                       ).reshape(DEPTH, bs * SLOT, D_FF)
    s_states = []
    for layer in range(DEPTH):
        x, conv_all, gdn, mc, mn, mm, ssd = _mixer_call(
            x, layer, prm, slot=True, tail=conv_tail[layer],
            states=(state_gdn, state_mlstm_c, state_mlstm_n, mm_in, state_ssd))
        x, gate_all = _ffn_call(x, layer, prm, slot=True, tail=ffn_tail[layer])
        s_states.append((conv_all.reshape(bs, SLOT, CONV_DIM)[:, SLOT_LAST - 2:SLOT_LAST + 1], gdn, mc, mn,
                         mm[:, 0, L_MF:L_MF + ML_HEADS], ssd,
                         gate_all.reshape(bs, SLOT, D_FF)[:, SLOT_LAST - 1:SLOT_LAST + 1]))
    y_sample = x.reshape(bs, SLOT, D_MODEL)[:, SLOT_FIRST:SLOT_LAST + 1]

    p_out = [jnp.stack(s, axis=0) for s in zip(*p_states)]
    s_out = [jnp.stack(s, axis=0) for s in zip(*s_states)]
    return (y_prompt, y_sample, *p_out, *s_out)
```

```python
import functools

import jax
import jax.numpy as jnp
from jax import lax
from jax.experimental import pallas as pl
from jax.experimental.pallas import tpu as pltpu

F32 = jnp.float32
BF16 = jnp.bfloat16

D_MODEL = 1024
DEPTH = 2
GDN_HEADS, GDN_DK, GDN_DV = 4, 128, 128
ML_HEADS, ML_DQK, ML_DV = 4, 64, 64
SSD_HEADS, SSD_P, SSD_GROUPS, SSD_N = 4, 64, 2, 128
D_FF = 2816
EPS = 1e-6

CONV_DIM = 2304
C_GQ, C_GK, C_GV, C_SX, C_SB, C_SC = 0, 512, 1024, 1536, 1792, 2048
C_GG, C_MQ, C_MK, C_MV, C_MO, C_SZ, C_GATE = 2304, 2816, 3072, 3328, 3584, 3840, 4096
N_IN = 4224
IN_NBLK = 1408
L_GA, L_GB, L_MI, L_MF, L_DT = 0, 4, 8, 12, 16

SUBLANES = 8
SLOT = 8
SLOT_FIRST, SLOT_LAST = 3, 6
FF_BLK = 256
VMEM_LIMIT = 56 * 1024 * 1024

PROMPT_TM, PROMPT_CHUNK, PROMPT_NA, PROMPT_NB = 512, 64, 4, 1
SAMPLE_TM, SAMPLE_NA, SAMPLE_NB = 64, 4, 4
FFN_TM = 1024


def _dot(a, b):
    return jnp.dot(a, b, preferred_element_type=F32)


def _dot_nt(a, b):
    return lax.dot_general(a, b, (((1,), (1,)), ((), ())), preferred_element_type=F32)


def _dot_tn(a, b):
    return lax.dot_general(a, b, (((0,), (0,)), ((), ())), preferred_element_type=F32)


def _sigmoid(x):
    return 1.0 / (1.0 + jnp.exp(-x))


def _silu(x):
    return x * _sigmoid(x)


def _rms(x, w):
    return x * lax.rsqrt(jnp.mean(x * x, axis=-1, keepdims=True) + EPS) * w


def _l2n(x):
    return x * lax.rsqrt(jnp.sum(x * x, axis=-1, keepdims=True) + EPS)


def _gelu_tanh(x):
    return 0.5 * x * (1.0 + jnp.tanh(0.7978845608028654 * (x + 0.044715 * (x * x * x))))


def _col(a, l):
    return a[:, l:l + 1]


def _cumsum_rows(tril_b, x):
    hi = x.astype(BF16)
    r1 = x - hi.astype(F32)
    mid = r1.astype(BF16)
    lo = (r1 - mid.astype(F32)).astype(BF16)
    return _dot(tril_b, hi) + _dot(tril_b, mid) + _dot(tril_b, lo)


def _conv_silu(proj_scr, r0, c, col0, width, cw_ref, cb_ref):
    win = proj_scr[pl.ds(r0, c + SUBLANES), col0:col0 + width]
    acc = cb_ref[0:1, col0:col0 + width] + cw_ref[3:4, col0:col0 + width] * win[SUBLANES:, :]
    for j in range(3):
        shifted = pltpu.roll(win, 3 - j, 0)[SUBLANES:, :]
        acc = acc + cw_ref[j:j + 1, col0:col0 + width] * shifted
    return _silu(acc)


def _gates(proj_scr, r0, c, slot, gprm_ref, tril_b):
    graw = proj_scr[pl.ds(r0 + SUBLANES, c), C_GATE:C_GATE + 128]
    lane = lax.broadcasted_iota(jnp.int32, (c, 128), 1)
    z = graw + gprm_ref[0:1, :]
    soft = jnp.log(1.0 + jnp.exp(-jnp.abs(z)))
    sp = jnp.maximum(z, 0.0) + soft
    log_sig = -(jnp.maximum(-z, 0.0) + soft)
    a_neg = -jnp.exp(gprm_ref[1:2, :])
    is_ga = lane < L_GB
    is_gb = (lane >= L_GB) & (lane < L_MI)
    is_mi = (lane >= L_MI) & (lane < L_MF)
    is_mf = (lane >= L_MF) & (lane < L_DT)
    is_dt = (lane >= L_DT) & (lane < L_DT + SSD_HEADS)
    cum_src = jnp.where(is_ga | is_dt, a_neg * sp, jnp.where(is_mf, log_sig, 0.0))
    elem = jnp.where(is_gb, _sigmoid(graw), jnp.where(is_mi, z, jnp.where(is_dt, sp, 0.0)))
    if slot:
        rr = lax.broadcasted_iota(jnp.int32, (c, 128), 0)
        valid = (rr >= SLOT_FIRST) & (rr <= SLOT_LAST)
        cum_src = jnp.where(valid, cum_src, 0.0)
        elem = jnp.where(valid, elem, jnp.where(is_mi, -jnp.inf, 0.0))
    cum = _cumsum_rows(tril_b, cum_src)
    if c == 128:
        return cum, elem, cum.T, elem.T
    parts = [cum, elem]
    if 2 * c < 128:
        parts.append(jnp.zeros((128 - 2 * c, 128), F32))
    zt = jnp.concatenate(parts, axis=0).T
    return cum, elem, zt[:, 0:c], zt[:, c:2 * c]


def _pass_a(j, *, c, slot, na, proj_scr, cw_ref, cb_ref, gprm_ref,
            cum_s, col_s, gqx_s, gob_s, ml_s, sy_s, sce_s, sdh_s):
    chunks = [j * na + a for a in range(na)]
    r0s = [pl.multiple_of(i * c, c) for i in chunks]
    conv = functools.partial(_conv_silu, proj_scr, c=c, cw_ref=cw_ref, cb_ref=cb_ref)

    ii = lax.broadcasted_iota(jnp.int32, (c, c), 0)
    jj = lax.broadcasted_iota(jnp.int32, (c, c), 1)
    tril = ii >= jj
    strict = ii > jj
    eye = (ii == jj).astype(F32)
    tril_b = tril.astype(BF16)
    lane = lax.broadcasted_iota(jnp.int32, (c, 128), 1)

    gates = [_gates(proj_scr, r0, c, slot, gprm_ref, tril_b) for r0 in r0s]
    cum = [g[0] for g in gates]
    elem = [g[1] for g in gates]
    cum_t = [g[2] for g in gates]
    elem_t = [g[3] for g in gates]
    for a in range(na):
        cum_s[chunks[a]] = cum[a]

    def row_cum(a, l):
        return cum_t[a][l:l + 1, :]

    def row_elem(a, l):
        return elem_t[a][l:l + 1, :]

    it = [(a, h) for a in range(na) for h in range(GDN_HEADS)]
    n = range(len(it))
    gd, ml, sd = {}, {}, {}

    def g_k():
        k = [_l2n(conv(r0=r0s[a], col0=C_GK + h * GDN_DK, width=GDN_DK)) for a, h in it]
        gam_c = [_col(cum[a], L_GA + h) for a, h in it]
        beta_c = [_col(elem[a], L_GB + h) for a, h in it]
        dmat = [jnp.exp(jnp.where(tril, gam_c[x] - row_cum(a, L_GA + h), -jnp.inf))
                for x, (a, h) in enumerate(it)]
        pk = [-jnp.where(strict, _dot_nt(k[x], k[x]) * dmat[x] * beta_c[x], 0.0) for x in n]
        gd.update(k=k, gam_c=gam_c, beta_c=beta_c, dmat=dmat, pk=pk, t_inv=[eye + pk[x] for x in n])

    def g_neumann():
        pk = [_dot(gd["pk"][x], gd["pk"][x]) for x in n]
        gd.update(pk=pk, t_inv=[gd["t_inv"][x] + _dot(gd["t_inv"][x], pk[x]) for x in n])

    def g_q():
        gd["q"] = [_l2n(conv(r0=r0s[a], col0=C_GQ + h * GDN_DK, width=GDN_DK)) * (GDN_DK ** -0.5)
                   for a, h in it]

    def g_v():
        v = [conv(r0=r0s[a], col0=C_GV + h * GDN_DV, width=GDN_DV) for a, h in it]
        eg = [jnp.exp(gd["gam_c"][x]) for x in n]
        gd.update(eg=eg, rhs=[jnp.concatenate([v[x] * gd["beta_c"][x],
                                               gd["k"][x] * (gd["beta_c"][x] * eg[x])], axis=1) for x in n])

    def g_uw():
        gd["uw"] = [_dot(gd["t_inv"][x], gd["rhs"][x]) for x in n]

    def g_qk():
        gd["qk"] = [_dot_nt(gd["q"][x], gd["k"][x]) * gd["dmat"][x] for x in n]
        gd["kd"] = [gd["k"][x] * jnp.exp(gd["gam_c"][x][c - 1:c, :] - gd["gam_c"][x]) for x in n]

    def g_out():
        quw = [_dot(gd["qk"][x], gd["uw"][x]) for x in n]
        kuw = [_dot_tn(gd["kd"][x], gd["uw"][x]) for x in n]
        for x, (a, h) in enumerate(it):
            idx = chunks[a] * GDN_HEADS + h
            gqx_s[idx, 0:c, :] = gd["q"][x] * gd["eg"][x] - quw[x][:, GDN_DV:]
            gqx_s[idx, c:c + GDN_DK, :] = kuw[x][:, GDN_DV:]
            gob_s[idx, 0:c, :] = quw[x][:, :GDN_DV]
            gob_s[idx, c:c + GDN_DK, :] = kuw[x][:, :GDN_DV]

    def piece(a, base, h):
        return proj_scr[pl.ds(r0s[a] + SUBLANES, c), base + h * ML_DQK:base + (h + 1) * ML_DQK]

    def m_1():
        mq = [piece(a, C_MQ, h) for a, h in it]
        mk = [piece(a, C_MK, h) * (ML_DQK ** -0.5) for a, h in it]
        b_c = [_col(cum[a], L_MF + h) for a, h in it]
        d = [jnp.where(tril, b_c[x] - row_cum(a, L_MF + h) + row_elem(a, L_MI + h), -jnp.inf)
             for x, (a, h) in enumerate(it)]
        dmax = [jnp.max(d[x], axis=-1, keepdims=True) for x in n]
        dsafe = [jnp.where(dmax[x] == -jnp.inf, 0.0, dmax[x]) for x in n]
        s0 = [_dot_nt(mq[x], mk[x]) * jnp.exp(d[x] - dsafe[x]) for x in n]
        for a in range(na):
            cols = jnp.zeros((c, 128), F32)
            for h in range(ML_HEADS):
                cols = jnp.where(lane == L_MF + h, dmax[a * ML_HEADS + h], cols)
            col_s[chunks[a]] = cols
        ml.update(mk=mk, b_c=b_c, dsafe=dsafe, s0=s0)

    def m_2():
        mv = [piece(a, C_MV, h) for a, h in it]
        i_c = [_col(elem[a], L_MI + h) for a, h in it]
        ones_col = (lax.broadcasted_iota(jnp.int32, (c, 128 - ML_DV), 1) == 0).astype(F32)
        v_aug = [jnp.concatenate([mv[x], ones_col], axis=1) for x in n]
        num0 = [_dot(ml["s0"][x], v_aug[x]) for x in n]
        b_c, dsafe = ml["b_c"], ml["dsafe"]
        kw0 = [ml["mk"][x] * jnp.exp(b_c[x][c - 1:c, :] - b_c[x] + i_c[x] - dsafe[x][c - 1:c, :]) for x in n]
        kv0 = [_dot_tn(kw0[x], v_aug[x]) for x in n]
        for x, (a, h) in enumerate(it):
            idx = chunks[a] * ML_HEADS + h
            ml_s[idx, 0:c, :] = num0[x]
            ml_s[idx, c:c + ML_DQK, :] = kv0[x]

    rep = SSD_HEADS // SSD_GROUPS
    gi = [(a, g) for a in range(na) for g in range(SSD_GROUPS)]
    grp = [a * SSD_GROUPS + h // rep for a, h in it]

    def s_1():
        bg = [conv(r0=r0s[a], col0=C_SB + g * SSD_N, width=SSD_N) for a, g in gi]
        cg = [conv(r0=r0s[a], col0=C_SC + g * SSD_N, width=SSD_N) for a, g in gi]
        sd.update(bg=bg, cg=cg, cb_raw=[_dot_nt(cg[y], bg[y]) for y in range(len(gi))])

    def s_2():
        xg = [conv(r0=r0s[a], col0=C_SX + g * rep * SSD_P, width=rep * SSD_P) for a, g in gi]
        xs = [xg[grp[x]][:, (h % rep) * SSD_P:(h % rep + 1) * SSD_P] for x, (a, h) in enumerate(it)]
        sg_c = [_col(cum[a], L_DT + h) for a, h in it]
        dt_c = [_col(elem[a], L_DT + h) for a, h in it]
        cb = [sd["cb_raw"][grp[x]] * jnp.exp(jnp.where(tril, sg_c[x] - row_cum(a, L_DT + h), -jnp.inf))
              * row_elem(a, L_DT + h) for x, (a, h) in enumerate(it)]
        y0 = [_dot(cb[x], xs[x]) + gprm_ref[2:3, L_DT + h:L_DT + h + 1] * xs[x]
              for x, (a, h) in enumerate(it)]
        dh = [_dot_tn(xs[x] * (jnp.exp(sg_c[x][c - 1:c, :] - sg_c[x]) * dt_c[x]), sd["bg"][grp[x]]) for x in n]
        for x, (a, h) in enumerate(it):
            idx = chunks[a] * SSD_HEADS + h
            sce_s[idx] = sd["cg"][grp[x]] * jnp.exp(sg_c[x])
            sdh_s[idx] = dh[x]
        for a in range(na):
            sy_s[chunks[a]] = jnp.concatenate(y0[a * SSD_HEADS:(a + 1) * SSD_HEADS], axis=1)

    chain = [g_k] + [g_neumann] * (c.bit_length() - 2) + [g_uw]
    fill = [g_q, g_v, m_1, m_2, s_1, s_2]
    for pos, link in enumerate(chain):
        link()
        if pos < len(fill):
            fill[pos]()
    for rest in fill[len(chain):]:
        rest()
    g_qk()
    g_out()


def _pass_b(j, *, c, slot, nb, proj_scr, mix_scr, gdnn_ref, mln_ref, ssdn_ref,
            cum_s, col_s, gqx_s, gob_s, ml_s, sy_s, sce_s, sdh_s, mlc_s, gdn_o, mm_o, ssd_o):
    hs = range(GDN_HEADS)
    lane_row = lax.broadcasted_iota(jnp.int32, (1, 128), 1)
    is_m = (lane_row >= L_MF) & (lane_row < L_MF + ML_HEADS)

    def load(seq):
        return dict(s=[gdn_o[seq, h] for h in hs], c=[mlc_s[seq * ML_HEADS + h] for h in hs],
                    h=[ssd_o[seq, h] for h in hs], m=mm_o[seq, 0:1, :])

    def store(seq, st):
        for h in hs:
            gdn_o[seq, h] = st["s"][h]
            mlc_s[seq * ML_HEADS + h] = st["c"][h]
            ssd_o[seq, h] = st["h"][h]
        mm_o[seq, 0:1, :] = st["m"]

    def advance(i, st):
        prow = pl.ds(pl.multiple_of(i * c, c) + SUBLANES, c)
        cum = cum_s[i]
        dmx = col_s[i]
        last = cum[c - 1:c, :]
        e_last = jnp.exp(last)
        mq = [proj_scr[prow, C_MQ + h * ML_DQK:C_MQ + (h + 1) * ML_DQK] for h in hs]
        r = [_dot(gqx_s[i * GDN_HEADS + h], st["s"][h]) for h in hs]
        full = [_dot(mq[h], st["c"][h]) for h in hs]
        yh = [_dot_nt(sce_s[i * SSD_HEADS + h], st["h"][h]) for h in hs]
        inter = cum + st["m"]
        m_t = jnp.maximum(inter, dmx)
        w_intra = jnp.exp(dmx - m_t)
        m_new = jnp.where(is_m, m_t[c - 1:c, :], 0.0)
        w_c_row = jnp.exp(last + st["m"] - m_new)
        w_l_row = w_intra[c - 1:c, :]
        ob = [gob_s[i * GDN_HEADS + h] for h in hs]
        blk = [ml_s[i * ML_HEADS + h] for h in hs]
        new = dict(
            s=[_col(e_last, L_GA + h) * st["s"][h] - r[h][c:, :] + ob[h][c:, :] for h in hs],
            c=[_col(w_c_row, L_MF + h) * st["c"][h] + _col(w_l_row, L_MF + h) * blk[h][c:c + ML_DQK, :]
               for h in hs],
            h=[_col(e_last, L_DT + h) * st["h"][h] + sdh_s[i * SSD_HEADS + h] for h in hs],
            m=m_new)
        return new, dict(i=i, r=r, full=full, yh=yh, ob=ob, blk=blk, inter=inter, m_t=m_t, w_intra=w_intra)

    def outputs(ctx):
        i = ctx["i"]
        r0 = pl.multiple_of(i * c, c)
        rows = pl.ds(r0, c)
        prow = pl.ds(r0 + SUBLANES, c)
        w_inter = jnp.exp(ctx["inter"] - ctx["m_t"])
        e_neg_m = jnp.exp(-ctx["m_t"])
        for h in hs:
            gg = proj_scr[prow, C_GG + h * GDN_DV:C_GG + (h + 1) * GDN_DV]
            mix_scr[rows, h * GDN_DV:(h + 1) * GDN_DV] = (
                _rms(ctx["r"][h][:c, :] + ctx["ob"][h][:c, :], gdnn_ref[...]) * _silu(gg))
        h_parts = []
        for h in hs:
            fl = (ctx["full"][h] * _col(w_inter, L_MF + h)
                  + _col(ctx["w_intra"], L_MF + h) * ctx["blk"][h][0:c, :])
            den = jnp.maximum(jnp.abs(fl[:, ML_DV:ML_DV + 1]), _col(e_neg_m, L_MF + h))
            mo = proj_scr[prow, C_MO + h * ML_DV:C_MO + (h + 1) * ML_DV]
            h_parts.append(_rms(_sigmoid(mo) * (fl[:, :ML_DV] / den), mln_ref[...]))
        mix_scr[rows, GDN_HEADS * GDN_DV:GDN_HEADS * GDN_DV + ML_HEADS * ML_DV] = (
            jnp.concatenate(h_parts, axis=1))
        sz = proj_scr[prow, C_SZ:C_SZ + SSD_HEADS * SSD_P]
        y_all = (sy_s[i] + jnp.concatenate(ctx["yh"], axis=1)) * _silu(sz)
        mix_scr[rows, GDN_HEADS * GDN_DV + ML_HEADS * ML_DV:D_MODEL] = _rms(y_all, ssdn_ref[...])

    ctxs = []
    st = None
    for a in range(nb):
        i = j * nb + a
        seq = i if slot else 0
        if slot or a == 0:
            st = load(seq)
        st, ctx = advance(i, st)
        if slot or a == nb - 1:
            store(seq, st)
        ctxs.append(ctx)
    for ctx in ctxs:
        outputs(ctx)


def _ml_state_in(c_mat, n_row):
    eye = (lax.broadcasted_iota(jnp.int32, (ML_DQK, ML_DQK), 0)
           == lax.broadcasted_iota(jnp.int32, (ML_DQK, ML_DQK), 1)).astype(F32)
    n_col = jnp.sum(eye * n_row, axis=-1, keepdims=True)
    first = lax.broadcasted_iota(jnp.int32, (ML_DQK, 128 - ML_DV), 1) == 0
    return jnp.concatenate([c_mat, jnp.where(first, n_col, 0.0)], axis=1)


def _ml_state_out(c_aug):
    eye = (lax.broadcasted_iota(jnp.int32, (ML_DQK, ML_DQK), 0)
           == lax.broadcasted_iota(jnp.int32, (ML_DQK, ML_DQK), 1)).astype(F32)
    n_row = jnp.sum(eye * c_aug[:, ML_DV:ML_DV + 1], axis=0, keepdims=True)
    return c_aug[:, :ML_DV], n_row


def _mixer_kernel(*refs, tm, c, slot, nt, na, nbk):
    if slot:
        (x_ref, tail_ref, win_ref, wout_ref, cw_ref, cb_ref, gprm_ref, npre_ref, npost_ref, gdnn_ref,
         mln_ref, ssdn_ref, gdn_i, mc_i, mn_i, mm_i, ssd_i,
         o_ref, conv_o, gdn_o, mc_o, mn_o, mm_o, ssd_o, proj_scr, mix_scr, *ab) = refs
    else:
        (x_ref, win_ref, wout_ref, cw_ref, cb_ref, gprm_ref, npre_ref, npost_ref, gdnn_ref,
         mln_ref, ssdn_ref,
         o_ref, conv_o, gdn_o, mc_o, mn_o, mm_o, ssd_o, proj_scr, mix_scr, *ab) = refs
    names = ("cum_s", "col_s", "gqx_s", "gob_s", "ml_s", "sy_s", "sce_s", "sdh_s", "mlc_s")
    ab = dict(zip(names, ab))
    mlc_s = ab["mlc_s"]
    nseq = gdn_o.shape[0]
    t = pl.program_id(1)

    if slot:
        proj_scr[0:SUBLANES, :] = jnp.zeros((SUBLANES, N_IN), F32)
        gdn_o[...] = gdn_i[...]
        mm_o[...] = mm_i[...]
        ssd_o[...] = ssd_i[...]

        def load_ml(s, carry):
            for h in range(ML_HEADS):
                mlc_s[s * ML_HEADS + h] = _ml_state_in(mc_i[s, h], mn_i[s, h:h + 1, :])
            return carry

        lax.fori_loop(0, nseq, load_ml, 0)
    else:
        @pl.when(t == 0)
        def _():
            proj_scr[0:SUBLANES, :] = jnp.zeros((SUBLANES, N_IN), F32)
            gdn_o[...] = jnp.zeros(gdn_o.shape, F32)
            mlc_s[...] = jnp.zeros(mlc_s.shape, F32)
            mm_o[...] = jnp.zeros(mm_o.shape, F32)
            ssd_o[...] = jnp.zeros(ssd_o.shape, F32)

    x = x_ref[...]
    hn = _rms(x, npre_ref[...]).astype(BF16)
    for nb in range(N_IN // IN_NBLK):
        cols = slice(nb * IN_NBLK, (nb + 1) * IN_NBLK)
        proj_scr[SUBLANES:SUBLANES + tm, cols] = _dot(hn, win_ref[:, cols])

    if slot:
        for s in range(nseq):
            base = SUBLANES + s * SLOT
            proj_scr[base:base + SLOT_FIRST, 0:CONV_DIM] = tail_ref[s]
            conv_o[s] = proj_scr[base + SLOT_LAST - 2:base + SLOT_LAST + 1, 0:CONV_DIM]

    pass_a = functools.partial(_pass_a, c=c, slot=slot, na=na, proj_scr=proj_scr, cw_ref=cw_ref,
                               cb_ref=cb_ref, gprm_ref=gprm_ref,
                               **{k: v for k, v in ab.items() if k != "mlc_s"})
    pass_b = functools.partial(_pass_b, c=c, slot=slot, nb=nbk, proj_scr=proj_scr, mix_scr=mix_scr,
                               gdnn_ref=gdnn_ref, mln_ref=mln_ref, ssdn_ref=ssdn_ref,
                               gdn_o=gdn_o, mm_o=mm_o, ssd_o=ssd_o, **ab)

    def body_a(j, carry):
        pass_a(j)
        return carry

    def body_b(i, carry):
        pass_b(i)
        return carry

    lax.fori_loop(0, tm // (c * na), body_a, 0)
    lax.fori_loop(0, tm // (c * nbk), body_b, 0)

    def store_ml(s, carry):
        for h in range(ML_HEADS):
            c_mat, n_row = _ml_state_out(mlc_s[s * ML_HEADS + h])
            mc_o[s, h] = c_mat
            mn_o[s, h:h + 1, :] = n_row
        return carry

    if slot:
        lax.fori_loop(0, nseq, store_ml, 0)
    else:
        @pl.when(t == nt - 1)
        def _():
            store_ml(0, 0)

    if not slot:
        last_rows = proj_scr[tm:tm + SUBLANES, 0:CONV_DIM]
        proj_scr[0:SUBLANES, 0:CONV_DIM] = last_rows

        @pl.when(t == nt - 1)
        def _():
            conv_o[0] = last_rows[SUBLANES - 3:, :]

    out = _dot(mix_scr[...].astype(BF16), wout_ref[...])
    o_ref[...] = x + _rms(out, npost_ref[...])


def _ffn_kernel(*refs, tm, slot, nt):
    if slot:
        (x_ref, ftail_ref, wup_ref, wdn_ref, fw_ref, fb_ref, npre_ref, npost_ref,
         o_ref, gate_o, tails_scr, act_scr) = refs
    else:
        (x_ref, wup_ref, wdn_ref, fw_ref, fb_ref, npre_ref, npost_ref,
         o_ref, gate_o, tails_scr, act_scr) = refs
    t = pl.program_id(1)

    if slot:
        tails_scr[...] = jnp.zeros(tails_scr.shape, F32)
    else:
        @pl.when(t == 0)
        def _():
            tails_scr[...] = jnp.zeros(tails_scr.shape, F32)

    x = x_ref[...]
    hn = _rms(x, npre_ref[...]).astype(BF16)
    nblk = D_FF // FF_BLK

    def up(blk):
        return (_dot(hn, wup_ref[:, blk * FF_BLK:(blk + 1) * FF_BLK]),
                _dot(hn, wup_ref[:, D_FF + blk * FF_BLK:D_FF + (blk + 1) * FF_BLK]))

    ahead = up(0)
    for blk in range(nblk):
        cols = slice(blk * FF_BLK, (blk + 1) * FF_BLK)
        gate, val = ahead
        if blk + 1 < nblk:
            ahead = up(blk + 1)
        if slot:
            g3 = gate.reshape(tm // SLOT, SLOT, FF_BLK)
            ft = ftail_ref[:, :, cols]
            rr = lax.broadcasted_iota(jnp.int32, g3.shape, 1)
            g3 = jnp.where(rr == SLOT_FIRST - 2, ft[:, 0:1, :], jnp.where(rr == SLOT_FIRST - 1, ft[:, 1:2, :], g3))
            gate_o[:, :, cols] = g3[:, SLOT_LAST - 1:SLOT_LAST + 1, :]
            gate = g3.reshape(tm, FF_BLK)
        full = jnp.concatenate([tails_scr[:, cols], gate], axis=0)
        conv = fb_ref[0:1, cols] + fw_ref[2:3, cols] * gate
        for j in range(2):
            conv = conv + fw_ref[j:j + 1, cols] * pltpu.roll(full, 2 - j, 0)[SUBLANES:, :]
        last_rows = full[tm:tm + SUBLANES, :]
        tails_scr[:, cols] = last_rows
        act_scr[:, cols] = (_gelu_tanh(conv) * val).astype(BF16)
    o_ref[...] = x + _rms(_dot(act_scr[...], wdn_ref[...]), npost_ref[...])
    if not slot:
        @pl.when(t == nt - 1)
        def _():
            gate_o[0] = tails_scr[SUBLANES - 2:, :]


def _const_spec(shape, layer):
    nd = len(shape)
    return pl.BlockSpec((None,) + tuple(shape), lambda b, t: (layer,) + (0,) * nd,
                        pipeline_mode=pl.Buffered(1))


def _mixer_call(x, layer, prm, *, slot, states=None, tail=None):
    rows = x.shape[0]
    if slot:
        tm, c, nt, na, nbk = SAMPLE_TM, SLOT, 1, SAMPLE_NA, SAMPLE_NB
        nseq = tm // SLOT
    else:
        tm, c, na, nbk = PROMPT_TM, PROMPT_CHUNK, PROMPT_NA, PROMPT_NB
        nt = 2048 // tm
        nseq = 1
    nch = tm // c
    ngrp = rows // (tm * nt)
    nb = ngrp * nseq
    row_spec = lambda w: pl.BlockSpec((tm, w), lambda b, t: (b * nt + t, 0))
    st_out = lambda *dims: pl.BlockSpec((nseq,) + dims, lambda b, t: (b,) + (0,) * len(dims))
    st_in = lambda *dims: pl.BlockSpec((None, nseq) + dims, lambda b, t: (layer, b) + (0,) * len(dims))

    in_specs = [row_spec(D_MODEL)]
    args = [x]
    if slot:
        in_specs.append(st_in(SLOT_FIRST, CONV_DIM))
        args.append(tail)
    in_specs += [
        _const_spec((D_MODEL, N_IN), layer), _const_spec((D_MODEL, D_MODEL), layer),
        _const_spec((4, CONV_DIM), layer), _const_spec((1, CONV_DIM), layer),
        _const_spec((8, 128), layer), _const_spec((1, D_MODEL), layer), _const_spec((1, D_MODEL), layer),
        _const_spec((1, GDN_DV), layer), _const_spec((1, ML_DV), layer),
        _const_spec((1, SSD_HEADS * SSD_P), layer)]
    args += [prm["w_in"], prm["w_out"], prm["conv_w"], prm["conv_b"], prm["gprm"], prm["norm_mix_pre"],
             prm["norm_mix_post"], prm["gdn_norm"], prm["mlstm_norm"], prm["ssd_norm"]]
    if slot:
        in_specs += [st_in(GDN_HEADS, GDN_DK, GDN_DV), st_in(ML_HEADS, ML_DQK, ML_DV),
                     st_in(ML_HEADS, ML_DQK), st_in(SUBLANES, 128), st_in(SSD_HEADS, SSD_P, SSD_N)]
        args += list(states)

    conv_spec, conv_shape = st_out(3, CONV_DIM), (nb, 3, CONV_DIM)
    out_specs = [row_spec(D_MODEL), conv_spec,
                 st_out(GDN_HEADS, GDN_DK, GDN_DV), st_out(ML_HEADS, ML_DQK, ML_DV),
                 st_out(ML_HEADS, ML_DQK), st_out(SUBLANES, 128), st_out(SSD_HEADS, SSD_P, SSD_N)]
    out_shape = [jax.ShapeDtypeStruct((rows, D_MODEL), F32), jax.ShapeDtypeStruct(conv_shape, F32),
                 jax.ShapeDtypeStruct((nb, GDN_HEADS, GDN_DK, GDN_DV), F32),
                 jax.ShapeDtypeStruct((nb, ML_HEADS, ML_DQK, ML_DV), F32),
                 jax.ShapeDtypeStruct((nb, ML_HEADS, ML_DQK), F32),
                 jax.ShapeDtypeStruct((nb, SUBLANES, 128), F32),
                 jax.ShapeDtypeStruct((nb, SSD_HEADS, SSD_P, SSD_N), F32)]
    scratch = [
        pltpu.VMEM((tm + SUBLANES, N_IN), F32),
        pltpu.VMEM((tm, D_MODEL), F32),
        pltpu.VMEM((nch, c, 128), F32),
        pltpu.VMEM((nch, c, 128), F32),
        pltpu.VMEM((nch * GDN_HEADS, c + GDN_DK, GDN_DV), F32),
        pltpu.VMEM((nch * GDN_HEADS, c + GDN_DK, GDN_DV), F32),
        pltpu.VMEM((nch * ML_HEADS, c + ML_DQK, 128), F32),
        pltpu.VMEM((nch, c, SSD_HEADS * SSD_P), F32),
        pltpu.VMEM((nch * SSD_HEADS, c, SSD_N), F32),
        pltpu.VMEM((nch * SSD_HEADS, SSD_P, SSD_N), F32),
        pltpu.VMEM((nseq * ML_HEADS, ML_DQK, 128), F32),
    ]
    return pl.pallas_call(
        functools.partial(_mixer_kernel, tm=tm, c=c, slot=slot, nt=nt, na=na, nbk=nbk),
        grid=(ngrp, nt), in_specs=in_specs, out_specs=out_specs, out_shape=out_shape,
        scratch_shapes=scratch,
        compiler_params=pltpu.CompilerParams(dimension_semantics=("arbitrary", "arbitrary"),
                                             vmem_limit_bytes=VMEM_LIMIT),
        name=("mixer_sample" if slot else "mixer_prompt"),
    )(*args)


def _ffn_call(x, layer, prm, *, slot, tail=None):
    rows = x.shape[0]
    tm = FFN_TM
    nt = 1 if slot else 2048 // tm
    ngrp = rows // (tm * nt)
    row_spec = lambda w: pl.BlockSpec((tm, w), lambda b, t: (b * nt + t, 0))
    in_specs = [row_spec(D_MODEL)]
    args = [x]
    nseq = tm // SLOT if slot else 1
    if slot:
        in_specs.append(pl.BlockSpec((None, nseq, 2, D_FF), lambda b, t: (layer, b, 0, 0)))
        args.append(tail)
    in_specs += [_const_spec((D_MODEL, 2 * D_FF), layer), _const_spec((D_FF, D_MODEL), layer),
                 _const_spec((3, D_FF), layer), _const_spec((1, D_FF), layer),
                 _const_spec((1, D_MODEL), layer), _const_spec((1, D_MODEL), layer)]
    args += [prm["ffn_w_up"], prm["ffn_w_down"], prm["ffn_conv_w"], prm["ffn_conv_b"],
             prm["norm_ffn_pre"], prm["norm_ffn_post"]]
    gate_spec = pl.BlockSpec((nseq, 2, D_FF), lambda b, t: (b, 0, 0))
    gate_shape = (ngrp * nseq, 2, D_FF)
    return pl.pallas_call(
        functools.partial(_ffn_kernel, tm=tm, slot=slot, nt=nt),
        grid=(ngrp, nt), in_specs=in_specs, out_specs=[row_spec(D_MODEL), gate_spec],
        out_shape=[jax.ShapeDtypeStruct((rows, D_MODEL), F32), jax.ShapeDtypeStruct(gate_shape, F32)],
        scratch_shapes=[pltpu.VMEM((SUBLANES, D_FF), F32), pltpu.VMEM((tm, D_FF), BF16)],
        compiler_params=pltpu.CompilerParams(dimension_semantics=("arbitrary", "arbitrary"),
                                             vmem_limit_bytes=VMEM_LIMIT),
        name=("ffn_sample" if slot else "ffn_prompt"),
    )(*args)


def _prepare_params(norm_mix_pre, norm_mix_post, norm_ffn_pre, norm_ffn_post, w_in, conv_w, conv_b,
                    gdn_a_log, gdn_dt_bias, gdn_norm, mlstm_i_bias, mlstm_f_bias, mlstm_norm,
                    ssd_a_log, ssd_dt_bias, ssd_d, ssd_norm, w_out, ffn_w_up, ffn_conv_w, ffn_conv_b,
                    ffn_w_down):
    w_in_p = jnp.concatenate(
        [w_in[..., :2816], w_in[..., 2824:3848], w_in[..., 3856:4112], w_in[..., 2816:2824],
         w_in[..., 3848:3856], w_in[..., 4112:4116],
         jnp.zeros((DEPTH, D_MODEL, N_IN - 4116), w_in.dtype)], axis=-1).astype(BF16)
    z4 = jnp.zeros((DEPTH, 4), F32)
    pad = jnp.zeros((DEPTH, 128 - 20), F32)
    gprm = jnp.stack(
        [jnp.concatenate([gdn_dt_bias, z4, mlstm_i_bias, mlstm_f_bias, ssd_dt_bias, pad], axis=-1),
         jnp.concatenate([gdn_a_log, z4, z4, z4, ssd_a_log, pad], axis=-1),
         jnp.concatenate([z4, z4, z4, z4, ssd_d, pad], axis=-1)]
        + [jnp.zeros((DEPTH, 128), F32)] * 5, axis=1)
    row = lambda a: a[:, None, :]
    return dict(
        w_in=w_in_p, w_out=w_out.astype(BF16), conv_w=conv_w, conv_b=row(conv_b), gprm=gprm,
        norm_mix_pre=row(norm_mix_pre), norm_mix_post=row(norm_mix_post),
        norm_ffn_pre=row(norm_ffn_pre), norm_ffn_post=row(norm_ffn_post),
        gdn_norm=row(gdn_norm), mlstm_norm=row(mlstm_norm), ssd_norm=row(ssd_norm),
        ffn_w_up=ffn_w_up.astype(BF16), ffn_w_down=ffn_w_down.astype(BF16),
        ffn_conv_w=ffn_conv_w, ffn_conv_b=row(ffn_conv_b))


def kernel(x_prompt, x_sample, state_conv, state_gdn, state_mlstm_c, state_mlstm_n, state_mlstm_m, state_ssd, state_ffn_conv, norm_mix_pre, norm_mix_post, norm_ffn_pre, norm_ffn_post, w_in, conv_w, conv_b, gdn_a_log, gdn_dt_bias, gdn_norm, mlstm_i_bias, mlstm_f_bias, mlstm_norm, ssd_a_log, ssd_dt_bias, ssd_d, ssd_norm, w_out, ffn_w_up, ffn_conv_w, ffn_conv_b, ffn_w_down):
    prm = _prepare_params(norm_mix_pre, norm_mix_post, norm_ffn_pre, norm_ffn_post, w_in, conv_w, conv_b,
                          gdn_a_log, gdn_dt_bias, gdn_norm, mlstm_i_bias, mlstm_f_bias, mlstm_norm,
                          ssd_a_log, ssd_dt_bias, ssd_d, ssd_norm, w_out, ffn_w_up, ffn_conv_w,
                          ffn_conv_b, ffn_w_down)
    bp, lp, _ = x_prompt.shape
    bs, ls, _ = x_sample.shape

    x = x_prompt.reshape(bp * lp, D_MODEL)
    p_states = []
    for layer in range(DEPTH):
        x, conv3, gdn, mc, mn, mm, ssd = _mixer_call(x, layer, prm, slot=False)
        x, gate2 = _ffn_call(x, layer, prm, slot=False)
        p_states.append((conv3, gdn, mc, mn, mm[:, 0, L_MF:L_MF + ML_HEADS], ssd, gate2))
    y_prompt = x.reshape(bp, lp, D_MODEL)

    x = jnp.pad(x_sample, ((0, 0), (SLOT_FIRST, SLOT - SLOT_FIRST - ls), (0, 0))).reshape(bs * SLOT, D_MODEL)
    mm_in = jnp.pad(state_mlstm_m[:, :, None, :],
                    ((0, 0), (0, 0), (0, SUBLANES - 1), (L_MF, 128 - L_MF - ML_HEADS)))
    s_states = []
    for layer in range(DEPTH):
        x, conv3, gdn, mc, mn, mm, ssd = _mixer_call(
            x, layer, prm, slot=True, tail=state_conv,
            states=(state_gdn, state_mlstm_c, state_mlstm_n, mm_in, state_ssd))
        x, gate2 = _ffn_call(x, layer, prm, slot=True, tail=state_ffn_conv)
        s_states.append((conv3, gdn, mc, mn, mm[:, 0, L_MF:L_MF + ML_HEADS], ssd, gate2))
    y_sample = x.reshape(bs, SLOT, D_MODEL)[:, SLOT_FIRST:SLOT_LAST + 1]

    p_out = [jnp.stack(s, axis=0) for s in zip(*p_states)]
    s_out = [jnp.stack(s, axis=0) for s in zip(*s_states)]
    return (y_prompt, y_sample, *p_out, *s_out)
```

```python
import functools

import jax
import jax.numpy as jnp
from jax import lax
from jax.experimental import pallas as pl
from jax.experimental.pallas import tpu as pltpu

F32 = jnp.float32
BF16 = jnp.bfloat16

D_MODEL = 1024
DEPTH = 2
GDN_HEADS, GDN_DK, GDN_DV = 4, 128, 128
ML_HEADS, ML_DQK, ML_DV = 4, 64, 64
SSD_HEADS, SSD_P, SSD_GROUPS, SSD_N = 4, 64, 2, 128
D_FF = 2816
EPS = 1e-6

CONV_DIM = 2304
C_GQ, C_GK, C_GV, C_SX, C_SB, C_SC = 0, 512, 1024, 1536, 1792, 2048
C_GG, C_MQ, C_MK, C_MV, C_MO, C_SZ, C_GATE = 2304, 2816, 3072, 3328, 3584, 3840, 4096
N_IN = 4224
IN_NBLK = 1408
L_GA, L_GB, L_MI, L_MF, L_DT = 0, 4, 8, 12, 16

SUBLANES = 8
SLOT = 8
SLOT_FIRST, SLOT_LAST = 3, 6
FF_BLK = 256
VMEM_LIMIT = 56 * 1024 * 1024

PROMPT_TM, PROMPT_CHUNK, PROMPT_NA, PROMPT_NB = 512, 64, 4, 1
SAMPLE_TM, SAMPLE_NA, SAMPLE_NB = 64, 4, 4
FFN_TM = 1024


def _dot(a, b):
    return jnp.dot(a, b, preferred_element_type=F32)


def _dot_nt(a, b):
    return lax.dot_general(a, b, (((1,), (1,)), ((), ())), preferred_element_type=F32)


def _dot_tn(a, b):
    return lax.dot_general(a, b, (((0,), (0,)), ((), ())), preferred_element_type=F32)


def _sigmoid(x):
    return 1.0 / (1.0 + jnp.exp(-x))


def _silu(x):
    return x * _sigmoid(x)


def _rms(x, w):
    return x * lax.rsqrt(jnp.mean(x * x, axis=-1, keepdims=True) + EPS) * w


def _l2n(x):
    return x * lax.rsqrt(jnp.sum(x * x, axis=-1, keepdims=True) + EPS)


def _gelu_tanh(x):
    return 0.5 * x * (1.0 + jnp.tanh(0.7978845608028654 * (x + 0.044715 * (x * x * x))))


def _col(a, l):
    return a[:, l:l + 1]


def _cumsum_rows(tril_b, x):
    hi = x.astype(BF16)
    r1 = x - hi.astype(F32)
    mid = r1.astype(BF16)
    lo = (r1 - mid.astype(F32)).astype(BF16)
    return _dot(tril_b, hi) + _dot(tril_b, mid) + _dot(tril_b, lo)


def _conv_silu(proj_scr, r0, c, col0, width, cw_ref, cb_ref):
    win = proj_scr[pl.ds(r0, c + SUBLANES), col0:col0 + width]
    acc = cb_ref[0:1, col0:col0 + width] + cw_ref[3:4, col0:col0 + width] * win[SUBLANES:, :]
    for j in range(3):
        shifted = pltpu.roll(win, 3 - j, 0)[SUBLANES:, :]
        acc = acc + cw_ref[j:j + 1, col0:col0 + width] * shifted
    return _silu(acc)


def _gates(proj_scr, r0, c, slot, gprm_ref, tril_b):
    graw = proj_scr[pl.ds(r0 + SUBLANES, c), C_GATE:C_GATE + 128]
    lane = lax.broadcasted_iota(jnp.int32, (c, 128), 1)
    z = graw + gprm_ref[0:1, :]
    soft = jnp.log(1.0 + jnp.exp(-jnp.abs(z)))
    sp = jnp.maximum(z, 0.0) + soft
    log_sig = -(jnp.maximum(-z, 0.0) + soft)
    a_neg = -jnp.exp(gprm_ref[1:2, :])
    is_ga = lane < L_GB
    is_gb = (lane >= L_GB) & (lane < L_MI)
    is_mi = (lane >= L_MI) & (lane < L_MF)
    is_mf = (lane >= L_MF) & (lane < L_DT)
    is_dt = (lane >= L_DT) & (lane < L_DT + SSD_HEADS)
    cum_src = jnp.where(is_ga | is_dt, a_neg * sp, jnp.where(is_mf, log_sig, 0.0))
    elem = jnp.where(is_gb, _sigmoid(graw), jnp.where(is_mi, z, jnp.where(is_dt, sp, 0.0)))
    if slot:
        rr = lax.broadcasted_iota(jnp.int32, (c, 128), 0)
        valid = (rr >= SLOT_FIRST) & (rr <= SLOT_LAST)
        cum_src = jnp.where(valid, cum_src, 0.0)
        elem = jnp.where(valid, elem, jnp.where(is_mi, -jnp.inf, 0.0))
    cum = _cumsum_rows(tril_b, cum_src)
    if c == 128:
        return cum, elem, cum.T, elem.T
    parts = [cum, elem]
    if 2 * c < 128:
        parts.append(jnp.zeros((128 - 2 * c, 128), F32))
    zt = jnp.concatenate(parts, axis=0).T
    return cum, elem, zt[:, 0:c], zt[:, c:2 * c]


def _row0(i, c):
    return i * c if isinstance(i, int) else pl.multiple_of(i * c, c)


def _pass_a(j, *, c, slot, na, proj_scr, cw_ref, cb_ref, gprm_ref,
            cum_s, col_s, gqx_s, gob_s, ml_s, sy_s, sce_s, sdh_s, defer=False):
    chunks = [j * na + a for a in range(na)]
    r0s = [_row0(i, c) for i in chunks]
    conv = functools.partial(_conv_silu, proj_scr, c=c, cw_ref=cw_ref, cb_ref=cb_ref)

    ii = lax.broadcasted_iota(jnp.int32, (c, c), 0)
    jj = lax.broadcasted_iota(jnp.int32, (c, c), 1)
    tril = ii >= jj
    strict = ii > jj
    eye = (ii == jj).astype(F32)
    tril_b = tril.astype(BF16)
    lane = lax.broadcasted_iota(jnp.int32, (c, 128), 1)

    gates = [_gates(proj_scr, r0, c, slot, gprm_ref, tril_b) for r0 in r0s]
    cum = [g[0] for g in gates]
    elem = [g[1] for g in gates]
    cum_t = [g[2] for g in gates]
    elem_t = [g[3] for g in gates]
    for a in range(na):
        cum_s[chunks[a]] = cum[a]

    def row_cum(a, l):
        return cum_t[a][l:l + 1, :]

    def row_elem(a, l):
        return elem_t[a][l:l + 1, :]

    it = [(a, h) for a in range(na) for h in range(GDN_HEADS)]
    n = range(len(it))
    gd, ml, sd = {}, {}, {}

    def g_k():
        k = [_l2n(conv(r0=r0s[a], col0=C_GK + h * GDN_DK, width=GDN_DK)) for a, h in it]
        gam_c = [_col(cum[a], L_GA + h) for a, h in it]
        beta_c = [_col(elem[a], L_GB + h) for a, h in it]
        dmat = [jnp.exp(jnp.where(tril, gam_c[x] - row_cum(a, L_GA + h), -jnp.inf))
                for x, (a, h) in enumerate(it)]
        pk = [-jnp.where(strict, _dot_nt(k[x], k[x]) * dmat[x] * beta_c[x], 0.0) for x in n]
        gd.update(k=k, gam_c=gam_c, beta_c=beta_c, dmat=dmat, pk=pk, t_inv=[eye + pk[x] for x in n])

    def g_neumann():
        pk = [_dot(gd["pk"][x], gd["pk"][x]) for x in n]
        gd.update(pk=pk, t_inv=[gd["t_inv"][x] + _dot(gd["t_inv"][x], pk[x]) for x in n])

    def g_q():
        gd["q"] = [_l2n(conv(r0=r0s[a], col0=C_GQ + h * GDN_DK, width=GDN_DK)) * (GDN_DK ** -0.5)
                   for a, h in it]

    def g_v():
        v = [conv(r0=r0s[a], col0=C_GV + h * GDN_DV, width=GDN_DV) for a, h in it]
        eg = [jnp.exp(gd["gam_c"][x]) for x in n]
        gd.update(eg=eg, rhs=[jnp.concatenate([v[x] * gd["beta_c"][x],
                                               gd["k"][x] * (gd["beta_c"][x] * eg[x])], axis=1) for x in n])

    def g_uw():
        gd["uw"] = [_dot(gd["t_inv"][x], gd["rhs"][x]) for x in n]

    def g_qk():
        gd["qk"] = [_dot_nt(gd["q"][x], gd["k"][x]) * gd["dmat"][x] for x in n]
        gd["kd"] = [gd["k"][x] * jnp.exp(gd["gam_c"][x][c - 1:c, :] - gd["gam_c"][x]) for x in n]

    def g_out():
        quw = [_dot(gd["qk"][x], gd["uw"][x]) for x in n]
        kuw = [_dot_tn(gd["kd"][x], gd["uw"][x]) for x in n]
        for x, (a, h) in enumerate(it):
            idx = chunks[a] * GDN_HEADS + h
            gqx_s[idx, 0:c, :] = gd["q"][x] * gd["eg"][x] - quw[x][:, GDN_DV:]
            gqx_s[idx, c:c + GDN_DK, :] = kuw[x][:, GDN_DV:]
            gob_s[idx, 0:c, :] = quw[x][:, :GDN_DV]
            gob_s[idx, c:c + GDN_DK, :] = kuw[x][:, :GDN_DV]

    def piece(a, base, h):
        return proj_scr[pl.ds(r0s[a] + SUBLANES, c), base + h * ML_DQK:base + (h + 1) * ML_DQK]

    def m_1():
        mq = [piece(a, C_MQ, h) for a, h in it]
        mk = [piece(a, C_MK, h) * (ML_DQK ** -0.5) for a, h in it]
        b_c = [_col(cum[a], L_MF + h) for a, h in it]
        d = [jnp.where(tril, b_c[x] - row_cum(a, L_MF + h) + row_elem(a, L_MI + h), -jnp.inf)
             for x, (a, h) in enumerate(it)]
        dmax = [jnp.max(d[x], axis=-1, keepdims=True) for x in n]
        dsafe = [jnp.where(dmax[x] == -jnp.inf, 0.0, dmax[x]) for x in n]
        s0 = [_dot_nt(mq[x], mk[x]) * jnp.exp(d[x] - dsafe[x]) for x in n]
        for a in range(na):
            cols = jnp.zeros((c, 128), F32)
            for h in range(ML_HEADS):
                cols = jnp.where(lane == L_MF + h, dmax[a * ML_HEADS + h], cols)
            col_s[chunks[a]] = cols
        ml.update(mk=mk, b_c=b_c, dsafe=dsafe, s0=s0)

    def m_2():
        mv = [piece(a, C_MV, h) for a, h in it]
        i_c = [_col(elem[a], L_MI + h) for a, h in it]
        ones_col = (lax.broadcasted_iota(jnp.int32, (c, 128 - ML_DV), 1) == 0).astype(F32)
        v_aug = [jnp.concatenate([mv[x], ones_col], axis=1) for x in n]
        num0 = [_dot(ml["s0"][x], v_aug[x]) for x in n]
        b_c, dsafe = ml["b_c"], ml["dsafe"]
        kw0 = [ml["mk"][x] * jnp.exp(b_c[x][c - 1:c, :] - b_c[x] + i_c[x] - dsafe[x][c - 1:c, :]) for x in n]
        kv0 = [_dot_tn(kw0[x], v_aug[x]) for x in n]
        for x, (a, h) in enumerate(it):
            idx = chunks[a] * ML_HEADS + h
            ml_s[idx, 0:c, :] = num0[x]
            ml_s[idx, c:c + ML_DQK, :] = kv0[x]

    rep = SSD_HEADS // SSD_GROUPS
    gi = [(a, g) for a in range(na) for g in range(SSD_GROUPS)]
    grp = [a * SSD_GROUPS + h // rep for a, h in it]

    def s_1():
        bg = [conv(r0=r0s[a], col0=C_SB + g * SSD_N, width=SSD_N) for a, g in gi]
        cg = [conv(r0=r0s[a], col0=C_SC + g * SSD_N, width=SSD_N) for a, g in gi]
        sd.update(bg=bg, cg=cg, cb_raw=[_dot_nt(cg[y], bg[y]) for y in range(len(gi))])

    def s_2():
        xg = [conv(r0=r0s[a], col0=C_SX + g * rep * SSD_P, width=rep * SSD_P) for a, g in gi]
        xs = [xg[grp[x]][:, (h % rep) * SSD_P:(h % rep + 1) * SSD_P] for x, (a, h) in enumerate(it)]
        sg_c = [_col(cum[a], L_DT + h) for a, h in it]
        dt_c = [_col(elem[a], L_DT + h) for a, h in it]
        cb = [sd["cb_raw"][grp[x]] * jnp.exp(jnp.where(tril, sg_c[x] - row_cum(a, L_DT + h), -jnp.inf))
              * row_elem(a, L_DT + h) for x, (a, h) in enumerate(it)]
        y0 = [_dot(cb[x], xs[x]) + gprm_ref[2:3, L_DT + h:L_DT + h + 1] * xs[x]
              for x, (a, h) in enumerate(it)]
        dh = [_dot_tn(xs[x] * (jnp.exp(sg_c[x][c - 1:c, :] - sg_c[x]) * dt_c[x]), sd["bg"][grp[x]]) for x in n]
        for x, (a, h) in enumerate(it):
            idx = chunks[a] * SSD_HEADS + h
            sce_s[idx] = sd["cg"][grp[x]] * jnp.exp(sg_c[x])
            sdh_s[idx] = dh[x]
        for a in range(na):
            sy_s[chunks[a]] = jnp.concatenate(y0[a * SSD_HEADS:(a + 1) * SSD_HEADS], axis=1)

    chain = [g_k] + [g_neumann] * (c.bit_length() - 2) + [g_uw]
    fill = [g_q, g_v, m_1, m_2, s_1, s_2]
    order = []
    for pos, link in enumerate(chain):
        order.append(link)
        if pos < len(fill):
            order.append(fill[pos])
    order += fill[len(chain):] + [g_qk, g_out]
    if defer:
        return order
    for stage in order:
        stage()


def _pass_b(j, *, c, slot, nb, proj_scr, mix_scr, gdnn_ref, mln_ref, ssdn_ref,
            cum_s, col_s, gqx_s, gob_s, ml_s, sy_s, sce_s, sdh_s, mlc_s, gdn_o, mm_o, ssd_o):
    hs = range(GDN_HEADS)
    lane_row = lax.broadcasted_iota(jnp.int32, (1, 128), 1)
    is_m = (lane_row >= L_MF) & (lane_row < L_MF + ML_HEADS)

    def load(seq):
        return dict(s=[gdn_o[seq, h] for h in hs], c=[mlc_s[seq * ML_HEADS + h] for h in hs],
                    h=[ssd_o[seq, h] for h in hs], m=mm_o[seq, 0:1, :])

    def store(seq, st):
        for h in hs:
            gdn_o[seq, h] = st["s"][h]
            mlc_s[seq * ML_HEADS + h] = st["c"][h]
            ssd_o[seq, h] = st["h"][h]
        mm_o[seq, 0:1, :] = st["m"]

    def advance(i, st):
        prow = pl.ds(_row0(i, c) + SUBLANES, c)
        cum = cum_s[i]
        dmx = col_s[i]
        last = cum[c - 1:c, :]
        e_last = jnp.exp(last)
        mq = [proj_scr[prow, C_MQ + h * ML_DQK:C_MQ + (h + 1) * ML_DQK] for h in hs]
        r = [_dot(gqx_s[i * GDN_HEADS + h], st["s"][h]) for h in hs]
        full = [_dot(mq[h], st["c"][h]) for h in hs]
        yh = [_dot_nt(sce_s[i * SSD_HEADS + h], st["h"][h]) for h in hs]
        inter = cum + st["m"]
        m_t = jnp.maximum(inter, dmx)
        w_intra = jnp.exp(dmx - m_t)
        m_new = jnp.where(is_m, m_t[c - 1:c, :], 0.0)
        w_c_row = jnp.exp(last + st["m"] - m_new)
        w_l_row = w_intra[c - 1:c, :]
        ob = [gob_s[i * GDN_HEADS + h] for h in hs]
        blk = [ml_s[i * ML_HEADS + h] for h in hs]
        new = dict(
            s=[_col(e_last, L_GA + h) * st["s"][h] - r[h][c:, :] + ob[h][c:, :] for h in hs],
            c=[_col(w_c_row, L_MF + h) * st["c"][h] + _col(w_l_row, L_MF + h) * blk[h][c:c + ML_DQK, :]
               for h in hs],
            h=[_col(e_last, L_DT + h) * st["h"][h] + sdh_s[i * SSD_HEADS + h] for h in hs],
            m=m_new)
        return new, dict(i=i, r=r, full=full, yh=yh, ob=ob, blk=blk, inter=inter, m_t=m_t, w_intra=w_intra)

    def outputs(ctx):
        i = ctx["i"]
        r0 = _row0(i, c)
        rows = pl.ds(r0, c)
        prow = pl.ds(r0 + SUBLANES, c)
        w_inter = jnp.exp(ctx["inter"] - ctx["m_t"])
        e_neg_m = jnp.exp(-ctx["m_t"])
        for h in hs:
            gg = proj_scr[prow, C_GG + h * GDN_DV:C_GG + (h + 1) * GDN_DV]
            mix_scr[rows, h * GDN_DV:(h + 1) * GDN_DV] = (
                _rms(ctx["r"][h][:c, :] + ctx["ob"][h][:c, :], gdnn_ref[...]) * _silu(gg))
        h_parts = []
        for h in hs:
            fl = (ctx["full"][h] * _col(w_inter, L_MF + h)
                  + _col(ctx["w_intra"], L_MF + h) * ctx["blk"][h][0:c, :])
            den = jnp.maximum(jnp.abs(fl[:, ML_DV:ML_DV + 1]), _col(e_neg_m, L_MF + h))
            mo = proj_scr[prow, C_MO + h * ML_DV:C_MO + (h + 1) * ML_DV]
            h_parts.append(_rms(_sigmoid(mo) * (fl[:, :ML_DV] / den), mln_ref[...]))
        mix_scr[rows, GDN_HEADS * GDN_DV:GDN_HEADS * GDN_DV + ML_HEADS * ML_DV] = (
            jnp.concatenate(h_parts, axis=1))
        sz = proj_scr[prow, C_SZ:C_SZ + SSD_HEADS * SSD_P]
        y_all = (sy_s[i] + jnp.concatenate(ctx["yh"], axis=1)) * _silu(sz)
        mix_scr[rows, GDN_HEADS * GDN_DV + ML_HEADS * ML_DV:D_MODEL] = _rms(y_all, ssdn_ref[...])

    ctxs = []
    st = None
    for a in range(nb):
        i = j * nb + a
        seq = i if slot else 0
        if slot or a == 0:
            st = load(seq)
        st, ctx = advance(i, st)
        if slot or a == nb - 1:
            store(seq, st)
        ctxs.append(ctx)
    for ctx in ctxs:
        outputs(ctx)


def _ml_state_in(c_mat, n_row):
    eye = (lax.broadcasted_iota(jnp.int32, (ML_DQK, ML_DQK), 0)
           == lax.broadcasted_iota(jnp.int32, (ML_DQK, ML_DQK), 1)).astype(F32)
    n_col = jnp.sum(eye * n_row, axis=-1, keepdims=True)
    first = lax.broadcasted_iota(jnp.int32, (ML_DQK, 128 - ML_DV), 1) == 0
    return jnp.concatenate([c_mat, jnp.where(first, n_col, 0.0)], axis=1)


def _ml_state_out(c_aug):
    eye = (lax.broadcasted_iota(jnp.int32, (ML_DQK, ML_DQK), 0)
           == lax.broadcasted_iota(jnp.int32, (ML_DQK, ML_DQK), 1)).astype(F32)
    n_row = jnp.sum(eye * c_aug[:, ML_DV:ML_DV + 1], axis=0, keepdims=True)
    return c_aug[:, :ML_DV], n_row


def _mixer_kernel(*refs, tm, c, slot, nt, na, nbk):
    if slot:
        (x_ref, tail_ref, win_ref, wout_ref, cw_ref, cb_ref, gprm_ref, npre_ref, npost_ref, gdnn_ref,
         mln_ref, ssdn_ref, gdn_i, mc_i, mn_i, mm_i, ssd_i,
         o_ref, conv_o, gdn_o, mc_o, mn_o, mm_o, ssd_o, proj_scr, mix_scr, *ab) = refs
    else:
        (x_ref, win_ref, wout_ref, cw_ref, cb_ref, gprm_ref, npre_ref, npost_ref, gdnn_ref,
         mln_ref, ssdn_ref,
         o_ref, conv_o, gdn_o, mc_o, mn_o, mm_o, ssd_o, proj_scr, mix_scr, *ab) = refs
    names = ("cum_s", "col_s", "gqx_s", "gob_s", "ml_s", "sy_s", "sce_s", "sdh_s", "mlc_s")
    ab = dict(zip(names, ab))
    mlc_s = ab["mlc_s"]
    nseq = gdn_o.shape[0]
    t = pl.program_id(1)

    if slot:
        proj_scr[0:SUBLANES, :] = jnp.zeros((SUBLANES, N_IN), F32)
        gdn_o[...] = gdn_i[...]
        mm_o[...] = mm_i[...]
        ssd_o[...] = ssd_i[...]

        def load_ml(s, carry):
            for h in range(ML_HEADS):
                mlc_s[s * ML_HEADS + h] = _ml_state_in(mc_i[s, h], mn_i[s, h:h + 1, :])
            return carry

        lax.fori_loop(0, nseq, load_ml, 0)
    else:
        @pl.when(t == 0)
        def _():
            proj_scr[0:SUBLANES, :] = jnp.zeros((SUBLANES, N_IN), F32)
            gdn_o[...] = jnp.zeros(gdn_o.shape, F32)
            mlc_s[...] = jnp.zeros(mlc_s.shape, F32)
            mm_o[...] = jnp.zeros(mm_o.shape, F32)
            ssd_o[...] = jnp.zeros(ssd_o.shape, F32)

    x = x_ref[...]
    hn = _rms(x, npre_ref[...]).astype(BF16)
    for nb in range(N_IN // IN_NBLK):
        cols = slice(nb * IN_NBLK, (nb + 1) * IN_NBLK)
        proj_scr[SUBLANES:SUBLANES + tm, cols] = _dot(hn, win_ref[:, cols])

    if slot:
        for s in range(nseq):
            base = SUBLANES + s * SLOT
            proj_scr[base:base + SLOT_FIRST, 0:CONV_DIM] = tail_ref[s]
            conv_o[s] = proj_scr[base + SLOT_LAST - 2:base + SLOT_LAST + 1, 0:CONV_DIM]

    pass_a = functools.partial(_pass_a, c=c, slot=slot, na=na, proj_scr=proj_scr, cw_ref=cw_ref,
                               cb_ref=cb_ref, gprm_ref=gprm_ref,
                               **{k: v for k, v in ab.items() if k != "mlc_s"})
    pass_b = functools.partial(_pass_b, c=c, slot=slot, nb=nbk, proj_scr=proj_scr, mix_scr=mix_scr,
                               gdnn_ref=gdnn_ref, mln_ref=mln_ref, ssdn_ref=ssdn_ref,
                               gdn_o=gdn_o, mm_o=mm_o, ssd_o=ssd_o, **ab)

    def body_a(j, carry):
        pass_a(j)
        return carry

    def body_b(i, carry):
        pass_b(i)
        return carry

    n_a, n_b = tm // (c * na), tm // (c * nbk)
    if slot:
        lax.fori_loop(0, n_a, body_a, 0)
        lax.fori_loop(0, n_b, body_b, 0)
    else:
        b_per_a = n_b // n_a
        pass_a(0)
        for g in range(1, n_a + 1):
            b_steps = [functools.partial(pass_b, (g - 1) * b_per_a + k) for k in range(b_per_a)]
            a_steps = pass_a(g, defer=True) if g < n_a else []
            stride = max(1, len(a_steps) // max(1, len(b_steps)))
            while a_steps or b_steps:
                for step in a_steps[:stride]:
                    step()
                a_steps = a_steps[stride:]
                if b_steps:
                    b_steps.pop(0)()

    def store_ml(s, carry):
        for h in range(ML_HEADS):
            c_mat, n_row = _ml_state_out(mlc_s[s * ML_HEADS + h])
            mc_o[s, h] = c_mat
            mn_o[s, h:h + 1, :] = n_row
        return carry

    if slot:
        lax.fori_loop(0, nseq, store_ml, 0)
    else:
        @pl.when(t == nt - 1)
        def _():
            store_ml(0, 0)

    if not slot:
        last_rows = proj_scr[tm:tm + SUBLANES, 0:CONV_DIM]
        proj_scr[0:SUBLANES, 0:CONV_DIM] = last_rows

        @pl.when(t == nt - 1)
        def _():
            conv_o[0] = last_rows[SUBLANES - 3:, :]

    out = _dot(mix_scr[...].astype(BF16), wout_ref[...])
    o_ref[...] = x + _rms(out, npost_ref[...])


def _ffn_kernel(*refs, tm, slot, nt):
    if slot:
        (x_ref, ftail_ref, wup_ref, wdn_ref, fw_ref, fb_ref, npre_ref, npost_ref,
         o_ref, gate_o, tails_scr, act_scr) = refs
    else:
        (x_ref, wup_ref, wdn_ref, fw_ref, fb_ref, npre_ref, npost_ref,
         o_ref, gate_o, tails_scr, act_scr) = refs
    t = pl.program_id(1)

    if slot:
        tails_scr[...] = jnp.zeros(tails_scr.shape, F32)
    else:
        @pl.when(t == 0)
        def _():
            tails_scr[...] = jnp.zeros(tails_scr.shape, F32)

    x = x_ref[...]
    hn = _rms(x, npre_ref[...]).astype(BF16)
    nblk = D_FF // FF_BLK

    def up(blk):
        return (_dot(hn, wup_ref[:, blk * FF_BLK:(blk + 1) * FF_BLK]),
                _dot(hn, wup_ref[:, D_FF + blk * FF_BLK:D_FF + (blk + 1) * FF_BLK]))

    ahead = up(0)
    for blk in range(nblk):
        cols = slice(blk * FF_BLK, (blk + 1) * FF_BLK)
        gate, val = ahead
        if blk + 1 < nblk:
            ahead = up(blk + 1)
        if slot:
            g3 = gate.reshape(tm // SLOT, SLOT, FF_BLK)
            ft = ftail_ref[:, :, cols]
            rr = lax.broadcasted_iota(jnp.int32, g3.shape, 1)
            g3 = jnp.where(rr == SLOT_FIRST - 2, ft[:, 0:1, :], jnp.where(rr == SLOT_FIRST - 1, ft[:, 1:2, :], g3))
            gate_o[:, :, cols] = g3[:, SLOT_LAST - 1:SLOT_LAST + 1, :]
            gate = g3.reshape(tm, FF_BLK)
        full = jnp.concatenate([tails_scr[:, cols], gate], axis=0)
        conv = fb_ref[0:1, cols] + fw_ref[2:3, cols] * gate
        for j in range(2):
            conv = conv + fw_ref[j:j + 1, cols] * pltpu.roll(full, 2 - j, 0)[SUBLANES:, :]
        last_rows = full[tm:tm + SUBLANES, :]
        tails_scr[:, cols] = last_rows
        act_scr[:, cols] = (_gelu_tanh(conv) * val).astype(BF16)
    o_ref[...] = x + _rms(_dot(act_scr[...], wdn_ref[...]), npost_ref[...])
    if not slot:
        @pl.when(t == nt - 1)
        def _():
            gate_o[0] = tails_scr[SUBLANES - 2:, :]


def _const_spec(shape, layer):
    nd = len(shape)
    return pl.BlockSpec((None,) + tuple(shape), lambda b, t: (layer,) + (0,) * nd,
                        pipeline_mode=pl.Buffered(1))


def _mixer_call(x, layer, prm, *, slot, states=None, tail=None):
    rows = x.shape[0]
    if slot:
        tm, c, nt, na, nbk = SAMPLE_TM, SLOT, 1, SAMPLE_NA, SAMPLE_NB
        nseq = tm // SLOT
    else:
        tm, c, na, nbk = PROMPT_TM, PROMPT_CHUNK, PROMPT_NA, PROMPT_NB
        nt = 2048 // tm
        nseq = 1
    nch = tm // c
    ngrp = rows // (tm * nt)
    nb = ngrp * nseq
    row_spec = lambda w: pl.BlockSpec((tm, w), lambda b, t: (b * nt + t, 0))
    st_out = lambda *dims: pl.BlockSpec((nseq,) + dims, lambda b, t: (b,) + (0,) * len(dims))
    st_in = lambda *dims: pl.BlockSpec((None, nseq) + dims, lambda b, t: (layer, b) + (0,) * len(dims))

    in_specs = [row_spec(D_MODEL)]
    args = [x]
    if slot:
        in_specs.append(st_in(SLOT_FIRST, CONV_DIM))
        args.append(tail)
    in_specs += [
        _const_spec((D_MODEL, N_IN), layer), _const_spec((D_MODEL, D_MODEL), layer),
        _const_spec((4, CONV_DIM), layer), _const_spec((1, CONV_DIM), layer),
        _const_spec((8, 128), layer), _const_spec((1, D_MODEL), layer), _const_spec((1, D_MODEL), layer),
        _const_spec((1, GDN_DV), layer), _const_spec((1, ML_DV), layer),
        _const_spec((1, SSD_HEADS * SSD_P), layer)]
    args += [prm["w_in"], prm["w_out"], prm["conv_w"], prm["conv_b"], prm["gprm"], prm["norm_mix_pre"],
             prm["norm_mix_post"], prm["gdn_norm"], prm["mlstm_norm"], prm["ssd_norm"]]
    if slot:
        in_specs += [st_in(GDN_HEADS, GDN_DK, GDN_DV), st_in(ML_HEADS, ML_DQK, ML_DV),
                     st_in(ML_HEADS, ML_DQK), st_in(SUBLANES, 128), st_in(SSD_HEADS, SSD_P, SSD_N)]
        args += list(states)

    conv_spec, conv_shape = st_out(3, CONV_DIM), (nb, 3, CONV_DIM)
    out_specs = [row_spec(D_MODEL), conv_spec,
                 st_out(GDN_HEADS, GDN_DK, GDN_DV), st_out(ML_HEADS, ML_DQK, ML_DV),
                 st_out(ML_HEADS, ML_DQK), st_out(SUBLANES, 128), st_out(SSD_HEADS, SSD_P, SSD_N)]
    out_shape = [jax.ShapeDtypeStruct((rows, D_MODEL), F32), jax.ShapeDtypeStruct(conv_shape, F32),
                 jax.ShapeDtypeStruct((nb, GDN_HEADS, GDN_DK, GDN_DV), F32),
                 jax.ShapeDtypeStruct((nb, ML_HEADS, ML_DQK, ML_DV), F32),
                 jax.ShapeDtypeStruct((nb, ML_HEADS, ML_DQK), F32),
                 jax.ShapeDtypeStruct((nb, SUBLANES, 128), F32),
                 jax.ShapeDtypeStruct((nb, SSD_HEADS, SSD_P, SSD_N), F32)]
    scratch = [
        pltpu.VMEM((tm + SUBLANES, N_IN), F32),
        pltpu.VMEM((tm, D_MODEL), F32),
        pltpu.VMEM((nch, c, 128), F32),
        pltpu.VMEM((nch, c, 128), F32),
        pltpu.VMEM((nch * GDN_HEADS, c + GDN_DK, GDN_DV), F32),
        pltpu.VMEM((nch * GDN_HEADS, c + GDN_DK, GDN_DV), F32),
        pltpu.VMEM((nch * ML_HEADS, c + ML_DQK, 128), F32),
        pltpu.VMEM((nch, c, SSD_HEADS * SSD_P), F32),
        pltpu.VMEM((nch * SSD_HEADS, c, SSD_N), F32),
        pltpu.VMEM((nch * SSD_HEADS, SSD_P, SSD_N), F32),
        pltpu.VMEM((nseq * ML_HEADS, ML_DQK, 128), F32),
    ]
    return pl.pallas_call(
        functools.partial(_mixer_kernel, tm=tm, c=c, slot=slot, nt=nt, na=na, nbk=nbk),
        grid=(ngrp, nt), in_specs=in_specs, out_specs=out_specs, out_shape=out_shape,
        scratch_shapes=scratch,
        compiler_params=pltpu.CompilerParams(dimension_semantics=("arbitrary", "arbitrary"),
                                             vmem_limit_bytes=VMEM_LIMIT),
        name=("mixer_sample" if slot else "mixer_prompt"),
    )(*args)


def _ffn_call(x, layer, prm, *, slot, tail=None):
    rows = x.shape[0]
    tm = FFN_TM
    nt = 1 if slot else 2048 // tm
    ngrp = rows // (tm * nt)
    row_spec = lambda w: pl.BlockSpec((tm, w), lambda b, t: (b * nt + t, 0))
    in_specs = [row_spec(D_MODEL)]
    args = [x]
    nseq = tm // SLOT if slot else 1
    if slot:
        in_specs.append(pl.BlockSpec((None, nseq, 2, D_FF), lambda b, t: (layer, b, 0, 0)))
        args.append(tail)
    in_specs += [_const_spec((D_MODEL, 2 * D_FF), layer), _const_spec((D_FF, D_MODEL), layer),
                 _const_spec((3, D_FF), layer), _const_spec((1, D_FF), layer),
                 _const_spec((1, D_MODEL), layer), _const_spec((1, D_MODEL), layer)]
    args += [prm["ffn_w_up"], prm["ffn_w_down"], prm["ffn_conv_w"], prm["ffn_conv_b"],
             prm["norm_ffn_pre"], prm["norm_ffn_post"]]
    gate_spec = pl.BlockSpec((nseq, 2, D_FF), lambda b, t: (b, 0, 0))
    gate_shape = (ngrp * nseq, 2, D_FF)
    return pl.pallas_call(
        functools.partial(_ffn_kernel, tm=tm, slot=slot, nt=nt),
        grid=(ngrp, nt), in_specs=in_specs, out_specs=[row_spec(D_MODEL), gate_spec],
        out_shape=[jax.ShapeDtypeStruct((rows, D_MODEL), F32), jax.ShapeDtypeStruct(gate_shape, F32)],
        scratch_shapes=[pltpu.VMEM((SUBLANES, D_FF), F32), pltpu.VMEM((tm, D_FF), BF16)],
        compiler_params=pltpu.CompilerParams(dimension_semantics=("arbitrary", "arbitrary"),
                                             vmem_limit_bytes=VMEM_LIMIT),
        name=("ffn_sample" if slot else "ffn_prompt"),
    )(*args)


def _prepare_params(norm_mix_pre, norm_mix_post, norm_ffn_pre, norm_ffn_post, w_in, conv_w, conv_b,
                    gdn_a_log, gdn_dt_bias, gdn_norm, mlstm_i_bias, mlstm_f_bias, mlstm_norm,
                    ssd_a_log, ssd_dt_bias, ssd_d, ssd_norm, w_out, ffn_w_up, ffn_conv_w, ffn_conv_b,
                    ffn_w_down):
    w_in_p = jnp.concatenate(
        [w_in[..., :2816], w_in[..., 2824:3848], w_in[..., 3856:4112], w_in[..., 2816:2824],
         w_in[..., 3848:3856], w_in[..., 4112:4116],
         jnp.zeros((DEPTH, D_MODEL, N_IN - 4116), w_in.dtype)], axis=-1).astype(BF16)
    z4 = jnp.zeros((DEPTH, 4), F32)
    pad = jnp.zeros((DEPTH, 128 - 20), F32)
    gprm = jnp.stack(
        [jnp.concatenate([gdn_dt_bias, z4, mlstm_i_bias, mlstm_f_bias, ssd_dt_bias, pad], axis=-1),
         jnp.concatenate([gdn_a_log, z4, z4, z4, ssd_a_log, pad], axis=-1),
         jnp.concatenate([z4, z4, z4, z4, ssd_d, pad], axis=-1)]
        + [jnp.zeros((DEPTH, 128), F32)] * 5, axis=1)
    row = lambda a: a[:, None, :]
    return dict(
        w_in=w_in_p, w_out=w_out.astype(BF16), conv_w=conv_w, conv_b=row(conv_b), gprm=gprm,
        norm_mix_pre=row(norm_mix_pre), norm_mix_post=row(norm_mix_post),
        norm_ffn_pre=row(norm_ffn_pre), norm_ffn_post=row(norm_ffn_post),
        gdn_norm=row(gdn_norm), mlstm_norm=row(mlstm_norm), ssd_norm=row(ssd_norm),
        ffn_w_up=ffn_w_up.astype(BF16), ffn_w_down=ffn_w_down.astype(BF16),
        ffn_conv_w=ffn_conv_w, ffn_conv_b=row(ffn_conv_b))


def kernel(x_prompt, x_sample, state_conv, state_gdn, state_mlstm_c, state_mlstm_n, state_mlstm_m, state_ssd, state_ffn_conv, norm_mix_pre, norm_mix_post, norm_ffn_pre, norm_ffn_post, w_in, conv_w, conv_b, gdn_a_log, gdn_dt_bias, gdn_norm, mlstm_i_bias, mlstm_f_bias, mlstm_norm, ssd_a_log, ssd_dt_bias, ssd_d, ssd_norm, w_out, ffn_w_up, ffn_conv_w, ffn_conv_b, ffn_w_down):
    prm = _prepare_params(norm_mix_pre, norm_mix_post, norm_ffn_pre, norm_ffn_post, w_in, conv_w, conv_b,
                          gdn_a_log, gdn_dt_bias, gdn_norm, mlstm_i_bias, mlstm_f_bias, mlstm_norm,
                          ssd_a_log, ssd_dt_bias, ssd_d, ssd_norm, w_out, ffn_w_up, ffn_conv_w,
                          ffn_conv_b, ffn_w_down)
    bp, lp, _ = x_prompt.shape
    bs, ls, _ = x_sample.shape

    x = x_prompt.reshape(bp * lp, D_MODEL)
    p_states = []
    for layer in range(DEPTH):
        x, conv3, gdn, mc, mn, mm, ssd = _mixer_call(x, layer, prm, slot=False)
        x, gate2 = _ffn_call(x, layer, prm, slot=False)
        p_states.append((conv3, gdn, mc, mn, mm[:, 0, L_MF:L_MF + ML_HEADS], ssd, gate2))
    y_prompt = x.reshape(bp, lp, D_MODEL)

    x = jnp.pad(x_sample, ((0, 0), (SLOT_FIRST, SLOT - SLOT_FIRST - ls), (0, 0))).reshape(bs * SLOT, D_MODEL)
    mm_in = jnp.pad(state_mlstm_m[:, :, None, :],
                    ((0, 0), (0, 0), (0, SUBLANES - 1), (L_MF, 128 - L_MF - ML_HEADS)))
    s_states = []
    for layer in range(DEPTH):
        x, conv3, gdn, mc, mn, mm, ssd = _mixer_call(
            x, layer, prm, slot=True, tail=state_conv,
            states=(state_gdn, state_mlstm_c, state_mlstm_n, mm_in, state_ssd))
        x, gate2 = _ffn_call(x, layer, prm, slot=True, tail=state_ffn_conv)
        s_states.append((conv3, gdn, mc, mn, mm[:, 0, L_MF:L_MF + ML_HEADS], ssd, gate2))
    y_sample = x.reshape(bs, SLOT, D_MODEL)[:, SLOT_FIRST:SLOT_LAST + 1]

    p_out = [jnp.stack(s, axis=0) for s in zip(*p_states)]
    s_out = [jnp.stack(s, axis=0) for s in zip(*s_states)]
    return (y_prompt, y_sample, *p_out, *s_out)
```

```python
import functools

import jax
import jax.numpy as jnp
from jax import lax
from jax.experimental import pallas as pl
from jax.experimental.pallas import tpu as pltpu

F32 = jnp.float32
BF16 = jnp.bfloat16

D_MODEL = 1024
DEPTH = 2
GDN_HEADS, GDN_DK, GDN_DV = 4, 128, 128
ML_HEADS, ML_DQK, ML_DV = 4, 64, 64
SSD_HEADS, SSD_P, SSD_GROUPS, SSD_N = 4, 64, 2, 128
D_FF = 2816
EPS = 1e-6

CONV_DIM = 2304
C_GQ, C_GK, C_GV, C_SX, C_SB, C_SC = 0, 512, 1024, 1536, 1792, 2048
C_GG, C_MQ, C_MK, C_MV, C_MO, C_SZ, C_GATE = 2304, 2816, 3072, 3328, 3584, 3840, 4096
N_IN = 4224
IN_NBLK = 1408
L_GA, L_GB, L_MI, L_MF, L_DT = 0, 4, 8, 12, 16

SUBLANES = 8
SLOT = 8
SLOT_FIRST, SLOT_LAST = 3, 6
FF_BLK = 256
VMEM_LIMIT = 56 * 1024 * 1024

PROMPT_TM, PROMPT_CHUNK, PROMPT_NA, PROMPT_NB = 512, 64, 4, 1
SAMPLE_TM, SAMPLE_NA, SAMPLE_NB = 64, 4, 4
FFN_TM = 1024


def _dot(a, b):
    return jnp.dot(a, b, preferred_element_type=F32)


def _dot_nt(a, b):
    return lax.dot_general(a, b, (((1,), (1,)), ((), ())), preferred_element_type=F32)


def _dot_tn(a, b):
    return lax.dot_general(a, b, (((0,), (0,)), ((), ())), preferred_element_type=F32)


def _sigmoid(x):
    return 1.0 / (1.0 + jnp.exp(-x))


def _silu(x):
    return x * _sigmoid(x)


def _rms(x, w):
    return x * lax.rsqrt(jnp.mean(x * x, axis=-1, keepdims=True) + EPS) * w


def _l2n(x):
    return x * lax.rsqrt(jnp.sum(x * x, axis=-1, keepdims=True) + EPS)


def _gelu_tanh(x):
    return 0.5 * x * (1.0 + jnp.tanh(0.7978845608028654 * (x + 0.044715 * (x * x * x))))


def _col(a, l):
    return a[:, l:l + 1]


def _cumsum_rows(tril_b, x):
    hi = x.astype(BF16)
    r1 = x - hi.astype(F32)
    mid = r1.astype(BF16)
    lo = (r1 - mid.astype(F32)).astype(BF16)
    return _dot(tril_b, hi) + _dot(tril_b, mid) + _dot(tril_b, lo)


def _conv_silu(proj_scr, r0, c, col0, width, cw_ref, cb_ref):
    win = proj_scr[pl.ds(r0, c + SUBLANES), col0:col0 + width]
    acc = cb_ref[0:1, col0:col0 + width] + cw_ref[3:4, col0:col0 + width] * win[SUBLANES:, :]
    for j in range(3):
        shifted = pltpu.roll(win, 3 - j, 0)[SUBLANES:, :]
        acc = acc + cw_ref[j:j + 1, col0:col0 + width] * shifted
    return _silu(acc)


def _gates(proj_scr, r0, c, slot, gprm_ref, tril_b):
    graw = proj_scr[pl.ds(r0 + SUBLANES, c), C_GATE:C_GATE + 128]
    lane = lax.broadcasted_iota(jnp.int32, (c, 128), 1)
    z = graw + gprm_ref[0:1, :]
    soft = jnp.log(1.0 + jnp.exp(-jnp.abs(z)))
    sp = jnp.maximum(z, 0.0) + soft
    log_sig = -(jnp.maximum(-z, 0.0) + soft)
    a_neg = -jnp.exp(gprm_ref[1:2, :])
    is_ga = lane < L_GB
    is_gb = (lane >= L_GB) & (lane < L_MI)
    is_mi = (lane >= L_MI) & (lane < L_MF)
    is_mf = (lane >= L_MF) & (lane < L_DT)
    is_dt = (lane >= L_DT) & (lane < L_DT + SSD_HEADS)
    cum_src = jnp.where(is_ga | is_dt, a_neg * sp, jnp.where(is_mf, log_sig, 0.0))
    elem = jnp.where(is_gb, _sigmoid(graw), jnp.where(is_mi, z, jnp.where(is_dt, sp, 0.0)))
    if slot:
        rr = lax.broadcasted_iota(jnp.int32, (c, 128), 0)
        valid = (rr >= SLOT_FIRST) & (rr <= SLOT_LAST)
        cum_src = jnp.where(valid, cum_src, 0.0)
        elem = jnp.where(valid, elem, jnp.where(is_mi, -jnp.inf, 0.0))
    cum = _cumsum_rows(tril_b, cum_src)
    if c == 128:
        return cum, elem, cum.T, elem.T
    parts = [cum, elem]
    if 2 * c < 128:
        parts.append(jnp.zeros((128 - 2 * c, 128), F32))
    zt = jnp.concatenate(parts, axis=0).T
    return cum, elem, zt[:, 0:c], zt[:, c:2 * c]


def _row0(i, c):
    return i * c if isinstance(i, int) else pl.multiple_of(i * c, c)


def _pass_a(j, *, c, slot, na, proj_scr, cw_ref, cb_ref, gprm_ref,
            cum_s, col_s, gqx_s, gob_s, ml_s, sy_s, sce_s, sdh_s, defer=False):
    chunks = [j * na + a for a in range(na)]
    r0s = [_row0(i, c) for i in chunks]
    conv = functools.partial(_conv_silu, proj_scr, c=c, cw_ref=cw_ref, cb_ref=cb_ref)

    ii = lax.broadcasted_iota(jnp.int32, (c, c), 0)
    jj = lax.broadcasted_iota(jnp.int32, (c, c), 1)
    tril = ii >= jj
    strict = ii > jj
    eye = (ii == jj).astype(F32)
    tril_b = tril.astype(BF16)
    lane = lax.broadcasted_iota(jnp.int32, (c, 128), 1)

    gates = [_gates(proj_scr, r0, c, slot, gprm_ref, tril_b) for r0 in r0s]
    cum = [g[0] for g in gates]
    elem = [g[1] for g in gates]
    cum_t = [g[2] for g in gates]
    elem_t = [g[3] for g in gates]
    for a in range(na):
        cum_s[chunks[a]] = cum[a]

    def row_cum(a, l):
        return cum_t[a][l:l + 1, :]

    def row_elem(a, l):
        return elem_t[a][l:l + 1, :]

    it = [(a, h) for a in range(na) for h in range(GDN_HEADS)]
    n = range(len(it))
    gd, ml, sd = {}, {}, {}

    def g_k():
        k = [_l2n(conv(r0=r0s[a], col0=C_GK + h * GDN_DK, width=GDN_DK)) for a, h in it]
        gam_c = [_col(cum[a], L_GA + h) for a, h in it]
        beta_c = [_col(elem[a], L_GB + h) for a, h in it]
        dmat = [jnp.exp(jnp.where(tril, gam_c[x] - row_cum(a, L_GA + h), -jnp.inf))
                for x, (a, h) in enumerate(it)]
        pk = [-jnp.where(strict, _dot_nt(k[x], k[x]) * dmat[x] * beta_c[x], 0.0) for x in n]
        gd.update(k=k, gam_c=gam_c, beta_c=beta_c, dmat=dmat, pk=pk, t_inv=[eye + pk[x] for x in n])

    def g_neumann():
        pk = [_dot(gd["pk"][x], gd["pk"][x]) for x in n]
        gd.update(pk=pk, t_inv=[gd["t_inv"][x] + _dot(gd["t_inv"][x], pk[x]) for x in n])

    def g_q():
        gd["q"] = [_l2n(conv(r0=r0s[a], col0=C_GQ + h * GDN_DK, width=GDN_DK)) * (GDN_DK ** -0.5)
                   for a, h in it]

    def g_v():
        v = [conv(r0=r0s[a], col0=C_GV + h * GDN_DV, width=GDN_DV) for a, h in it]
        eg = [jnp.exp(gd["gam_c"][x]) for x in n]
        gd.update(eg=eg, rhs=[jnp.concatenate([v[x] * gd["beta_c"][x],
                                               gd["k"][x] * (gd["beta_c"][x] * eg[x])], axis=1) for x in n])

    def g_uw():
        gd["uw"] = [_dot(gd["t_inv"][x], gd["rhs"][x]) for x in n]

    def g_qk():
        gd["qk"] = [_dot_nt(gd["q"][x], gd["k"][x]) * gd["dmat"][x] for x in n]
        gd["kd"] = [gd["k"][x] * jnp.exp(gd["gam_c"][x][c - 1:c, :] - gd["gam_c"][x]) for x in n]

    def g_out():
        quw = [_dot(gd["qk"][x], gd["uw"][x]) for x in n]
        kuw = [_dot_tn(gd["kd"][x], gd["uw"][x]) for x in n]
        for x, (a, h) in enumerate(it):
            idx = chunks[a] * GDN_HEADS + h
            gqx_s[idx, 0:c, :] = gd["q"][x] * gd["eg"][x] - quw[x][:, GDN_DV:]
            gqx_s[idx, c:c + GDN_DK, :] = kuw[x][:, GDN_DV:]
            gob_s[idx, 0:c, :] = quw[x][:, :GDN_DV]
            gob_s[idx, c:c + GDN_DK, :] = kuw[x][:, :GDN_DV]

    def piece(a, base, h):
        return proj_scr[pl.ds(r0s[a] + SUBLANES, c), base + h * ML_DQK:base + (h + 1) * ML_DQK]

    def m_1():
        mq = [piece(a, C_MQ, h) for a, h in it]
        mk = [piece(a, C_MK, h) * (ML_DQK ** -0.5) for a, h in it]
        b_c = [_col(cum[a], L_MF + h) for a, h in it]
        d = [jnp.where(tril, b_c[x] - row_cum(a, L_MF + h) + row_elem(a, L_MI + h), -jnp.inf)
             for x, (a, h) in enumerate(it)]
        dmax = [jnp.max(d[x], axis=-1, keepdims=True) for x in n]
        dsafe = [jnp.where(dmax[x] == -jnp.inf, 0.0, dmax[x]) for x in n]
        s0 = [_dot_nt(mq[x], mk[x]) * jnp.exp(d[x] - dsafe[x]) for x in n]
        for a in range(na):
            cols = jnp.zeros((c, 128), F32)
            for h in range(ML_HEADS):
                cols = jnp.where(lane == L_MF + h, dmax[a * ML_HEADS + h], cols)
            col_s[chunks[a]] = cols
        ml.update(mk=mk, b_c=b_c, dsafe=dsafe, s0=s0)

    def m_2():
        mv = [piece(a, C_MV, h) for a, h in it]
        i_c = [_col(elem[a], L_MI + h) for a, h in it]
        ones_col = (lax.broadcasted_iota(jnp.int32, (c, 128 - ML_DV), 1) == 0).astype(F32)
        v_aug = [jnp.concatenate([mv[x], ones_col], axis=1) for x in n]
        num0 = [_dot(ml["s0"][x], v_aug[x]) for x in n]
        b_c, dsafe = ml["b_c"], ml["dsafe"]
        kw0 = [ml["mk"][x] * jnp.exp(b_c[x][c - 1:c, :] - b_c[x] + i_c[x] - dsafe[x][c - 1:c, :]) for x in n]
        kv0 = [_dot_tn(kw0[x], v_aug[x]) for x in n]
        for x, (a, h) in enumerate(it):
            idx = chunks[a] * ML_HEADS + h
            ml_s[idx, 0:c, :] = num0[x]
            ml_s[idx, c:c + ML_DQK, :] = kv0[x]

    rep = SSD_HEADS // SSD_GROUPS
    gi = [(a, g) for a in range(na) for g in range(SSD_GROUPS)]
    grp = [a * SSD_GROUPS + h // rep for a, h in it]

    def s_1():
        bg = [conv(r0=r0s[a], col0=C_SB + g * SSD_N, width=SSD_N) for a, g in gi]
        cg = [conv(r0=r0s[a], col0=C_SC + g * SSD_N, width=SSD_N) for a, g in gi]
        sd.update(bg=bg, cg=cg, cb_raw=[_dot_nt(cg[y], bg[y]) for y in range(len(gi))])

    def s_2():
        xg = [conv(r0=r0s[a], col0=C_SX + g * rep * SSD_P, width=rep * SSD_P) for a, g in gi]
        xs = [xg[grp[x]][:, (h % rep) * SSD_P:(h % rep + 1) * SSD_P] for x, (a, h) in enumerate(it)]
        sg_c = [_col(cum[a], L_DT + h) for a, h in it]
        dt_c = [_col(elem[a], L_DT + h) for a, h in it]
        cb = [sd["cb_raw"][grp[x]] * jnp.exp(jnp.where(tril, sg_c[x] - row_cum(a, L_DT + h), -jnp.inf))
              * row_elem(a, L_DT + h) for x, (a, h) in enumerate(it)]
        y0 = [_dot(cb[x], xs[x]) + gprm_ref[2:3, L_DT + h:L_DT + h + 1] * xs[x]
              for x, (a, h) in enumerate(it)]
        dh = [_dot_tn(xs[x] * (jnp.exp(sg_c[x][c - 1:c, :] - sg_c[x]) * dt_c[x]), sd["bg"][grp[x]]) for x in n]
        for x, (a, h) in enumerate(it):
            idx = chunks[a] * SSD_HEADS + h
            sce_s[idx] = sd["cg"][grp[x]] * jnp.exp(sg_c[x])
            sdh_s[idx] = dh[x]
        for a in range(na):
            sy_s[chunks[a]] = jnp.concatenate(y0[a * SSD_HEADS:(a + 1) * SSD_HEADS], axis=1)

    chain = [g_k] + [g_neumann] * (c.bit_length() - 2) + [g_uw]
    fill = [g_q, g_v, m_1, m_2, s_1, s_2]
    order = []
    for pos, link in enumerate(chain):
        order.append(link)
        if pos < len(fill):
            order.append(fill[pos])
    order += fill[len(chain):] + [g_qk, g_out]
    if defer:
        return order
    for stage in order:
        stage()


def _pass_b(j, *, c, slot, nb, proj_scr, mix_scr, gdnn_ref, mln_ref, ssdn_ref,
            cum_s, col_s, gqx_s, gob_s, ml_s, sy_s, sce_s, sdh_s, mlc_s, gdn_o, mm_o, ssd_o):
    hs = range(GDN_HEADS)
    lane_row = lax.broadcasted_iota(jnp.int32, (1, 128), 1)
    is_m = (lane_row >= L_MF) & (lane_row < L_MF + ML_HEADS)

    def load(seq):
        return dict(s=[gdn_o[seq, h] for h in hs], c=[mlc_s[seq * ML_HEADS + h] for h in hs],
                    h=[ssd_o[seq, h] for h in hs], m=mm_o[seq, 0:1, :])

    def store(seq, st):
        for h in hs:
            gdn_o[seq, h] = st["s"][h]
            mlc_s[seq * ML_HEADS + h] = st["c"][h]
            ssd_o[seq, h] = st["h"][h]
        mm_o[seq, 0:1, :] = st["m"]

    def advance(i, st):
        prow = pl.ds(_row0(i, c) + SUBLANES, c)
        cum = cum_s[i]
        dmx = col_s[i]
        last = cum[c - 1:c, :]
        e_last = jnp.exp(last)
        mq = [proj_scr[prow, C_MQ + h * ML_DQK:C_MQ + (h + 1) * ML_DQK] for h in hs]
        r = [_dot(gqx_s[i * GDN_HEADS + h], st["s"][h]) for h in hs]
        full = [_dot(mq[h], st["c"][h]) for h in hs]
        yh = [_dot_nt(sce_s[i * SSD_HEADS + h], st["h"][h]) for h in hs]
        inter = cum + st["m"]
        m_t = jnp.maximum(inter, dmx)
        w_intra = jnp.exp(dmx - m_t)
        m_new = jnp.where(is_m, m_t[c - 1:c, :], 0.0)
        w_c_row = jnp.exp(last + st["m"] - m_new)
        w_l_row = w_intra[c - 1:c, :]
        ob = [gob_s[i * GDN_HEADS + h] for h in hs]
        blk = [ml_s[i * ML_HEADS + h] for h in hs]
        new = dict(
            s=[_col(e_last, L_GA + h) * st["s"][h] - r[h][c:, :] + ob[h][c:, :] for h in hs],
            c=[_col(w_c_row, L_MF + h) * st["c"][h] + _col(w_l_row, L_MF + h) * blk[h][c:c + ML_DQK, :]
               for h in hs],
            h=[_col(e_last, L_DT + h) * st["h"][h] + sdh_s[i * SSD_HEADS + h] for h in hs],
            m=m_new)
        return new, dict(i=i, r=r, full=full, yh=yh, ob=ob, blk=blk, inter=inter, m_t=m_t, w_intra=w_intra)

    def outputs(ctx):
        i = ctx["i"]
        r0 = _row0(i, c)
        rows = pl.ds(r0, c)
        prow = pl.ds(r0 + SUBLANES, c)
        w_inter = jnp.exp(ctx["inter"] - ctx["m_t"])
        e_neg_m = jnp.exp(-ctx["m_t"])
        for h in hs:
            gg = proj_scr[prow, C_GG + h * GDN_DV:C_GG + (h + 1) * GDN_DV]
            mix_scr[rows, h * GDN_DV:(h + 1) * GDN_DV] = (
                _rms(ctx["r"][h][:c, :] + ctx["ob"][h][:c, :], gdnn_ref[...]) * _silu(gg))
        h_parts = []
        for h in hs:
            fl = (ctx["full"][h] * _col(w_inter, L_MF + h)
                  + _col(ctx["w_intra"], L_MF + h) * ctx["blk"][h][0:c, :])
            den = jnp.maximum(jnp.abs(fl[:, ML_DV:ML_DV + 1]), _col(e_neg_m, L_MF + h))
            mo = proj_scr[prow, C_MO + h * ML_DV:C_MO + (h + 1) * ML_DV]
            h_parts.append(_rms(_sigmoid(mo) * (fl[:, :ML_DV] / den), mln_ref[...]))
        mix_scr[rows, GDN_HEADS * GDN_DV:GDN_HEADS * GDN_DV + ML_HEADS * ML_DV] = (
            jnp.concatenate(h_parts, axis=1))
        sz = proj_scr[prow, C_SZ:C_SZ + SSD_HEADS * SSD_P]
        y_all = (sy_s[i] + jnp.concatenate(ctx["yh"], axis=1)) * _silu(sz)
        mix_scr[rows, GDN_HEADS * GDN_DV + ML_HEADS * ML_DV:D_MODEL] = _rms(y_all, ssdn_ref[...])

    ctxs = []
    st = None
    for a in range(nb):
        i = j * nb + a
        seq = i if slot else 0
        if slot or a == 0:
            st = load(seq)
        st, ctx = advance(i, st)
        if slot or a == nb - 1:
            store(seq, st)
        ctxs.append(ctx)
    for ctx in ctxs:
        outputs(ctx)


def _ml_state_in(c_mat, n_row):
    eye = (lax.broadcasted_iota(jnp.int32, (ML_DQK, ML_DQK), 0)
           == lax.broadcasted_iota(jnp.int32, (ML_DQK, ML_DQK), 1)).astype(F32)
    n_col = jnp.sum(eye * n_row, axis=-1, keepdims=True)
    first = lax.broadcasted_iota(jnp.int32, (ML_DQK, 128 - ML_DV), 1) == 0
    return jnp.concatenate([c_mat, jnp.where(first, n_col, 0.0)], axis=1)


def _ml_state_out(c_aug):
    eye = (lax.broadcasted_iota(jnp.int32, (ML_DQK, ML_DQK), 0)
           == lax.broadcasted_iota(jnp.int32, (ML_DQK, ML_DQK), 1)).astype(F32)
    n_row = jnp.sum(eye * c_aug[:, ML_DV:ML_DV + 1], axis=0, keepdims=True)
    return c_aug[:, :ML_DV], n_row


def _mixer_kernel(*refs, tm, c, slot, nt, na, nbk):
    if slot:
        (x_ref, tail_ref, win_ref, wout_ref, cw_ref, cb_ref, gprm_ref, npre_ref, npost_ref, gdnn_ref,
         mln_ref, ssdn_ref, gdn_i, mc_i, mn_i, mm_i, ssd_i, _, _, _, _, _, _,
         o_ref, conv_o, gdn_o, mc_o, mn_o, mm_o, ssd_o, proj_scr, mix_scr, *ab) = refs
    else:
        (x_ref, win_ref, wout_ref, cw_ref, cb_ref, gprm_ref, npre_ref, npost_ref, gdnn_ref,
         mln_ref, ssdn_ref, _, _, _, _, _, _,
         o_ref, conv_o, gdn_o, mc_o, mn_o, mm_o, ssd_o, proj_scr, mix_scr, *ab) = refs
    names = ("cum_s", "col_s", "gqx_s", "gob_s", "ml_s", "sy_s", "sce_s", "sdh_s", "mlc_s")
    ab = dict(zip(names, ab))
    mlc_s = ab["mlc_s"]
    nseq = gdn_o.shape[0]
    t = pl.program_id(1)

    if slot:
        proj_scr[0:SUBLANES, :] = jnp.zeros((SUBLANES, N_IN), F32)
        gdn_o[...] = gdn_i[...]
        mm_o[...] = mm_i[...]
        ssd_o[...] = ssd_i[...]

        def load_ml(s, carry):
            for h in range(ML_HEADS):
                mlc_s[s * ML_HEADS + h] = _ml_state_in(mc_i[s, h], mn_i[s, h:h + 1, :])
            return carry

        lax.fori_loop(0, nseq, load_ml, 0)
    else:
        @pl.when(t == 0)
        def _():
            proj_scr[0:SUBLANES, :] = jnp.zeros((SUBLANES, N_IN), F32)
            gdn_o[...] = jnp.zeros(gdn_o.shape, F32)
            mlc_s[...] = jnp.zeros(mlc_s.shape, F32)
            mm_o[...] = jnp.zeros(mm_o.shape, F32)
            ssd_o[...] = jnp.zeros(ssd_o.shape, F32)

    x = x_ref[...]
    hn = _rms(x, npre_ref[...]).astype(BF16)
    for nb in range(N_IN // IN_NBLK):
        cols = slice(nb * IN_NBLK, (nb + 1) * IN_NBLK)
        proj_scr[SUBLANES:SUBLANES + tm, cols] = _dot(hn, win_ref[:, cols])

    if slot:
        for s in range(nseq):
            base = SUBLANES + s * SLOT
            proj_scr[base:base + SLOT_FIRST, 0:CONV_DIM] = tail_ref[s]
            conv_o[s] = proj_scr[base + SLOT_LAST - 2:base + SLOT_LAST + 1, 0:CONV_DIM]

    pass_a = functools.partial(_pass_a, c=c, slot=slot, na=na, proj_scr=proj_scr, cw_ref=cw_ref,
                               cb_ref=cb_ref, gprm_ref=gprm_ref,
                               **{k: v for k, v in ab.items() if k != "mlc_s"})
    pass_b = functools.partial(_pass_b, c=c, slot=slot, nb=nbk, proj_scr=proj_scr, mix_scr=mix_scr,
                               gdnn_ref=gdnn_ref, mln_ref=mln_ref, ssdn_ref=ssdn_ref,
                               gdn_o=gdn_o, mm_o=mm_o, ssd_o=ssd_o, **ab)

    def body_a(j, carry):
        pass_a(j)
        return carry

    def body_b(i, carry):
        pass_b(i)
        return carry

    n_a, n_b = tm // (c * na), tm // (c * nbk)
    if slot:
        lax.fori_loop(0, n_a, body_a, 0)
        lax.fori_loop(0, n_b, body_b, 0)
    else:
        b_per_a = n_b // n_a
        pass_a(0)
        for g in range(1, n_a + 1):
            b_steps = [functools.partial(pass_b, (g - 1) * b_per_a + k) for k in range(b_per_a)]
            a_steps = pass_a(g, defer=True) if g < n_a else []
            stride = max(1, len(a_steps) // max(1, len(b_steps)))
            while a_steps or b_steps:
                for step in a_steps[:stride]:
                    step()
                a_steps = a_steps[stride:]
                if b_steps:
                    b_steps.pop(0)()

    def store_ml(s, carry):
        for h in range(ML_HEADS):
            c_mat, n_row = _ml_state_out(mlc_s[s * ML_HEADS + h])
            mc_o[s, h] = c_mat
            mn_o[s, h:h + 1, :] = n_row
        return carry

    if slot:
        lax.fori_loop(0, nseq, store_ml, 0)
    else:
        @pl.when(t == nt - 1)
        def _():
            store_ml(0, 0)

    if not slot:
        last_rows = proj_scr[tm:tm + SUBLANES, 0:CONV_DIM]
        proj_scr[0:SUBLANES, 0:CONV_DIM] = last_rows

        @pl.when(t == nt - 1)
        def _():
            conv_o[0] = last_rows[SUBLANES - 3:, :]

    out = _dot(mix_scr[...].astype(BF16), wout_ref[...])
    o_ref[...] = x + _rms(out, npost_ref[...])


def _ffn_kernel(*refs, tm, slot, nt):
    if slot:
        (x_ref, ftail_ref, wup_ref, wdn_ref, fw_ref, fb_ref, npre_ref, npost_ref, _,
         o_ref, gate_o, tails_scr, act_scr) = refs
    else:
        (x_ref, wup_ref, wdn_ref, fw_ref, fb_ref, npre_ref, npost_ref, _,
         o_ref, gate_o, tails_scr, act_scr) = refs
    t = pl.program_id(1)

    if slot:
        tails_scr[...] = jnp.zeros(tails_scr.shape, F32)
    else:
        @pl.when(t == 0)
        def _():
            tails_scr[...] = jnp.zeros(tails_scr.shape, F32)

    x = x_ref[...]
    hn = _rms(x, npre_ref[...]).astype(BF16)
    nblk = D_FF // FF_BLK

    def up(blk):
        return (_dot(hn, wup_ref[:, blk * FF_BLK:(blk + 1) * FF_BLK]),
                _dot(hn, wup_ref[:, D_FF + blk * FF_BLK:D_FF + (blk + 1) * FF_BLK]))

    ahead = up(0)
    for blk in range(nblk):
        cols = slice(blk * FF_BLK, (blk + 1) * FF_BLK)
        gate, val = ahead
        if blk + 1 < nblk:
            ahead = up(blk + 1)
        if slot:
            g3 = gate.reshape(tm // SLOT, SLOT, FF_BLK)
            ft = ftail_ref[:, :, cols]
            rr = lax.broadcasted_iota(jnp.int32, g3.shape, 1)
            g3 = jnp.where(rr == SLOT_FIRST - 2, ft[:, 0:1, :], jnp.where(rr == SLOT_FIRST - 1, ft[:, 1:2, :], g3))
            gate_o[:, :, cols] = g3[:, SLOT_LAST - 1:SLOT_LAST + 1, :]
            gate = g3.reshape(tm, FF_BLK)
        full = jnp.concatenate([tails_scr[:, cols], gate], axis=0)
        conv = fb_ref[0:1, cols] + fw_ref[2:3, cols] * gate
        for j in range(2):
            conv = conv + fw_ref[j:j + 1, cols] * pltpu.roll(full, 2 - j, 0)[SUBLANES:, :]
        last_rows = full[tm:tm + SUBLANES, :]
        tails_scr[:, cols] = last_rows
        act_scr[:, cols] = (_gelu_tanh(conv) * val).astype(BF16)
    o_ref[...] = x + _rms(_dot(act_scr[...], wdn_ref[...]), npost_ref[...])
    if not slot:
        @pl.when(t == nt - 1)
        def _():
            gate_o[0] = tails_scr[SUBLANES - 2:, :]


def _const_spec(shape, layer):
    nd = len(shape)
    return pl.BlockSpec((None,) + tuple(shape), lambda b, t: (layer,) + (0,) * nd,
                        pipeline_mode=pl.Buffered(1))


def _state_dims():
    return ((3, CONV_DIM), (GDN_HEADS, GDN_DK, GDN_DV), (ML_HEADS, ML_DQK, ML_DV), (ML_HEADS, ML_DQK),
            (SUBLANES, 128), (SSD_HEADS, SSD_P, SSD_N))


def _mixer_call(x, layer, prm, bufs, *, slot, states=None, tail=None):
    rows = x.shape[0]
    if slot:
        tm, c, nt, na, nbk = SAMPLE_TM, SLOT, 1, SAMPLE_NA, SAMPLE_NB
        nseq = tm // SLOT
    else:
        tm, c, na, nbk = PROMPT_TM, PROMPT_CHUNK, PROMPT_NA, PROMPT_NB
        nt = 2048 // tm
        nseq = 1
    nch = tm // c
    ngrp = rows // (tm * nt)
    nb = ngrp * nseq
    row_spec = lambda w: pl.BlockSpec((tm, w), lambda b, t: (b * nt + t, 0))
    st_in = lambda *dims: pl.BlockSpec((None, nseq) + dims, lambda b, t: (layer, b) + (0,) * len(dims))

    in_specs = [row_spec(D_MODEL)]
    args = [x]
    if slot:
        in_specs.append(st_in(SLOT_FIRST, CONV_DIM))
        args.append(tail)
    in_specs += [
        _const_spec((D_MODEL, N_IN), layer), _const_spec((D_MODEL, D_MODEL), layer),
        _const_spec((4, CONV_DIM), layer), _const_spec((1, CONV_DIM), layer),
        _const_spec((8, 128), layer), _const_spec((1, D_MODEL), layer), _const_spec((1, D_MODEL), layer),
        _const_spec((1, GDN_DV), layer), _const_spec((1, ML_DV), layer),
        _const_spec((1, SSD_HEADS * SSD_P), layer)]
    args += [prm["w_in"], prm["w_out"], prm["conv_w"], prm["conv_b"], prm["gprm"], prm["norm_mix_pre"],
             prm["norm_mix_post"], prm["gdn_norm"], prm["mlstm_norm"], prm["ssd_norm"]]
    if slot:
        in_specs += [st_in(GDN_HEADS, GDN_DK, GDN_DV), st_in(ML_HEADS, ML_DQK, ML_DV),
                     st_in(ML_HEADS, ML_DQK), st_in(SUBLANES, 128), st_in(SSD_HEADS, SSD_P, SSD_N)]
        args += list(states)
    aliases = {len(args) + k: 1 + k for k in range(len(bufs))}
    in_specs += [pl.BlockSpec(memory_space=pl.ANY)] * len(bufs)
    args += list(bufs)

    out_specs = [row_spec(D_MODEL)] + [st_in(*dims) for dims in _state_dims()]
    out_shape = [jax.ShapeDtypeStruct((rows, D_MODEL), F32)] + [
        jax.ShapeDtypeStruct((DEPTH, nb) + dims, F32) for dims in _state_dims()]
    scratch = [
        pltpu.VMEM((tm + SUBLANES, N_IN), F32),
        pltpu.VMEM((tm, D_MODEL), F32),
        pltpu.VMEM((nch, c, 128), F32),
        pltpu.VMEM((nch, c, 128), F32),
        pltpu.VMEM((nch * GDN_HEADS, c + GDN_DK, GDN_DV), F32),
        pltpu.VMEM((nch * GDN_HEADS, c + GDN_DK, GDN_DV), F32),
        pltpu.VMEM((nch * ML_HEADS, c + ML_DQK, 128), F32),
        pltpu.VMEM((nch, c, SSD_HEADS * SSD_P), F32),
        pltpu.VMEM((nch * SSD_HEADS, c, SSD_N), F32),
        pltpu.VMEM((nch * SSD_HEADS, SSD_P, SSD_N), F32),
        pltpu.VMEM((nseq * ML_HEADS, ML_DQK, 128), F32),
    ]
    return pl.pallas_call(
        functools.partial(_mixer_kernel, tm=tm, c=c, slot=slot, nt=nt, na=na, nbk=nbk),
        grid=(ngrp, nt), in_specs=in_specs, out_specs=out_specs, out_shape=out_shape,
        scratch_shapes=scratch, input_output_aliases=aliases,
        compiler_params=pltpu.CompilerParams(dimension_semantics=("arbitrary", "arbitrary"),
                                             vmem_limit_bytes=VMEM_LIMIT),
        name=("mixer_sample" if slot else "mixer_prompt"),
    )(*args)


def _ffn_call(x, layer, prm, buf, *, slot, tail=None):
    rows = x.shape[0]
    tm = FFN_TM
    nt = 1 if slot else 2048 // tm
    ngrp = rows // (tm * nt)
    row_spec = lambda w: pl.BlockSpec((tm, w), lambda b, t: (b * nt + t, 0))
    in_specs = [row_spec(D_MODEL)]
    args = [x]
    nseq = tm // SLOT if slot else 1
    if slot:
        in_specs.append(pl.BlockSpec((None, nseq, 2, D_FF), lambda b, t: (layer, b, 0, 0)))
        args.append(tail)
    in_specs += [_const_spec((D_MODEL, 2 * D_FF), layer), _const_spec((D_FF, D_MODEL), layer),
                 _const_spec((3, D_FF), layer), _const_spec((1, D_FF), layer),
                 _const_spec((1, D_MODEL), layer), _const_spec((1, D_MODEL), layer)]
    args += [prm["ffn_w_up"], prm["ffn_w_down"], prm["ffn_conv_w"], prm["ffn_conv_b"],
             prm["norm_ffn_pre"], prm["norm_ffn_post"]]
    aliases = {len(args): 1}
    in_specs.append(pl.BlockSpec(memory_space=pl.ANY))
    args.append(buf)
    gate_spec = pl.BlockSpec((None, nseq, 2, D_FF), lambda b, t: (layer, b, 0, 0))
    gate_shape = (DEPTH, ngrp * nseq, 2, D_FF)
    return pl.pallas_call(
        functools.partial(_ffn_kernel, tm=tm, slot=slot, nt=nt),
        grid=(ngrp, nt), in_specs=in_specs, out_specs=[row_spec(D_MODEL), gate_spec],
        out_shape=[jax.ShapeDtypeStruct((rows, D_MODEL), F32), jax.ShapeDtypeStruct(gate_shape, F32)],
        scratch_shapes=[pltpu.VMEM((SUBLANES, D_FF), F32), pltpu.VMEM((tm, D_FF), BF16)],
        input_output_aliases=aliases,
        compiler_params=pltpu.CompilerParams(dimension_semantics=("arbitrary", "arbitrary"),
                                             vmem_limit_bytes=VMEM_LIMIT),
        name=("ffn_sample" if slot else "ffn_prompt"),
    )(*args)


def _prepare_params(norm_mix_pre, norm_mix_post, norm_ffn_pre, norm_ffn_post, w_in, conv_w, conv_b,
                    gdn_a_log, gdn_dt_bias, gdn_norm, mlstm_i_bias, mlstm_f_bias, mlstm_norm,
                    ssd_a_log, ssd_dt_bias, ssd_d, ssd_norm, w_out, ffn_w_up, ffn_conv_w, ffn_conv_b,
                    ffn_w_down):
    w_in_p = jnp.concatenate(
        [w_in[..., :2816], w_in[..., 2824:3848], w_in[..., 3856:4112], w_in[..., 2816:2824],
         w_in[..., 3848:3856], w_in[..., 4112:4116],
         jnp.zeros((DEPTH, D_MODEL, N_IN - 4116), w_in.dtype)], axis=-1).astype(BF16)
    z4 = jnp.zeros((DEPTH, 4), F32)
    pad = jnp.zeros((DEPTH, 128 - 20), F32)
    gprm = jnp.stack(
        [jnp.concatenate([gdn_dt_bias, z4, mlstm_i_bias, mlstm_f_bias, ssd_dt_bias, pad], axis=-1),
         jnp.concatenate([gdn_a_log, z4, z4, z4, ssd_a_log, pad], axis=-1),
         jnp.concatenate([z4, z4, z4, z4, ssd_d, pad], axis=-1)]
        + [jnp.zeros((DEPTH, 128), F32)] * 5, axis=1)
    row = lambda a: a[:, None, :]
    return dict(
        w_in=w_in_p, w_out=w_out.astype(BF16), conv_w=conv_w, conv_b=row(conv_b), gprm=gprm,
        norm_mix_pre=row(norm_mix_pre), norm_mix_post=row(norm_mix_post),
        norm_ffn_pre=row(norm_ffn_pre), norm_ffn_post=row(norm_ffn_post),
        gdn_norm=row(gdn_norm), mlstm_norm=row(mlstm_norm), ssd_norm=row(ssd_norm),
        ffn_w_up=ffn_w_up.astype(BF16), ffn_w_down=ffn_w_down.astype(BF16),
        ffn_conv_w=ffn_conv_w, ffn_conv_b=row(ffn_conv_b))


def kernel(x_prompt, x_sample, state_conv, state_gdn, state_mlstm_c, state_mlstm_n, state_mlstm_m, state_ssd, state_ffn_conv, norm_mix_pre, norm_mix_post, norm_ffn_pre, norm_ffn_post, w_in, conv_w, conv_b, gdn_a_log, gdn_dt_bias, gdn_norm, mlstm_i_bias, mlstm_f_bias, mlstm_norm, ssd_a_log, ssd_dt_bias, ssd_d, ssd_norm, w_out, ffn_w_up, ffn_conv_w, ffn_conv_b, ffn_w_down):
    prm = _prepare_params(norm_mix_pre, norm_mix_post, norm_ffn_pre, norm_ffn_post, w_in, conv_w, conv_b,
                          gdn_a_log, gdn_dt_bias, gdn_norm, mlstm_i_bias, mlstm_f_bias, mlstm_norm,
                          ssd_a_log, ssd_dt_bias, ssd_d, ssd_norm, w_out, ffn_w_up, ffn_conv_w,
                          ffn_conv_b, ffn_w_down)
    bp, lp, _ = x_prompt.shape
    bs, ls, _ = x_sample.shape

    def state_bufs(nseq):
        return ([jnp.zeros((DEPTH, nseq) + dims, F32) for dims in _state_dims()],
                jnp.zeros((DEPTH, nseq, 2, D_FF), F32))

    def assemble(bufs, gate):
        conv, gdn, mc, mn, mm, ssd = bufs
        return [conv, gdn, mc, mn, mm[:, :, 0, L_MF:L_MF + ML_HEADS], ssd, gate]

    x = x_prompt.reshape(bp * lp, D_MODEL)
    bufs, gate = state_bufs(bp)
    for layer in range(DEPTH):
        x, *bufs = _mixer_call(x, layer, prm, bufs, slot=False)
        x, gate = _ffn_call(x, layer, prm, gate, slot=False)
    p_out = assemble(bufs, gate)
    y_prompt = x.reshape(bp, lp, D_MODEL)

    x = jnp.pad(x_sample, ((0, 0), (SLOT_FIRST, SLOT - SLOT_FIRST - ls), (0, 0))).reshape(bs * SLOT, D_MODEL)
    mm_in = jnp.pad(state_mlstm_m[:, :, None, :],
                    ((0, 0), (0, 0), (0, SUBLANES - 1), (L_MF, 128 - L_MF - ML_HEADS)))
    bufs, gate = state_bufs(bs)
    for layer in range(DEPTH):
        x, *bufs = _mixer_call(x, layer, prm, bufs, slot=True, tail=state_conv,
                               states=(state_gdn, state_mlstm_c, state_mlstm_n, mm_in, state_ssd))
        x, gate = _ffn_call(x, layer, prm, gate, slot=True, tail=state_ffn_conv)
    s_out = assemble(bufs, gate)
    y_sample = x.reshape(bs, SLOT, D_MODEL)[:, SLOT_FIRST:SLOT_LAST + 1]
    return (y_prompt, y_sample, *p_out, *s_out)
```

```python
import functools

import jax
import jax.numpy as jnp
from jax import lax
from jax.experimental import pallas as pl
from jax.experimental.pallas import tpu as pltpu

F32 = jnp.float32
BF16 = jnp.bfloat16

D_MODEL = 1024
DEPTH = 2
GDN_HEADS, GDN_DK, GDN_DV = 4, 128, 128
ML_HEADS, ML_DQK, ML_DV = 4, 64, 64
SSD_HEADS, SSD_P, SSD_GROUPS, SSD_N = 4, 64, 2, 128
D_FF = 2816
EPS = 1e-6

CONV_DIM = 2304
C_GQ, C_GK, C_GV, C_SX, C_SB, C_SC = 0, 512, 1024, 1536, 1792, 2048
C_GG, C_MQ, C_MK, C_MV, C_MO, C_SZ, C_GATE = 2304, 2816, 3072, 3328, 3584, 3840, 4096
N_IN = 4224
IN_NBLK = 1408
L_GA, L_GB, L_MI, L_MF, L_DT = 0, 4, 8, 12, 16

SUBLANES = 8
SLOT = 8
SLOT_FIRST, SLOT_LAST = 3, 6
FF_BLK = 256
VMEM_LIMIT = 56 * 1024 * 1024

PROMPT_TM, PROMPT_CHUNK, PROMPT_NA, PROMPT_NB = 512, 64, 4, 1
SAMPLE_TM, SAMPLE_NA, SAMPLE_NB = 64, 4, 4
FFN_TM = 1024


def _dot(a, b):
    return jnp.dot(a, b, preferred_element_type=F32)


def _dot_nt(a, b):
    return lax.dot_general(a, b, (((1,), (1,)), ((), ())), preferred_element_type=F32)


def _dot_tn(a, b):
    return lax.dot_general(a, b, (((0,), (0,)), ((), ())), preferred_element_type=F32)


def _sigmoid(x):
    return 1.0 / (1.0 + jnp.exp(-x))


def _silu(x):
    return x * _sigmoid(x)


def _rms(x, w):
    return x * lax.rsqrt(jnp.mean(x * x, axis=-1, keepdims=True) + EPS) * w


def _l2n(x):
    return x * lax.rsqrt(jnp.sum(x * x, axis=-1, keepdims=True) + EPS)


def _gelu_tanh(x):
    return 0.5 * x * (1.0 + jnp.tanh(0.7978845608028654 * (x + 0.044715 * (x * x * x))))


def _col(a, l):
    return a[:, l:l + 1]


def _cumsum_rows(tril_b, x):
    hi = x.astype(BF16)
    r1 = x - hi.astype(F32)
    mid = r1.astype(BF16)
    lo = (r1 - mid.astype(F32)).astype(BF16)
    return _dot(tril_b, hi) + _dot(tril_b, mid) + _dot(tril_b, lo)


def _conv_silu(proj_scr, r0, c, col0, width, cw_ref, cb_ref):
    win = proj_scr[pl.ds(r0, c + SUBLANES), col0:col0 + width]
    acc = cw_ref[0:1, col0:col0 + width] * win
    for j in range(1, 4):
        acc = cw_ref[j:j + 1, col0:col0 + width] * win + pltpu.roll(acc, 1, 0)
    return _silu(acc[SUBLANES:, :] + cb_ref[0:1, col0:col0 + width])


def _gates(proj_scr, r0, c, slot, gprm_ref, tril_b):
    graw = proj_scr[pl.ds(r0 + SUBLANES, c), C_GATE:C_GATE + 128]
    lane = lax.broadcasted_iota(jnp.int32, (c, 128), 1)
    z = graw + gprm_ref[0:1, :]
    soft = jnp.log(1.0 + jnp.exp(-jnp.abs(z)))
    sp = jnp.maximum(z, 0.0) + soft
    log_sig = -(jnp.maximum(-z, 0.0) + soft)
    a_neg = -jnp.exp(gprm_ref[1:2, :])
    is_ga = lane < L_GB
    is_gb = (lane >= L_GB) & (lane < L_MI)
    is_mi = (lane >= L_MI) & (lane < L_MF)
    is_mf = (lane >= L_MF) & (lane < L_DT)
    is_dt = (lane >= L_DT) & (lane < L_DT + SSD_HEADS)
    cum_src = jnp.where(is_ga | is_dt, a_neg * sp, jnp.where(is_mf, log_sig, 0.0))
    elem = jnp.where(is_gb, _sigmoid(graw), jnp.where(is_mi, z, jnp.where(is_dt, sp, 0.0)))
    if slot:
        rr = lax.broadcasted_iota(jnp.int32, (c, 128), 0)
        valid = (rr >= SLOT_FIRST) & (rr <= SLOT_LAST)
        cum_src = jnp.where(valid, cum_src, 0.0)
        elem = jnp.where(valid, elem, jnp.where(is_mi, -jnp.inf, 0.0))
    cum = _cumsum_rows(tril_b, cum_src)
    if c == 128:
        return cum, elem, cum.T, elem.T
    parts = [cum, elem]
    if 2 * c < 128:
        parts.append(jnp.zeros((128 - 2 * c, 128), F32))
    zt = jnp.concatenate(parts, axis=0).T
    return cum, elem, zt[:, 0:c], zt[:, c:2 * c]


def _row0(i, c):
    return i * c if isinstance(i, int) else pl.multiple_of(i * c, c)


def _pass_a(j, *, c, slot, na, proj_scr, cw_ref, cb_ref, gprm_ref,
            cum_s, col_s, gqx_s, gob_s, ml_s, sy_s, sce_s, sdh_s, defer=False):
    chunks = [j * na + a for a in range(na)]
    r0s = [_row0(i, c) for i in chunks]
    conv = functools.partial(_conv_silu, proj_scr, c=c, cw_ref=cw_ref, cb_ref=cb_ref)

    ii = lax.broadcasted_iota(jnp.int32, (c, c), 0)
    jj = lax.broadcasted_iota(jnp.int32, (c, c), 1)
    tril = ii >= jj
    strict = ii > jj
    eye = (ii == jj).astype(F32)
    tril_b = tril.astype(BF16)
    lane = lax.broadcasted_iota(jnp.int32, (c, 128), 1)

    gates = [_gates(proj_scr, r0, c, slot, gprm_ref, tril_b) for r0 in r0s]
    cum = [g[0] for g in gates]
    elem = [g[1] for g in gates]
    cum_t = [g[2] for g in gates]
    elem_t = [g[3] for g in gates]
    for a in range(na):
        cum_s[chunks[a]] = cum[a]

    def row_cum(a, l):
        return cum_t[a][l:l + 1, :]

    def row_elem(a, l):
        return elem_t[a][l:l + 1, :]

    it = [(a, h) for a in range(na) for h in range(GDN_HEADS)]
    n = range(len(it))
    gd, ml, sd = {}, {}, {}

    def g_k():
        k = [_l2n(conv(r0=r0s[a], col0=C_GK + h * GDN_DK, width=GDN_DK)) for a, h in it]
        gam_c = [_col(cum[a], L_GA + h) for a, h in it]
        beta_c = [_col(elem[a], L_GB + h) for a, h in it]
        dmat = [jnp.exp(jnp.where(tril, gam_c[x] - row_cum(a, L_GA + h), -jnp.inf))
                for x, (a, h) in enumerate(it)]
        pk = [-jnp.where(strict, _dot_nt(k[x], k[x]) * dmat[x] * beta_c[x], 0.0) for x in n]
        gd.update(k=k, gam_c=gam_c, beta_c=beta_c, dmat=dmat, pk=pk, t_inv=[eye + pk[x] for x in n])

    def g_neumann():
        pk = [_dot(gd["pk"][x], gd["pk"][x]) for x in n]
        gd.update(pk=pk, t_inv=[gd["t_inv"][x] + _dot(gd["t_inv"][x], pk[x]) for x in n])

    def g_q():
        gd["q"] = [_l2n(conv(r0=r0s[a], col0=C_GQ + h * GDN_DK, width=GDN_DK)) * (GDN_DK ** -0.5)
                   for a, h in it]

    def g_v():
        v = [conv(r0=r0s[a], col0=C_GV + h * GDN_DV, width=GDN_DV) for a, h in it]
        eg = [jnp.exp(gd["gam_c"][x]) for x in n]
        gd.update(eg=eg, rhs=[jnp.concatenate([v[x] * gd["beta_c"][x],
                                               gd["k"][x] * (gd["beta_c"][x] * eg[x])], axis=1) for x in n])

    def g_uw():
        gd["uw"] = [_dot(gd["t_inv"][x], gd["rhs"][x]) for x in n]

    def g_qk():
        gd["qk"] = [_dot_nt(gd["q"][x], gd["k"][x]) * gd["dmat"][x] for x in n]
        gd["kd"] = [gd["k"][x] * jnp.exp(gd["gam_c"][x][c - 1:c, :] - gd["gam_c"][x]) for x in n]

    def g_out():
        quw = [_dot(gd["qk"][x], gd["uw"][x]) for x in n]
        kuw = [_dot_tn(gd["kd"][x], gd["uw"][x]) for x in n]
        for x, (a, h) in enumerate(it):
            idx = chunks[a] * GDN_HEADS + h
            gqx_s[idx, 0:c, :] = gd["q"][x] * gd["eg"][x] - quw[x][:, GDN_DV:]
            gqx_s[idx, c:c + GDN_DK, :] = kuw[x][:, GDN_DV:]
            gob_s[idx, 0:c, :] = quw[x][:, :GDN_DV]
            gob_s[idx, c:c + GDN_DK, :] = kuw[x][:, :GDN_DV]

    def piece(a, base, h):
        return proj_scr[pl.ds(r0s[a] + SUBLANES, c), base + h * ML_DQK:base + (h + 1) * ML_DQK]

    def m_1():
        mq = [piece(a, C_MQ, h) for a, h in it]
        mk = [piece(a, C_MK, h) * (ML_DQK ** -0.5) for a, h in it]
        b_c = [_col(cum[a], L_MF + h) for a, h in it]
        d = [jnp.where(tril, b_c[x] - row_cum(a, L_MF + h) + row_elem(a, L_MI + h), -jnp.inf)
             for x, (a, h) in enumerate(it)]
        dmax = [jnp.max(d[x], axis=-1, keepdims=True) for x in n]
        dsafe = [jnp.where(dmax[x] == -jnp.inf, 0.0, dmax[x]) for x in n]
        s0 = [_dot_nt(mq[x], mk[x]) * jnp.exp(d[x] - dsafe[x]) for x in n]
        for a in range(na):
            cols = jnp.zeros((c, 128), F32)
            for h in range(ML_HEADS):
                cols = jnp.where(lane == L_MF + h, dmax[a * ML_HEADS + h], cols)
            col_s[chunks[a]] = cols
        ml.update(mk=mk, b_c=b_c, dsafe=dsafe, s0=s0)

    def m_2():
        mv = [piece(a, C_MV, h) for a, h in it]
        i_c = [_col(elem[a], L_MI + h) for a, h in it]
        ones_col = (lax.broadcasted_iota(jnp.int32, (c, 128 - ML_DV), 1) == 0).astype(F32)
        v_aug = [jnp.concatenate([mv[x], ones_col], axis=1) for x in n]
        num0 = [_dot(ml["s0"][x], v_aug[x]) for x in n]
        b_c, dsafe = ml["b_c"], ml["dsafe"]
        kw0 = [ml["mk"][x] * jnp.exp(b_c[x][c - 1:c, :] - b_c[x] + i_c[x] - dsafe[x][c - 1:c, :]) for x in n]
        kv0 = [_dot_tn(kw0[x], v_aug[x]) for x in n]
        for x, (a, h) in enumerate(it):
            idx = chunks[a] * ML_HEADS + h
            ml_s[idx, 0:c, :] = num0[x]
            ml_s[idx, c:c + ML_DQK, :] = kv0[x]

    rep = SSD_HEADS // SSD_GROUPS
    gi = [(a, g) for a in range(na) for g in range(SSD_GROUPS)]
    grp = [a * SSD_GROUPS + h // rep for a, h in it]

    def s_1():
        bg = [conv(r0=r0s[a], col0=C_SB + g * SSD_N, width=SSD_N) for a, g in gi]
        cg = [conv(r0=r0s[a], col0=C_SC + g * SSD_N, width=SSD_N) for a, g in gi]
        sd.update(bg=bg, cg=cg, cb_raw=[_dot_nt(cg[y], bg[y]) for y in range(len(gi))])

    def s_2():
        xg = [conv(r0=r0s[a], col0=C_SX + g * rep * SSD_P, width=rep * SSD_P) for a, g in gi]
        xs = [xg[grp[x]][:, (h % rep) * SSD_P:(h % rep + 1) * SSD_P] for x, (a, h) in enumerate(it)]
        sg_c = [_col(cum[a], L_DT + h) for a, h in it]
        dt_c = [_col(elem[a], L_DT + h) for a, h in it]
        cb = [sd["cb_raw"][grp[x]] * jnp.exp(jnp.where(tril, sg_c[x] - row_cum(a, L_DT + h), -jnp.inf))
              * row_elem(a, L_DT + h) for x, (a, h) in enumerate(it)]
        y0 = [_dot(cb[x], xs[x]) + gprm_ref[2:3, L_DT + h:L_DT + h + 1] * xs[x]
              for x, (a, h) in enumerate(it)]
        dh = [_dot_tn(xs[x] * (jnp.exp(sg_c[x][c - 1:c, :] - sg_c[x]) * dt_c[x]), sd["bg"][grp[x]]) for x in n]
        for x, (a, h) in enumerate(it):
            idx = chunks[a] * SSD_HEADS + h
            sce_s[idx] = sd["cg"][grp[x]] * jnp.exp(sg_c[x])
            sdh_s[idx] = dh[x]
        for a in range(na):
            sy_s[chunks[a]] = jnp.concatenate(y0[a * SSD_HEADS:(a + 1) * SSD_HEADS], axis=1)

    chain = [g_k] + [g_neumann] * (c.bit_length() - 2) + [g_uw]
    fill = [g_q, g_v, m_1, m_2, s_1, s_2]
    order = []
    for pos, link in enumerate(chain):
        order.append(link)
        if pos < len(fill):
            order.append(fill[pos])
    order += fill[len(chain):] + [g_qk, g_out]
    if defer:
        return order
    for stage in order:
        stage()


def _pass_b(j, *, c, slot, nb, proj_scr, mix_scr, gdnn_ref, mln_ref, ssdn_ref,
            cum_s, col_s, gqx_s, gob_s, ml_s, sy_s, sce_s, sdh_s, mlc_s, gdn_o, mm_o, ssd_o):
    hs = range(GDN_HEADS)
    lane_row = lax.broadcasted_iota(jnp.int32, (1, 128), 1)
    is_m = (lane_row >= L_MF) & (lane_row < L_MF + ML_HEADS)

    def load(seq):
        return dict(s=[gdn_o[seq, h] for h in hs], c=[mlc_s[seq * ML_HEADS + h] for h in hs],
                    h=[ssd_o[seq, h] for h in hs], m=mm_o[seq, 0:1, :])

    def store(seq, st):
        for h in hs:
            gdn_o[seq, h] = st["s"][h]
            mlc_s[seq * ML_HEADS + h] = st["c"][h]
            ssd_o[seq, h] = st["h"][h]
        mm_o[seq, 0:1, :] = st["m"]

    def advance(i, st):
        prow = pl.ds(_row0(i, c) + SUBLANES, c)
        cum = cum_s[i]
        dmx = col_s[i]
        last = cum[c - 1:c, :]
        e_last = jnp.exp(last)
        mq = [proj_scr[prow, C_MQ + h * ML_DQK:C_MQ + (h + 1) * ML_DQK] for h in hs]
        r = [_dot(gqx_s[i * GDN_HEADS + h], st["s"][h]) for h in hs]
        full = [_dot(mq[h], st["c"][h]) for h in hs]
        yh = [_dot_nt(sce_s[i * SSD_HEADS + h], st["h"][h]) for h in hs]
        inter = cum + st["m"]
        m_t = jnp.maximum(inter, dmx)
        w_intra = jnp.exp(dmx - m_t)
        m_new = jnp.where(is_m, m_t[c - 1:c, :], 0.0)
        w_c_row = jnp.exp(last + st["m"] - m_new)
        w_l_row = w_intra[c - 1:c, :]
        ob = [gob_s[i * GDN_HEADS + h] for h in hs]
        blk = [ml_s[i * ML_HEADS + h] for h in hs]
        new = dict(
            s=[_col(e_last, L_GA + h) * st["s"][h] - r[h][c:, :] + ob[h][c:, :] for h in hs],
            c=[_col(w_c_row, L_MF + h) * st["c"][h] + _col(w_l_row, L_MF + h) * blk[h][c:c + ML_DQK, :]
               for h in hs],
            h=[_col(e_last, L_DT + h) * st["h"][h] + sdh_s[i * SSD_HEADS + h] for h in hs],
            m=m_new)
        return new, dict(i=i, r=r, full=full, yh=yh, ob=ob, blk=blk, inter=inter, m_t=m_t, w_intra=w_intra)

    def outputs(ctx):
        i = ctx["i"]
        r0 = _row0(i, c)
        rows = pl.ds(r0, c)
        prow = pl.ds(r0 + SUBLANES, c)
        w_inter = jnp.exp(ctx["inter"] - ctx["m_t"])
        e_neg_m = jnp.exp(-ctx["m_t"])
        for h in hs:
            gg = proj_scr[prow, C_GG + h * GDN_DV:C_GG + (h + 1) * GDN_DV]
            mix_scr[rows, h * GDN_DV:(h + 1) * GDN_DV] = (
                _rms(ctx["r"][h][:c, :] + ctx["ob"][h][:c, :], gdnn_ref[...]) * _silu(gg))
        h_parts = []
        for h in hs:
            fl = (ctx["full"][h] * _col(w_inter, L_MF + h)
                  + _col(ctx["w_intra"], L_MF + h) * ctx["blk"][h][0:c, :])
            den = jnp.maximum(jnp.abs(fl[:, ML_DV:ML_DV + 1]), _col(e_neg_m, L_MF + h))
            mo = proj_scr[prow, C_MO + h * ML_DV:C_MO + (h + 1) * ML_DV]
            h_parts.append(_rms(_sigmoid(mo) * (fl[:, :ML_DV] / den), mln_ref[...]))
        mix_scr[rows, GDN_HEADS * GDN_DV:GDN_HEADS * GDN_DV + ML_HEADS * ML_DV] = (
            jnp.concatenate(h_parts, axis=1))
        sz = proj_scr[prow, C_SZ:C_SZ + SSD_HEADS * SSD_P]
        y_all = (sy_s[i] + jnp.concatenate(ctx["yh"], axis=1)) * _silu(sz)
        mix_scr[rows, GDN_HEADS * GDN_DV + ML_HEADS * ML_DV:D_MODEL] = _rms(y_all, ssdn_ref[...])

    ctxs = []
    st = None
    for a in range(nb):
        i = j * nb + a
        seq = i if slot else 0
        if slot or a == 0:
            st = load(seq)
        st, ctx = advance(i, st)
        if slot or a == nb - 1:
            store(seq, st)
        ctxs.append(ctx)
    for ctx in ctxs:
        outputs(ctx)


def _ml_state_in(c_mat, n_row):
    eye = (lax.broadcasted_iota(jnp.int32, (ML_DQK, ML_DQK), 0)
           == lax.broadcasted_iota(jnp.int32, (ML_DQK, ML_DQK), 1)).astype(F32)
    n_col = jnp.sum(eye * n_row, axis=-1, keepdims=True)
    first = lax.broadcasted_iota(jnp.int32, (ML_DQK, 128 - ML_DV), 1) == 0
    return jnp.concatenate([c_mat, jnp.where(first, n_col, 0.0)], axis=1)


def _ml_state_out(c_aug):
    eye = (lax.broadcasted_iota(jnp.int32, (ML_DQK, ML_DQK), 0)
           == lax.broadcasted_iota(jnp.int32, (ML_DQK, ML_DQK), 1)).astype(F32)
    n_row = jnp.sum(eye * c_aug[:, ML_DV:ML_DV + 1], axis=0, keepdims=True)
    return c_aug[:, :ML_DV], n_row


def _mixer_kernel(*refs, tm, c, slot, nt, na, nbk):
    if slot:
        (x_ref, tail_ref, win_ref, wout_ref, cw_ref, cb_ref, gprm_ref, npre_ref, npost_ref, gdnn_ref,
         mln_ref, ssdn_ref, gdn_i, mc_i, mn_i, mm_i, ssd_i, _, _, _, _, _, _,
         o_ref, conv_o, gdn_o, mc_o, mn_o, mm_o, ssd_o, proj_scr, mix_scr, *ab) = refs
    else:
        (x_ref, win_ref, wout_ref, cw_ref, cb_ref, gprm_ref, npre_ref, npost_ref, gdnn_ref,
         mln_ref, ssdn_ref, _, _, _, _, _, _,
         o_ref, conv_o, gdn_o, mc_o, mn_o, mm_o, ssd_o, proj_scr, mix_scr, *ab) = refs
    names = ("cum_s", "col_s", "gqx_s", "gob_s", "ml_s", "sy_s", "sce_s", "sdh_s", "mlc_s")
    ab = dict(zip(names, ab))
    mlc_s = ab["mlc_s"]
    nseq = gdn_o.shape[0]
    t = pl.program_id(1)

    if slot:
        proj_scr[0:SUBLANES, :] = jnp.zeros((SUBLANES, N_IN), F32)
        gdn_o[...] = gdn_i[...]
        mm_o[...] = mm_i[...]
        ssd_o[...] = ssd_i[...]

        def load_ml(s, carry):
            for h in range(ML_HEADS):
                mlc_s[s * ML_HEADS + h] = _ml_state_in(mc_i[s, h], mn_i[s, h:h + 1, :])
            return carry

        lax.fori_loop(0, nseq, load_ml, 0)
    else:
        @pl.when(t == 0)
        def _():
            proj_scr[0:SUBLANES, :] = jnp.zeros((SUBLANES, N_IN), F32)
            gdn_o[...] = jnp.zeros(gdn_o.shape, F32)
            mlc_s[...] = jnp.zeros(mlc_s.shape, F32)
            mm_o[...] = jnp.zeros(mm_o.shape, F32)
            ssd_o[...] = jnp.zeros(ssd_o.shape, F32)

    x = x_ref[...]
    hn = _rms(x, npre_ref[...]).astype(BF16)

    def in_proj(r_lo, r_hi, nb):
        cols = slice(nb * IN_NBLK, (nb + 1) * IN_NBLK)
        proj_scr[SUBLANES + r_lo:SUBLANES + r_hi, cols] = _dot(hn[r_lo:r_hi, :], win_ref[:, cols])

    def out_proj(r_lo, r_hi):
        out = _dot(mix_scr[r_lo:r_hi, :].astype(BF16), wout_ref[...])
        o_ref[r_lo:r_hi, :] = x[r_lo:r_hi, :] + _rms(out, npost_ref[...])

    def weave(major, minor):
        major, minor = list(major), list(minor)
        stride = max(1, len(major) // max(1, len(minor)))
        while major or minor:
            for step in major[:stride]:
                step()
            major = major[stride:]
            if minor:
                minor.pop(0)()

    n_in_blk = N_IN // IN_NBLK
    if slot:
        for nb in range(n_in_blk):
            in_proj(0, tm, nb)
    if slot:
        for s in range(nseq):
            base = SUBLANES + s * SLOT
            proj_scr[base:base + SLOT_FIRST, 0:CONV_DIM] = tail_ref[s]
            conv_o[s] = proj_scr[base + SLOT_LAST - 2:base + SLOT_LAST + 1, 0:CONV_DIM]

    pass_a = functools.partial(_pass_a, c=c, slot=slot, na=na, proj_scr=proj_scr, cw_ref=cw_ref,
                               cb_ref=cb_ref, gprm_ref=gprm_ref,
                               **{k: v for k, v in ab.items() if k != "mlc_s"})
    pass_b = functools.partial(_pass_b, c=c, slot=slot, nb=nbk, proj_scr=proj_scr, mix_scr=mix_scr,
                               gdnn_ref=gdnn_ref, mln_ref=mln_ref, ssdn_ref=ssdn_ref,
                               gdn_o=gdn_o, mm_o=mm_o, ssd_o=ssd_o, **ab)

    def body_a(j, carry):
        pass_a(j)
        return carry

    def body_b(i, carry):
        pass_b(i)
        return carry

    n_a, n_b = tm // (c * na), tm // (c * nbk)
    if slot:
        lax.fori_loop(0, n_a, body_a, 0)
        lax.fori_loop(0, n_b, body_b, 0)
        out_proj(0, tm)
    else:
        assert n_a == 2 and nbk == 1
        half, b_half = tm // 2, n_b // 2
        for nb in range(n_in_blk):
            in_proj(0, half, nb)
        weave(pass_a(0, defer=True), [functools.partial(in_proj, half, tm, nb) for nb in range(n_in_blk)])
        weave(pass_a(1, defer=True), [functools.partial(pass_b, k) for k in range(b_half)])
        weave([functools.partial(pass_b, b_half + k) for k in range(b_half)],
              [functools.partial(out_proj, 0, half)])
        out_proj(half, tm)

    def store_ml(s, carry):
        for h in range(ML_HEADS):
            c_mat, n_row = _ml_state_out(mlc_s[s * ML_HEADS + h])
            mc_o[s, h] = c_mat
            mn_o[s, h:h + 1, :] = n_row
        return carry

    if slot:
        lax.fori_loop(0, nseq, store_ml, 0)
    else:
        @pl.when(t == nt - 1)
        def _():
            store_ml(0, 0)

    if not slot:
        last_rows = proj_scr[tm:tm + SUBLANES, 0:CONV_DIM]
        proj_scr[0:SUBLANES, 0:CONV_DIM] = last_rows

        @pl.when(t == nt - 1)
        def _():
            conv_o[0] = last_rows[SUBLANES - 3:, :]


def _ffn_kernel(*refs, tm, slot, nt):
    if slot:
        (x_ref, ftail_ref, wup_ref, wdn_ref, fw_ref, fb_ref, npre_ref, npost_ref, _,
         o_ref, gate_o, tails_scr, act_scr) = refs
    else:
        (x_ref, wup_ref, wdn_ref, fw_ref, fb_ref, npre_ref, npost_ref, _,
         o_ref, gate_o, tails_scr, act_scr) = refs
    t = pl.program_id(1)

    if slot:
        tails_scr[...] = jnp.zeros(tails_scr.shape, F32)
    else:
        @pl.when(t == 0)
        def _():
            tails_scr[...] = jnp.zeros(tails_scr.shape, F32)

    x = x_ref[...]
    hn = _rms(x, npre_ref[...]).astype(BF16)
    nblk = D_FF // FF_BLK

    def up(blk):
        return (_dot(hn, wup_ref[:, blk * FF_BLK:(blk + 1) * FF_BLK]),
                _dot(hn, wup_ref[:, D_FF + blk * FF_BLK:D_FF + (blk + 1) * FF_BLK]))

    ahead = up(0)
    for blk in range(nblk):
        cols = slice(blk * FF_BLK, (blk + 1) * FF_BLK)
        gate, val = ahead
        if blk + 1 < nblk:
            ahead = up(blk + 1)
        if slot:
            g3 = gate.reshape(tm // SLOT, SLOT, FF_BLK)
            ft = ftail_ref[:, :, cols]
            rr = lax.broadcasted_iota(jnp.int32, g3.shape, 1)
            g3 = jnp.where(rr == SLOT_FIRST - 2, ft[:, 0:1, :], jnp.where(rr == SLOT_FIRST - 1, ft[:, 1:2, :], g3))
            gate_o[:, :, cols] = g3[:, SLOT_LAST - 1:SLOT_LAST + 1, :]
            gate = g3.reshape(tm, FF_BLK)
        full = jnp.concatenate([tails_scr[:, cols], gate], axis=0)
        conv = fb_ref[0:1, cols] + fw_ref[2:3, cols] * gate
        for j in range(2):
            conv = conv + fw_ref[j:j + 1, cols] * pltpu.roll(full, 2 - j, 0)[SUBLANES:, :]
        last_rows = full[tm:tm + SUBLANES, :]
        tails_scr[:, cols] = last_rows
        act_scr[:, cols] = (_gelu_tanh(conv) * val).astype(BF16)
    o_ref[...] = x + _rms(_dot(act_scr[...], wdn_ref[...]), npost_ref[...])
    if not slot:
        @pl.when(t == nt - 1)
        def _():
            gate_o[0] = tails_scr[SUBLANES - 2:, :]


def _const_spec(shape, layer):
    nd = len(shape)
    return pl.BlockSpec((None,) + tuple(shape), lambda b, t: (layer,) + (0,) * nd,
                        pipeline_mode=pl.Buffered(1))


def _state_dims():
    return ((3, CONV_DIM), (GDN_HEADS, GDN_DK, GDN_DV), (ML_HEADS, ML_DQK, ML_DV), (ML_HEADS, ML_DQK),
            (SUBLANES, 128), (SSD_HEADS, SSD_P, SSD_N))


def _mixer_call(x, layer, prm, bufs, *, slot, states=None, tail=None):
    rows = x.shape[0]
    if slot:
        tm, c, nt, na, nbk = SAMPLE_TM, SLOT, 1, SAMPLE_NA, SAMPLE_NB
        nseq = tm // SLOT
    else:
        tm, c, na, nbk = PROMPT_TM, PROMPT_CHUNK, PROMPT_NA, PROMPT_NB
        nt = 2048 // tm
        nseq = 1
    nch = tm // c
    ngrp = rows // (tm * nt)
    nb = ngrp * nseq
    row_spec = lambda w: pl.BlockSpec((tm, w), lambda b, t: (b * nt + t, 0))
    st_in = lambda *dims: pl.BlockSpec((None, nseq) + dims, lambda b, t: (layer, b) + (0,) * len(dims))

    in_specs = [row_spec(D_MODEL)]
    args = [x]
    if slot:
        in_specs.append(st_in(SLOT_FIRST, CONV_DIM))
        args.append(tail)
    in_specs += [
        _const_spec((D_MODEL, N_IN), layer), _const_spec((D_MODEL, D_MODEL), layer),
        _const_spec((4, CONV_DIM), layer), _const_spec((1, CONV_DIM), layer),
        _const_spec((8, 128), layer), _const_spec((1, D_MODEL), layer), _const_spec((1, D_MODEL), layer),
        _const_spec((1, GDN_DV), layer), _const_spec((1, ML_DV), layer),
        _const_spec((1, SSD_HEADS * SSD_P), layer)]
    args += [prm["w_in"], prm["w_out"], prm["conv_w"], prm["conv_b"], prm["gprm"], prm["norm_mix_pre"],
             prm["norm_mix_post"], prm["gdn_norm"], prm["mlstm_norm"], prm["ssd_norm"]]
    if slot:
        in_specs += [st_in(GDN_HEADS, GDN_DK, GDN_DV), st_in(ML_HEADS, ML_DQK, ML_DV),
                     st_in(ML_HEADS, ML_DQK), st_in(SUBLANES, 128), st_in(SSD_HEADS, SSD_P, SSD_N)]
        args += list(states)
    aliases = {len(args) + k: 1 + k for k in range(len(bufs))}
    in_specs += [pl.BlockSpec(memory_space=pl.ANY)] * len(bufs)
    args += list(bufs)

    out_specs = [row_spec(D_MODEL)] + [st_in(*dims) for dims in _state_dims()]
    out_shape = [jax.ShapeDtypeStruct((rows, D_MODEL), F32)] + [
        jax.ShapeDtypeStruct((DEPTH, nb) + dims, F32) for dims in _state_dims()]
    scratch = [
        pltpu.VMEM((tm + SUBLANES, N_IN), F32),
        pltpu.VMEM((tm, D_MODEL), F32),
        pltpu.VMEM((nch, c, 128), F32),
        pltpu.VMEM((nch, c, 128), F32),
        pltpu.VMEM((nch * GDN_HEADS, c + GDN_DK, GDN_DV), F32),
        pltpu.VMEM((nch * GDN_HEADS, c + GDN_DK, GDN_DV), F32),
        pltpu.VMEM((nch * ML_HEADS, c + ML_DQK, 128), F32),
        pltpu.VMEM((nch, c, SSD_HEADS * SSD_P), F32),
        pltpu.VMEM((nch * SSD_HEADS, c, SSD_N), F32),
        pltpu.VMEM((nch * SSD_HEADS, SSD_P, SSD_N), F32),
        pltpu.VMEM((nseq * ML_HEADS, ML_DQK, 128), F32),
    ]
    return pl.pallas_call(
        functools.partial(_mixer_kernel, tm=tm, c=c, slot=slot, nt=nt, na=na, nbk=nbk),
        grid=(ngrp, nt), in_specs=in_specs, out_specs=out_specs, out_shape=out_shape,
        scratch_shapes=scratch, input_output_aliases=aliases,
        compiler_params=pltpu.CompilerParams(dimension_semantics=("arbitrary", "arbitrary"),
                                             vmem_limit_bytes=VMEM_LIMIT),
        name=("mixer_sample" if slot else "mixer_prompt"),
    )(*args)


def _ffn_call(x, layer, prm, buf, *, slot, tail=None):
    rows = x.shape[0]
    tm = FFN_TM
    nt = 1 if slot else 2048 // tm
    ngrp = rows // (tm * nt)
    row_spec = lambda w: pl.BlockSpec((tm, w), lambda b, t: (b * nt + t, 0))
    in_specs = [row_spec(D_MODEL)]
    args = [x]
    nseq = tm // SLOT if slot else 1
    if slot:
        in_specs.append(pl.BlockSpec((None, nseq, 2, D_FF), lambda b, t: (layer, b, 0, 0)))
        args.append(tail)
    in_specs += [_const_spec((D_MODEL, 2 * D_FF), layer), _const_spec((D_FF, D_MODEL), layer),
                 _const_spec((3, D_FF), layer), _const_spec((1, D_FF), layer),
                 _const_spec((1, D_MODEL), layer), _const_spec((1, D_MODEL), layer)]
    args += [prm["ffn_w_up"], prm["ffn_w_down"], prm["ffn_conv_w"], prm["ffn_conv_b"],
             prm["norm_ffn_pre"], prm["norm_ffn_post"]]
    aliases = {len(args): 1}
    in_specs.append(pl.BlockSpec(memory_space=pl.ANY))
    args.append(buf)
    gate_spec = pl.BlockSpec((None, nseq, 2, D_FF), lambda b, t: (layer, b, 0, 0))
    gate_shape = (DEPTH, ngrp * nseq, 2, D_FF)
    return pl.pallas_call(
        functools.partial(_ffn_kernel, tm=tm, slot=slot, nt=nt),
        grid=(ngrp, nt), in_specs=in_specs, out_specs=[row_spec(D_MODEL), gate_spec],
        out_shape=[jax.ShapeDtypeStruct((rows, D_MODEL), F32), jax.ShapeDtypeStruct(gate_shape, F32)],
        scratch_shapes=[pltpu.VMEM((SUBLANES, D_FF), F32), pltpu.VMEM((tm, D_FF), BF16)],
        input_output_aliases=aliases,
        compiler_params=pltpu.CompilerParams(dimension_semantics=("arbitrary", "arbitrary"),
                                             vmem_limit_bytes=VMEM_LIMIT),
        name=("ffn_sample" if slot else "ffn_prompt"),
    )(*args)


def _prepare_params(norm_mix_pre, norm_mix_post, norm_ffn_pre, norm_ffn_post, w_in, conv_w, conv_b,
                    gdn_a_log, gdn_dt_bias, gdn_norm, mlstm_i_bias, mlstm_f_bias, mlstm_norm,
                    ssd_a_log, ssd_dt_bias, ssd_d, ssd_norm, w_out, ffn_w_up, ffn_conv_w, ffn_conv_b,
                    ffn_w_down):
    w_in_p = jnp.concatenate(
        [w_in[..., :2816], w_in[..., 2824:3848], w_in[..., 3856:4112], w_in[..., 2816:2824],
         w_in[..., 3848:3856], w_in[..., 4112:4116],
         jnp.zeros((DEPTH, D_MODEL, N_IN - 4116), w_in.dtype)], axis=-1).astype(BF16)
    z4 = jnp.zeros((DEPTH, 4), F32)
    pad = jnp.zeros((DEPTH, 128 - 20), F32)
    gprm = jnp.stack(
        [jnp.concatenate([gdn_dt_bias, z4, mlstm_i_bias, mlstm_f_bias, ssd_dt_bias, pad], axis=-1),
         jnp.concatenate([gdn_a_log, z4, z4, z4, ssd_a_log, pad], axis=-1),
         jnp.concatenate([z4, z4, z4, z4, ssd_d, pad], axis=-1)]
        + [jnp.zeros((DEPTH, 128), F32)] * 5, axis=1)
    row = lambda a: a[:, None, :]
    return dict(
        w_in=w_in_p, w_out=w_out.astype(BF16), conv_w=conv_w, conv_b=row(conv_b), gprm=gprm,
        norm_mix_pre=row(norm_mix_pre), norm_mix_post=row(norm_mix_post),
        norm_ffn_pre=row(norm_ffn_pre), norm_ffn_post=row(norm_ffn_post),
        gdn_norm=row(gdn_norm), mlstm_norm=row(mlstm_norm), ssd_norm=row(ssd_norm),
        ffn_w_up=ffn_w_up.astype(BF16), ffn_w_down=ffn_w_down.astype(BF16),
        ffn_conv_w=ffn_conv_w, ffn_conv_b=row(ffn_conv_b))


def kernel(x_prompt, x_sample, state_conv, state_gdn, state_mlstm_c, state_mlstm_n, state_mlstm_m, state_ssd, state_ffn_conv, norm_mix_pre, norm_mix_post, norm_ffn_pre, norm_ffn_post, w_in, conv_w, conv_b, gdn_a_log, gdn_dt_bias, gdn_norm, mlstm_i_bias, mlstm_f_bias, mlstm_norm, ssd_a_log, ssd_dt_bias, ssd_d, ssd_norm, w_out, ffn_w_up, ffn_conv_w, ffn_conv_b, ffn_w_down):
    prm = _prepare_params(norm_mix_pre, norm_mix_post, norm_ffn_pre, norm_ffn_post, w_in, conv_w, conv_b,
                          gdn_a_log, gdn_dt_bias, gdn_norm, mlstm_i_bias, mlstm_f_bias, mlstm_norm,
                          ssd_a_log, ssd_dt_bias, ssd_d, ssd_norm, w_out, ffn_w_up, ffn_conv_w,
                          ffn_conv_b, ffn_w_down)
    bp, lp, _ = x_prompt.shape
    bs, ls, _ = x_sample.shape

    def state_bufs(nseq):
        return ([jnp.zeros((DEPTH, nseq) + dims, F32) for dims in _state_dims()],
                jnp.zeros((DEPTH, nseq, 2, D_FF), F32))

    def assemble(bufs, gate):
        conv, gdn, mc, mn, mm, ssd = bufs
        return [conv, gdn, mc, mn, mm[:, :, 0, L_MF:L_MF + ML_HEADS], ssd, gate]

    x = x_prompt.reshape(bp * lp, D_MODEL)
    bufs, gate = state_bufs(bp)
    for layer in range(DEPTH):
        x, *bufs = _mixer_call(x, layer, prm, bufs, slot=False)
        x, gate = _ffn_call(x, layer, prm, gate, slot=False)
    p_out = assemble(bufs, gate)
    y_prompt = x.reshape(bp, lp, D_MODEL)

    x = jnp.pad(x_sample, ((0, 0), (SLOT_FIRST, SLOT - SLOT_FIRST - ls), (0, 0))).reshape(bs * SLOT, D_MODEL)
    mm_in = jnp.pad(state_mlstm_m[:, :, None, :],
                    ((0, 0), (0, 0), (0, SUBLANES - 1), (L_MF, 128 - L_MF - ML_HEADS)))
    bufs, gate = state_bufs(bs)
    for layer in range(DEPTH):
        x, *bufs = _mixer_call(x, layer, prm, bufs, slot=True, tail=state_conv,
                               states=(state_gdn, state_mlstm_c, state_mlstm_n, mm_in, state_ssd))
        x, gate = _ffn_call(x, layer, prm, gate, slot=True, tail=state_ffn_conv)
    s_out = assemble(bufs, gate)
    y_sample = x.reshape(bs, SLOT, D_MODEL)[:, SLOT_FIRST:SLOT_LAST + 1]
    return (y_prompt, y_sample, *p_out, *s_out)
```

```python
import functools

import jax
import jax.numpy as jnp
from jax import lax
from jax.experimental import pallas as pl
from jax.experimental.pallas import tpu as pltpu

F32 = jnp.float32
BF16 = jnp.bfloat16

D_MODEL = 1024
DEPTH = 2
GDN_HEADS, GDN_DK, GDN_DV = 4, 128, 128
ML_HEADS, ML_DQK, ML_DV = 4, 64, 64
SSD_HEADS, SSD_P, SSD_GROUPS, SSD_N = 4, 64, 2, 128
D_FF = 2816
EPS = 1e-6

CONV_DIM = 2304
C_GQ, C_GK, C_GV, C_SX, C_SB, C_SC = 0, 512, 1024, 1536, 1792, 2048
C_GG, C_MQ, C_MK, C_MV, C_MO, C_SZ, C_GATE = 2304, 2816, 3072, 3328, 3584, 3840, 4096
N_IN = 4224
IN_NBLK = 1408
L_GA, L_GB, L_MI, L_MF, L_DT = 0, 4, 8, 12, 16

SUBLANES = 8
SLOT = 8
SLOT_FIRST, SLOT_LAST = 3, 6
FF_BLK = 256
VMEM_LIMIT = 56 * 1024 * 1024

PROMPT_TM, PROMPT_CHUNK, PROMPT_NA, PROMPT_NB = 512, 64, 4, 1
SAMPLE_TM, SAMPLE_NA, SAMPLE_NB = 64, 4, 4
FFN_TM = 1024


def _dot(a, b):
    return jnp.dot(a, b, preferred_element_type=F32)


def _dot_nt(a, b):
    return lax.dot_general(a, b, (((1,), (1,)), ((), ())), preferred_element_type=F32)


def _dot_tn(a, b):
    return lax.dot_general(a, b, (((0,), (0,)), ((), ())), preferred_element_type=F32)


def _sigmoid(x):
    return 1.0 / (1.0 + jnp.exp(-x))


def _silu(x):
    return x * _sigmoid(x)


def _rms(x, w):
    return x * lax.rsqrt(jnp.mean(x * x, axis=-1, keepdims=True) + EPS) * w


def _l2n(x):
    return x * lax.rsqrt(jnp.sum(x * x, axis=-1, keepdims=True) + EPS)


def _gelu_tanh(x):
    return 0.5 * x * (1.0 + jnp.tanh(0.7978845608028654 * (x + 0.044715 * (x * x * x))))


def _col(a, l):
    return a[:, l:l + 1]


def _cumsum_rows(tril_b, x):
    hi = x.astype(BF16)
    r1 = x - hi.astype(F32)
    mid = r1.astype(BF16)
    lo = (r1 - mid.astype(F32)).astype(BF16)
    return _dot(tril_b, hi) + _dot(tril_b, mid) + _dot(tril_b, lo)


def _conv_silu(proj_scr, r0, c, col0, width, cw_ref, cb_ref):
    win = proj_scr[pl.ds(r0, c + SUBLANES), col0:col0 + width]
    acc = cw_ref[0:1, col0:col0 + width] * win
    for j in range(1, 4):
        acc = cw_ref[j:j + 1, col0:col0 + width] * win + pltpu.roll(acc, 1, 0)
    return _silu(acc[SUBLANES:, :] + cb_ref[0:1, col0:col0 + width])


def _gates(proj_scr, r0, c, slot, gprm_ref, tril_b):
    graw = proj_scr[pl.ds(r0 + SUBLANES, c), C_GATE:C_GATE + 128]
    lane = lax.broadcasted_iota(jnp.int32, (c, 128), 1)
    z = graw + gprm_ref[0:1, :]
    soft = jnp.log(1.0 + jnp.exp(-jnp.abs(z)))
    sp = jnp.maximum(z, 0.0) + soft
    log_sig = -(jnp.maximum(-z, 0.0) + soft)
    a_neg = -jnp.exp(gprm_ref[1:2, :])
    is_ga = lane < L_GB
    is_gb = (lane >= L_GB) & (lane < L_MI)
    is_mi = (lane >= L_MI) & (lane < L_MF)
    is_mf = (lane >= L_MF) & (lane < L_DT)
    is_dt = (lane >= L_DT) & (lane < L_DT + SSD_HEADS)
    cum_src = jnp.where(is_ga | is_dt, a_neg * sp, jnp.where(is_mf, log_sig, 0.0))
    elem = jnp.where(is_gb, _sigmoid(graw), jnp.where(is_mi, z, jnp.where(is_dt, sp, 0.0)))
    if slot:
        rr = lax.broadcasted_iota(jnp.int32, (c, 128), 0)
        valid = (rr >= SLOT_FIRST) & (rr <= SLOT_LAST)
        cum_src = jnp.where(valid, cum_src, 0.0)
        elem = jnp.where(valid, elem, jnp.where(is_mi, -jnp.inf, 0.0))
    cum = _cumsum_rows(tril_b, cum_src)
    if c == 128:
        return cum, elem, cum.T, elem.T
    parts = [cum, elem]
    if 2 * c < 128:
        parts.append(jnp.zeros((128 - 2 * c, 128), F32))
    zt = jnp.concatenate(parts, axis=0).T
    return cum, elem, zt[:, 0:c], zt[:, c:2 * c]


def _row0(i, c):
    return i * c if isinstance(i, int) else pl.multiple_of(i * c, c)


def _pass_a(j, *, c, slot, na, proj_scr, cw_ref, cb_ref, gprm_ref,
            cum_s, col_s, gqx_s, gob_s, ml_s, sy_s, sce_s, sdh_s, defer=False):
    chunks = [j * na + a for a in range(na)]
    r0s = [_row0(i, c) for i in chunks]
    conv = functools.partial(_conv_silu, proj_scr, c=c, cw_ref=cw_ref, cb_ref=cb_ref)

    ii = lax.broadcasted_iota(jnp.int32, (c, c), 0)
    jj = lax.broadcasted_iota(jnp.int32, (c, c), 1)
    tril = ii >= jj
    strict = ii > jj
    eye = (ii == jj).astype(F32)
    tril_b = tril.astype(BF16)
    lane = lax.broadcasted_iota(jnp.int32, (c, 128), 1)

    gates = [_gates(proj_scr, r0, c, slot, gprm_ref, tril_b) for r0 in r0s]
    cum = [g[0] for g in gates]
    elem = [g[1] for g in gates]
    cum_t = [g[2] for g in gates]
    elem_t = [g[3] for g in gates]
    for a in range(na):
        cum_s[chunks[a]] = cum[a]

    def row_cum(a, l):
        return cum_t[a][l:l + 1, :]

    def row_elem(a, l):
        return elem_t[a][l:l + 1, :]

    it = [(a, h) for a in range(na) for h in range(GDN_HEADS)]
    n = range(len(it))
    gd, ml, sd = {}, {}, {}

    def g_k():
        k = [_l2n(conv(r0=r0s[a], col0=C_GK + h * GDN_DK, width=GDN_DK)) for a, h in it]
        gam_c = [_col(cum[a], L_GA + h) for a, h in it]
        beta_c = [_col(elem[a], L_GB + h) for a, h in it]
        dmat = [jnp.exp(jnp.where(tril, gam_c[x] - row_cum(a, L_GA + h), -jnp.inf))
                for x, (a, h) in enumerate(it)]
        pk = [-jnp.where(strict, _dot_nt(k[x], k[x]) * dmat[x] * beta_c[x], 0.0) for x in n]
        gd.update(k=k, gam_c=gam_c, beta_c=beta_c, dmat=dmat, pk=pk, t_inv=[eye + pk[x] for x in n])

    def g_neumann():
        pk = [_dot(gd["pk"][x], gd["pk"][x]) for x in n]
        gd.update(pk=pk, t_inv=[gd["t_inv"][x] + _dot(gd["t_inv"][x], pk[x]) for x in n])

    def g_q():
        gd["q"] = [_l2n(conv(r0=r0s[a], col0=C_GQ + h * GDN_DK, width=GDN_DK)) * (GDN_DK ** -0.5)
                   for a, h in it]

    def g_v():
        v = [conv(r0=r0s[a], col0=C_GV + h * GDN_DV, width=GDN_DV) for a, h in it]
        eg = [jnp.exp(gd["gam_c"][x]) for x in n]
        gd.update(eg=eg, rhs=[jnp.concatenate([v[x] * gd["beta_c"][x],
                                               gd["k"][x] * (gd["beta_c"][x] * eg[x])], axis=1) for x in n])

    def g_uw():
        gd["uw"] = [_dot(gd["t_inv"][x], gd["rhs"][x]) for x in n]

    def g_qk():
        gd["qk"] = [_dot_nt(gd["q"][x], gd["k"][x]) * gd["dmat"][x] for x in n]
        gd["kd"] = [gd["k"][x] * jnp.exp(gd["gam_c"][x][c - 1:c, :] - gd["gam_c"][x]) for x in n]

    def g_out():
        quw = [_dot(gd["qk"][x], gd["uw"][x]) for x in n]
        kuw = [_dot_tn(gd["kd"][x], gd["uw"][x]) for x in n]
        for x, (a, h) in enumerate(it):
            idx = chunks[a] * GDN_HEADS + h
            gqx_s[idx, 0:c, :] = gd["q"][x] * gd["eg"][x] - quw[x][:, GDN_DV:]
            gqx_s[idx, c:c + GDN_DK, :] = kuw[x][:, GDN_DV:]
            gob_s[idx, 0:c, :] = quw[x][:, :GDN_DV]
            gob_s[idx, c:c + GDN_DK, :] = kuw[x][:, :GDN_DV]

    def piece(a, base, h):
        return proj_scr[pl.ds(r0s[a] + SUBLANES, c), base + h * ML_DQK:base + (h + 1) * ML_DQK]

    def m_1():
        mq = [piece(a, C_MQ, h) for a, h in it]
        mk = [piece(a, C_MK, h) * (ML_DQK ** -0.5) for a, h in it]
        b_c = [_col(cum[a], L_MF + h) for a, h in it]
        d = [jnp.where(tril, b_c[x] - row_cum(a, L_MF + h) + row_elem(a, L_MI + h), -jnp.inf)
             for x, (a, h) in enumerate(it)]
        dmax = [jnp.max(d[x], axis=-1, keepdims=True) for x in n]
        dsafe = [jnp.where(dmax[x] == -jnp.inf, 0.0, dmax[x]) for x in n]
        s0 = [_dot_nt(mq[x], mk[x]) * jnp.exp(d[x] - dsafe[x]) for x in n]
        for a in range(na):
            cols = jnp.zeros((c, 128), F32)
            for h in range(ML_HEADS):
                cols = jnp.where(lane == L_MF + h, dmax[a * ML_HEADS + h], cols)
            col_s[chunks[a]] = cols
        ml.update(mk=mk, b_c=b_c, dsafe=dsafe, s0=s0)

    def m_2():
        mv = [piece(a, C_MV, h) for a, h in it]
        i_c = [_col(elem[a], L_MI + h) for a, h in it]
        ones_col = (lax.broadcasted_iota(jnp.int32, (c, 128 - ML_DV), 1) == 0).astype(F32)
        v_aug = [jnp.concatenate([mv[x], ones_col], axis=1) for x in n]
        num0 = [_dot(ml["s0"][x], v_aug[x]) for x in n]
        b_c, dsafe = ml["b_c"], ml["dsafe"]
        kw0 = [ml["mk"][x] * jnp.exp(b_c[x][c - 1:c, :] - b_c[x] + i_c[x] - dsafe[x][c - 1:c, :]) for x in n]
        kv0 = [_dot_tn(kw0[x], v_aug[x]) for x in n]
        for x, (a, h) in enumerate(it):
            idx = chunks[a] * ML_HEADS + h
            ml_s[idx, 0:c, :] = num0[x]
            ml_s[idx, c:c + ML_DQK, :] = kv0[x]

    rep = SSD_HEADS // SSD_GROUPS
    gi = [(a, g) for a in range(na) for g in range(SSD_GROUPS)]
    grp = [a * SSD_GROUPS + h // rep for a, h in it]

    def s_1():
        bg = [conv(r0=r0s[a], col0=C_SB + g * SSD_N, width=SSD_N) for a, g in gi]
        cg = [conv(r0=r0s[a], col0=C_SC + g * SSD_N, width=SSD_N) for a, g in gi]
        sd.update(bg=bg, cg=cg, cb_raw=[_dot_nt(cg[y], bg[y]) for y in range(len(gi))])

    def s_2():
        xg = [conv(r0=r0s[a], col0=C_SX + g * rep * SSD_P, width=rep * SSD_P) for a, g in gi]
        xs = [xg[grp[x]][:, (h % rep) * SSD_P:(h % rep + 1) * SSD_P] for x, (a, h) in enumerate(it)]
        sg_c = [_col(cum[a], L_DT + h) for a, h in it]
        dt_c = [_col(elem[a], L_DT + h) for a, h in it]
        cb = [sd["cb_raw"][grp[x]] * jnp.exp(jnp.where(tril, sg_c[x] - row_cum(a, L_DT + h), -jnp.inf))
              * row_elem(a, L_DT + h) for x, (a, h) in enumerate(it)]
        y0 = [_dot(cb[x], xs[x]) + gprm_ref[2:3, L_DT + h:L_DT + h + 1] * xs[x]
              for x, (a, h) in enumerate(it)]
        dh = [_dot_tn(xs[x] * (jnp.exp(sg_c[x][c - 1:c, :] - sg_c[x]) * dt_c[x]), sd["bg"][grp[x]]) for x in n]
        for x, (a, h) in enumerate(it):
            idx = chunks[a] * SSD_HEADS + h
            sce_s[idx] = sd["cg"][grp[x]] * jnp.exp(sg_c[x])
            sdh_s[idx] = dh[x]
        for a in range(na):
            sy_s[chunks[a]] = jnp.concatenate(y0[a * SSD_HEADS:(a + 1) * SSD_HEADS], axis=1)

    chain = [g_k] + [g_neumann] * (c.bit_length() - 2) + [g_uw]
    fill = [g_q, g_v, m_1, m_2, s_1, s_2]
    order = []
    for pos, link in enumerate(chain):
        order.append(link)
        if pos < len(fill):
            order.append(fill[pos])
    order += fill[len(chain):] + [g_qk, g_out]
    if defer:
        return order
    for stage in order:
        stage()


def _pass_b(j, *, c, slot, nb, proj_scr, mix_scr, gdnn_ref, mln_ref, ssdn_ref,
            cum_s, col_s, gqx_s, gob_s, ml_s, sy_s, sce_s, sdh_s, mlc_s, gdn_o, mm_o, ssd_o):
    hs = range(GDN_HEADS)
    lane_row = lax.broadcasted_iota(jnp.int32, (1, 128), 1)
    is_m = (lane_row >= L_MF) & (lane_row < L_MF + ML_HEADS)

    def load(seq):
        return dict(s=[gdn_o[seq, h] for h in hs], c=[mlc_s[seq * ML_HEADS + h] for h in hs],
                    h=[ssd_o[seq, h] for h in hs], m=mm_o[seq, 0:1, :])

    def store(seq, st):
        for h in hs:
            gdn_o[seq, h] = st["s"][h]
            mlc_s[seq * ML_HEADS + h] = st["c"][h]
            ssd_o[seq, h] = st["h"][h]
        mm_o[seq, 0:1, :] = st["m"]

    def advance(i, st):
        prow = pl.ds(_row0(i, c) + SUBLANES, c)
        cum = cum_s[i]
        dmx = col_s[i]
        last = cum[c - 1:c, :]
        e_last = jnp.exp(last)
        mq = [proj_scr[prow, C_MQ + h * ML_DQK:C_MQ + (h + 1) * ML_DQK] for h in hs]
        r = [_dot(gqx_s[i * GDN_HEADS + h], st["s"][h]) for h in hs]
        full = [_dot(mq[h], st["c"][h]) for h in hs]
        yh = [_dot_nt(sce_s[i * SSD_HEADS + h], st["h"][h]) for h in hs]
        inter = cum + st["m"]
        m_t = jnp.maximum(inter, dmx)
        w_intra = jnp.exp(dmx - m_t)
        m_new = jnp.where(is_m, m_t[c - 1:c, :], 0.0)
        w_c_row = jnp.exp(last + st["m"] - m_new)
        w_l_row = w_intra[c - 1:c, :]
        ob = [gob_s[i * GDN_HEADS + h] for h in hs]
        blk = [ml_s[i * ML_HEADS + h] for h in hs]
        new = dict(
            s=[_col(e_last, L_GA + h) * st["s"][h] - r[h][c:, :] + ob[h][c:, :] for h in hs],
            c=[_col(w_c_row, L_MF + h) * st["c"][h] + _col(w_l_row, L_MF + h) * blk[h][c:c + ML_DQK, :]
               for h in hs],
            h=[_col(e_last, L_DT + h) * st["h"][h] + sdh_s[i * SSD_HEADS + h] for h in hs],
            m=m_new)
        return new, dict(i=i, r=r, full=full, yh=yh, ob=ob, blk=blk, inter=inter, m_t=m_t, w_intra=w_intra)

    def outputs(ctx):
        i = ctx["i"]
        r0 = _row0(i, c)
        rows = pl.ds(r0, c)
        prow = pl.ds(r0 + SUBLANES, c)
        w_inter = jnp.exp(ctx["inter"] - ctx["m_t"])
        e_neg_m = jnp.exp(-ctx["m_t"])
        for h in hs:
            gg = proj_scr[prow, C_GG + h * GDN_DV:C_GG + (h + 1) * GDN_DV]
            mix_scr[rows, h * GDN_DV:(h + 1) * GDN_DV] = (
                _rms(ctx["r"][h][:c, :] + ctx["ob"][h][:c, :], gdnn_ref[...]) * _silu(gg))
        h_parts = []
        for h in hs:
            fl = (ctx["full"][h] * _col(w_inter, L_MF + h)
                  + _col(ctx["w_intra"], L_MF + h) * ctx["blk"][h][0:c, :])
            den = jnp.maximum(jnp.abs(fl[:, ML_DV:ML_DV + 1]), _col(e_neg_m, L_MF + h))
            mo = proj_scr[prow, C_MO + h * ML_DV:C_MO + (h + 1) * ML_DV]
            h_parts.append(_rms(_sigmoid(mo) * (fl[:, :ML_DV] / den), mln_ref[...]))
        mix_scr[rows, GDN_HEADS * GDN_DV:GDN_HEADS * GDN_DV + ML_HEADS * ML_DV] = (
            jnp.concatenate(h_parts, axis=1))
        sz = proj_scr[prow, C_SZ:C_SZ + SSD_HEADS * SSD_P]
        y_all = (sy_s[i] + jnp.concatenate(ctx["yh"], axis=1)) * _silu(sz)
        mix_scr[rows, GDN_HEADS * GDN_DV + ML_HEADS * ML_DV:D_MODEL] = _rms(y_all, ssdn_ref[...])

    ctxs = []
    st = None
    for a in range(nb):
        i = j * nb + a
        seq = i if slot else 0
        if slot or a == 0:
            st = load(seq)
        st, ctx = advance(i, st)
        if slot or a == nb - 1:
            store(seq, st)
        ctxs.append(ctx)
    for ctx in ctxs:
        outputs(ctx)


def _ml_state_in(c_mat, n_row):
    eye = (lax.broadcasted_iota(jnp.int32, (ML_DQK, ML_DQK), 0)
           == lax.broadcasted_iota(jnp.int32, (ML_DQK, ML_DQK), 1)).astype(F32)
    n_col = jnp.sum(eye * n_row, axis=-1, keepdims=True)
    first = lax.broadcasted_iota(jnp.int32, (ML_DQK, 128 - ML_DV), 1) == 0
    return jnp.concatenate([c_mat, jnp.where(first, n_col, 0.0)], axis=1)


def _ml_state_out(c_aug):
    eye = (lax.broadcasted_iota(jnp.int32, (ML_DQK, ML_DQK), 0)
           == lax.broadcasted_iota(jnp.int32, (ML_DQK, ML_DQK), 1)).astype(F32)
    n_row = jnp.sum(eye * c_aug[:, ML_DV:ML_DV + 1], axis=0, keepdims=True)
    return c_aug[:, :ML_DV], n_row


def _mixer_kernel(*refs, tm, c, slot, nt, na, nbk):
    if slot:
        (x_ref, tail_ref, win_ref, wout_ref, cw_ref, cb_ref, gprm_ref, npre_ref, npost_ref, gdnn_ref,
         mln_ref, ssdn_ref, gdn_i, mc_i, mn_i, mm_i, ssd_i, _, _, _, _, _, _,
         o_ref, conv_o, gdn_o, mc_o, mn_o, mm_o, ssd_o, proj_scr, mix_scr, *ab) = refs
    else:
        (x_ref, win_ref, wout_ref, cw_ref, cb_ref, gprm_ref, npre_ref, npost_ref, gdnn_ref,
         mln_ref, ssdn_ref, _, _, _, _, _, _,
         o_ref, conv_o, gdn_o, mc_o, mn_o, mm_o, ssd_o, proj_scr, mix_scr, *ab) = refs
    names = ("cum_s", "col_s", "gqx_s", "gob_s", "ml_s", "sy_s", "sce_s", "sdh_s", "mlc_s")
    ab = dict(zip(names, ab))
    mlc_s = ab["mlc_s"]
    nseq = gdn_o.shape[0]
    t = pl.program_id(1)

    if slot:
        proj_scr[0:SUBLANES, :] = jnp.zeros((SUBLANES, N_IN), F32)
        gdn_o[...] = gdn_i[...]
        mm_o[...] = mm_i[...]
        ssd_o[...] = ssd_i[...]

        def load_ml(s, carry):
            for h in range(ML_HEADS):
                mlc_s[s * ML_HEADS + h] = _ml_state_in(mc_i[s, h], mn_i[s, h:h + 1, :])
            return carry

        lax.fori_loop(0, nseq, load_ml, 0)
    else:
        @pl.when(t == 0)
        def _():
            proj_scr[0:SUBLANES, :] = jnp.zeros((SUBLANES, N_IN), F32)
            gdn_o[...] = jnp.zeros(gdn_o.shape, F32)
            mlc_s[...] = jnp.zeros(mlc_s.shape, F32)
            mm_o[...] = jnp.zeros(mm_o.shape, F32)
            ssd_o[...] = jnp.zeros(ssd_o.shape, F32)

    x = x_ref[...]
    hn = _rms(x, npre_ref[...]).astype(BF16)

    def in_proj(r_lo, r_hi, nb):
        cols = slice(nb * IN_NBLK, (nb + 1) * IN_NBLK)
        proj_scr[SUBLANES + r_lo:SUBLANES + r_hi, cols] = _dot(hn[r_lo:r_hi, :], win_ref[:, cols])

    def out_proj(r_lo, r_hi):
        out = _dot(mix_scr[r_lo:r_hi, :].astype(BF16), wout_ref[...])
        o_ref[r_lo:r_hi, :] = x[r_lo:r_hi, :] + _rms(out, npost_ref[...])

    def weave(major, minor):
        major, minor = list(major), list(minor)
        stride = max(1, len(major) // max(1, len(minor)))
        while major or minor:
            for step in major[:stride]:
                step()
            major = major[stride:]
            if minor:
                minor.pop(0)()

    n_in_blk = N_IN // IN_NBLK
    if slot:
        for nb in range(n_in_blk):
            in_proj(0, tm, nb)
        for s in range(nseq):
            base = SUBLANES + s * SLOT
            proj_scr[base:base + SLOT_FIRST, 0:CONV_DIM] = tail_ref[s]
            conv_o[s] = proj_scr[base + SLOT_LAST - 2:base + SLOT_LAST + 1, 0:CONV_DIM]

    pass_a = functools.partial(_pass_a, c=c, slot=slot, na=na, proj_scr=proj_scr, cw_ref=cw_ref,
                               cb_ref=cb_ref, gprm_ref=gprm_ref,
                               **{k: v for k, v in ab.items() if k != "mlc_s"})
    pass_b = functools.partial(_pass_b, c=c, slot=slot, nb=nbk, proj_scr=proj_scr, mix_scr=mix_scr,
                               gdnn_ref=gdnn_ref, mln_ref=mln_ref, ssdn_ref=ssdn_ref,
                               gdn_o=gdn_o, mm_o=mm_o, ssd_o=ssd_o, **ab)

    n_a, n_b = tm // (c * na), tm // (c * nbk)
    if slot:
        def body_a(j, carry):
            pass_a(j)
            return carry

        def body_b(j, carry):
            pass_b(j)
            return carry

        lax.fori_loop(0, n_a, body_a, 0)
        lax.fori_loop(0, n_b, body_b, 0)
        out_proj(0, tm)
    else:
        assert n_a == 2 and n_b % 2 == 0
        half, b_half = tm // 2, n_b // 2
        for nb in range(n_in_blk):
            in_proj(0, half, nb)
        weave(pass_a(0, defer=True), [functools.partial(in_proj, half, tm, nb) for nb in range(n_in_blk)])
        weave(pass_a(1, defer=True), [functools.partial(pass_b, k) for k in range(b_half)])
        weave([functools.partial(pass_b, b_half + k) for k in range(b_half)],
              [functools.partial(out_proj, 0, half)])
        out_proj(half, tm)

    def store_ml(s, carry):
        for h in range(ML_HEADS):
            c_mat, n_row = _ml_state_out(mlc_s[s * ML_HEADS + h])
            mc_o[s, h] = c_mat
            mn_o[s, h:h + 1, :] = n_row
        return carry

    if slot:
        lax.fori_loop(0, nseq, store_ml, 0)
    else:
        last_rows = proj_scr[tm:tm + SUBLANES, 0:CONV_DIM]
        proj_scr[0:SUBLANES, 0:CONV_DIM] = last_rows

        @pl.when(t == nt - 1)
        def _():
            conv_o[0] = last_rows[SUBLANES - 3:, :]
            store_ml(0, 0)


def _ffn_kernel(*refs, tm, slot, nt):
    if slot:
        (x_ref, ftail_ref, wup_ref, wdn_ref, fw_ref, fb_ref, npre_ref, npost_ref, _,
         o_ref, gate_o, tails_scr, act_scr) = refs
    else:
        (x_ref, wup_ref, wdn_ref, fw_ref, fb_ref, npre_ref, npost_ref, _,
         o_ref, gate_o, tails_scr, act_scr) = refs
    t = pl.program_id(1)

    if slot:
        tails_scr[...] = jnp.zeros(tails_scr.shape, F32)
    else:
        @pl.when(t == 0)
        def _():
            tails_scr[...] = jnp.zeros(tails_scr.shape, F32)

    x = x_ref[...]
    hn = _rms(x, npre_ref[...]).astype(BF16)
    nblk = D_FF // FF_BLK

    def up(blk):
        return (_dot(hn, wup_ref[:, blk * FF_BLK:(blk + 1) * FF_BLK]),
                _dot(hn, wup_ref[:, D_FF + blk * FF_BLK:D_FF + (blk + 1) * FF_BLK]))

    ahead = up(0)
    for blk in range(nblk):
        cols = slice(blk * FF_BLK, (blk + 1) * FF_BLK)
        gate, val = ahead
        if blk + 1 < nblk:
            ahead = up(blk + 1)
        if slot:
            g3 = gate.reshape(tm // SLOT, SLOT, FF_BLK)
            ft = ftail_ref[:, :, cols]
            rr = lax.broadcasted_iota(jnp.int32, g3.shape, 1)
            g3 = jnp.where(rr == SLOT_FIRST - 2, ft[:, 0:1, :], jnp.where(rr == SLOT_FIRST - 1, ft[:, 1:2, :], g3))
            gate_o[:, :, cols] = g3[:, SLOT_LAST - 1:SLOT_LAST + 1, :]
            gate = g3.reshape(tm, FF_BLK)
        full = jnp.concatenate([tails_scr[:, cols], gate], axis=0)
        conv = fb_ref[0:1, cols] + fw_ref[2:3, cols] * gate
        for j in range(2):
            conv = conv + fw_ref[j:j + 1, cols] * pltpu.roll(full, 2 - j, 0)[SUBLANES:, :]
        last_rows = full[tm:tm + SUBLANES, :]
        tails_scr[:, cols] = last_rows
        act_scr[:, cols] = (_gelu_tanh(conv) * val).astype(BF16)
    o_ref[...] = x + _rms(_dot(act_scr[...], wdn_ref[...]), npost_ref[...])
    if not slot:
        @pl.when(t == nt - 1)
        def _():
            gate_o[0] = tails_scr[SUBLANES - 2:, :]


def _const_spec(shape, layer):
    nd = len(shape)
    return pl.BlockSpec((None,) + tuple(shape), lambda b, t: (layer,) + (0,) * nd,
                        pipeline_mode=pl.Buffered(1))


def _state_dims():
    return ((3, CONV_DIM), (GDN_HEADS, GDN_DK, GDN_DV), (ML_HEADS, ML_DQK, ML_DV), (ML_HEADS, ML_DQK),
            (SUBLANES, 128), (SSD_HEADS, SSD_P, SSD_N))


def _mixer_call(x, layer, prm, bufs, *, slot, states=None, tail=None):
    rows = x.shape[0]
    if slot:
        tm, c, nt, na, nbk = SAMPLE_TM, SLOT, 1, SAMPLE_NA, SAMPLE_NB
        nseq = tm // SLOT
    else:
        tm, c, na, nbk = PROMPT_TM, PROMPT_CHUNK, PROMPT_NA, PROMPT_NB
        nt = 2048 // tm
        nseq = 1
    nch = tm // c
    ngrp = rows // (tm * nt)
    nb = ngrp * nseq
    row_spec = lambda w: pl.BlockSpec((tm, w), lambda b, t: (b * nt + t, 0))
    st_in = lambda *dims: pl.BlockSpec((None, nseq) + dims, lambda b, t: (layer, b) + (0,) * len(dims))

    in_specs = [row_spec(D_MODEL)]
    args = [x]
    if slot:
        in_specs.append(st_in(SLOT_FIRST, CONV_DIM))
        args.append(tail)
    in_specs += [
        _const_spec((D_MODEL, N_IN), layer), _const_spec((D_MODEL, D_MODEL), layer),
        _const_spec((4, CONV_DIM), layer), _const_spec((1, CONV_DIM), layer),
        _const_spec((8, 128), layer), _const_spec((1, D_MODEL), layer), _const_spec((1, D_MODEL), layer),
        _const_spec((1, GDN_DV), layer), _const_spec((1, ML_DV), layer),
        _const_spec((1, SSD_HEADS * SSD_P), layer)]
    args += [prm["w_in"], prm["w_out"], prm["conv_w"], prm["conv_b"], prm["gprm"], prm["norm_mix_pre"],
             prm["norm_mix_post"], prm["gdn_norm"], prm["mlstm_norm"], prm["ssd_norm"]]
    if slot:
        in_specs += [st_in(GDN_HEADS, GDN_DK, GDN_DV), st_in(ML_HEADS, ML_DQK, ML_DV),
                     st_in(ML_HEADS, ML_DQK), st_in(SUBLANES, 128), st_in(SSD_HEADS, SSD_P, SSD_N)]
        args += list(states)
    aliases = {len(args) + k: 1 + k for k in range(len(bufs))}
    in_specs += [pl.BlockSpec(memory_space=pl.ANY)] * len(bufs)
    args += list(bufs)

    out_specs = [row_spec(D_MODEL)] + [st_in(*dims) for dims in _state_dims()]
    out_shape = [jax.ShapeDtypeStruct((rows, D_MODEL), F32)] + [
        jax.ShapeDtypeStruct((DEPTH, nb) + dims, F32) for dims in _state_dims()]
    scratch = [
        pltpu.VMEM((tm + SUBLANES, N_IN), F32),
        pltpu.VMEM((tm, D_MODEL), F32),
        pltpu.VMEM((nch, c, 128), F32),
        pltpu.VMEM((nch, c, 128), F32),
        pltpu.VMEM((nch * GDN_HEADS, c + GDN_DK, GDN_DV), F32),
        pltpu.VMEM((nch * GDN_HEADS, c + GDN_DK, GDN_DV), F32),
        pltpu.VMEM((nch * ML_HEADS, c + ML_DQK, 128), F32),
        pltpu.VMEM((nch, c, SSD_HEADS * SSD_P), F32),
        pltpu.VMEM((nch * SSD_HEADS, c, SSD_N), F32),
        pltpu.VMEM((nch * SSD_HEADS, SSD_P, SSD_N), F32),
        pltpu.VMEM((nseq * ML_HEADS, ML_DQK, 128), F32),
    ]
    return pl.pallas_call(
        functools.partial(_mixer_kernel, tm=tm, c=c, slot=slot, nt=nt, na=na, nbk=nbk),
        grid=(ngrp, nt), in_specs=in_specs, out_specs=out_specs, out_shape=out_shape,
        scratch_shapes=scratch, input_output_aliases=aliases,
        compiler_params=pltpu.CompilerParams(dimension_semantics=("arbitrary", "arbitrary"),
                                             vmem_limit_bytes=VMEM_LIMIT),
        name=("mixer_sample" if slot else "mixer_prompt"),
    )(*args)


def _ffn_call(x, layer, prm, buf, *, slot, tail=None):
    rows = x.shape[0]
    tm = FFN_TM
    nt = 1 if slot else 2048 // tm
    ngrp = rows // (tm * nt)
    row_spec = lambda w: pl.BlockSpec((tm, w), lambda b, t: (b * nt + t, 0))
    in_specs = [row_spec(D_MODEL)]
    args = [x]
    nseq = tm // SLOT if slot else 1
    if slot:
        in_specs.append(pl.BlockSpec((None, nseq, 2, D_FF), lambda b, t: (layer, b, 0, 0)))
        args.append(tail)
    in_specs += [_const_spec((D_MODEL, 2 * D_FF), layer), _const_spec((D_FF, D_MODEL), layer),
                 _const_spec((3, D_FF), layer), _const_spec((1, D_FF), layer),
                 _const_spec((1, D_MODEL), layer), _const_spec((1, D_MODEL), layer)]
    args += [prm["ffn_w_up"], prm["ffn_w_down"], prm["ffn_conv_w"], prm["ffn_conv_b"],
             prm["norm_ffn_pre"], prm["norm_ffn_post"]]
    aliases = {len(args): 1}
    in_specs.append(pl.BlockSpec(memory_space=pl.ANY))
    args.append(buf)
    gate_spec = pl.BlockSpec((None, nseq, 2, D_FF), lambda b, t: (layer, b, 0, 0))
    gate_shape = (DEPTH, ngrp * nseq, 2, D_FF)
    return pl.pallas_call(
        functools.partial(_ffn_kernel, tm=tm, slot=slot, nt=nt),
        grid=(ngrp, nt), in_specs=in_specs, out_specs=[row_spec(D_MODEL), gate_spec],
        out_shape=[jax.ShapeDtypeStruct((rows, D_MODEL), F32), jax.ShapeDtypeStruct(gate_shape, F32)],
        scratch_shapes=[pltpu.VMEM((SUBLANES, D_FF), F32), pltpu.VMEM((tm, D_FF), BF16)],
        input_output_aliases=aliases,
        compiler_params=pltpu.CompilerParams(dimension_semantics=("arbitrary", "arbitrary"),
                                             vmem_limit_bytes=VMEM_LIMIT),
        name=("ffn_sample" if slot else "ffn_prompt"),
    )(*args)


W_IN_COLS = 4116
W_IN_ROWS_BLK = 256
W_IN_PIECES = ((0, 2816), (2824, 3848), (3856, 4112), (2816, 2824), (3848, 3856), (4112, 4116))


def _w_in_kernel(w_ref, o_ref):
    w = w_ref[...]
    parts = [w[:, lo:hi] for lo, hi in W_IN_PIECES]
    parts.append(jnp.zeros((w.shape[0], N_IN - W_IN_COLS), w.dtype))
    o_ref[...] = jnp.concatenate(parts, axis=1).astype(BF16)


def _regroup_w_in(w_in):
    return pl.pallas_call(
        _w_in_kernel, grid=(DEPTH, D_MODEL // W_IN_ROWS_BLK),
        in_specs=[pl.BlockSpec((None, W_IN_ROWS_BLK, W_IN_COLS), lambda l, r: (l, r, 0))],
        out_specs=pl.BlockSpec((None, W_IN_ROWS_BLK, N_IN), lambda l, r: (l, r, 0)),
        out_shape=jax.ShapeDtypeStruct((DEPTH, D_MODEL, N_IN), BF16),
        compiler_params=pltpu.CompilerParams(dimension_semantics=("arbitrary", "arbitrary")),
        name="regroup_w_in",
    )(w_in)


def _prepare_params(norm_mix_pre, norm_mix_post, norm_ffn_pre, norm_ffn_post, w_in, conv_w, conv_b,
                    gdn_a_log, gdn_dt_bias, gdn_norm, mlstm_i_bias, mlstm_f_bias, mlstm_norm,
                    ssd_a_log, ssd_dt_bias, ssd_d, ssd_norm, w_out, ffn_w_up, ffn_conv_w, ffn_conv_b,
                    ffn_w_down):
    w_in_p = _regroup_w_in(w_in)
    z4 = jnp.zeros((DEPTH, 4), F32)
    pad = jnp.zeros((DEPTH, 128 - 20), F32)
    gprm = jnp.stack(
        [jnp.concatenate([gdn_dt_bias, z4, mlstm_i_bias, mlstm_f_bias, ssd_dt_bias, pad], axis=-1),
         jnp.concatenate([gdn_a_log, z4, z4, z4, ssd_a_log, pad], axis=-1),
         jnp.concatenate([z4, z4, z4, z4, ssd_d, pad], axis=-1)]
        + [jnp.zeros((DEPTH, 128), F32)] * 5, axis=1)
    row = lambda a: a[:, None, :]
    return dict(
        w_in=w_in_p, w_out=w_out.astype(BF16), conv_w=conv_w, conv_b=row(conv_b), gprm=gprm,
        norm_mix_pre=row(norm_mix_pre), norm_mix_post=row(norm_mix_post),
        norm_ffn_pre=row(norm_ffn_pre), norm_ffn_post=row(norm_ffn_post),
        gdn_norm=row(gdn_norm), mlstm_norm=row(mlstm_norm), ssd_norm=row(ssd_norm),
        ffn_w_up=ffn_w_up.astype(BF16), ffn_w_down=ffn_w_down.astype(BF16),
        ffn_conv_w=ffn_conv_w, ffn_conv_b=row(ffn_conv_b))


def kernel(x_prompt, x_sample, state_conv, state_gdn, state_mlstm_c, state_mlstm_n, state_mlstm_m, state_ssd, state_ffn_conv, norm_mix_pre, norm_mix_post, norm_ffn_pre, norm_ffn_post, w_in, conv_w, conv_b, gdn_a_log, gdn_dt_bias, gdn_norm, mlstm_i_bias, mlstm_f_bias, mlstm_norm, ssd_a_log, ssd_dt_bias, ssd_d, ssd_norm, w_out, ffn_w_up, ffn_conv_w, ffn_conv_b, ffn_w_down):
    prm = _prepare_params(norm_mix_pre, norm_mix_post, norm_ffn_pre, norm_ffn_post, w_in, conv_w, conv_b,
                          gdn_a_log, gdn_dt_bias, gdn_norm, mlstm_i_bias, mlstm_f_bias, mlstm_norm,
                          ssd_a_log, ssd_dt_bias, ssd_d, ssd_norm, w_out, ffn_w_up, ffn_conv_w,
                          ffn_conv_b, ffn_w_down)
    bp, lp, _ = x_prompt.shape
    bs, ls, _ = x_sample.shape

    def state_bufs(nseq):
        return ([jnp.zeros((DEPTH, nseq) + dims, F32) for dims in _state_dims()],
                jnp.zeros((DEPTH, nseq, 2, D_FF), F32))

    def assemble(bufs, gate):
        conv, gdn, mc, mn, mm, ssd = bufs
        return [conv, gdn, mc, mn, mm[:, :, 0, L_MF:L_MF + ML_HEADS], ssd, gate]

    x = x_prompt.reshape(bp * lp, D_MODEL)
    bufs, gate = state_bufs(bp)
    for layer in range(DEPTH):
        x, *bufs = _mixer_call(x, layer, prm, bufs, slot=False)
        x, gate = _ffn_call(x, layer, prm, gate, slot=False)
    p_out = assemble(bufs, gate)
    y_prompt = x.reshape(bp, lp, D_MODEL)

    x = jnp.pad(x_sample, ((0, 0), (SLOT_FIRST, SLOT - SLOT_FIRST - ls), (0, 0))).reshape(bs * SLOT, D_MODEL)
    mm_in = jnp.pad(state_mlstm_m[:, :, None, :],
                    ((0, 0), (0, 0), (0, SUBLANES - 1), (L_MF, 128 - L_MF - ML_HEADS)))
    bufs, gate = state_bufs(bs)
    for layer in range(DEPTH):
        x, *bufs = _mixer_call(x, layer, prm, bufs, slot=True, tail=state_conv,
                               states=(state_gdn, state_mlstm_c, state_mlstm_n, mm_in, state_ssd))
        x, gate = _ffn_call(x, layer, prm, gate, slot=True, tail=state_ffn_conv)
    s_out = assemble(bufs, gate)
    y_sample = x.reshape(bs, SLOT, D_MODEL)[:, SLOT_FIRST:SLOT_LAST + 1]
    return (y_prompt, y_sample, *p_out, *s_out)
```

```python
import functools

import jax
import jax.numpy as jnp
from jax import lax
from jax.experimental import pallas as pl
from jax.experimental.pallas import tpu as pltpu

F32 = jnp.float32
BF16 = jnp.bfloat16

D_MODEL = 1024
DEPTH = 2
GDN_HEADS, GDN_DK, GDN_DV = 4, 128, 128
ML_HEADS, ML_DQK, ML_DV = 4, 64, 64
SSD_HEADS, SSD_P, SSD_GROUPS, SSD_N = 4, 64, 2, 128
D_FF = 2816
EPS = 1e-6

CONV_DIM = 2304
C_GQ, C_GK, C_GV, C_SX, C_SB, C_SC = 0, 512, 1024, 1536, 1792, 2048
C_GG, C_MQ, C_MK, C_MV, C_MO, C_SZ, C_GATE = 2304, 2816, 3072, 3328, 3584, 3840, 4096
N_IN = 4224
IN_NBLK = 1408
L_GA, L_GB, L_MI, L_MF, L_DT = 0, 4, 8, 12, 16

SUBLANES = 8
SLOT = 8
SLOT_FIRST, SLOT_LAST = 3, 6
FF_BLK = 256
VMEM_LIMIT = 56 * 1024 * 1024

PROMPT_TM, PROMPT_CHUNK, PROMPT_NA, PROMPT_NB = 512, 64, 4, 1
SAMPLE_TM, SAMPLE_NA, SAMPLE_NB = 64, 4, 4
FFN_TM = 1024


def _dot(a, b):
    return jnp.dot(a, b, preferred_element_type=F32)


def _dot_nt(a, b):
    return lax.dot_general(a, b, (((1,), (1,)), ((), ())), preferred_element_type=F32)


def _dot_tn(a, b):
    return lax.dot_general(a, b, (((0,), (0,)), ((), ())), preferred_element_type=F32)


def _sigmoid(x):
    return 1.0 / (1.0 + jnp.exp(-x))


def _silu(x):
    return x * _sigmoid(x)


def _rms(x, w):
    return x * lax.rsqrt(jnp.mean(x * x, axis=-1, keepdims=True) + EPS) * w


def _l2n(x):
    return x * lax.rsqrt(jnp.sum(x * x, axis=-1, keepdims=True) + EPS)


def _gelu_tanh(x):
    return 0.5 * x * (1.0 + jnp.tanh(0.7978845608028654 * (x + 0.044715 * (x * x * x))))


def _col(a, l):
    return a[:, l:l + 1]


def _cumsum_rows(tril_b, x):
    hi = x.astype(BF16)
    r1 = x - hi.astype(F32)
    mid = r1.astype(BF16)
    lo = (r1 - mid.astype(F32)).astype(BF16)
    return _dot(tril_b, hi) + _dot(tril_b, mid) + _dot(tril_b, lo)


def _conv_silu(proj_scr, r0, c, col0, width, cw_ref, cb_ref):
    win = proj_scr[pl.ds(r0, c + SUBLANES), col0:col0 + width]
    acc = cw_ref[0:1, col0:col0 + width] * win
    for j in range(1, 4):
        acc = cw_ref[j:j + 1, col0:col0 + width] * win + pltpu.roll(acc, 1, 0)
    return _silu(acc[SUBLANES:, :] + cb_ref[0:1, col0:col0 + width])


def _gates(proj_scr, r0, c, slot, gprm_ref, tril_b):
    graw = proj_scr[pl.ds(r0 + SUBLANES, c), C_GATE:C_GATE + 128]
    lane = lax.broadcasted_iota(jnp.int32, (c, 128), 1)
    z = graw + gprm_ref[0:1, :]
    soft = jnp.log(1.0 + jnp.exp(-jnp.abs(z)))
    sp = jnp.maximum(z, 0.0) + soft
    log_sig = -(jnp.maximum(-z, 0.0) + soft)
    a_neg = -jnp.exp(gprm_ref[1:2, :])
    is_ga = lane < L_GB
    is_gb = (lane >= L_GB) & (lane < L_MI)
    is_mi = (lane >= L_MI) & (lane < L_MF)
    is_mf = (lane >= L_MF) & (lane < L_DT)
    is_dt = (lane >= L_DT) & (lane < L_DT + SSD_HEADS)
    cum_src = jnp.where(is_ga | is_dt, a_neg * sp, jnp.where(is_mf, log_sig, 0.0))
    elem = jnp.where(is_gb, _sigmoid(graw), jnp.where(is_mi, z, jnp.where(is_dt, sp, 0.0)))
    if slot:
        rr = lax.broadcasted_iota(jnp.int32, (c, 128), 0)
        valid = (rr >= SLOT_FIRST) & (rr <= SLOT_LAST)
        cum_src = jnp.where(valid, cum_src, 0.0)
        elem = jnp.where(valid, elem, jnp.where(is_mi, -jnp.inf, 0.0))
    cum = _cumsum_rows(tril_b, cum_src)
    if c == 128:
        return cum, elem, cum.T, elem.T
    parts = [cum, elem]
    if 2 * c < 128:
        parts.append(jnp.zeros((128 - 2 * c, 128), F32))
    zt = jnp.concatenate(parts, axis=0).T
    return cum, elem, zt[:, 0:c], zt[:, c:2 * c]


def _row0(i, c):
    return i * c if isinstance(i, int) else pl.multiple_of(i * c, c)


def _pass_a(j, *, c, slot, na, proj_scr, cw_ref, cb_ref, gprm_ref,
            cum_s, col_s, gqx_s, gob_s, ml_s, sy_s, sce_s, sdh_s, defer=False):
    chunks = [j * na + a for a in range(na)]
    r0s = [_row0(i, c) for i in chunks]
    conv = functools.partial(_conv_silu, proj_scr, c=c, cw_ref=cw_ref, cb_ref=cb_ref)

    ii = lax.broadcasted_iota(jnp.int32, (c, c), 0)
    jj = lax.broadcasted_iota(jnp.int32, (c, c), 1)
    tril = ii >= jj
    strict = ii > jj
    eye = (ii == jj).astype(F32)
    tril_b = tril.astype(BF16)
    lane = lax.broadcasted_iota(jnp.int32, (c, 128), 1)

    gates = [_gates(proj_scr, r0, c, slot, gprm_ref, tril_b) for r0 in r0s]
    cum = [g[0] for g in gates]
    elem = [g[1] for g in gates]
    cum_t = [g[2] for g in gates]
    elem_t = [g[3] for g in gates]
    for a in range(na):
        cum_s[chunks[a]] = cum[a]

    def row_cum(a, l):
        return cum_t[a][l:l + 1, :]

    def row_elem(a, l):
        return elem_t[a][l:l + 1, :]

    it = [(a, h) for a in range(na) for h in range(GDN_HEADS)]
    n = range(len(it))
    gd, ml, sd = {}, {}, {}

    def g_k():
        k = [_l2n(conv(r0=r0s[a], col0=C_GK + h * GDN_DK, width=GDN_DK)) for a, h in it]
        gam_c = [_col(cum[a], L_GA + h) for a, h in it]
        beta_c = [_col(elem[a], L_GB + h) for a, h in it]
        dmat = [jnp.exp(jnp.where(tril, gam_c[x] - row_cum(a, L_GA + h), -jnp.inf))
                for x, (a, h) in enumerate(it)]
        pk = [-jnp.where(strict, _dot_nt(k[x], k[x]) * dmat[x] * beta_c[x], 0.0) for x in n]
        gd.update(k=k, gam_c=gam_c, beta_c=beta_c, dmat=dmat, pk=pk, t_inv=[eye + pk[x] for x in n])

    def g_neumann():
        pk = [_dot(gd["pk"][x], gd["pk"][x]) for x in n]
        gd.update(pk=pk, t_inv=[gd["t_inv"][x] + _dot(gd["t_inv"][x], pk[x]) for x in n])

    def g_q():
        gd["q"] = [_l2n(conv(r0=r0s[a], col0=C_GQ + h * GDN_DK, width=GDN_DK)) * (GDN_DK ** -0.5)
                   for a, h in it]

    def g_v():
        v = [conv(r0=r0s[a], col0=C_GV + h * GDN_DV, width=GDN_DV) for a, h in it]
        eg = [jnp.exp(gd["gam_c"][x]) for x in n]
        gd.update(eg=eg, rhs=[jnp.concatenate([v[x] * gd["beta_c"][x],
                                               gd["k"][x] * (gd["beta_c"][x] * eg[x])], axis=1) for x in n])

    def g_uw():
        gd["uw"] = [_dot(gd["t_inv"][x], gd["rhs"][x]) for x in n]

    def g_qk():
        gd["qk"] = [_dot_nt(gd["q"][x], gd["k"][x]) * gd["dmat"][x] for x in n]
        gd["kd"] = [gd["k"][x] * jnp.exp(gd["gam_c"][x][c - 1:c, :] - gd["gam_c"][x]) for x in n]

    def g_out():
        quw = [_dot(gd["qk"][x], gd["uw"][x]) for x in n]
        kuw = [_dot_tn(gd["kd"][x], gd["uw"][x]) for x in n]
        for x, (a, h) in enumerate(it):
            idx = chunks[a] * GDN_HEADS + h
            gqx_s[idx, 0:c, :] = gd["q"][x] * gd["eg"][x] - quw[x][:, GDN_DV:]
            gqx_s[idx, c:c + GDN_DK, :] = kuw[x][:, GDN_DV:]
            gob_s[idx, 0:c, :] = quw[x][:, :GDN_DV]
            gob_s[idx, c:c + GDN_DK, :] = kuw[x][:, :GDN_DV]

    def piece(a, base, h):
        return proj_scr[pl.ds(r0s[a] + SUBLANES, c), base + h * ML_DQK:base + (h + 1) * ML_DQK]

    def m_1():
        mq = [piece(a, C_MQ, h) for a, h in it]
        mk = [piece(a, C_MK, h) * (ML_DQK ** -0.5) for a, h in it]
        b_c = [_col(cum[a], L_MF + h) for a, h in it]
        d = [jnp.where(tril, b_c[x] - row_cum(a, L_MF + h) + row_elem(a, L_MI + h), -jnp.inf)
             for x, (a, h) in enumerate(it)]
        dmax = [jnp.max(d[x], axis=-1, keepdims=True) for x in n]
        dsafe = [jnp.where(dmax[x] == -jnp.inf, 0.0, dmax[x]) for x in n]
        s0 = [_dot_nt(mq[x], mk[x]) * jnp.exp(d[x] - dsafe[x]) for x in n]
        for a in range(na):
            cols = jnp.zeros((c, 128), F32)
            for h in range(ML_HEADS):
                cols = jnp.where(lane == L_MF + h, dmax[a * ML_HEADS + h], cols)
            col_s[chunks[a]] = cols
        ml.update(mk=mk, b_c=b_c, dsafe=dsafe, s0=s0)

    def m_2():
        mv = [piece(a, C_MV, h) for a, h in it]
        i_c = [_col(elem[a], L_MI + h) for a, h in it]
        ones_col = (lax.broadcasted_iota(jnp.int32, (c, 128 - ML_DV), 1) == 0).astype(F32)
        v_aug = [jnp.concatenate([mv[x], ones_col], axis=1) for x in n]
        num0 = [_dot(ml["s0"][x], v_aug[x]) for x in n]
        b_c, dsafe = ml["b_c"], ml["dsafe"]
        kw0 = [ml["mk"][x] * jnp.exp(b_c[x][c - 1:c, :] - b_c[x] + i_c[x] - dsafe[x][c - 1:c, :]) for x in n]
        kv0 = [_dot_tn(kw0[x], v_aug[x]) for x in n]
        for x, (a, h) in enumerate(it):
            idx = chunks[a] * ML_HEADS + h
            ml_s[idx, 0:c, :] = num0[x]
            ml_s[idx, c:c + ML_DQK, :] = kv0[x]

    rep = SSD_HEADS // SSD_GROUPS
    gi = [(a, g) for a in range(na) for g in range(SSD_GROUPS)]
    grp = [a * SSD_GROUPS + h // rep for a, h in it]

    def s_1():
        bg = [conv(r0=r0s[a], col0=C_SB + g * SSD_N, width=SSD_N) for a, g in gi]
        cg = [conv(r0=r0s[a], col0=C_SC + g * SSD_N, width=SSD_N) for a, g in gi]
        sd.update(bg=bg, cg=cg, cb_raw=[_dot_nt(cg[y], bg[y]) for y in range(len(gi))])

    def s_2():
        xg = [conv(r0=r0s[a], col0=C_SX + g * rep * SSD_P, width=rep * SSD_P) for a, g in gi]
        xs = [xg[grp[x]][:, (h % rep) * SSD_P:(h % rep + 1) * SSD_P] for x, (a, h) in enumerate(it)]
        sg_c = [_col(cum[a], L_DT + h) for a, h in it]
        dt_c = [_col(elem[a], L_DT + h) for a, h in it]
        cb = [sd["cb_raw"][grp[x]] * jnp.exp(jnp.where(tril, sg_c[x] - row_cum(a, L_DT + h), -jnp.inf))
              * row_elem(a, L_DT + h) for x, (a, h) in enumerate(it)]
        y0 = [_dot(cb[x], xs[x]) + gprm_ref[2:3, L_DT + h:L_DT + h + 1] * xs[x]
              for x, (a, h) in enumerate(it)]
        dh = [_dot_tn(xs[x] * (jnp.exp(sg_c[x][c - 1:c, :] - sg_c[x]) * dt_c[x]), sd["bg"][grp[x]]) for x in n]
        for x, (a, h) in enumerate(it):
            idx = chunks[a] * SSD_HEADS + h
            sce_s[idx] = sd["cg"][grp[x]] * jnp.exp(sg_c[x])
            sdh_s[idx] = dh[x]
        for a in range(na):
            sy_s[chunks[a]] = jnp.concatenate(y0[a * SSD_HEADS:(a + 1) * SSD_HEADS], axis=1)

    chain = [g_k] + [g_neumann] * (c.bit_length() - 2) + [g_uw]
    if c >= 64:
        fill = [m_1, m_2, s_1, s_2, g_q, g_v]
    else:
        fill = [g_q, g_v, m_1, m_2, s_1, s_2]
    order = []
    for pos, link in enumerate(chain):
        order.append(link)
        if pos < len(fill):
            order.append(fill[pos])
    order += fill[len(chain):] + [g_qk, g_out]
    if defer:
        return order
    for stage in order:
        stage()


def _pass_b(j, *, c, slot, nb, proj_scr, mix_scr, gdnn_ref, mln_ref, ssdn_ref,
            cum_s, col_s, gqx_s, gob_s, ml_s, sy_s, sce_s, sdh_s, mlc_s, gdn_o, mm_o, ssd_o):
    hs = range(GDN_HEADS)
    lane_row = lax.broadcasted_iota(jnp.int32, (1, 128), 1)
    is_m = (lane_row >= L_MF) & (lane_row < L_MF + ML_HEADS)

    def load(seq):
        return dict(s=[gdn_o[seq, h] for h in hs], c=[mlc_s[seq * ML_HEADS + h] for h in hs],
                    h=[ssd_o[seq, h] for h in hs], m=mm_o[seq, 0:1, :])

    def store(seq, st):
        for h in hs:
            gdn_o[seq, h] = st["s"][h]
            mlc_s[seq * ML_HEADS + h] = st["c"][h]
            ssd_o[seq, h] = st["h"][h]
        mm_o[seq, 0:1, :] = st["m"]

    def advance(i, st):
        prow = pl.ds(_row0(i, c) + SUBLANES, c)
        cum = cum_s[i]
        dmx = col_s[i]
        last = cum[c - 1:c, :]
        e_last = jnp.exp(last)
        mq = [proj_scr[prow, C_MQ + h * ML_DQK:C_MQ + (h + 1) * ML_DQK] for h in hs]
        r = [_dot(gqx_s[i * GDN_HEADS + h], st["s"][h]) for h in hs]
        full = [_dot(mq[h], st["c"][h]) for h in hs]
        yh = [_dot_nt(sce_s[i * SSD_HEADS + h], st["h"][h]) for h in hs]
        inter = cum + st["m"]
        m_t = jnp.maximum(inter, dmx)
        w_intra = jnp.exp(dmx - m_t)
        m_new = jnp.where(is_m, m_t[c - 1:c, :], 0.0)
        w_c_row = jnp.exp(last + st["m"] - m_new)
        w_l_row = w_intra[c - 1:c, :]
        ob = [gob_s[i * GDN_HEADS + h] for h in hs]
        blk = [ml_s[i * ML_HEADS + h] for h in hs]
        new = dict(
            s=[_col(e_last, L_GA + h) * st["s"][h] - r[h][c:, :] + ob[h][c:, :] for h in hs],
            c=[_col(w_c_row, L_MF + h) * st["c"][h] + _col(w_l_row, L_MF + h) * blk[h][c:c + ML_DQK, :]
               for h in hs],
            h=[_col(e_last, L_DT + h) * st["h"][h] + sdh_s[i * SSD_HEADS + h] for h in hs],
            m=m_new)
        return new, dict(i=i, r=r, full=full, yh=yh, ob=ob, blk=blk, inter=inter, m_t=m_t, w_intra=w_intra)

    def outputs(ctx):
        i = ctx["i"]
        r0 = _row0(i, c)
        rows = pl.ds(r0, c)
        prow = pl.ds(r0 + SUBLANES, c)
        w_inter = jnp.exp(ctx["inter"] - ctx["m_t"])
        e_neg_m = jnp.exp(-ctx["m_t"])
        for h in hs:
            gg = proj_scr[prow, C_GG + h * GDN_DV:C_GG + (h + 1) * GDN_DV]
            mix_scr[rows, h * GDN_DV:(h + 1) * GDN_DV] = (
                _rms(ctx["r"][h][:c, :] + ctx["ob"][h][:c, :], gdnn_ref[...]) * _silu(gg))
        h_parts = []
        for h in hs:
            fl = (ctx["full"][h] * _col(w_inter, L_MF + h)
                  + _col(ctx["w_intra"], L_MF + h) * ctx["blk"][h][0:c, :])
            den = jnp.maximum(jnp.abs(fl[:, ML_DV:ML_DV + 1]), _col(e_neg_m, L_MF + h))
            mo = proj_scr[prow, C_MO + h * ML_DV:C_MO + (h + 1) * ML_DV]
            h_parts.append(_rms(_sigmoid(mo) * (fl[:, :ML_DV] / den), mln_ref[...]))
        mix_scr[rows, GDN_HEADS * GDN_DV:GDN_HEADS * GDN_DV + ML_HEADS * ML_DV] = (
            jnp.concatenate(h_parts, axis=1))
        sz = proj_scr[prow, C_SZ:C_SZ + SSD_HEADS * SSD_P]
        y_all = (sy_s[i] + jnp.concatenate(ctx["yh"], axis=1)) * _silu(sz)
        mix_scr[rows, GDN_HEADS * GDN_DV + ML_HEADS * ML_DV:D_MODEL] = _rms(y_all, ssdn_ref[...])

    ctxs = []
    st = None
    for a in range(nb):
        i = j * nb + a
        seq = i if slot else 0
        if slot or a == 0:
            st = load(seq)
        st, ctx = advance(i, st)
        if slot or a == nb - 1:
            store(seq, st)
        ctxs.append(ctx)
    for ctx in ctxs:
        outputs(ctx)


def _ml_state_in(c_mat, n_row):
    eye = (lax.broadcasted_iota(jnp.int32, (ML_DQK, ML_DQK), 0)
           == lax.broadcasted_iota(jnp.int32, (ML_DQK, ML_DQK), 1)).astype(F32)
    n_col = jnp.sum(eye * n_row, axis=-1, keepdims=True)
    first = lax.broadcasted_iota(jnp.int32, (ML_DQK, 128 - ML_DV), 1) == 0
    return jnp.concatenate([c_mat, jnp.where(first, n_col, 0.0)], axis=1)


def _ml_state_out(c_aug):
    eye = (lax.broadcasted_iota(jnp.int32, (ML_DQK, ML_DQK), 0)
           == lax.broadcasted_iota(jnp.int32, (ML_DQK, ML_DQK), 1)).astype(F32)
    n_row = jnp.sum(eye * c_aug[:, ML_DV:ML_DV + 1], axis=0, keepdims=True)
    return c_aug[:, :ML_DV], n_row


def _mixer_kernel(*refs, tm, c, slot, nt, na, nbk):
    if slot:
        (x_ref, tail_ref, win_ref, wout_ref, cw_ref, cb_ref, gprm_ref, npre_ref, npost_ref, gdnn_ref,
         mln_ref, ssdn_ref, gdn_i, mc_i, mn_i, mm_i, ssd_i, _, _, _, _, _, _,
         o_ref, conv_o, gdn_o, mc_o, mn_o, mm_o, ssd_o, proj_scr, mix_scr, *ab) = refs
    else:
        (x_ref, win_ref, wout_ref, cw_ref, cb_ref, gprm_ref, npre_ref, npost_ref, gdnn_ref,
         mln_ref, ssdn_ref, _, _, _, _, _, _,
         o_ref, conv_o, gdn_o, mc_o, mn_o, mm_o, ssd_o, proj_scr, mix_scr, *ab) = refs
    names = ("cum_s", "col_s", "gqx_s", "gob_s", "ml_s", "sy_s", "sce_s", "sdh_s", "mlc_s")
    ab = dict(zip(names, ab))
    mlc_s = ab["mlc_s"]
    nseq = gdn_o.shape[0]
    t = pl.program_id(1)

    if slot:
        proj_scr[0:SUBLANES, :] = jnp.zeros((SUBLANES, N_IN), F32)
        gdn_o[...] = gdn_i[...]
        mm_o[...] = mm_i[...]
        ssd_o[...] = ssd_i[...]

        def load_ml(s, carry):
            for h in range(ML_HEADS):
                mlc_s[s * ML_HEADS + h] = _ml_state_in(mc_i[s, h], mn_i[s, h:h + 1, :])
            return carry

        lax.fori_loop(0, nseq, load_ml, 0)
    else:
        @pl.when(t == 0)
        def _():
            proj_scr[0:SUBLANES, :] = jnp.zeros((SUBLANES, N_IN), F32)
            gdn_o[...] = jnp.zeros(gdn_o.shape, F32)
            mlc_s[...] = jnp.zeros(mlc_s.shape, F32)
            mm_o[...] = jnp.zeros(mm_o.shape, F32)
            ssd_o[...] = jnp.zeros(ssd_o.shape, F32)

    x = x_ref[...]
    hn = _rms(x, npre_ref[...]).astype(BF16)

    def in_proj(r_lo, r_hi, nb):
        cols = slice(nb * IN_NBLK, (nb + 1) * IN_NBLK)
        proj_scr[SUBLANES + r_lo:SUBLANES + r_hi, cols] = _dot(hn[r_lo:r_hi, :], win_ref[:, cols])

    def out_proj(r_lo, r_hi):
        out = _dot(mix_scr[r_lo:r_hi, :].astype(BF16), wout_ref[...])
        o_ref[r_lo:r_hi, :] = x[r_lo:r_hi, :] + _rms(out, npost_ref[...])

    def weave(major, minor, minor_first=False):
        major, minor = list(major), list(minor)
        stride = max(1, len(major) // max(1, len(minor)))
        while major or minor:
            if minor and minor_first:
                minor.pop(0)()
            for step in major[:stride]:
                step()
            major = major[stride:]
            if minor and not minor_first:
                minor.pop(0)()

    n_in_blk = N_IN // IN_NBLK
    if slot:
        for nb in range(n_in_blk):
            in_proj(0, tm, nb)
        for s in range(nseq):
            base = SUBLANES + s * SLOT
            proj_scr[base:base + SLOT_FIRST, 0:CONV_DIM] = tail_ref[s]
            conv_o[s] = proj_scr[base + SLOT_LAST - 2:base + SLOT_LAST + 1, 0:CONV_DIM]

    pass_a = functools.partial(_pass_a, c=c, slot=slot, na=na, proj_scr=proj_scr, cw_ref=cw_ref,
                               cb_ref=cb_ref, gprm_ref=gprm_ref,
                               **{k: v for k, v in ab.items() if k != "mlc_s"})
    pass_b = functools.partial(_pass_b, c=c, slot=slot, nb=nbk, proj_scr=proj_scr, mix_scr=mix_scr,
                               gdnn_ref=gdnn_ref, mln_ref=mln_ref, ssdn_ref=ssdn_ref,
                               gdn_o=gdn_o, mm_o=mm_o, ssd_o=ssd_o, **ab)

    n_a, n_b = tm // (c * na), tm // (c * nbk)
    if slot:
        def body_a(j, carry):
            pass_a(j)
            return carry

        def body_b(j, carry):
            pass_b(j)
            return carry

        lax.fori_loop(0, n_a, body_a, 0)
        lax.fori_loop(0, n_b, body_b, 0)
        out_proj(0, tm)
    else:
        assert n_a == 2 and n_b % 2 == 0
        half, b_half = tm // 2, n_b // 2
        for nb in range(n_in_blk):
            in_proj(0, half, nb)
        weave(pass_a(0, defer=True), [functools.partial(in_proj, half, tm, nb) for nb in range(n_in_blk)],
              minor_first=True)
        weave(pass_a(1, defer=True), [functools.partial(pass_b, k) for k in range(b_half)])
        weave([functools.partial(pass_b, b_half + k) for k in range(b_half)],
              [functools.partial(out_proj, 0, half)])
        out_proj(half, tm)

    def store_ml(s, carry):
        for h in range(ML_HEADS):
            c_mat, n_row = _ml_state_out(mlc_s[s * ML_HEADS + h])
            mc_o[s, h] = c_mat
            mn_o[s, h:h + 1, :] = n_row
        return carry

    if slot:
        lax.fori_loop(0, nseq, store_ml, 0)
    else:
        last_rows = proj_scr[tm:tm + SUBLANES, 0:CONV_DIM]
        proj_scr[0:SUBLANES, 0:CONV_DIM] = last_rows

        @pl.when(t == nt - 1)
        def _():
            conv_o[0] = last_rows[SUBLANES - 3:, :]
            store_ml(0, 0)


def _ffn_kernel(*refs, tm, slot, nt):
    if slot:
        (x_ref, ftail_ref, wup_ref, wdn_ref, fw_ref, fb_ref, npre_ref, npost_ref, _,
         o_ref, gate_o, tails_scr, act_scr) = refs
    else:
        (x_ref, wup_ref, wdn_ref, fw_ref, fb_ref, npre_ref, npost_ref, _,
         o_ref, gate_o, tails_scr, act_scr) = refs
    t = pl.program_id(1)

    if slot:
        tails_scr[...] = jnp.zeros(tails_scr.shape, F32)
    else:
        @pl.when(t == 0)
        def _():
            tails_scr[...] = jnp.zeros(tails_scr.shape, F32)

    x = x_ref[...]
    hn = _rms(x, npre_ref[...]).astype(BF16)
    nblk = D_FF // FF_BLK

    def up(blk):
        return (_dot(hn, wup_ref[:, blk * FF_BLK:(blk + 1) * FF_BLK]),
                _dot(hn, wup_ref[:, D_FF + blk * FF_BLK:D_FF + (blk + 1) * FF_BLK]))

    ahead = up(0)
    for blk in range(nblk):
        cols = slice(blk * FF_BLK, (blk + 1) * FF_BLK)
        gate, val = ahead
        if blk + 1 < nblk:
            ahead = up(blk + 1)
        if slot:
            g3 = gate.reshape(tm // SLOT, SLOT, FF_BLK)
            ft = ftail_ref[:, :, cols]
            rr = lax.broadcasted_iota(jnp.int32, g3.shape, 1)
            g3 = jnp.where(rr == SLOT_FIRST - 2, ft[:, 0:1, :], jnp.where(rr == SLOT_FIRST - 1, ft[:, 1:2, :], g3))
            gate_o[:, :, cols] = g3[:, SLOT_LAST - 1:SLOT_LAST + 1, :]
            gate = g3.reshape(tm, FF_BLK)
        full = jnp.concatenate([tails_scr[:, cols], gate], axis=0)
        conv = fb_ref[0:1, cols] + fw_ref[2:3, cols] * gate
        for j in range(2):
            conv = conv + fw_ref[j:j + 1, cols] * pltpu.roll(full, 2 - j, 0)[SUBLANES:, :]
        last_rows = full[tm:tm + SUBLANES, :]
        tails_scr[:, cols] = last_rows
        act_scr[:, cols] = (_gelu_tanh(conv) * val).astype(BF16)
    o_ref[...] = x + _rms(_dot(act_scr[...], wdn_ref[...]), npost_ref[...])
    if not slot:
        @pl.when(t == nt - 1)
        def _():
            gate_o[0] = tails_scr[SUBLANES - 2:, :]


def _const_spec(shape, layer):
    nd = len(shape)
    return pl.BlockSpec((None,) + tuple(shape), lambda b, t: (layer,) + (0,) * nd,
                        pipeline_mode=pl.Buffered(1))


def _state_dims():
    return ((3, CONV_DIM), (GDN_HEADS, GDN_DK, GDN_DV), (ML_HEADS, ML_DQK, ML_DV), (ML_HEADS, ML_DQK),
            (SUBLANES, 128), (SSD_HEADS, SSD_P, SSD_N))


def _mixer_call(x, layer, prm, bufs, *, slot, states=None, tail=None):
    rows = x.shape[0]
    if slot:
        tm, c, nt, na, nbk = SAMPLE_TM, SLOT, 1, SAMPLE_NA, SAMPLE_NB
        nseq = tm // SLOT
    else:
        tm, c, na, nbk = PROMPT_TM, PROMPT_CHUNK, PROMPT_NA, PROMPT_NB
        nt = 2048 // tm
        nseq = 1
    nch = tm // c
    ngrp = rows // (tm * nt)
    nb = ngrp * nseq
    row_spec = lambda w: pl.BlockSpec((tm, w), lambda b, t: (b * nt + t, 0))
    st_in = lambda *dims: pl.BlockSpec((None, nseq) + dims, lambda b, t: (layer, b) + (0,) * len(dims))

    in_specs = [row_spec(D_MODEL)]
    args = [x]
    if slot:
        in_specs.append(st_in(SLOT_FIRST, CONV_DIM))
        args.append(tail)
    in_specs += [
        _const_spec((D_MODEL, N_IN), layer), _const_spec((D_MODEL, D_MODEL), layer),
        _const_spec((4, CONV_DIM), layer), _const_spec((1, CONV_DIM), layer),
        _const_spec((8, 128), layer), _const_spec((1, D_MODEL), layer), _const_spec((1, D_MODEL), layer),
        _const_spec((1, GDN_DV), layer), _const_spec((1, ML_DV), layer),
        _const_spec((1, SSD_HEADS * SSD_P), layer)]
    args += [prm["w_in"], prm["w_out"], prm["conv_w"], prm["conv_b"], prm["gprm"], prm["norm_mix_pre"],
             prm["norm_mix_post"], prm["gdn_norm"], prm["mlstm_norm"], prm["ssd_norm"]]
    if slot:
        in_specs += [st_in(GDN_HEADS, GDN_DK, GDN_DV), st_in(ML_HEADS, ML_DQK, ML_DV),
                     st_in(ML_HEADS, ML_DQK), st_in(SUBLANES, 128), st_in(SSD_HEADS, SSD_P, SSD_N)]
        args += list(states)
    aliases = {len(args) + k: 1 + k for k in range(len(bufs))}
    in_specs += [pl.BlockSpec(memory_space=pl.ANY)] * len(bufs)
    args += list(bufs)

    out_specs = [row_spec(D_MODEL)] + [st_in(*dims) for dims in _state_dims()]
    out_shape = [jax.ShapeDtypeStruct((rows, D_MODEL), F32)] + [
        jax.ShapeDtypeStruct((DEPTH, nb) + dims, F32) for dims in _state_dims()]
    scratch = [
        pltpu.VMEM((tm + SUBLANES, N_IN), F32),
        pltpu.VMEM((tm, D_MODEL), F32),
        pltpu.VMEM((nch, c, 128), F32),
        pltpu.VMEM((nch, c, 128), F32),
        pltpu.VMEM((nch * GDN_HEADS, c + GDN_DK, GDN_DV), F32),
        pltpu.VMEM((nch * GDN_HEADS, c + GDN_DK, GDN_DV), F32),
        pltpu.VMEM((nch * ML_HEADS, c + ML_DQK, 128), F32),
        pltpu.VMEM((nch, c, SSD_HEADS * SSD_P), F32),
        pltpu.VMEM((nch * SSD_HEADS, c, SSD_N), F32),
        pltpu.VMEM((nch * SSD_HEADS, SSD_P, SSD_N), F32),
        pltpu.VMEM((nseq * ML_HEADS, ML_DQK, 128), F32),
    ]
    return pl.pallas_call(
        functools.partial(_mixer_kernel, tm=tm, c=c, slot=slot, nt=nt, na=na, nbk=nbk),
        grid=(ngrp, nt), in_specs=in_specs, out_specs=out_specs, out_shape=out_shape,
        scratch_shapes=scratch, input_output_aliases=aliases,
        compiler_params=pltpu.CompilerParams(dimension_semantics=("arbitrary", "arbitrary"),
                                             vmem_limit_bytes=VMEM_LIMIT),
        name=("mixer_sample" if slot else "mixer_prompt"),
    )(*args)


def _ffn_call(x, layer, prm, buf, *, slot, tail=None):
    rows = x.shape[0]
    tm = FFN_TM
    nt = 1 if slot else 2048 // tm
    ngrp = rows // (tm * nt)
    row_spec = lambda w: pl.BlockSpec((tm, w), lambda b, t: (b * nt + t, 0))
    in_specs = [row_spec(D_MODEL)]
    args = [x]
    nseq = tm // SLOT if slot else 1
    if slot:
        in_specs.append(pl.BlockSpec((None, nseq, 2, D_FF), lambda b, t: (layer, b, 0, 0)))
        args.append(tail)
    in_specs += [_const_spec((D_MODEL, 2 * D_FF), layer), _const_spec((D_FF, D_MODEL), layer),
                 _const_spec((3, D_FF), layer), _const_spec((1, D_FF), layer),
                 _const_spec((1, D_MODEL), layer), _const_spec((1, D_MODEL), layer)]
    args += [prm["ffn_w_up"], prm["ffn_w_down"], prm["ffn_conv_w"], prm["ffn_conv_b"],
             prm["norm_ffn_pre"], prm["norm_ffn_post"]]
    aliases = {len(args): 1}
    in_specs.append(pl.BlockSpec(memory_space=pl.ANY))
    args.append(buf)
    gate_spec = pl.BlockSpec((None, nseq, 2, D_FF), lambda b, t: (layer, b, 0, 0))
    gate_shape = (DEPTH, ngrp * nseq, 2, D_FF)
    return pl.pallas_call(
        functools.partial(_ffn_kernel, tm=tm, slot=slot, nt=nt),
        grid=(ngrp, nt), in_specs=in_specs, out_specs=[row_spec(D_MODEL), gate_spec],
        out_shape=[jax.ShapeDtypeStruct((rows, D_MODEL), F32), jax.ShapeDtypeStruct(gate_shape, F32)],
        scratch_shapes=[pltpu.VMEM((SUBLANES, D_FF), F32), pltpu.VMEM((tm, D_FF), BF16)],
        input_output_aliases=aliases,
        compiler_params=pltpu.CompilerParams(dimension_semantics=("arbitrary", "arbitrary"),
                                             vmem_limit_bytes=VMEM_LIMIT),
        name=("ffn_sample" if slot else "ffn_prompt"),
    )(*args)


def _prepare_params(norm_mix_pre, norm_mix_post, norm_ffn_pre, norm_ffn_post, w_in, conv_w, conv_b,
                    gdn_a_log, gdn_dt_bias, gdn_norm, mlstm_i_bias, mlstm_f_bias, mlstm_norm,
                    ssd_a_log, ssd_dt_bias, ssd_d, ssd_norm, w_out, ffn_w_up, ffn_conv_w, ffn_conv_b,
                    ffn_w_down):
    w_in_p = jnp.concatenate(
        [w_in[..., :2816], w_in[..., 2824:3848], w_in[..., 3856:4112], w_in[..., 2816:2824],
         w_in[..., 3848:3856], w_in[..., 4112:4116],
         jnp.zeros((DEPTH, D_MODEL, N_IN - 4116), w_in.dtype)], axis=-1).astype(BF16)
    z4 = jnp.zeros((DEPTH, 4), F32)
    pad = jnp.zeros((DEPTH, 128 - 20), F32)
    gprm = jnp.stack(
        [jnp.concatenate([gdn_dt_bias, z4, mlstm_i_bias, mlstm_f_bias, ssd_dt_bias, pad], axis=-1),
         jnp.concatenate([gdn_a_log, z4, z4, z4, ssd_a_log, pad], axis=-1),
         jnp.concatenate([z4, z4, z4, z4, ssd_d, pad], axis=-1)]
        + [jnp.zeros((DEPTH, 128), F32)] * 5, axis=1)
    row = lambda a: a[:, None, :]
    return dict(
        w_in=w_in_p, w_out=w_out.astype(BF16), conv_w=conv_w, conv_b=row(conv_b), gprm=gprm,
        norm_mix_pre=row(norm_mix_pre), norm_mix_post=row(norm_mix_post),
        norm_ffn_pre=row(norm_ffn_pre), norm_ffn_post=row(norm_ffn_post),
        gdn_norm=row(gdn_norm), mlstm_norm=row(mlstm_norm), ssd_norm=row(ssd_norm),
        ffn_w_up=ffn_w_up.astype(BF16), ffn_w_down=ffn_w_down.astype(BF16),
        ffn_conv_w=ffn_conv_w, ffn_conv_b=row(ffn_conv_b))


def kernel(x_prompt, x_sample, state_conv, state_gdn, state_mlstm_c, state_mlstm_n, state_mlstm_m, state_ssd, state_ffn_conv, norm_mix_pre, norm_mix_post, norm_ffn_pre, norm_ffn_post, w_in, conv_w, conv_b, gdn_a_log, gdn_dt_bias, gdn_norm, mlstm_i_bias, mlstm_f_bias, mlstm_norm, ssd_a_log, ssd_dt_bias, ssd_d, ssd_norm, w_out, ffn_w_up, ffn_conv_w, ffn_conv_b, ffn_w_down):
    prm = _prepare_params(norm_mix_pre, norm_mix_post, norm_ffn_pre, norm_ffn_post, w_in, conv_w, conv_b,
                          gdn_a_log, gdn_dt_bias, gdn_norm, mlstm_i_bias, mlstm_f_bias, mlstm_norm,
                          ssd_a_log, ssd_dt_bias, ssd_d, ssd_norm, w_out, ffn_w_up, ffn_conv_w,
                          ffn_conv_b, ffn_w_down)
    bp, lp, _ = x_prompt.shape
    bs, ls, _ = x_sample.shape

    def state_bufs(nseq):
        return ([jnp.zeros((DEPTH, nseq) + dims, F32) for dims in _state_dims()],
                jnp.zeros((DEPTH, nseq, 2, D_FF), F32))

    def assemble(bufs, gate):
        conv, gdn, mc, mn, mm, ssd = bufs
        return [conv, gdn, mc, mn, mm[:, :, 0, L_MF:L_MF + ML_HEADS], ssd, gate]

    x = x_prompt.reshape(bp * lp, D_MODEL)
    bufs, gate = state_bufs(bp)
    for layer in range(DEPTH):
        x, *bufs = _mixer_call(x, layer, prm, bufs, slot=False)
        x, gate = _ffn_call(x, layer, prm, gate, slot=False)
    p_out = assemble(bufs, gate)
    y_prompt = x.reshape(bp, lp, D_MODEL)

    x = jnp.pad(x_sample, ((0, 0), (SLOT_FIRST, SLOT - SLOT_FIRST - ls), (0, 0))).reshape(bs * SLOT, D_MODEL)
    mm_in = jnp.pad(state_mlstm_m[:, :, None, :],
                    ((0, 0), (0, 0), (0, SUBLANES - 1), (L_MF, 128 - L_MF - ML_HEADS)))
    bufs, gate = state_bufs(bs)
    for layer in range(DEPTH):
        x, *bufs = _mixer_call(x, layer, prm, bufs, slot=True, tail=state_conv,
                               states=(state_gdn, state_mlstm_c, state_mlstm_n, mm_in, state_ssd))
        x, gate = _ffn_call(x, layer, prm, gate, slot=True, tail=state_ffn_conv)
    s_out = assemble(bufs, gate)
    y_sample = x.reshape(bs, SLOT, D_MODEL)[:, SLOT_FIRST:SLOT_LAST + 1]
    return (y_prompt, y_sample, *p_out, *s_out)
```

```python
import functools

import jax
import jax.numpy as jnp
from jax import lax
from jax.experimental import pallas as pl
from jax.experimental.pallas import tpu as pltpu

F32 = jnp.float32
BF16 = jnp.bfloat16

D_MODEL = 1024
DEPTH = 2
GDN_HEADS, GDN_DK, GDN_DV = 4, 128, 128
ML_HEADS, ML_DQK, ML_DV = 4, 64, 64
SSD_HEADS, SSD_P, SSD_GROUPS, SSD_N = 4, 64, 2, 128
D_FF = 2816
EPS = 1e-6

CONV_DIM = 2304
C_GQ, C_GK, C_GV, C_SX, C_SB, C_SC = 0, 512, 1024, 1536, 1792, 2048
C_GG, C_MQ, C_MK, C_MV, C_MO, C_SZ, C_GATE = 2304, 2816, 3072, 3328, 3584, 3840, 4096
N_IN = 4224
IN_NBLK = 1408
L_GA, L_GB, L_MI, L_MF, L_DT = 0, 4, 8, 12, 16

SUBLANES = 8
SLOT = 8
SLOT_FIRST, SLOT_LAST = 3, 6
FF_BLK = 256
VMEM_LIMIT = 56 * 1024 * 1024

PROMPT_TM, PROMPT_CHUNK, PROMPT_NA, PROMPT_NB = 512, 64, 4, 1
SAMPLE_TM, SAMPLE_NA, SAMPLE_NB = 64, 4, 4
FFN_TM = 1024


def _dot(a, b):
    return jnp.dot(a, b, preferred_element_type=F32)


def _dot_nt(a, b):
    return lax.dot_general(a, b, (((1,), (1,)), ((), ())), preferred_element_type=F32)


def _dot_tn(a, b):
    return lax.dot_general(a, b, (((0,), (0,)), ((), ())), preferred_element_type=F32)


def _sigmoid(x):
    return 1.0 / (1.0 + jnp.exp(-x))


def _silu(x):
    return x * _sigmoid(x)


def _rms(x, w):
    return x * lax.rsqrt(jnp.mean(x * x, axis=-1, keepdims=True) + EPS) * w


def _l2n(x):
    return x * lax.rsqrt(jnp.sum(x * x, axis=-1, keepdims=True) + EPS)


def _gelu_tanh(x):
    return 0.5 * x * (1.0 + jnp.tanh(0.7978845608028654 * (x + 0.044715 * (x * x * x))))


def _col(a, l):
    return a[:, l:l + 1]


def _cumsum_rows(tril_b, x):
    hi = x.astype(BF16)
    r1 = x - hi.astype(F32)
    mid = r1.astype(BF16)
    lo = (r1 - mid.astype(F32)).astype(BF16)
    return _dot(tril_b, hi) + _dot(tril_b, mid) + _dot(tril_b, lo)


def _conv_silu(proj_scr, r0, c, col0, width, cw_ref, cb_ref):
    win = proj_scr[pl.ds(r0, c + SUBLANES), col0:col0 + width]
    acc = cw_ref[0:1, col0:col0 + width] * win
    for j in range(1, 4):
        acc = cw_ref[j:j + 1, col0:col0 + width] * win + pltpu.roll(acc, 1, 0)
    return _silu(acc[SUBLANES:, :] + cb_ref[0:1, col0:col0 + width])


def _gates(proj_scr, r0, c, slot, gprm_ref, tril_b):
    graw = proj_scr[pl.ds(r0 + SUBLANES, c), C_GATE:C_GATE + 128]
    lane = lax.broadcasted_iota(jnp.int32, (c, 128), 1)
    z = graw + gprm_ref[0:1, :]
    soft = jnp.log(1.0 + jnp.exp(-jnp.abs(z)))
    sp = jnp.maximum(z, 0.0) + soft
    log_sig = -(jnp.maximum(-z, 0.0) + soft)
    a_neg = -jnp.exp(gprm_ref[1:2, :])
    is_ga = lane < L_GB
    is_gb = (lane >= L_GB) & (lane < L_MI)
    is_mi = (lane >= L_MI) & (lane < L_MF)
    is_mf = (lane >= L_MF) & (lane < L_DT)
    is_dt = (lane >= L_DT) & (lane < L_DT + SSD_HEADS)
    cum_src = jnp.where(is_ga | is_dt, a_neg * sp, jnp.where(is_mf, log_sig, 0.0))
    elem = jnp.where(is_gb, _sigmoid(graw), jnp.where(is_mi, z, jnp.where(is_dt, sp, 0.0)))
    if slot:
        rr = lax.broadcasted_iota(jnp.int32, (c, 128), 0)
        valid = (rr >= SLOT_FIRST) & (rr <= SLOT_LAST)
        cum_src = jnp.where(valid, cum_src, 0.0)
        elem = jnp.where(valid, elem, jnp.where(is_mi, -jnp.inf, 0.0))
    cum = _cumsum_rows(tril_b, cum_src)
    if c == 128:
        return cum, elem, cum.T, elem.T
    parts = [cum, elem]
    if 2 * c < 128:
        parts.append(jnp.zeros((128 - 2 * c, 128), F32))
    zt = jnp.concatenate(parts, axis=0).T
    return cum, elem, zt[:, 0:c], zt[:, c:2 * c]


def _row0(i, c):
    return i * c if isinstance(i, int) else pl.multiple_of(i * c, c)


def _pass_a(j, *, c, slot, na, proj_scr, cw_ref, cb_ref, gprm_ref,
            cum_s, col_s, gqx_s, gob_s, ml_s, sy_s, sce_s, sdh_s, defer=False):
    chunks = [j * na + a for a in range(na)]
    r0s = [_row0(i, c) for i in chunks]
    conv = functools.partial(_conv_silu, proj_scr, c=c, cw_ref=cw_ref, cb_ref=cb_ref)

    ii = lax.broadcasted_iota(jnp.int32, (c, c), 0)
    jj = lax.broadcasted_iota(jnp.int32, (c, c), 1)
    tril = ii >= jj
    strict = ii > jj
    eye = (ii == jj).astype(F32)
    tril_b = tril.astype(BF16)
    lane = lax.broadcasted_iota(jnp.int32, (c, 128), 1)

    gates = [_gates(proj_scr, r0, c, slot, gprm_ref, tril_b) for r0 in r0s]
    cum = [g[0] for g in gates]
    elem = [g[1] for g in gates]
    cum_t = [g[2] for g in gates]
    elem_t = [g[3] for g in gates]
    for a in range(na):
        cum_s[chunks[a]] = cum[a]

    def row_cum(a, l):
        return cum_t[a][l:l + 1, :]

    def row_elem(a, l):
        return elem_t[a][l:l + 1, :]

    it = [(a, h) for a in range(na) for h in range(GDN_HEADS)]
    n = range(len(it))
    gd, ml, sd = {}, {}, {}

    def g_k():
        k = [_l2n(conv(r0=r0s[a], col0=C_GK + h * GDN_DK, width=GDN_DK)) for a, h in it]
        gam_c = [_col(cum[a], L_GA + h) for a, h in it]
        beta_c = [_col(elem[a], L_GB + h) for a, h in it]
        dmat = [jnp.exp(jnp.where(tril, gam_c[x] - row_cum(a, L_GA + h), -jnp.inf))
                for x, (a, h) in enumerate(it)]
        pk = [-jnp.where(strict, _dot_nt(k[x], k[x]) * dmat[x] * beta_c[x], 0.0) for x in n]
        gd.update(k=k, gam_c=gam_c, beta_c=beta_c, dmat=dmat, pk=pk, t_inv=[eye + pk[x] for x in n])

    def g_neumann():
        pk = [_dot(gd["pk"][x], gd["pk"][x]) for x in n]
        gd.update(pk=pk, t_inv=[gd["t_inv"][x] + _dot(gd["t_inv"][x], pk[x]) for x in n])

    def g_q():
        gd["q"] = [_l2n(conv(r0=r0s[a], col0=C_GQ + h * GDN_DK, width=GDN_DK)) * (GDN_DK ** -0.5)
                   for a, h in it]

    def g_v():
        v = [conv(r0=r0s[a], col0=C_GV + h * GDN_DV, width=GDN_DV) for a, h in it]
        eg = [jnp.exp(gd["gam_c"][x]) for x in n]
        gd.update(eg=eg, rhs=[jnp.concatenate([v[x] * gd["beta_c"][x],
                                               gd["k"][x] * (gd["beta_c"][x] * eg[x])], axis=1) for x in n])

    def g_uw():
        gd["uw"] = [_dot(gd["t_inv"][x], gd["rhs"][x]) for x in n]

    def g_qk():
        gd["qk"] = [_dot_nt(gd["q"][x], gd["k"][x]) * gd["dmat"][x] for x in n]
        gd["kd"] = [gd["k"][x] * jnp.exp(gd["gam_c"][x][c - 1:c, :] - gd["gam_c"][x]) for x in n]

    def g_out():
        quw = [_dot(gd["qk"][x], gd["uw"][x]) for x in n]
        kuw = [_dot_tn(gd["kd"][x], gd["uw"][x]) for x in n]
        for x, (a, h) in enumerate(it):
            idx = chunks[a] * GDN_HEADS + h
            gqx_s[idx, 0:c, :] = gd["q"][x] * gd["eg"][x] - quw[x][:, GDN_DV:]
            gqx_s[idx, c:c + GDN_DK, :] = kuw[x][:, GDN_DV:]
            gob_s[idx, 0:c, :] = quw[x][:, :GDN_DV]
            gob_s[idx, c:c + GDN_DK, :] = kuw[x][:, :GDN_DV]

    def piece(a, base, h):
        return proj_scr[pl.ds(r0s[a] + SUBLANES, c), base + h * ML_DQK:base + (h + 1) * ML_DQK]

    def m_1():
        mq = [piece(a, C_MQ, h) for a, h in it]
        mk = [piece(a, C_MK, h) * (ML_DQK ** -0.5) for a, h in it]
        b_c = [_col(cum[a], L_MF + h) for a, h in it]
        d = [jnp.where(tril, b_c[x] - row_cum(a, L_MF + h) + row_elem(a, L_MI + h), -jnp.inf)
             for x, (a, h) in enumerate(it)]
        dmax = [jnp.max(d[x], axis=-1, keepdims=True) for x in n]
        dsafe = [jnp.where(dmax[x] == -jnp.inf, 0.0, dmax[x]) for x in n]
        s0 = [_dot_nt(mq[x], mk[x]) * jnp.exp(d[x] - dsafe[x]) for x in n]
        for a in range(na):
            cols = jnp.zeros((c, 128), F32)
            for h in range(ML_HEADS):
                cols = jnp.where(lane == L_MF + h, dmax[a * ML_HEADS + h], cols)
            col_s[chunks[a]] = cols
        ml.update(mk=mk, b_c=b_c, dsafe=dsafe, s0=s0)

    def m_2():
        mv = [piece(a, C_MV, h) for a, h in it]
        i_c = [_col(elem[a], L_MI + h) for a, h in it]
        ones_col = (lax.broadcasted_iota(jnp.int32, (c, 128 - ML_DV), 1) == 0).astype(F32)
        v_aug = [jnp.concatenate([mv[x], ones_col], axis=1) for x in n]
        num0 = [_dot(ml["s0"][x], v_aug[x]) for x in n]
        b_c, dsafe = ml["b_c"], ml["dsafe"]
        kw0 = [ml["mk"][x] * jnp.exp(b_c[x][c - 1:c, :] - b_c[x] + i_c[x] - dsafe[x][c - 1:c, :]) for x in n]
        kv0 = [_dot_tn(kw0[x], v_aug[x]) for x in n]
        for x, (a, h) in enumerate(it):
            idx = chunks[a] * ML_HEADS + h
            ml_s[idx, 0:c, :] = num0[x]
            ml_s[idx, c:c + ML_DQK, :] = kv0[x]

    rep = SSD_HEADS // SSD_GROUPS
    gi = [(a, g) for a in range(na) for g in range(SSD_GROUPS)]
    grp = [a * SSD_GROUPS + h // rep for a, h in it]

    def s_1():
        bg = [conv(r0=r0s[a], col0=C_SB + g * SSD_N, width=SSD_N) for a, g in gi]
        cg = [conv(r0=r0s[a], col0=C_SC + g * SSD_N, width=SSD_N) for a, g in gi]
        sd.update(bg=bg, cg=cg, cb_raw=[_dot_nt(cg[y], bg[y]) for y in range(len(gi))])

    def s_2():
        xg = [conv(r0=r0s[a], col0=C_SX + g * rep * SSD_P, width=rep * SSD_P) for a, g in gi]
        xs = [xg[grp[x]][:, (h % rep) * SSD_P:(h % rep + 1) * SSD_P] for x, (a, h) in enumerate(it)]
        sg_c = [_col(cum[a], L_DT + h) for a, h in it]
        dt_c = [_col(elem[a], L_DT + h) for a, h in it]
        cb = [sd["cb_raw"][grp[x]] * jnp.exp(jnp.where(tril, sg_c[x] - row_cum(a, L_DT + h), -jnp.inf))
              * row_elem(a, L_DT + h) for x, (a, h) in enumerate(it)]
        y0 = [_dot(cb[x], xs[x]) + gprm_ref[2:3, L_DT + h:L_DT + h + 1] * xs[x]
              for x, (a, h) in enumerate(it)]
        dh = [_dot_tn(xs[x] * (jnp.exp(sg_c[x][c - 1:c, :] - sg_c[x]) * dt_c[x]), sd["bg"][grp[x]]) for x in n]
        for x, (a, h) in enumerate(it):
            idx = chunks[a] * SSD_HEADS + h
            sce_s[idx] = sd["cg"][grp[x]] * jnp.exp(sg_c[x])
            sdh_s[idx] = dh[x]
        for a in range(na):
            sy_s[chunks[a]] = jnp.concatenate(y0[a * SSD_HEADS:(a + 1) * SSD_HEADS], axis=1)

    chain = [g_k] + [g_neumann] * (c.bit_length() - 2) + [g_uw]
    if c >= 64:
        fill = [m_1, m_2, s_1, s_2, g_q, g_v]
    else:
        fill = [g_q, g_v, m_1, m_2, s_1, s_2]
    order = []
    for pos, link in enumerate(chain):
        order.append(link)
        if pos < len(fill):
            order.append(fill[pos])
    order += fill[len(chain):] + [g_qk, g_out]
    if defer:
        return order
    for stage in order:
        stage()


def _pass_b(j, *, c, slot, nb, proj_scr, mix_scr, gdnn_ref, mln_ref, ssdn_ref,
            cum_s, col_s, gqx_s, gob_s, ml_s, sy_s, sce_s, sdh_s, mlc_s, gdn_o, mm_o, ssd_o):
    hs = range(GDN_HEADS)
    lane_row = lax.broadcasted_iota(jnp.int32, (1, 128), 1)
    is_m = (lane_row >= L_MF) & (lane_row < L_MF + ML_HEADS)

    def load(seq):
        return dict(s=[gdn_o[seq, h] for h in hs], c=[mlc_s[seq * ML_HEADS + h] for h in hs],
                    h=[ssd_o[seq, h] for h in hs], m=mm_o[seq, 0:1, :])

    def store(seq, st):
        for h in hs:
            gdn_o[seq, h] = st["s"][h]
            mlc_s[seq * ML_HEADS + h] = st["c"][h]
            ssd_o[seq, h] = st["h"][h]
        mm_o[seq, 0:1, :] = st["m"]

    def advance(i, st):
        prow = pl.ds(_row0(i, c) + SUBLANES, c)
        cum = cum_s[i]
        dmx = col_s[i]
        last = cum[c - 1:c, :]
        e_last = jnp.exp(last)
        mq = [proj_scr[prow, C_MQ + h * ML_DQK:C_MQ + (h + 1) * ML_DQK] for h in hs]
        r = [_dot(gqx_s[i * GDN_HEADS + h], st["s"][h]) for h in hs]
        full = [_dot(mq[h], st["c"][h]) for h in hs]
        yh = [_dot_nt(sce_s[i * SSD_HEADS + h], st["h"][h]) for h in hs]
        inter = cum + st["m"]
        m_t = jnp.maximum(inter, dmx)
        w_intra = jnp.exp(dmx - m_t)
        m_new = jnp.where(is_m, m_t[c - 1:c, :], 0.0)
        w_c_row = jnp.exp(last + st["m"] - m_new)
        w_l_row = w_intra[c - 1:c, :]
        ob = [gob_s[i * GDN_HEADS + h] for h in hs]
        blk = [ml_s[i * ML_HEADS + h] for h in hs]
        new = dict(
            s=[_col(e_last, L_GA + h) * st["s"][h] - r[h][c:, :] + ob[h][c:, :] for h in hs],
            c=[_col(w_c_row, L_MF + h) * st["c"][h] + _col(w_l_row, L_MF + h) * blk[h][c:c + ML_DQK, :]
               for h in hs],
            h=[_col(e_last, L_DT + h) * st["h"][h] + sdh_s[i * SSD_HEADS + h] for h in hs],
            m=m_new)
        return new, dict(i=i, r=r, full=full, yh=yh, ob=ob, blk=blk, inter=inter, m_t=m_t, w_intra=w_intra)

    def outputs(ctx):
        i = ctx["i"]
        r0 = _row0(i, c)
        rows = pl.ds(r0, c)
        prow = pl.ds(r0 + SUBLANES, c)
        w_inter = jnp.exp(ctx["inter"] - ctx["m_t"])
        e_neg_m = jnp.exp(-ctx["m_t"])
        for h in hs:
            gg = proj_scr[prow, C_GG + h * GDN_DV:C_GG + (h + 1) * GDN_DV]
            mix_scr[rows, h * GDN_DV:(h + 1) * GDN_DV] = (
                _rms(ctx["r"][h][:c, :] + ctx["ob"][h][:c, :], gdnn_ref[...]) * _silu(gg))
        h_parts = []
        for h in hs:
            fl = (ctx["full"][h] * _col(w_inter, L_MF + h)
                  + _col(ctx["w_intra"], L_MF + h) * ctx["blk"][h][0:c, :])
            den = jnp.maximum(jnp.abs(fl[:, ML_DV:ML_DV + 1]), _col(e_neg_m, L_MF + h))
            mo = proj_scr[prow, C_MO + h * ML_DV:C_MO + (h + 1) * ML_DV]
            h_parts.append(_rms(_sigmoid(mo) * (fl[:, :ML_DV] / den), mln_ref[...]))
        mix_scr[rows, GDN_HEADS * GDN_DV:GDN_HEADS * GDN_DV + ML_HEADS * ML_DV] = (
            jnp.concatenate(h_parts, axis=1))
        sz = proj_scr[prow, C_SZ:C_SZ + SSD_HEADS * SSD_P]
        y_all = (sy_s[i] + jnp.concatenate(ctx["yh"], axis=1)) * _silu(sz)
        mix_scr[rows, GDN_HEADS * GDN_DV + ML_HEADS * ML_DV:D_MODEL] = _rms(y_all, ssdn_ref[...])

    ctxs = []
    st = None
    for a in range(nb):
        i = j * nb + a
        seq = i if slot else 0
        if slot or a == 0:
            st = load(seq)
        st, ctx = advance(i, st)
        if slot or a == nb - 1:
            store(seq, st)
        ctxs.append(ctx)
    for ctx in ctxs:
        outputs(ctx)


def _ml_state_in(c_mat, n_row):
    eye = (lax.broadcasted_iota(jnp.int32, (ML_DQK, ML_DQK), 0)
           == lax.broadcasted_iota(jnp.int32, (ML_DQK, ML_DQK), 1)).astype(F32)
    n_col = jnp.sum(eye * n_row, axis=-1, keepdims=True)
    first = lax.broadcasted_iota(jnp.int32, (ML_DQK, 128 - ML_DV), 1) == 0
    return jnp.concatenate([c_mat, jnp.where(first, n_col, 0.0)], axis=1)


def _ml_state_out(c_aug):
    eye = (lax.broadcasted_iota(jnp.int32, (ML_DQK, ML_DQK), 0)
           == lax.broadcasted_iota(jnp.int32, (ML_DQK, ML_DQK), 1)).astype(F32)
    n_row = jnp.sum(eye * c_aug[:, ML_DV:ML_DV + 1], axis=0, keepdims=True)
    return c_aug[:, :ML_DV], n_row


def _mixer_kernel(*refs, tm, c, slot, nt, na, nbk, first):
    n_in = (17 if slot else 11) + (0 if first else 6)
    ins, (o_ref, *state_o), (proj_scr, mix_scr, *ab) = refs[:n_in], refs[n_in:n_in + 7], refs[n_in + 7:]
    if slot:
        (x_ref, tail_ref, win_ref, wout_ref, cw_ref, cb_ref, gprm_ref, npre_ref, npost_ref, gdnn_ref,
         mln_ref, ssdn_ref, gdn_i, mc_i, mn_i, mm_i, ssd_i) = ins[:17]
    else:
        (x_ref, win_ref, wout_ref, cw_ref, cb_ref, gprm_ref, npre_ref, npost_ref, gdnn_ref,
         mln_ref, ssdn_ref) = ins[:11]
    if first:
        for ref in state_o:
            ref[1] = jnp.zeros(ref.shape[1:], F32)
        state_o = [ref.at[0] for ref in state_o]
    conv_o, gdn_o, mc_o, mn_o, mm_o, ssd_o = state_o
    names = ("cum_s", "col_s", "gqx_s", "gob_s", "ml_s", "sy_s", "sce_s", "sdh_s", "mlc_s")
    ab = dict(zip(names, ab))
    mlc_s = ab["mlc_s"]
    nseq = gdn_o.shape[0]
    t = pl.program_id(1)

    if slot:
        proj_scr[0:SUBLANES, :] = jnp.zeros((SUBLANES, N_IN), F32)
        gdn_o[...] = gdn_i[...]
        mm_o[...] = mm_i[...]
        ssd_o[...] = ssd_i[...]

        def load_ml(s, carry):
            for h in range(ML_HEADS):
                mlc_s[s * ML_HEADS + h] = _ml_state_in(mc_i[s, h], mn_i[s, h:h + 1, :])
            return carry

        lax.fori_loop(0, nseq, load_ml, 0)
    else:
        @pl.when(t == 0)
        def _():
            proj_scr[0:SUBLANES, :] = jnp.zeros((SUBLANES, N_IN), F32)
            gdn_o[...] = jnp.zeros(gdn_o.shape, F32)
            mlc_s[...] = jnp.zeros(mlc_s.shape, F32)
            mm_o[...] = jnp.zeros(mm_o.shape, F32)
            ssd_o[...] = jnp.zeros(ssd_o.shape, F32)

    x = x_ref[...]
    hn = _rms(x, npre_ref[...]).astype(BF16)

    def in_proj(r_lo, r_hi, nb):
        cols = slice(nb * IN_NBLK, (nb + 1) * IN_NBLK)
        proj_scr[SUBLANES + r_lo:SUBLANES + r_hi, cols] = _dot(hn[r_lo:r_hi, :], win_ref[:, cols])

    def out_proj(r_lo, r_hi):
        out = _dot(mix_scr[r_lo:r_hi, :].astype(BF16), wout_ref[...])
        o_ref[r_lo:r_hi, :] = x[r_lo:r_hi, :] + _rms(out, npost_ref[...])

    def weave(major, minor, minor_first=False):
        major, minor = list(major), list(minor)
        stride = max(1, len(major) // max(1, len(minor)))
        while major or minor:
            if minor and minor_first:
                minor.pop(0)()
            for step in major[:stride]:
                step()
            major = major[stride:]
            if minor and not minor_first:
                minor.pop(0)()

    n_in_blk = N_IN // IN_NBLK
    if slot:
        for nb in range(n_in_blk):
            in_proj(0, tm, nb)
        for s in range(nseq):
            base = SUBLANES + s * SLOT
            proj_scr[base:base + SLOT_FIRST, 0:CONV_DIM] = tail_ref[s]
            conv_o[s] = proj_scr[base + SLOT_LAST - 2:base + SLOT_LAST + 1, 0:CONV_DIM]

    pass_a = functools.partial(_pass_a, c=c, slot=slot, na=na, proj_scr=proj_scr, cw_ref=cw_ref,
                               cb_ref=cb_ref, gprm_ref=gprm_ref,
                               **{k: v for k, v in ab.items() if k != "mlc_s"})
    pass_b = functools.partial(_pass_b, c=c, slot=slot, nb=nbk, proj_scr=proj_scr, mix_scr=mix_scr,
                               gdnn_ref=gdnn_ref, mln_ref=mln_ref, ssdn_ref=ssdn_ref,
                               gdn_o=gdn_o, mm_o=mm_o, ssd_o=ssd_o, **ab)

    n_a, n_b = tm // (c * na), tm // (c * nbk)
    if slot:
        def body_a(j, carry):
            pass_a(j)
            return carry

        def body_b(j, carry):
            pass_b(j)
            return carry

        lax.fori_loop(0, n_a, body_a, 0)
        lax.fori_loop(0, n_b, body_b, 0)
        out_proj(0, tm)
    else:
        assert n_a % 2 == 0 and n_b % n_a == 0
        half, b_per_a = tm // 2, n_b // n_a
        for nb in range(n_in_blk):
            in_proj(0, half, nb)
        weave(pass_a(0, defer=True), [functools.partial(in_proj, half, tm, nb) for nb in range(n_in_blk)],
              minor_first=True)
        for g in range(1, n_a):
            weave(pass_a(g, defer=True), [functools.partial(pass_b, (g - 1) * b_per_a + k) for k in range(b_per_a)])
        weave([functools.partial(pass_b, (n_a - 1) * b_per_a + k) for k in range(b_per_a)],
              [functools.partial(out_proj, 0, half)])
        out_proj(half, tm)

    def store_ml(s, carry):
        for h in range(ML_HEADS):
            c_mat, n_row = _ml_state_out(mlc_s[s * ML_HEADS + h])
            mc_o[s, h] = c_mat
            mn_o[s, h:h + 1, :] = n_row
        return carry

    if slot:
        lax.fori_loop(0, nseq, store_ml, 0)
    else:
        last_rows = proj_scr[tm:tm + SUBLANES, 0:CONV_DIM]
        proj_scr[0:SUBLANES, 0:CONV_DIM] = last_rows

        @pl.when(t == nt - 1)
        def _():
            conv_o[0] = last_rows[SUBLANES - 3:, :]
            store_ml(0, 0)


def _ffn_kernel(*refs, tm, slot, nt):
    if slot:
        (x_ref, ftail_ref, wup_ref, wdn_ref, fw_ref, fb_ref, npre_ref, npost_ref, _,
         o_ref, gate_o, tails_scr, act_scr) = refs
    else:
        (x_ref, wup_ref, wdn_ref, fw_ref, fb_ref, npre_ref, npost_ref, _,
         o_ref, gate_o, tails_scr, act_scr) = refs
    t = pl.program_id(1)

    if slot:
        tails_scr[...] = jnp.zeros(tails_scr.shape, F32)
    else:
        @pl.when(t == 0)
        def _():
            tails_scr[...] = jnp.zeros(tails_scr.shape, F32)

    x = x_ref[...]
    hn = _rms(x, npre_ref[...]).astype(BF16)
    nblk = D_FF // FF_BLK

    def up(blk):
        return (_dot(hn, wup_ref[:, blk * FF_BLK:(blk + 1) * FF_BLK]),
                _dot(hn, wup_ref[:, D_FF + blk * FF_BLK:D_FF + (blk + 1) * FF_BLK]))

    ahead = up(0)
    for blk in range(nblk):
        cols = slice(blk * FF_BLK, (blk + 1) * FF_BLK)
        gate, val = ahead
        if blk + 1 < nblk:
            ahead = up(blk + 1)
        if slot:
            g3 = gate.reshape(tm // SLOT, SLOT, FF_BLK)
            ft = ftail_ref[:, :, cols]
            rr = lax.broadcasted_iota(jnp.int32, g3.shape, 1)
            g3 = jnp.where(rr == SLOT_FIRST - 2, ft[:, 0:1, :], jnp.where(rr == SLOT_FIRST - 1, ft[:, 1:2, :], g3))
            gate_o[:, :, cols] = g3[:, SLOT_LAST - 1:SLOT_LAST + 1, :]
            gate = g3.reshape(tm, FF_BLK)
        full = jnp.concatenate([tails_scr[:, cols], gate], axis=0)
        conv = fb_ref[0:1, cols] + fw_ref[2:3, cols] * gate
        for j in range(2):
            conv = conv + fw_ref[j:j + 1, cols] * pltpu.roll(full, 2 - j, 0)[SUBLANES:, :]
        last_rows = full[tm:tm + SUBLANES, :]
        tails_scr[:, cols] = last_rows
        act_scr[:, cols] = (_gelu_tanh(conv) * val).astype(BF16)
    o_ref[...] = x + _rms(_dot(act_scr[...], wdn_ref[...]), npost_ref[...])
    if not slot:
        @pl.when(t == nt - 1)
        def _():
            gate_o[0] = tails_scr[SUBLANES - 2:, :]


def _const_spec(shape, layer):
    nd = len(shape)
    return pl.BlockSpec((None,) + tuple(shape), lambda b, t: (layer,) + (0,) * nd,
                        pipeline_mode=pl.Buffered(1))


def _state_dims():
    return ((3, CONV_DIM), (GDN_HEADS, GDN_DK, GDN_DV), (ML_HEADS, ML_DQK, ML_DV), (ML_HEADS, ML_DQK),
            (SUBLANES, 128), (SSD_HEADS, SSD_P, SSD_N))


def _mixer_call(x, layer, prm, bufs, *, slot, states=None, tail=None):
    rows = x.shape[0]
    if slot:
        tm, c, nt, na, nbk = SAMPLE_TM, SLOT, 1, SAMPLE_NA, SAMPLE_NB
        nseq = tm // SLOT
    else:
        tm, c, na, nbk = PROMPT_TM, PROMPT_CHUNK, PROMPT_NA, PROMPT_NB
        nt = 2048 // tm
        nseq = 1
    nch = tm // c
    ngrp = rows // (tm * nt)
    nb = ngrp * nseq
    row_spec = lambda w: pl.BlockSpec((tm, w), lambda b, t: (b * nt + t, 0))
    st_in = lambda *dims: pl.BlockSpec((None, nseq) + dims, lambda b, t: (layer, b) + (0,) * len(dims))

    in_specs = [row_spec(D_MODEL)]
    args = [x]
    if slot:
        in_specs.append(st_in(SLOT_FIRST, CONV_DIM))
        args.append(tail)
    in_specs += [
        _const_spec((D_MODEL, N_IN), layer), _const_spec((D_MODEL, D_MODEL), layer),
        _const_spec((4, CONV_DIM), layer), _const_spec((1, CONV_DIM), layer),
        _const_spec((8, 128), layer), _const_spec((1, D_MODEL), layer), _const_spec((1, D_MODEL), layer),
        _const_spec((1, GDN_DV), layer), _const_spec((1, ML_DV), layer),
        _const_spec((1, SSD_HEADS * SSD_P), layer)]
    args += [prm["w_in"], prm["w_out"], prm["conv_w"], prm["conv_b"], prm["gprm"], prm["norm_mix_pre"],
             prm["norm_mix_post"], prm["gdn_norm"], prm["mlstm_norm"], prm["ssd_norm"]]
    if slot:
        in_specs += [st_in(GDN_HEADS, GDN_DK, GDN_DV), st_in(ML_HEADS, ML_DQK, ML_DV),
                     st_in(ML_HEADS, ML_DQK), st_in(SUBLANES, 128), st_in(SSD_HEADS, SSD_P, SSD_N)]
        args += list(states)
    first = bufs is None
    if first:
        aliases = {}
        st_o = lambda *dims: pl.BlockSpec((DEPTH, nseq) + dims, lambda b, t: (0, b) + (0,) * len(dims))
    else:
        aliases = {len(args) + k: 1 + k for k in range(len(bufs))}
        in_specs += [pl.BlockSpec(memory_space=pl.ANY)] * len(bufs)
        args += list(bufs)
        st_o = st_in

    out_specs = [row_spec(D_MODEL)] + [st_o(*dims) for dims in _state_dims()]
    out_shape = [jax.ShapeDtypeStruct((rows, D_MODEL), F32)] + [
        jax.ShapeDtypeStruct((DEPTH, nb) + dims, F32) for dims in _state_dims()]
    scratch = [
        pltpu.VMEM((tm + SUBLANES, N_IN), F32),
        pltpu.VMEM((tm, D_MODEL), F32),
        pltpu.VMEM((nch, c, 128), F32),
        pltpu.VMEM((nch, c, 128), F32),
        pltpu.VMEM((nch * GDN_HEADS, c + GDN_DK, GDN_DV), F32),
        pltpu.VMEM((nch * GDN_HEADS, c + GDN_DK, GDN_DV), F32),
        pltpu.VMEM((nch * ML_HEADS, c + ML_DQK, 128), F32),
        pltpu.VMEM((nch, c, SSD_HEADS * SSD_P), F32),
        pltpu.VMEM((nch * SSD_HEADS, c, SSD_N), F32),
        pltpu.VMEM((nch * SSD_HEADS, SSD_P, SSD_N), F32),
        pltpu.VMEM((nseq * ML_HEADS, ML_DQK, 128), F32),
    ]
    return pl.pallas_call(
        functools.partial(_mixer_kernel, tm=tm, c=c, slot=slot, nt=nt, na=na, nbk=nbk, first=first),
        grid=(ngrp, nt), in_specs=in_specs, out_specs=out_specs, out_shape=out_shape,
        scratch_shapes=scratch, input_output_aliases=aliases,
        compiler_params=pltpu.CompilerParams(dimension_semantics=("arbitrary", "arbitrary"),
                                             vmem_limit_bytes=VMEM_LIMIT),
        name=("mixer_sample" if slot else "mixer_prompt"),
    )(*args)


def _ffn_call(x, layer, prm, buf, *, slot, tail=None):
    rows = x.shape[0]
    tm = FFN_TM
    nt = 1 if slot else 2048 // tm
    ngrp = rows // (tm * nt)
    row_spec = lambda w: pl.BlockSpec((tm, w), lambda b, t: (b * nt + t, 0))
    in_specs = [row_spec(D_MODEL)]
    args = [x]
    nseq = tm // SLOT if slot else 1
    if slot:
        in_specs.append(pl.BlockSpec((None, nseq, 2, D_FF), lambda b, t: (layer, b, 0, 0)))
        args.append(tail)
    in_specs += [_const_spec((D_MODEL, 2 * D_FF), layer), _const_spec((D_FF, D_MODEL), layer),
                 _const_spec((3, D_FF), layer), _const_spec((1, D_FF), layer),
                 _const_spec((1, D_MODEL), layer), _const_spec((1, D_MODEL), layer)]
    args += [prm["ffn_w_up"], prm["ffn_w_down"], prm["ffn_conv_w"], prm["ffn_conv_b"],
             prm["norm_ffn_pre"], prm["norm_ffn_post"]]
    aliases = {len(args): 1}
    in_specs.append(pl.BlockSpec(memory_space=pl.ANY))
    args.append(buf)
    gate_spec = pl.BlockSpec((None, nseq, 2, D_FF), lambda b, t: (layer, b, 0, 0))
    gate_shape = (DEPTH, ngrp * nseq, 2, D_FF)
    return pl.pallas_call(
        functools.partial(_ffn_kernel, tm=tm, slot=slot, nt=nt),
        grid=(ngrp, nt), in_specs=in_specs, out_specs=[row_spec(D_MODEL), gate_spec],
        out_shape=[jax.ShapeDtypeStruct((rows, D_MODEL), F32), jax.ShapeDtypeStruct(gate_shape, F32)],
        scratch_shapes=[pltpu.VMEM((SUBLANES, D_FF), F32), pltpu.VMEM((tm, D_FF), BF16)],
        input_output_aliases=aliases,
        compiler_params=pltpu.CompilerParams(dimension_semantics=("arbitrary", "arbitrary"),
                                             vmem_limit_bytes=VMEM_LIMIT),
        name=("ffn_sample" if slot else "ffn_prompt"),
    )(*args)


def _prepare_params(norm_mix_pre, norm_mix_post, norm_ffn_pre, norm_ffn_post, w_in, conv_w, conv_b,
                    gdn_a_log, gdn_dt_bias, gdn_norm, mlstm_i_bias, mlstm_f_bias, mlstm_norm,
                    ssd_a_log, ssd_dt_bias, ssd_d, ssd_norm, w_out, ffn_w_up, ffn_conv_w, ffn_conv_b,
                    ffn_w_down):
    w_in_p = jnp.concatenate(
        [w_in[..., :2816], w_in[..., 2824:3848], w_in[..., 3856:4112], w_in[..., 2816:2824],
         w_in[..., 3848:3856], w_in[..., 4112:4116],
         jnp.zeros((DEPTH, D_MODEL, N_IN - 4116), w_in.dtype)], axis=-1).astype(BF16)
    z4 = jnp.zeros((DEPTH, 4), F32)
    pad = jnp.zeros((DEPTH, 128 - 20), F32)
    gprm = jnp.stack(
        [jnp.concatenate([gdn_dt_bias, z4, mlstm_i_bias, mlstm_f_bias, ssd_dt_bias, pad], axis=-1),
         jnp.concatenate([gdn_a_log, z4, z4, z4, ssd_a_log, pad], axis=-1),
         jnp.concatenate([z4, z4, z4, z4, ssd_d, pad], axis=-1)]
        + [jnp.zeros((DEPTH, 128), F32)] * 5, axis=1)
    row = lambda a: a[:, None, :]
    return dict(
        w_in=w_in_p, w_out=w_out.astype(BF16), conv_w=conv_w, conv_b=row(conv_b), gprm=gprm,
        norm_mix_pre=row(norm_mix_pre), norm_mix_post=row(norm_mix_post),
        norm_ffn_pre=row(norm_ffn_pre), norm_ffn_post=row(norm_ffn_post),
        gdn_norm=row(gdn_norm), mlstm_norm=row(mlstm_norm), ssd_norm=row(ssd_norm),
        ffn_w_up=ffn_w_up.astype(BF16), ffn_w_down=ffn_w_down.astype(BF16),
        ffn_conv_w=ffn_conv_w, ffn_conv_b=row(ffn_conv_b))


def kernel(x_prompt, x_sample, state_conv, state_gdn, state_mlstm_c, state_mlstm_n, state_mlstm_m, state_ssd, state_ffn_conv, norm_mix_pre, norm_mix_post, norm_ffn_pre, norm_ffn_post, w_in, conv_w, conv_b, gdn_a_log, gdn_dt_bias, gdn_norm, mlstm_i_bias, mlstm_f_bias, mlstm_norm, ssd_a_log, ssd_dt_bias, ssd_d, ssd_norm, w_out, ffn_w_up, ffn_conv_w, ffn_conv_b, ffn_w_down):
    prm = _prepare_params(norm_mix_pre, norm_mix_post, norm_ffn_pre, norm_ffn_post, w_in, conv_w, conv_b,
                          gdn_a_log, gdn_dt_bias, gdn_norm, mlstm_i_bias, mlstm_f_bias, mlstm_norm,
                          ssd_a_log, ssd_dt_bias, ssd_d, ssd_norm, w_out, ffn_w_up, ffn_conv_w,
                          ffn_conv_b, ffn_w_down)
    bp, lp, _ = x_prompt.shape
    bs, ls, _ = x_sample.shape

    def state_bufs(nseq):
        return None, jnp.zeros((DEPTH, nseq, 2, D_FF), F32)

    def assemble(bufs, gate):
        conv, gdn, mc, mn, mm, ssd = bufs
        return [conv, gdn, mc, mn, mm[:, :, 0, L_MF:L_MF + ML_HEADS], ssd, gate]

    x = x_prompt.reshape(bp * lp, D_MODEL)
    bufs, gate = state_bufs(bp)
    for layer in range(DEPTH):
        x, *bufs = _mixer_call(x, layer, prm, bufs, slot=False)
        x, gate = _ffn_call(x, layer, prm, gate, slot=False)
    p_out = assemble(bufs, gate)
    y_prompt = x.reshape(bp, lp, D_MODEL)

    x = jnp.pad(x_sample, ((0, 0), (SLOT_FIRST, SLOT - SLOT_FIRST - ls), (0, 0))).reshape(bs * SLOT, D_MODEL)
    mm_in = jnp.pad(state_mlstm_m[:, :, None, :],
                    ((0, 0), (0, 0), (0, SUBLANES - 1), (L_MF, 128 - L_MF - ML_HEADS)))
    bufs, gate = state_bufs(bs)
    for layer in range(DEPTH):
        x, *bufs = _mixer_call(x, layer, prm, bufs, slot=True, tail=state_conv,
                               states=(state_gdn, state_mlstm_c, state_mlstm_n, mm_in, state_ssd))
        x, gate = _ffn_call(x, layer, prm, gate, slot=True, tail=state_ffn_conv)
    s_out = assemble(bufs, gate)
    y_sample = x.reshape(bs, SLOT, D_MODEL)[:, SLOT_FIRST:SLOT_LAST + 1]
    return (y_prompt, y_sample, *p_out, *s_out)
```

```python
import functools

import jax
import jax.numpy as jnp
from jax import lax
from jax.experimental import pallas as pl
from jax.experimental.pallas import tpu as pltpu

F32 = jnp.float32
BF16 = jnp.bfloat16

D_MODEL = 1024
DEPTH = 2
GDN_HEADS, GDN_DK, GDN_DV = 4, 128, 128
ML_HEADS, ML_DQK, ML_DV = 4, 64, 64
SSD_HEADS, SSD_P, SSD_GROUPS, SSD_N = 4, 64, 2, 128
D_FF = 2816
EPS = 1e-6

CONV_DIM = 2304
C_GQ, C_GK, C_GV, C_SX, C_SB, C_SC = 0, 512, 1024, 1536, 1792, 2048
C_GG, C_MQ, C_MK, C_MV, C_MO, C_SZ, C_GATE = 2304, 2816, 3072, 3328, 3584, 3840, 4096
N_IN = 4224
IN_NBLK = 1408
L_GA, L_GB, L_MI, L_MF, L_DT = 0, 4, 8, 12, 16

SUBLANES = 8
SLOT = 8
SLOT_FIRST, SLOT_LAST = 3, 6
FF_BLK = 256
VMEM_LIMIT = 56 * 1024 * 1024

PROMPT_TM, PROMPT_CHUNK, PROMPT_NA, PROMPT_NB = 512, 64, 4, 1
SAMPLE_TM, SAMPLE_NA, SAMPLE_NB = 64, 8, 4
FFN_TM = 1024


def _dot(a, b):
    return jnp.dot(a, b, preferred_element_type=F32)


def _dot_nt(a, b):
    return lax.dot_general(a, b, (((1,), (1,)), ((), ())), preferred_element_type=F32)


def _dot_tn(a, b):
    return lax.dot_general(a, b, (((0,), (0,)), ((), ())), preferred_element_type=F32)


def _sigmoid(x):
    return 1.0 / (1.0 + jnp.exp(-x))


def _silu(x):
    return x * _sigmoid(x)


def _rms(x, w):
    return x * lax.rsqrt(jnp.mean(x * x, axis=-1, keepdims=True) + EPS) * w


def _l2n(x):
    return x * lax.rsqrt(jnp.sum(x * x, axis=-1, keepdims=True) + EPS)


def _gelu_tanh(x):
    return 0.5 * x * (1.0 + jnp.tanh(0.7978845608028654 * (x + 0.044715 * (x * x * x))))


def _col(a, l):
    return a[:, l:l + 1]


def _cumsum_rows(tril_b, x):
    hi = x.astype(BF16)
    r1 = x - hi.astype(F32)
    mid = r1.astype(BF16)
    lo = (r1 - mid.astype(F32)).astype(BF16)
    return _dot(tril_b, hi) + _dot(tril_b, mid) + _dot(tril_b, lo)


def _conv_silu(proj_scr, r0, c, col0, width, cw_ref, cb_ref):
    win = proj_scr[pl.ds(r0, c + SUBLANES), col0:col0 + width]
    acc = cw_ref[0:1, col0:col0 + width] * win
    for j in range(1, 4):
        acc = cw_ref[j:j + 1, col0:col0 + width] * win + pltpu.roll(acc, 1, 0)
    return _silu(acc[SUBLANES:, :] + cb_ref[0:1, col0:col0 + width])


def _gates(proj_scr, r0, c, slot, gprm_ref, tril_b):
    graw = proj_scr[pl.ds(r0 + SUBLANES, c), C_GATE:C_GATE + 128]
    lane = lax.broadcasted_iota(jnp.int32, (c, 128), 1)
    z = graw + gprm_ref[0:1, :]
    soft = jnp.log(1.0 + jnp.exp(-jnp.abs(z)))
    sp = jnp.maximum(z, 0.0) + soft
    log_sig = -(jnp.maximum(-z, 0.0) + soft)
    a_neg = -jnp.exp(gprm_ref[1:2, :])
    is_ga = lane < L_GB
    is_gb = (lane >= L_GB) & (lane < L_MI)
    is_mi = (lane >= L_MI) & (lane < L_MF)
    is_mf = (lane >= L_MF) & (lane < L_DT)
    is_dt = (lane >= L_DT) & (lane < L_DT + SSD_HEADS)
    cum_src = jnp.where(is_ga | is_dt, a_neg * sp, jnp.where(is_mf, log_sig, 0.0))
    elem = jnp.where(is_gb, _sigmoid(graw), jnp.where(is_mi, z, jnp.where(is_dt, sp, 0.0)))
    if slot:
        rr = lax.broadcasted_iota(jnp.int32, (c, 128), 0)
        valid = (rr >= SLOT_FIRST) & (rr <= SLOT_LAST)
        cum_src = jnp.where(valid, cum_src, 0.0)
        elem = jnp.where(valid, elem, jnp.where(is_mi, -jnp.inf, 0.0))
    cum = _cumsum_rows(tril_b, cum_src)
    if c == 128:
        return cum, elem, cum.T, elem.T
    parts = [cum, elem]
    if 2 * c < 128:
        parts.append(jnp.zeros((128 - 2 * c, 128), F32))
    zt = jnp.concatenate(parts, axis=0).T
    return cum, elem, zt[:, 0:c], zt[:, c:2 * c]


def _row0(i, c):
    return i * c if isinstance(i, int) else pl.multiple_of(i * c, c)


def _pass_a(j, *, c, slot, na, proj_scr, cw_ref, cb_ref, gprm_ref,
            cum_s, col_s, gqx_s, gob_s, ml_s, sy_s, sce_s, sdh_s, defer=False):
    chunks = [j * na + a for a in range(na)]
    r0s = [_row0(i, c) for i in chunks]
    conv = functools.partial(_conv_silu, proj_scr, c=c, cw_ref=cw_ref, cb_ref=cb_ref)

    ii = lax.broadcasted_iota(jnp.int32, (c, c), 0)
    jj = lax.broadcasted_iota(jnp.int32, (c, c), 1)
    tril = ii >= jj
    strict = ii > jj
    eye = (ii == jj).astype(F32)
    tril_b = tril.astype(BF16)
    lane = lax.broadcasted_iota(jnp.int32, (c, 128), 1)

    gates = [_gates(proj_scr, r0, c, slot, gprm_ref, tril_b) for r0 in r0s]
    cum = [g[0] for g in gates]
    elem = [g[1] for g in gates]
    cum_t = [g[2] for g in gates]
    elem_t = [g[3] for g in gates]
    for a in range(na):
        cum_s[chunks[a]] = cum[a]

    def row_cum(a, l):
        return cum_t[a][l:l + 1, :]

    def row_elem(a, l):
        return elem_t[a][l:l + 1, :]

    it = [(a, h) for a in range(na) for h in range(GDN_HEADS)]
    n = range(len(it))
    gd, ml, sd = {}, {}, {}

    def g_k():
        k = [_l2n(conv(r0=r0s[a], col0=C_GK + h * GDN_DK, width=GDN_DK)) for a, h in it]
        gam_c = [_col(cum[a], L_GA + h) for a, h in it]
        beta_c = [_col(elem[a], L_GB + h) for a, h in it]
        dmat = [jnp.exp(jnp.where(tril, gam_c[x] - row_cum(a, L_GA + h), -jnp.inf))
                for x, (a, h) in enumerate(it)]
        pk = [-jnp.where(strict, _dot_nt(k[x], k[x]) * dmat[x] * beta_c[x], 0.0) for x in n]
        gd.update(k=k, gam_c=gam_c, beta_c=beta_c, dmat=dmat, pk=pk, t_inv=[eye + pk[x] for x in n])

    def g_neumann():
        pk = [_dot(gd["pk"][x], gd["pk"][x]) for x in n]
        gd.update(pk=pk, t_inv=[gd["t_inv"][x] + _dot(gd["t_inv"][x], pk[x]) for x in n])

    def g_q():
        gd["q"] = [_l2n(conv(r0=r0s[a], col0=C_GQ + h * GDN_DK, width=GDN_DK)) * (GDN_DK ** -0.5)
                   for a, h in it]

    def g_v():
        v = [conv(r0=r0s[a], col0=C_GV + h * GDN_DV, width=GDN_DV) for a, h in it]
        eg = [jnp.exp(gd["gam_c"][x]) for x in n]
        gd.update(eg=eg, rhs=[jnp.concatenate([v[x] * gd["beta_c"][x],
                                               gd["k"][x] * (gd["beta_c"][x] * eg[x])], axis=1) for x in n])

    def g_uw():
        gd["uw"] = [_dot(gd["t_inv"][x], gd["rhs"][x]) for x in n]

    def g_qk():
        gd["qk"] = [_dot_nt(gd["q"][x], gd["k"][x]) * gd["dmat"][x] for x in n]
        gd["kd"] = [gd["k"][x] * jnp.exp(gd["gam_c"][x][c - 1:c, :] - gd["gam_c"][x]) for x in n]

    def g_out():
        quw = [_dot(gd["qk"][x], gd["uw"][x]) for x in n]
        kuw = [_dot_tn(gd["kd"][x], gd["uw"][x]) for x in n]
        for x, (a, h) in enumerate(it):
            idx = chunks[a] * GDN_HEADS + h
            gqx_s[idx, 0:c, :] = gd["q"][x] * gd["eg"][x] - quw[x][:, GDN_DV:]
            gqx_s[idx, c:c + GDN_DK, :] = kuw[x][:, GDN_DV:]
            gob_s[idx, 0:c, :] = quw[x][:, :GDN_DV]
            gob_s[idx, c:c + GDN_DK, :] = kuw[x][:, :GDN_DV]

    def piece(a, base, h):
        return proj_scr[pl.ds(r0s[a] + SUBLANES, c), base + h * ML_DQK:base + (h + 1) * ML_DQK]

    def m_1():
        mq = [piece(a, C_MQ, h) for a, h in it]
        mk = [piece(a, C_MK, h) * (ML_DQK ** -0.5) for a, h in it]
        b_c = [_col(cum[a], L_MF + h) for a, h in it]
        d = [jnp.where(tril, b_c[x] - row_cum(a, L_MF + h) + row_elem(a, L_MI + h), -jnp.inf)
             for x, (a, h) in enumerate(it)]
        dmax = [jnp.max(d[x], axis=-1, keepdims=True) for x in n]
        dsafe = [jnp.where(dmax[x] == -jnp.inf, 0.0, dmax[x]) for x in n]
        s0 = [_dot_nt(mq[x], mk[x]) * jnp.exp(d[x] - dsafe[x]) for x in n]
        for a in range(na):
            cols = jnp.zeros((c, 128), F32)
            for h in range(ML_HEADS):
                cols = jnp.where(lane == L_MF + h, dmax[a * ML_HEADS + h], cols)
            col_s[chunks[a]] = cols
        ml.update(mk=mk, b_c=b_c, dsafe=dsafe, s0=s0)

    def m_2():
        mv = [piece(a, C_MV, h) for a, h in it]
        i_c = [_col(elem[a], L_MI + h) for a, h in it]
        ones_col = (lax.broadcasted_iota(jnp.int32, (c, 128 - ML_DV), 1) == 0).astype(F32)
        v_aug = [jnp.concatenate([mv[x], ones_col], axis=1) for x in n]
        num0 = [_dot(ml["s0"][x], v_aug[x]) for x in n]
        b_c, dsafe = ml["b_c"], ml["dsafe"]
        kw0 = [ml["mk"][x] * jnp.exp(b_c[x][c - 1:c, :] - b_c[x] + i_c[x] - dsafe[x][c - 1:c, :]) for x in n]
        kv0 = [_dot_tn(kw0[x], v_aug[x]) for x in n]
        for x, (a, h) in enumerate(it):
            idx = chunks[a] * ML_HEADS + h
            ml_s[idx, 0:c, :] = num0[x]
            ml_s[idx, c:c + ML_DQK, :] = kv0[x]

    rep = SSD_HEADS // SSD_GROUPS
    gi = [(a, g) for a in range(na) for g in range(SSD_GROUPS)]
    grp = [a * SSD_GROUPS + h // rep for a, h in it]

    def s_1():
        bg = [conv(r0=r0s[a], col0=C_SB + g * SSD_N, width=SSD_N) for a, g in gi]
        cg = [conv(r0=r0s[a], col0=C_SC + g * SSD_N, width=SSD_N) for a, g in gi]
        sd.update(bg=bg, cg=cg, cb_raw=[_dot_nt(cg[y], bg[y]) for y in range(len(gi))])

    def s_2():
        xg = [conv(r0=r0s[a], col0=C_SX + g * rep * SSD_P, width=rep * SSD_P) for a, g in gi]
        xs = [xg[grp[x]][:, (h % rep) * SSD_P:(h % rep + 1) * SSD_P] for x, (a, h) in enumerate(it)]
        sg_c = [_col(cum[a], L_DT + h) for a, h in it]
        dt_c = [_col(elem[a], L_DT + h) for a, h in it]
        cb = [sd["cb_raw"][grp[x]] * jnp.exp(jnp.where(tril, sg_c[x] - row_cum(a, L_DT + h), -jnp.inf))
              * row_elem(a, L_DT + h) for x, (a, h) in enumerate(it)]
        y0 = [_dot(cb[x], xs[x]) + gprm_ref[2:3, L_DT + h:L_DT + h + 1] * xs[x]
              for x, (a, h) in enumerate(it)]
        dh = [_dot_tn(xs[x] * (jnp.exp(sg_c[x][c - 1:c, :] - sg_c[x]) * dt_c[x]), sd["bg"][grp[x]]) for x in n]
        for x, (a, h) in enumerate(it):
            idx = chunks[a] * SSD_HEADS + h
            sce_s[idx] = sd["cg"][grp[x]] * jnp.exp(sg_c[x])
            sdh_s[idx] = dh[x]
        for a in range(na):
            sy_s[chunks[a]] = jnp.concatenate(y0[a * SSD_HEADS:(a + 1) * SSD_HEADS], axis=1)

    chain = [g_k] + [g_neumann] * (c.bit_length() - 2) + [g_uw]
    if c >= 64:
        fill = [m_1, m_2, s_1, s_2, g_q, g_v]
    else:
        fill = [g_q, g_v, m_1, m_2, s_1, s_2]
    order = []
    for pos, link in enumerate(chain):
        order.append(link)
        if pos < len(fill):
            order.append(fill[pos])
    order += fill[len(chain):] + [g_qk, g_out]
    if defer:
        return order
    for stage in order:
        stage()


def _pass_b(j, *, c, slot, nb, proj_scr, mix_scr, gdnn_ref, mln_ref, ssdn_ref,
            cum_s, col_s, gqx_s, gob_s, ml_s, sy_s, sce_s, sdh_s, mlc_s, gdn_o, mm_o, ssd_o):
    hs = range(GDN_HEADS)
    lane_row = lax.broadcasted_iota(jnp.int32, (1, 128), 1)
    is_m = (lane_row >= L_MF) & (lane_row < L_MF + ML_HEADS)

    def load(seq):
        return dict(s=[gdn_o[seq, h] for h in hs], c=[mlc_s[seq * ML_HEADS + h] for h in hs],
                    h=[ssd_o[seq, h] for h in hs], m=mm_o[seq, 0:1, :])

    def store(seq, st):
        for h in hs:
            gdn_o[seq, h] = st["s"][h]
            mlc_s[seq * ML_HEADS + h] = st["c"][h]
            ssd_o[seq, h] = st["h"][h]
        mm_o[seq, 0:1, :] = st["m"]

    def advance(i, st):
        prow = pl.ds(_row0(i, c) + SUBLANES, c)
        cum = cum_s[i]
        dmx = col_s[i]
        last = cum[c - 1:c, :]
        e_last = jnp.exp(last)
        mq = [proj_scr[prow, C_MQ + h * ML_DQK:C_MQ + (h + 1) * ML_DQK] for h in hs]
        r = [_dot(gqx_s[i * GDN_HEADS + h], st["s"][h]) for h in hs]
        full = [_dot(mq[h], st["c"][h]) for h in hs]
        yh = [_dot_nt(sce_s[i * SSD_HEADS + h], st["h"][h]) for h in hs]
        inter = cum + st["m"]
        m_t = jnp.maximum(inter, dmx)
        w_intra = jnp.exp(dmx - m_t)
        m_new = jnp.where(is_m, m_t[c - 1:c, :], 0.0)
        w_c_row = jnp.exp(last + st["m"] - m_new)
        w_l_row = w_intra[c - 1:c, :]
        ob = [gob_s[i * GDN_HEADS + h] for h in hs]
        blk = [ml_s[i * ML_HEADS + h] for h in hs]
        new = dict(
            s=[_col(e_last, L_GA + h) * st["s"][h] - r[h][c:, :] + ob[h][c:, :] for h in hs],
            c=[_col(w_c_row, L_MF + h) * st["c"][h] + _col(w_l_row, L_MF + h) * blk[h][c:c + ML_DQK, :]
               for h in hs],
            h=[_col(e_last, L_DT + h) * st["h"][h] + sdh_s[i * SSD_HEADS + h] for h in hs],
            m=m_new)
        return new, dict(i=i, r=r, full=full, yh=yh, ob=ob, blk=blk, inter=inter, m_t=m_t, w_intra=w_intra)

    def outputs(ctx):
        i = ctx["i"]
        r0 = _row0(i, c)
        rows = pl.ds(r0, c)
        prow = pl.ds(r0 + SUBLANES, c)
        w_inter = jnp.exp(ctx["inter"] - ctx["m_t"])
        e_neg_m = jnp.exp(-ctx["m_t"])
        for h in hs:
            gg = proj_scr[prow, C_GG + h * GDN_DV:C_GG + (h + 1) * GDN_DV]
            mix_scr[rows, h * GDN_DV:(h + 1) * GDN_DV] = (
                _rms(ctx["r"][h][:c, :] + ctx["ob"][h][:c, :], gdnn_ref[...]) * _silu(gg))
        h_parts = []
        for h in hs:
            fl = (ctx["full"][h] * _col(w_inter, L_MF + h)
                  + _col(ctx["w_intra"], L_MF + h) * ctx["blk"][h][0:c, :])
            den = jnp.maximum(jnp.abs(fl[:, ML_DV:ML_DV + 1]), _col(e_neg_m, L_MF + h))
            mo = proj_scr[prow, C_MO + h * ML_DV:C_MO + (h + 1) * ML_DV]
            h_parts.append(_rms(_sigmoid(mo) * (fl[:, :ML_DV] / den), mln_ref[...]))
        mix_scr[rows, GDN_HEADS * GDN_DV:GDN_HEADS * GDN_DV + ML_HEADS * ML_DV] = (
            jnp.concatenate(h_parts, axis=1))
        sz = proj_scr[prow, C_SZ:C_SZ + SSD_HEADS * SSD_P]
        y_all = (sy_s[i] + jnp.concatenate(ctx["yh"], axis=1)) * _silu(sz)
        mix_scr[rows, GDN_HEADS * GDN_DV + ML_HEADS * ML_DV:D_MODEL] = _rms(y_all, ssdn_ref[...])

    ctxs = []
    st = None
    for a in range(nb):
        i = j * nb + a
        seq = i if slot else 0
        if slot or a == 0:
            st = load(seq)
        st, ctx = advance(i, st)
        if slot or a == nb - 1:
            store(seq, st)
        ctxs.append(ctx)
    for ctx in ctxs:
        outputs(ctx)


def _ml_state_in(c_mat, n_row):
    eye = (lax.broadcasted_iota(jnp.int32, (ML_DQK, ML_DQK), 0)
           == lax.broadcasted_iota(jnp.int32, (ML_DQK, ML_DQK), 1)).astype(F32)
    n_col = jnp.sum(eye * n_row, axis=-1, keepdims=True)
    first = lax.broadcasted_iota(jnp.int32, (ML_DQK, 128 - ML_DV), 1) == 0
    return jnp.concatenate([c_mat, jnp.where(first, n_col, 0.0)], axis=1)


def _ml_state_out(c_aug):
    eye = (lax.broadcasted_iota(jnp.int32, (ML_DQK, ML_DQK), 0)
           == lax.broadcasted_iota(jnp.int32, (ML_DQK, ML_DQK), 1)).astype(F32)
    n_row = jnp.sum(eye * c_aug[:, ML_DV:ML_DV + 1], axis=0, keepdims=True)
    return c_aug[:, :ML_DV], n_row


def _mixer_kernel(*refs, tm, c, slot, nt, na, nbk, first):
    n_in = (17 if slot else 11) + (0 if first else 6)
    ins, (o_ref, *state_o), (proj_scr, mix_scr, *ab) = refs[:n_in], refs[n_in:n_in + 7], refs[n_in + 7:]
    if slot:
        (x_ref, tail_ref, win_ref, wout_ref, cw_ref, cb_ref, gprm_ref, npre_ref, npost_ref, gdnn_ref,
         mln_ref, ssdn_ref, gdn_i, mc_i, mn_i, mm_i, ssd_i) = ins[:17]
    else:
        (x_ref, win_ref, wout_ref, cw_ref, cb_ref, gprm_ref, npre_ref, npost_ref, gdnn_ref,
         mln_ref, ssdn_ref) = ins[:11]
    if first:
        for ref in state_o:
            ref[1] = jnp.zeros(ref.shape[1:], F32)
        state_o = [ref.at[0] for ref in state_o]
    conv_o, gdn_o, mc_o, mn_o, mm_o, ssd_o = state_o
    names = ("cum_s", "col_s", "gqx_s", "gob_s", "ml_s", "sy_s", "sce_s", "sdh_s", "mlc_s")
    ab = dict(zip(names, ab))
    mlc_s = ab["mlc_s"]
    nseq = gdn_o.shape[0]
    t = pl.program_id(1)

    if slot:
        proj_scr[0:SUBLANES, :] = jnp.zeros((SUBLANES, N_IN), F32)
        gdn_o[...] = gdn_i[...]
        mm_o[...] = mm_i[...]
        ssd_o[...] = ssd_i[...]

        def load_ml(s, carry):
            for h in range(ML_HEADS):
                mlc_s[s * ML_HEADS + h] = _ml_state_in(mc_i[s, h], mn_i[s, h:h + 1, :])
            return carry

        lax.fori_loop(0, nseq, load_ml, 0)
    else:
        @pl.when(t == 0)
        def _():
            proj_scr[0:SUBLANES, :] = jnp.zeros((SUBLANES, N_IN), F32)
            gdn_o[...] = jnp.zeros(gdn_o.shape, F32)
            mlc_s[...] = jnp.zeros(mlc_s.shape, F32)
            mm_o[...] = jnp.zeros(mm_o.shape, F32)
            ssd_o[...] = jnp.zeros(ssd_o.shape, F32)

    x = x_ref[...]
    hn = _rms(x, npre_ref[...]).astype(BF16)

    def in_proj(r_lo, r_hi, nb):
        cols = slice(nb * IN_NBLK, (nb + 1) * IN_NBLK)
        proj_scr[SUBLANES + r_lo:SUBLANES + r_hi, cols] = _dot(hn[r_lo:r_hi, :], win_ref[:, cols])

    def out_proj(r_lo, r_hi):
        out = _dot(mix_scr[r_lo:r_hi, :].astype(BF16), wout_ref[...])
        o_ref[r_lo:r_hi, :] = x[r_lo:r_hi, :] + _rms(out, npost_ref[...])

    def weave(major, minor, minor_first=False):
        major, minor = list(major), list(minor)
        stride = max(1, len(major) // max(1, len(minor)))
        while major or minor:
            if minor and minor_first:
                minor.pop(0)()
            for step in major[:stride]:
                step()
            major = major[stride:]
            if minor and not minor_first:
                minor.pop(0)()

    n_in_blk = N_IN // IN_NBLK
    if slot:
        for nb in range(n_in_blk):
            in_proj(0, tm, nb)
        for s in range(nseq):
            base = SUBLANES + s * SLOT
            proj_scr[base:base + SLOT_FIRST, 0:CONV_DIM] = tail_ref[s]
            conv_o[s] = proj_scr[base + SLOT_LAST - 2:base + SLOT_LAST + 1, 0:CONV_DIM]

    pass_a = functools.partial(_pass_a, c=c, slot=slot, na=na, proj_scr=proj_scr, cw_ref=cw_ref,
                               cb_ref=cb_ref, gprm_ref=gprm_ref,
                               **{k: v for k, v in ab.items() if k != "mlc_s"})
    pass_b = functools.partial(_pass_b, c=c, slot=slot, nb=nbk, proj_scr=proj_scr, mix_scr=mix_scr,
                               gdnn_ref=gdnn_ref, mln_ref=mln_ref, ssdn_ref=ssdn_ref,
                               gdn_o=gdn_o, mm_o=mm_o, ssd_o=ssd_o, **ab)

    n_a, n_b = tm // (c * na), tm // (c * nbk)
    if slot:
        def body_a(j, carry):
            pass_a(j)
            return carry

        def body_b(j, carry):
            pass_b(j)
            return carry

        lax.fori_loop(0, n_a, body_a, 0)
        lax.fori_loop(0, n_b, body_b, 0)
        out_proj(0, tm)
    else:
        assert n_a % 2 == 0 and n_b % n_a == 0
        half, b_per_a = tm // 2, n_b // n_a
        for nb in range(n_in_blk):
            in_proj(0, half, nb)
        weave(pass_a(0, defer=True), [functools.partial(in_proj, half, tm, nb) for nb in range(n_in_blk)],
              minor_first=True)
        for g in range(1, n_a):
            weave(pass_a(g, defer=True), [functools.partial(pass_b, (g - 1) * b_per_a + k) for k in range(b_per_a)])
        weave([functools.partial(pass_b, (n_a - 1) * b_per_a + k) for k in range(b_per_a)],
              [functools.partial(out_proj, 0, half)])
        out_proj(half, tm)

    def store_ml(s, carry):
        for h in range(ML_HEADS):
            c_mat, n_row = _ml_state_out(mlc_s[s * ML_HEADS + h])
            mc_o[s, h] = c_mat
            mn_o[s, h:h + 1, :] = n_row
        return carry

    if slot:
        lax.fori_loop(0, nseq, store_ml, 0)
    else:
        last_rows = proj_scr[tm:tm + SUBLANES, 0:CONV_DIM]
        proj_scr[0:SUBLANES, 0:CONV_DIM] = last_rows

        @pl.when(t == nt - 1)
        def _():
            conv_o[0] = last_rows[SUBLANES - 3:, :]
            store_ml(0, 0)


def _ffn_kernel(*refs, tm, slot, nt):
    if slot:
        (x_ref, ftail_ref, wup_ref, wdn_ref, fw_ref, fb_ref, npre_ref, npost_ref, _,
         o_ref, gate_o, tails_scr, act_scr) = refs
    else:
        (x_ref, wup_ref, wdn_ref, fw_ref, fb_ref, npre_ref, npost_ref, _,
         o_ref, gate_o, tails_scr, act_scr) = refs
    t = pl.program_id(1)

    if slot:
        tails_scr[...] = jnp.zeros(tails_scr.shape, F32)
    else:
        @pl.when(t == 0)
        def _():
            tails_scr[...] = jnp.zeros(tails_scr.shape, F32)

    x = x_ref[...]
    hn = _rms(x, npre_ref[...]).astype(BF16)
    nblk = D_FF // FF_BLK

    def up(blk):
        return (_dot(hn, wup_ref[:, blk * FF_BLK:(blk + 1) * FF_BLK]),
                _dot(hn, wup_ref[:, D_FF + blk * FF_BLK:D_FF + (blk + 1) * FF_BLK]))

    ahead = up(0)
    for blk in range(nblk):
        cols = slice(blk * FF_BLK, (blk + 1) * FF_BLK)
        gate, val = ahead
        if blk + 1 < nblk:
            ahead = up(blk + 1)
        if slot:
            g3 = gate.reshape(tm // SLOT, SLOT, FF_BLK)
            ft = ftail_ref[:, :, cols]
            rr = lax.broadcasted_iota(jnp.int32, g3.shape, 1)
            g3 = jnp.where(rr == SLOT_FIRST - 2, ft[:, 0:1, :], jnp.where(rr == SLOT_FIRST - 1, ft[:, 1:2, :], g3))
            gate_o[:, :, cols] = g3[:, SLOT_LAST - 1:SLOT_LAST + 1, :]
            gate = g3.reshape(tm, FF_BLK)
        full = jnp.concatenate([tails_scr[:, cols], gate], axis=0)
        conv = fb_ref[0:1, cols] + fw_ref[2:3, cols] * gate
        for j in range(2):
            conv = conv + fw_ref[j:j + 1, cols] * pltpu.roll(full, 2 - j, 0)[SUBLANES:, :]
        last_rows = full[tm:tm + SUBLANES, :]
        tails_scr[:, cols] = last_rows
        act_scr[:, cols] = (_gelu_tanh(conv) * val).astype(BF16)
    o_ref[...] = x + _rms(_dot(act_scr[...], wdn_ref[...]), npost_ref[...])
    if not slot:
        @pl.when(t == nt - 1)
        def _():
            gate_o[0] = tails_scr[SUBLANES - 2:, :]


def _const_spec(shape, layer):
    nd = len(shape)
    return pl.BlockSpec((None,) + tuple(shape), lambda b, t: (layer,) + (0,) * nd,
                        pipeline_mode=pl.Buffered(1))


def _state_dims():
    return ((3, CONV_DIM), (GDN_HEADS, GDN_DK, GDN_DV), (ML_HEADS, ML_DQK, ML_DV), (ML_HEADS, ML_DQK),
            (SUBLANES, 128), (SSD_HEADS, SSD_P, SSD_N))


def _mixer_call(x, layer, prm, bufs, *, slot, states=None, tail=None):
    rows = x.shape[0]
    if slot:
        tm, c, nt, na, nbk = SAMPLE_TM, SLOT, 1, SAMPLE_NA, SAMPLE_NB
        nseq = tm // SLOT
    else:
        tm, c, na, nbk = PROMPT_TM, PROMPT_CHUNK, PROMPT_NA, PROMPT_NB
        nt = 2048 // tm
        nseq = 1
    nch = tm // c
    ngrp = rows // (tm * nt)
    nb = ngrp * nseq
    row_spec = lambda w: pl.BlockSpec((tm, w), lambda b, t: (b * nt + t, 0))
    st_in = lambda *dims: pl.BlockSpec((None, nseq) + dims, lambda b, t: (layer, b) + (0,) * len(dims))

    in_specs = [row_spec(D_MODEL)]
    args = [x]
    if slot:
        in_specs.append(st_in(SLOT_FIRST, CONV_DIM))
        args.append(tail)
    in_specs += [
        _const_spec((D_MODEL, N_IN), layer), _const_spec((D_MODEL, D_MODEL), layer),
        _const_spec((4, CONV_DIM), layer), _const_spec((1, CONV_DIM), layer),
        _const_spec((8, 128), layer), _const_spec((1, D_MODEL), layer), _const_spec((1, D_MODEL), layer),
        _const_spec((1, GDN_DV), layer), _const_spec((1, ML_DV), layer),
        _const_spec((1, SSD_HEADS * SSD_P), layer)]
    args += [prm["w_in"], prm["w_out"], prm["conv_w"], prm["conv_b"], prm["gprm"], prm["norm_mix_pre"],
             prm["norm_mix_post"], prm["gdn_norm"], prm["mlstm_norm"], prm["ssd_norm"]]
    if slot:
        in_specs += [st_in(GDN_HEADS, GDN_DK, GDN_DV), st_in(ML_HEADS, ML_DQK, ML_DV),
                     st_in(ML_HEADS, ML_DQK), st_in(SUBLANES, 128), st_in(SSD_HEADS, SSD_P, SSD_N)]
        args += list(states)
    first = bufs is None
    if first:
        aliases = {}
        st_o = lambda *dims: pl.BlockSpec((DEPTH, nseq) + dims, lambda b, t: (0, b) + (0,) * len(dims))
    else:
        aliases = {len(args) + k: 1 + k for k in range(len(bufs))}
        in_specs += [pl.BlockSpec(memory_space=pl.ANY)] * len(bufs)
        args += list(bufs)
        st_o = st_in

    out_specs = [row_spec(D_MODEL)] + [st_o(*dims) for dims in _state_dims()]
    out_shape = [jax.ShapeDtypeStruct((rows, D_MODEL), F32)] + [
        jax.ShapeDtypeStruct((DEPTH, nb) + dims, F32) for dims in _state_dims()]
    scratch = [
        pltpu.VMEM((tm + SUBLANES, N_IN), F32),
        pltpu.VMEM((tm, D_MODEL), F32),
        pltpu.VMEM((nch, c, 128), F32),
        pltpu.VMEM((nch, c, 128), F32),
        pltpu.VMEM((nch * GDN_HEADS, c + GDN_DK, GDN_DV), F32),
        pltpu.VMEM((nch * GDN_HEADS, c + GDN_DK, GDN_DV), F32),
        pltpu.VMEM((nch * ML_HEADS, c + ML_DQK, 128), F32),
        pltpu.VMEM((nch, c, SSD_HEADS * SSD_P), F32),
        pltpu.VMEM((nch * SSD_HEADS, c, SSD_N), F32),
        pltpu.VMEM((nch * SSD_HEADS, SSD_P, SSD_N), F32),
        pltpu.VMEM((nseq * ML_HEADS, ML_DQK, 128), F32),
    ]
    return pl.pallas_call(
        functools.partial(_mixer_kernel, tm=tm, c=c, slot=slot, nt=nt, na=na, nbk=nbk, first=first),
        grid=(ngrp, nt), in_specs=in_specs, out_specs=out_specs, out_shape=out_shape,
        scratch_shapes=scratch, input_output_aliases=aliases,
        compiler_params=pltpu.CompilerParams(dimension_semantics=("arbitrary", "arbitrary"),
                                             vmem_limit_bytes=VMEM_LIMIT),
        name=("mixer_sample" if slot else "mixer_prompt"),
    )(*args)


def _ffn_call(x, layer, prm, buf, *, slot, tail=None):
    rows = x.shape[0]
    tm = FFN_TM
    nt = 1 if slot else 2048 // tm
    ngrp = rows // (tm * nt)
    row_spec = lambda w: pl.BlockSpec((tm, w), lambda b, t: (b * nt + t, 0))
    in_specs = [row_spec(D_MODEL)]
    args = [x]
    nseq = tm // SLOT if slot else 1
    if slot:
        in_specs.append(pl.BlockSpec((None, nseq, 2, D_FF), lambda b, t: (layer, b, 0, 0)))
        args.append(tail)
    in_specs += [_const_spec((D_MODEL, 2 * D_FF), layer), _const_spec((D_FF, D_MODEL), layer),
                 _const_spec((3, D_FF), layer), _const_spec((1, D_FF), layer),
                 _const_spec((1, D_MODEL), layer), _const_spec((1, D_MODEL), layer)]
    args += [prm["ffn_w_up"], prm["ffn_w_down"], prm["ffn_conv_w"], prm["ffn_conv_b"],
             prm["norm_ffn_pre"], prm["norm_ffn_post"]]
    aliases = {len(args): 1}
    in_specs.append(pl.BlockSpec(memory_space=pl.ANY))
    args.append(buf)
    gate_spec = pl.BlockSpec((None, nseq, 2, D_FF), lambda b, t: (layer, b, 0, 0))
    gate_shape = (DEPTH, ngrp * nseq, 2, D_FF)
    return pl.pallas_call(
        functools.partial(_ffn_kernel, tm=tm, slot=slot, nt=nt),
        grid=(ngrp, nt), in_specs=in_specs, out_specs=[row_spec(D_MODEL), gate_spec],
        out_shape=[jax.ShapeDtypeStruct((rows, D_MODEL), F32), jax.ShapeDtypeStruct(gate_shape, F32)],
        scratch_shapes=[pltpu.VMEM((SUBLANES, D_FF), F32), pltpu.VMEM((tm, D_FF), BF16)],
        input_output_aliases=aliases,
        compiler_params=pltpu.CompilerParams(dimension_semantics=("arbitrary", "arbitrary"),
                                             vmem_limit_bytes=VMEM_LIMIT),
        name=("ffn_sample" if slot else "ffn_prompt"),
    )(*args)


def _prepare_params(norm_mix_pre, norm_mix_post, norm_ffn_pre, norm_ffn_post, w_in, conv_w, conv_b,
                    gdn_a_log, gdn_dt_bias, gdn_norm, mlstm_i_bias, mlstm_f_bias, mlstm_norm,
                    ssd_a_log, ssd_dt_bias, ssd_d, ssd_norm, w_out, ffn_w_up, ffn_conv_w, ffn_conv_b,
                    ffn_w_down):
    w_in_p = jnp.concatenate(
        [w_in[..., :2816], w_in[..., 2824:3848], w_in[..., 3856:4112], w_in[..., 2816:2824],
         w_in[..., 3848:3856], w_in[..., 4112:4116],
         jnp.zeros((DEPTH, D_MODEL, N_IN - 4116), w_in.dtype)], axis=-1).astype(BF16)
    z4 = jnp.zeros((DEPTH, 4), F32)
    pad = jnp.zeros((DEPTH, 128 - 20), F32)
    gprm = jnp.stack(
        [jnp.concatenate([gdn_dt_bias, z4, mlstm_i_bias, mlstm_f_bias, ssd_dt_bias, pad], axis=-1),
         jnp.concatenate([gdn_a_log, z4, z4, z4, ssd_a_log, pad], axis=-1),
         jnp.concatenate([z4, z4, z4, z4, ssd_d, pad], axis=-1)]
        + [jnp.zeros((DEPTH, 128), F32)] * 5, axis=1)
    row = lambda a: a[:, None, :]
    return dict(
        w_in=w_in_p, w_out=w_out.astype(BF16), conv_w=conv_w, conv_b=row(conv_b), gprm=gprm,
        norm_mix_pre=row(norm_mix_pre), norm_mix_post=row(norm_mix_post),
        norm_ffn_pre=row(norm_ffn_pre), norm_ffn_post=row(norm_ffn_post),
        gdn_norm=row(gdn_norm), mlstm_norm=row(mlstm_norm), ssd_norm=row(ssd_norm),
        ffn_w_up=ffn_w_up.astype(BF16), ffn_w_down=ffn_w_down.astype(BF16),
        ffn_conv_w=ffn_conv_w, ffn_conv_b=row(ffn_conv_b))


def kernel(x_prompt, x_sample, state_conv, state_gdn, state_mlstm_c, state_mlstm_n, state_mlstm_m, state_ssd, state_ffn_conv, norm_mix_pre, norm_mix_post, norm_ffn_pre, norm_ffn_post, w_in, conv_w, conv_b, gdn_a_log, gdn_dt_bias, gdn_norm, mlstm_i_bias, mlstm_f_bias, mlstm_norm, ssd_a_log, ssd_dt_bias, ssd_d, ssd_norm, w_out, ffn_w_up, ffn_conv_w, ffn_conv_b, ffn_w_down):
    prm = _prepare_params(norm_mix_pre, norm_mix_post, norm_ffn_pre, norm_ffn_post, w_in, conv_w, conv_b,
                          gdn_a_log, gdn_dt_bias, gdn_norm, mlstm_i_bias, mlstm_f_bias, mlstm_norm,
                          ssd_a_log, ssd_dt_bias, ssd_d, ssd_norm, w_out, ffn_w_up, ffn_conv_w,
                          ffn_conv_b, ffn_w_down)
    bp, lp, _ = x_prompt.shape
    bs, ls, _ = x_sample.shape

    def state_bufs(nseq):
        return None, jnp.zeros((DEPTH, nseq, 2, D_FF), F32)

    def assemble(bufs, gate):
        conv, gdn, mc, mn, mm, ssd = bufs
        return [conv, gdn, mc, mn, mm[:, :, 0, L_MF:L_MF + ML_HEADS], ssd, gate]

    x = x_prompt.reshape(bp * lp, D_MODEL)
    bufs, gate = state_bufs(bp)
    for layer in range(DEPTH):
        x, *bufs = _mixer_call(x, layer, prm, bufs, slot=False)
        x, gate = _ffn_call(x, layer, prm, gate, slot=False)
    p_out = assemble(bufs, gate)
    y_prompt = x.reshape(bp, lp, D_MODEL)

    x = jnp.pad(x_sample, ((0, 0), (SLOT_FIRST, SLOT - SLOT_FIRST - ls), (0, 0))).reshape(bs * SLOT, D_MODEL)
    mm_in = jnp.pad(state_mlstm_m[:, :, None, :],
                    ((0, 0), (0, 0), (0, SUBLANES - 1), (L_MF, 128 - L_MF - ML_HEADS)))
    bufs, gate = state_bufs(bs)
    for layer in range(DEPTH):
        x, *bufs = _mixer_call(x, layer, prm, bufs, slot=True, tail=state_conv,
                               states=(state_gdn, state_mlstm_c, state_mlstm_n, mm_in, state_ssd))
        x, gate = _ffn_call(x, layer, prm, gate, slot=True, tail=state_ffn_conv)
    s_out = assemble(bufs, gate)
    y_sample = x.reshape(bs, SLOT, D_MODEL)[:, SLOT_FIRST:SLOT_LAST + 1]
    return (y_prompt, y_sample, *p_out, *s_out)
```

```python
import functools

import jax
import jax.numpy as jnp
from jax import lax
from jax.experimental import pallas as pl
from jax.experimental.pallas import tpu as pltpu

F32 = jnp.float32
BF16 = jnp.bfloat16

D_MODEL = 1024
DEPTH = 2
GDN_HEADS, GDN_DK, GDN_DV = 4, 128, 128
ML_HEADS, ML_DQK, ML_DV = 4, 64, 64
SSD_HEADS, SSD_P, SSD_GROUPS, SSD_N = 4, 64, 2, 128
D_FF = 2816
EPS = 1e-6

CONV_DIM = 2304
C_GQ, C_GK, C_GV, C_SX, C_SB, C_SC = 0, 512, 1024, 1536, 1792, 2048
C_GG, C_MQ, C_MK, C_MV, C_MO, C_SZ, C_GATE = 2304, 2816, 3072, 3328, 3584, 3840, 4096
N_IN = 4224
IN_NBLK = 1408
L_GA, L_GB, L_MI, L_MF, L_DT = 0, 4, 8, 12, 16

SUBLANES = 8
SLOT = 8
SLOT_FIRST, SLOT_LAST = 3, 6
FF_BLK = 256
VMEM_LIMIT = 56 * 1024 * 1024

PROMPT_TM, PROMPT_CHUNK, PROMPT_NA, PROMPT_NB = 512, 64, 4, 1
SAMPLE_TM, SAMPLE_NA, SAMPLE_NB = 64, 8, 8
FFN_TM = 1024


def _dot(a, b):
    return jnp.dot(a, b, preferred_element_type=F32)


def _dot_nt(a, b):
    return lax.dot_general(a, b, (((1,), (1,)), ((), ())), preferred_element_type=F32)


def _dot_tn(a, b):
    return lax.dot_general(a, b, (((0,), (0,)), ((), ())), preferred_element_type=F32)


def _sigmoid(x):
    return 1.0 / (1.0 + jnp.exp(-x))


def _silu(x):
    return x * _sigmoid(x)


def _rms(x, w):
    return x * lax.rsqrt(jnp.mean(x * x, axis=-1, keepdims=True) + EPS) * w


def _l2n(x):
    return x * lax.rsqrt(jnp.sum(x * x, axis=-1, keepdims=True) + EPS)


def _gelu_tanh(x):
    return 0.5 * x * (1.0 + jnp.tanh(0.7978845608028654 * (x + 0.044715 * (x * x * x))))


def _col(a, l):
    return a[:, l:l + 1]


def _cumsum_rows(tril_b, x):
    hi = x.astype(BF16)
    r1 = x - hi.astype(F32)
    mid = r1.astype(BF16)
    lo = (r1 - mid.astype(F32)).astype(BF16)
    return _dot(tril_b, hi) + _dot(tril_b, mid) + _dot(tril_b, lo)


def _conv_silu(proj_scr, r0, c, col0, width, cw_ref, cb_ref):
    win = proj_scr[pl.ds(r0, c + SUBLANES), col0:col0 + width]
    acc = cw_ref[0:1, col0:col0 + width] * win
    for j in range(1, 4):
        acc = cw_ref[j:j + 1, col0:col0 + width] * win + pltpu.roll(acc, 1, 0)
    return _silu(acc[SUBLANES:, :] + cb_ref[0:1, col0:col0 + width])


def _gates(proj_scr, r0, c, slot, gprm_ref, tril_b):
    graw = proj_scr[pl.ds(r0 + SUBLANES, c), C_GATE:C_GATE + 128]
    lane = lax.broadcasted_iota(jnp.int32, (c, 128), 1)
    z = graw + gprm_ref[0:1, :]
    soft = jnp.log(1.0 + jnp.exp(-jnp.abs(z)))
    sp = jnp.maximum(z, 0.0) + soft
    log_sig = -(jnp.maximum(-z, 0.0) + soft)
    a_neg = -jnp.exp(gprm_ref[1:2, :])
    is_ga = lane < L_GB
    is_gb = (lane >= L_GB) & (lane < L_MI)
    is_mi = (lane >= L_MI) & (lane < L_MF)
    is_mf = (lane >= L_MF) & (lane < L_DT)
    is_dt = (lane >= L_DT) & (lane < L_DT + SSD_HEADS)
    cum_src = jnp.where(is_ga | is_dt, a_neg * sp, jnp.where(is_mf, log_sig, 0.0))
    elem = jnp.where(is_gb, _sigmoid(graw), jnp.where(is_mi, z, jnp.where(is_dt, sp, 0.0)))
    if slot:
        rr = lax.broadcasted_iota(jnp.int32, (c, 128), 0)
        valid = (rr >= SLOT_FIRST) & (rr <= SLOT_LAST)
        cum_src = jnp.where(valid, cum_src, 0.0)
        elem = jnp.where(valid, elem, jnp.where(is_mi, -jnp.inf, 0.0))
    cum = _cumsum_rows(tril_b, cum_src)
    if c == 128:
        return cum, elem, cum.T, elem.T
    parts = [cum, elem]
    if 2 * c < 128:
        parts.append(jnp.zeros((128 - 2 * c, 128), F32))
    zt = jnp.concatenate(parts, axis=0).T
    return cum, elem, zt[:, 0:c], zt[:, c:2 * c]


def _row0(i, c):
    return i * c if isinstance(i, int) else pl.multiple_of(i * c, c)


def _pass_a(j, *, c, slot, na, proj_scr, cw_ref, cb_ref, gprm_ref,
            cum_s, col_s, gqx_s, gob_s, ml_s, sy_s, sce_s, sdh_s, defer=False):
    chunks = [j * na + a for a in range(na)]
    r0s = [_row0(i, c) for i in chunks]
    conv = functools.partial(_conv_silu, proj_scr, c=c, cw_ref=cw_ref, cb_ref=cb_ref)

    ii = lax.broadcasted_iota(jnp.int32, (c, c), 0)
    jj = lax.broadcasted_iota(jnp.int32, (c, c), 1)
    tril = ii >= jj
    strict = ii > jj
    eye = (ii == jj).astype(F32)
    tril_b = tril.astype(BF16)
    lane = lax.broadcasted_iota(jnp.int32, (c, 128), 1)

    gates = [_gates(proj_scr, r0, c, slot, gprm_ref, tril_b) for r0 in r0s]
    cum = [g[0] for g in gates]
    elem = [g[1] for g in gates]
    cum_t = [g[2] for g in gates]
    elem_t = [g[3] for g in gates]
    for a in range(na):
        cum_s[chunks[a]] = cum[a]

    def row_cum(a, l):
        return cum_t[a][l:l + 1, :]

    def row_elem(a, l):
        return elem_t[a][l:l + 1, :]

    it = [(a, h) for a in range(na) for h in range(GDN_HEADS)]
    n = range(len(it))
    gd, ml, sd = {}, {}, {}

    def g_k():
        k = [_l2n(conv(r0=r0s[a], col0=C_GK + h * GDN_DK, width=GDN_DK)) for a, h in it]
        gam_c = [_col(cum[a], L_GA + h) for a, h in it]
        beta_c = [_col(elem[a], L_GB + h) for a, h in it]
        dmat = [jnp.exp(jnp.where(tril, gam_c[x] - row_cum(a, L_GA + h), -jnp.inf))
                for x, (a, h) in enumerate(it)]
        pk = [-jnp.where(strict, _dot_nt(k[x], k[x]) * dmat[x] * beta_c[x], 0.0) for x in n]
        gd.update(k=k, gam_c=gam_c, beta_c=beta_c, dmat=dmat, pk=pk, t_inv=[eye + pk[x] for x in n])

    def g_neumann():
        pk = [_dot(gd["pk"][x], gd["pk"][x]) for x in n]
        gd.update(pk=pk, t_inv=[gd["t_inv"][x] + _dot(gd["t_inv"][x], pk[x]) for x in n])

    def g_q():
        gd["q"] = [_l2n(conv(r0=r0s[a], col0=C_GQ + h * GDN_DK, width=GDN_DK)) * (GDN_DK ** -0.5)
                   for a, h in it]

    def g_v():
        v = [conv(r0=r0s[a], col0=C_GV + h * GDN_DV, width=GDN_DV) for a, h in it]
        eg = [jnp.exp(gd["gam_c"][x]) for x in n]
        gd.update(eg=eg, rhs=[jnp.concatenate([v[x] * gd["beta_c"][x],
                                               gd["k"][x] * (gd["beta_c"][x] * eg[x])], axis=1) for x in n])

    def g_uw():
        gd["uw"] = [_dot(gd["t_inv"][x], gd["rhs"][x]) for x in n]

    def g_qk():
        gd["qk"] = [_dot_nt(gd["q"][x], gd["k"][x]) * gd["dmat"][x] for x in n]
        gd["kd"] = [gd["k"][x] * jnp.exp(gd["gam_c"][x][c - 1:c, :] - gd["gam_c"][x]) for x in n]

    def g_out():
        quw = [_dot(gd["qk"][x], gd["uw"][x]) for x in n]
        kuw = [_dot_tn(gd["kd"][x], gd["uw"][x]) for x in n]
        for x, (a, h) in enumerate(it):
            idx = chunks[a] * GDN_HEADS + h
            gqx_s[idx, 0:c, :] = gd["q"][x] * gd["eg"][x] - quw[x][:, GDN_DV:]
            gqx_s[idx, c:c + GDN_DK, :] = kuw[x][:, GDN_DV:]
            gob_s[idx, 0:c, :] = quw[x][:, :GDN_DV]
            gob_s[idx, c:c + GDN_DK, :] = kuw[x][:, :GDN_DV]

    def piece(a, base, h):
        return proj_scr[pl.ds(r0s[a] + SUBLANES, c), base + h * ML_DQK:base + (h + 1) * ML_DQK]

    def m_1():
        mq = [piece(a, C_MQ, h) for a, h in it]
        mk = [piece(a, C_MK, h) * (ML_DQK ** -0.5) for a, h in it]
        b_c = [_col(cum[a], L_MF + h) for a, h in it]
        d = [jnp.where(tril, b_c[x] - row_cum(a, L_MF + h) + row_elem(a, L_MI + h), -jnp.inf)
             for x, (a, h) in enumerate(it)]
        dmax = [jnp.max(d[x], axis=-1, keepdims=True) for x in n]
        dsafe = [jnp.where(dmax[x] == -jnp.inf, 0.0, dmax[x]) for x in n]
        s0 = [_dot_nt(mq[x], mk[x]) * jnp.exp(d[x] - dsafe[x]) for x in n]
        for a in range(na):
            cols = jnp.zeros((c, 128), F32)
            for h in range(ML_HEADS):
                cols = jnp.where(lane == L_MF + h, dmax[a * ML_HEADS + h], cols)
            col_s[chunks[a]] = cols
        ml.update(mk=mk, b_c=b_c, dsafe=dsafe, s0=s0)

    def m_2():
        mv = [piece(a, C_MV, h) for a, h in it]
        i_c = [_col(elem[a], L_MI + h) for a, h in it]
        ones_col = (lax.broadcasted_iota(jnp.int32, (c, 128 - ML_DV), 1) == 0).astype(F32)
        v_aug = [jnp.concatenate([mv[x], ones_col], axis=1) for x in n]
        num0 = [_dot(ml["s0"][x], v_aug[x]) for x in n]
        b_c, dsafe = ml["b_c"], ml["dsafe"]
        kw0 = [ml["mk"][x] * jnp.exp(b_c[x][c - 1:c, :] - b_c[x] + i_c[x] - dsafe[x][c - 1:c, :]) for x in n]
        kv0 = [_dot_tn(kw0[x], v_aug[x]) for x in n]
        for x, (a, h) in enumerate(it):
            idx = chunks[a] * ML_HEADS + h
            ml_s[idx, 0:c, :] = num0[x]
            ml_s[idx, c:c + ML_DQK, :] = kv0[x]

    rep = SSD_HEADS // SSD_GROUPS
    gi = [(a, g) for a in range(na) for g in range(SSD_GROUPS)]
    grp = [a * SSD_GROUPS + h // rep for a, h in it]

    def s_1():
        bg = [conv(r0=r0s[a], col0=C_SB + g * SSD_N, width=SSD_N) for a, g in gi]
        cg = [conv(r0=r0s[a], col0=C_SC + g * SSD_N, width=SSD_N) for a, g in gi]
        sd.update(bg=bg, cg=cg, cb_raw=[_dot_nt(cg[y], bg[y]) for y in range(len(gi))])

    def s_2():
        xg = [conv(r0=r0s[a], col0=C_SX + g * rep * SSD_P, width=rep * SSD_P) for a, g in gi]
        xs = [xg[grp[x]][:, (h % rep) * SSD_P:(h % rep + 1) * SSD_P] for x, (a, h) in enumerate(it)]
        sg_c = [_col(cum[a], L_DT + h) for a, h in it]
        dt_c = [_col(elem[a], L_DT + h) for a, h in it]
        cb = [sd["cb_raw"][grp[x]] * jnp.exp(jnp.where(tril, sg_c[x] - row_cum(a, L_DT + h), -jnp.inf))
              * row_elem(a, L_DT + h) for x, (a, h) in enumerate(it)]
        y0 = [_dot(cb[x], xs[x]) + gprm_ref[2:3, L_DT + h:L_DT + h + 1] * xs[x]
              for x, (a, h) in enumerate(it)]
        dh = [_dot_tn(xs[x] * (jnp.exp(sg_c[x][c - 1:c, :] - sg_c[x]) * dt_c[x]), sd["bg"][grp[x]]) for x in n]
        for x, (a, h) in enumerate(it):
            idx = chunks[a] * SSD_HEADS + h
            sce_s[idx] = sd["cg"][grp[x]] * jnp.exp(sg_c[x])
            sdh_s[idx] = dh[x]
        for a in range(na):
            sy_s[chunks[a]] = jnp.concatenate(y0[a * SSD_HEADS:(a + 1) * SSD_HEADS], axis=1)

    chain = [g_k] + [g_neumann] * (c.bit_length() - 2) + [g_uw]
    if c >= 64:
        fill = [m_1, m_2, s_1, s_2, g_q, g_v]
    else:
        fill = [g_q, g_v, m_1, m_2, s_1, s_2]
    order = []
    for pos, link in enumerate(chain):
        order.append(link)
        if pos < len(fill):
            order.append(fill[pos])
    order += fill[len(chain):] + [g_qk, g_out]
    if defer:
        return order
    for stage in order:
        stage()


def _pass_b(j, *, c, slot, nb, proj_scr, mix_scr, gdnn_ref, mln_ref, ssdn_ref,
            cum_s, col_s, gqx_s, gob_s, ml_s, sy_s, sce_s, sdh_s, mlc_s, gdn_o, mm_o, ssd_o):
    hs = range(GDN_HEADS)
    lane_row = lax.broadcasted_iota(jnp.int32, (1, 128), 1)
    is_m = (lane_row >= L_MF) & (lane_row < L_MF + ML_HEADS)

    def load(seq):
        return dict(s=[gdn_o[seq, h] for h in hs], c=[mlc_s[seq * ML_HEADS + h] for h in hs],
                    h=[ssd_o[seq, h] for h in hs], m=mm_o[seq, 0:1, :])

    def store(seq, st):
        for h in hs:
            gdn_o[seq, h] = st["s"][h]
            mlc_s[seq * ML_HEADS + h] = st["c"][h]
            ssd_o[seq, h] = st["h"][h]
        mm_o[seq, 0:1, :] = st["m"]

    def advance(i, st):
        prow = pl.ds(_row0(i, c) + SUBLANES, c)
        cum = cum_s[i]
        dmx = col_s[i]
        last = cum[c - 1:c, :]
        e_last = jnp.exp(last)
        mq = [proj_scr[prow, C_MQ + h * ML_DQK:C_MQ + (h + 1) * ML_DQK] for h in hs]
        r = [_dot(gqx_s[i * GDN_HEADS + h], st["s"][h]) for h in hs]
        full = [_dot(mq[h], st["c"][h]) for h in hs]
        yh = [_dot_nt(sce_s[i * SSD_HEADS + h], st["h"][h]) for h in hs]
        inter = cum + st["m"]
        m_t = jnp.maximum(inter, dmx)
        w_intra = jnp.exp(dmx - m_t)
        m_new = jnp.where(is_m, m_t[c - 1:c, :], 0.0)
        w_c_row = jnp.exp(last + st["m"] - m_new)
        w_l_row = w_intra[c - 1:c, :]
        ob = [gob_s[i * GDN_HEADS + h] for h in hs]
        blk = [ml_s[i * ML_HEADS + h] for h in hs]
        new = dict(
            s=[_col(e_last, L_GA + h) * st["s"][h] - r[h][c:, :] + ob[h][c:, :] for h in hs],
            c=[_col(w_c_row, L_MF + h) * st["c"][h] + _col(w_l_row, L_MF + h) * blk[h][c:c + ML_DQK, :]
               for h in hs],
            h=[_col(e_last, L_DT + h) * st["h"][h] + sdh_s[i * SSD_HEADS + h] for h in hs],
            m=m_new)
        return new, dict(i=i, r=r, full=full, yh=yh, ob=ob, blk=blk, inter=inter, m_t=m_t, w_intra=w_intra)

    def outputs(ctx):
        i = ctx["i"]
        r0 = _row0(i, c)
        rows = pl.ds(r0, c)
        prow = pl.ds(r0 + SUBLANES, c)
        w_inter = jnp.exp(ctx["inter"] - ctx["m_t"])
        e_neg_m = jnp.exp(-ctx["m_t"])
        for h in hs:
            gg = proj_scr[prow, C_GG + h * GDN_DV:C_GG + (h + 1) * GDN_DV]
            mix_scr[rows, h * GDN_DV:(h + 1) * GDN_DV] = (
                _rms(ctx["r"][h][:c, :] + ctx["ob"][h][:c, :], gdnn_ref[...]) * _silu(gg))
        h_parts = []
        for h in hs:
            fl = (ctx["full"][h] * _col(w_inter, L_MF + h)
                  + _col(ctx["w_intra"], L_MF + h) * ctx["blk"][h][0:c, :])
            den = jnp.maximum(jnp.abs(fl[:, ML_DV:ML_DV + 1]), _col(e_neg_m, L_MF + h))
            mo = proj_scr[prow, C_MO + h * ML_DV:C_MO + (h + 1) * ML_DV]
            h_parts.append(_rms(_sigmoid(mo) * (fl[:, :ML_DV] / den), mln_ref[...]))
        mix_scr[rows, GDN_HEADS * GDN_DV:GDN_HEADS * GDN_DV + ML_HEADS * ML_DV] = (
            jnp.concatenate(h_parts, axis=1))
        sz = proj_scr[prow, C_SZ:C_SZ + SSD_HEADS * SSD_P]
        y_all = (sy_s[i] + jnp.concatenate(ctx["yh"], axis=1)) * _silu(sz)
        mix_scr[rows, GDN_HEADS * GDN_DV + ML_HEADS * ML_DV:D_MODEL] = _rms(y_all, ssdn_ref[...])

    ctxs = []
    st = None
    for a in range(nb):
        i = j * nb + a
        seq = i if slot else 0
        if slot or a == 0:
            st = load(seq)
        st, ctx = advance(i, st)
        if slot or a == nb - 1:
            store(seq, st)
        ctxs.append(ctx)
    for ctx in ctxs:
        outputs(ctx)


def _ml_state_in(c_mat, n_row):
    eye = (lax.broadcasted_iota(jnp.int32, (ML_DQK, ML_DQK), 0)
           == lax.broadcasted_iota(jnp.int32, (ML_DQK, ML_DQK), 1)).astype(F32)
    n_col = jnp.sum(eye * n_row, axis=-1, keepdims=True)
    first = lax.broadcasted_iota(jnp.int32, (ML_DQK, 128 - ML_DV), 1) == 0
    return jnp.concatenate([c_mat, jnp.where(first, n_col, 0.0)], axis=1)


def _ml_state_out(c_aug):
    eye = (lax.broadcasted_iota(jnp.int32, (ML_DQK, ML_DQK), 0)
           == lax.broadcasted_iota(jnp.int32, (ML_DQK, ML_DQK), 1)).astype(F32)
    n_row = jnp.sum(eye * c_aug[:, ML_DV:ML_DV + 1], axis=0, keepdims=True)
    return c_aug[:, :ML_DV], n_row


def _mixer_kernel(*refs, tm, c, slot, nt, na, nbk, first):
    n_in = (17 if slot else 11) + (0 if first else 6)
    ins, (o_ref, *state_o), (proj_scr, mix_scr, *ab) = refs[:n_in], refs[n_in:n_in + 7], refs[n_in + 7:]
    if slot:
        (x_ref, tail_ref, win_ref, wout_ref, cw_ref, cb_ref, gprm_ref, npre_ref, npost_ref, gdnn_ref,
         mln_ref, ssdn_ref, gdn_i, mc_i, mn_i, mm_i, ssd_i) = ins[:17]
    else:
        (x_ref, win_ref, wout_ref, cw_ref, cb_ref, gprm_ref, npre_ref, npost_ref, gdnn_ref,
         mln_ref, ssdn_ref) = ins[:11]
    if first:
        for ref in state_o:
            ref[1] = jnp.zeros(ref.shape[1:], F32)
        state_o = [ref.at[0] for ref in state_o]
    conv_o, gdn_o, mc_o, mn_o, mm_o, ssd_o = state_o
    names = ("cum_s", "col_s", "gqx_s", "gob_s", "ml_s", "sy_s", "sce_s", "sdh_s", "mlc_s")
    ab = dict(zip(names, ab))
    mlc_s = ab["mlc_s"]
    nseq = gdn_o.shape[0]
    t = pl.program_id(1)

    if slot:
        proj_scr[0:SUBLANES, :] = jnp.zeros((SUBLANES, N_IN), F32)
        gdn_o[...] = gdn_i[...]
        mm_o[...] = mm_i[...]
        ssd_o[...] = ssd_i[...]

        def load_ml(s, carry):
            for h in range(ML_HEADS):
                mlc_s[s * ML_HEADS + h] = _ml_state_in(mc_i[s, h], mn_i[s, h:h + 1, :])
            return carry

        lax.fori_loop(0, nseq, load_ml, 0)
    else:
        @pl.when(t == 0)
        def _():
            proj_scr[0:SUBLANES, :] = jnp.zeros((SUBLANES, N_IN), F32)
            gdn_o[...] = jnp.zeros(gdn_o.shape, F32)
            mlc_s[...] = jnp.zeros(mlc_s.shape, F32)
            mm_o[...] = jnp.zeros(mm_o.shape, F32)
            ssd_o[...] = jnp.zeros(ssd_o.shape, F32)

    x = x_ref[...]
    hn = _rms(x, npre_ref[...]).astype(BF16)

    def in_proj(r_lo, r_hi, nb):
        cols = slice(nb * IN_NBLK, (nb + 1) * IN_NBLK)
        proj_scr[SUBLANES + r_lo:SUBLANES + r_hi, cols] = _dot(hn[r_lo:r_hi, :], win_ref[:, cols])

    def out_proj(r_lo, r_hi):
        out = _dot(mix_scr[r_lo:r_hi, :].astype(BF16), wout_ref[...])
        o_ref[r_lo:r_hi, :] = x[r_lo:r_hi, :] + _rms(out, npost_ref[...])

    def weave(major, minor, minor_first=False):
        major, minor = list(major), list(minor)
        stride = max(1, len(major) // max(1, len(minor)))
        while major or minor:
            if minor and minor_first:
                minor.pop(0)()
            for step in major[:stride]:
                step()
            major = major[stride:]
            if minor and not minor_first:
                minor.pop(0)()

    n_in_blk = N_IN // IN_NBLK
    if slot:
        for nb in range(n_in_blk):
            in_proj(0, tm, nb)
        for s in range(nseq):
            base = SUBLANES + s * SLOT
            proj_scr[base:base + SLOT_FIRST, 0:CONV_DIM] = tail_ref[s]
            conv_o[s] = proj_scr[base + SLOT_LAST - 2:base + SLOT_LAST + 1, 0:CONV_DIM]

    pass_a = functools.partial(_pass_a, c=c, slot=slot, na=na, proj_scr=proj_scr, cw_ref=cw_ref,
                               cb_ref=cb_ref, gprm_ref=gprm_ref,
                               **{k: v for k, v in ab.items() if k != "mlc_s"})
    pass_b = functools.partial(_pass_b, c=c, slot=slot, nb=nbk, proj_scr=proj_scr, mix_scr=mix_scr,
                               gdnn_ref=gdnn_ref, mln_ref=mln_ref, ssdn_ref=ssdn_ref,
                               gdn_o=gdn_o, mm_o=mm_o, ssd_o=ssd_o, **ab)

    n_a, n_b = tm // (c * na), tm // (c * nbk)
    if slot:
        def body_a(j, carry):
            pass_a(j)
            return carry

        def body_b(j, carry):
            pass_b(j)
            return carry

        lax.fori_loop(0, n_a, body_a, 0)
        lax.fori_loop(0, n_b, body_b, 0)
        out_proj(0, tm)
    else:
        assert n_a % 2 == 0 and n_b % n_a == 0
        half, b_per_a = tm // 2, n_b // n_a
        for nb in range(n_in_blk):
            in_proj(0, half, nb)
        weave(pass_a(0, defer=True), [functools.partial(in_proj, half, tm, nb) for nb in range(n_in_blk)],
              minor_first=True)
        for g in range(1, n_a):
            weave(pass_a(g, defer=True), [functools.partial(pass_b, (g - 1) * b_per_a + k) for k in range(b_per_a)])
        weave([functools.partial(pass_b, (n_a - 1) * b_per_a + k) for k in range(b_per_a)],
              [functools.partial(out_proj, 0, half)])
        out_proj(half, tm)

    def store_ml(s, carry):
        for h in range(ML_HEADS):
            c_mat, n_row = _ml_state_out(mlc_s[s * ML_HEADS + h])
            mc_o[s, h] = c_mat
            mn_o[s, h:h + 1, :] = n_row
        return carry

    if slot:
        lax.fori_loop(0, nseq, store_ml, 0)
    else:
        last_rows = proj_scr[tm:tm + SUBLANES, 0:CONV_DIM]
        proj_scr[0:SUBLANES, 0:CONV_DIM] = last_rows

        @pl.when(t == nt - 1)
        def _():
            conv_o[0] = last_rows[SUBLANES - 3:, :]
            store_ml(0, 0)


def _ffn_kernel(*refs, tm, slot, nt):
    if slot:
        (x_ref, ftail_ref, wup_ref, wdn_ref, fw_ref, fb_ref, npre_ref, npost_ref, _,
         o_ref, gate_o, tails_scr, act_scr) = refs
    else:
        (x_ref, wup_ref, wdn_ref, fw_ref, fb_ref, npre_ref, npost_ref, _,
         o_ref, gate_o, tails_scr, act_scr) = refs
    t = pl.program_id(1)

    if slot:
        tails_scr[...] = jnp.zeros(tails_scr.shape, F32)
    else:
        @pl.when(t == 0)
        def _():
            tails_scr[...] = jnp.zeros(tails_scr.shape, F32)

    x = x_ref[...]
    hn = _rms(x, npre_ref[...]).astype(BF16)
    nblk = D_FF // FF_BLK

    def up(blk):
        return (_dot(hn, wup_ref[:, blk * FF_BLK:(blk + 1) * FF_BLK]),
                _dot(hn, wup_ref[:, D_FF + blk * FF_BLK:D_FF + (blk + 1) * FF_BLK]))

    ahead = up(0)
    for blk in range(nblk):
        cols = slice(blk * FF_BLK, (blk + 1) * FF_BLK)
        gate, val = ahead
        if blk + 1 < nblk:
            ahead = up(blk + 1)
        if slot:
            g3 = gate.reshape(tm // SLOT, SLOT, FF_BLK)
            ft = ftail_ref[:, :, cols]
            rr = lax.broadcasted_iota(jnp.int32, g3.shape, 1)
            g3 = jnp.where(rr == SLOT_FIRST - 2, ft[:, 0:1, :], jnp.where(rr == SLOT_FIRST - 1, ft[:, 1:2, :], g3))
            gate_o[:, :, cols] = g3[:, SLOT_LAST - 1:SLOT_LAST + 1, :]
            gate = g3.reshape(tm, FF_BLK)
        full = jnp.concatenate([tails_scr[:, cols], gate], axis=0)
        conv = fb_ref[0:1, cols] + fw_ref[2:3, cols] * gate
        for j in range(2):
            conv = conv + fw_ref[j:j + 1, cols] * pltpu.roll(full, 2 - j, 0)[SUBLANES:, :]
        last_rows = full[tm:tm + SUBLANES, :]
        tails_scr[:, cols] = last_rows
        act_scr[:, cols] = (_gelu_tanh(conv) * val).astype(BF16)
    o_ref[...] = x + _rms(_dot(act_scr[...], wdn_ref[...]), npost_ref[...])
    if not slot:
        @pl.when(t == nt - 1)
        def _():
            gate_o[0] = tails_scr[SUBLANES - 2:, :]


def _const_spec(shape, layer):
    nd = len(shape)
    return pl.BlockSpec((None,) + tuple(shape), lambda b, t: (layer,) + (0,) * nd,
                        pipeline_mode=pl.Buffered(1))


def _state_dims():
    return ((3, CONV_DIM), (GDN_HEADS, GDN_DK, GDN_DV), (ML_HEADS, ML_DQK, ML_DV), (ML_HEADS, ML_DQK),
            (SUBLANES, 128), (SSD_HEADS, SSD_P, SSD_N))


def _mixer_call(x, layer, prm, bufs, *, slot, states=None, tail=None):
    rows = x.shape[0]
    if slot:
        tm, c, nt, na, nbk = SAMPLE_TM, SLOT, 1, SAMPLE_NA, SAMPLE_NB
        nseq = tm // SLOT
    else:
        tm, c, na, nbk = PROMPT_TM, PROMPT_CHUNK, PROMPT_NA, PROMPT_NB
        nt = 2048 // tm
        nseq = 1
    nch = tm // c
    ngrp = rows // (tm * nt)
    nb = ngrp * nseq
    row_spec = lambda w: pl.BlockSpec((tm, w), lambda b, t: (b * nt + t, 0))
    st_in = lambda *dims: pl.BlockSpec((None, nseq) + dims, lambda b, t: (layer, b) + (0,) * len(dims))

    in_specs = [row_spec(D_MODEL)]
    args = [x]
    if slot:
        in_specs.append(st_in(SLOT_FIRST, CONV_DIM))
        args.append(tail)
    in_specs += [
        _const_spec((D_MODEL, N_IN), layer), _const_spec((D_MODEL, D_MODEL), layer),
        _const_spec((4, CONV_DIM), layer), _const_spec((1, CONV_DIM), layer),
        _const_spec((8, 128), layer), _const_spec((1, D_MODEL), layer), _const_spec((1, D_MODEL), layer),
        _const_spec((1, GDN_DV), layer), _const_spec((1, ML_DV), layer),
        _const_spec((1, SSD_HEADS * SSD_P), layer)]
    args += [prm["w_in"], prm["w_out"], prm["conv_w"], prm["conv_b"], prm["gprm"], prm["norm_mix_pre"],
             prm["norm_mix_post"], prm["gdn_norm"], prm["mlstm_norm"], prm["ssd_norm"]]
    if slot:
        in_specs += [st_in(GDN_HEADS, GDN_DK, GDN_DV), st_in(ML_HEADS, ML_DQK, ML_DV),
                     st_in(ML_HEADS, ML_DQK), st_in(SUBLANES, 128), st_in(SSD_HEADS, SSD_P, SSD_N)]
        args += list(states)
    first = bufs is None
    if first:
        aliases = {}
        st_o = lambda *dims: pl.BlockSpec((DEPTH, nseq) + dims, lambda b, t: (0, b) + (0,) * len(dims))
    else:
        aliases = {len(args) + k: 1 + k for k in range(len(bufs))}
        in_specs += [pl.BlockSpec(memory_space=pl.ANY)] * len(bufs)
        args += list(bufs)
        st_o = st_in

    out_specs = [row_spec(D_MODEL)] + [st_o(*dims) for dims in _state_dims()]
    out_shape = [jax.ShapeDtypeStruct((rows, D_MODEL), F32)] + [
        jax.ShapeDtypeStruct((DEPTH, nb) + dims, F32) for dims in _state_dims()]
    scratch = [
        pltpu.VMEM((tm + SUBLANES, N_IN), F32),
        pltpu.VMEM((tm, D_MODEL), F32),
        pltpu.VMEM((nch, c, 128), F32),
        pltpu.VMEM((nch, c, 128), F32),
        pltpu.VMEM((nch * GDN_HEADS, c + GDN_DK, GDN_DV), F32),
        pltpu.VMEM((nch * GDN_HEADS, c + GDN_DK, GDN_DV), F32),
        pltpu.VMEM((nch * ML_HEADS, c + ML_DQK, 128), F32),
        pltpu.VMEM((nch, c, SSD_HEADS * SSD_P), F32),
        pltpu.VMEM((nch * SSD_HEADS, c, SSD_N), F32),
        pltpu.VMEM((nch * SSD_HEADS, SSD_P, SSD_N), F32),
        pltpu.VMEM((nseq * ML_HEADS, ML_DQK, 128), F32),
    ]
    return pl.pallas_call(
        functools.partial(_mixer_kernel, tm=tm, c=c, slot=slot, nt=nt, na=na, nbk=nbk, first=first),
        grid=(ngrp, nt), in_specs=in_specs, out_specs=out_specs, out_shape=out_shape,
        scratch_shapes=scratch, input_output_aliases=aliases,
        compiler_params=pltpu.CompilerParams(dimension_semantics=("arbitrary", "arbitrary"),
                                             vmem_limit_bytes=VMEM_LIMIT),
        name=("mixer_sample" if slot else "mixer_prompt"),
    )(*args)


def _ffn_call(x, layer, prm, buf, *, slot, tail=None):
    rows = x.shape[0]
    tm = FFN_TM
    nt = 1 if slot else 2048 // tm
    ngrp = rows // (tm * nt)
    row_spec = lambda w: pl.BlockSpec((tm, w), lambda b, t: (b * nt + t, 0))
    in_specs = [row_spec(D_MODEL)]
    args = [x]
    nseq = tm // SLOT if slot else 1
    if slot:
        in_specs.append(pl.BlockSpec((None, nseq, 2, D_FF), lambda b, t: (layer, b, 0, 0)))
        args.append(tail)
    in_specs += [_const_spec((D_MODEL, 2 * D_FF), layer), _const_spec((D_FF, D_MODEL), layer),
                 _const_spec((3, D_FF), layer), _const_spec((1, D_FF), layer),
                 _const_spec((1, D_MODEL), layer), _const_spec((1, D_MODEL), layer)]
    args += [prm["ffn_w_up"], prm["ffn_w_down"], prm["ffn_conv_w"], prm["ffn_conv_b"],
             prm["norm_ffn_pre"], prm["norm_ffn_post"]]
    aliases = {len(args): 1}
    in_specs.append(pl.BlockSpec(memory_space=pl.ANY))
    args.append(buf)
    gate_spec = pl.BlockSpec((None, nseq, 2, D_FF), lambda b, t: (layer, b, 0, 0))
    gate_shape = (DEPTH, ngrp * nseq, 2, D_FF)
    return pl.pallas_call(
        functools.partial(_ffn_kernel, tm=tm, slot=slot, nt=nt),
        grid=(ngrp, nt), in_specs=in_specs, out_specs=[row_spec(D_MODEL), gate_spec],
        out_shape=[jax.ShapeDtypeStruct((rows, D_MODEL), F32), jax.ShapeDtypeStruct(gate_shape, F32)],
        scratch_shapes=[pltpu.VMEM((SUBLANES, D_FF), F32), pltpu.VMEM((tm, D_FF), BF16)],
        input_output_aliases=aliases,
        compiler_params=pltpu.CompilerParams(dimension_semantics=("arbitrary", "arbitrary"),
                                             vmem_limit_bytes=VMEM_LIMIT),
        name=("ffn_sample" if slot else "ffn_prompt"),
    )(*args)


def _prepare_params(norm_mix_pre, norm_mix_post, norm_ffn_pre, norm_ffn_post, w_in, conv_w, conv_b,
                    gdn_a_log, gdn_dt_bias, gdn_norm, mlstm_i_bias, mlstm_f_bias, mlstm_norm,
                    ssd_a_log, ssd_dt_bias, ssd_d, ssd_norm, w_out, ffn_w_up, ffn_conv_w, ffn_conv_b,
                    ffn_w_down):
    w_in_p = jnp.concatenate(
        [w_in[..., :2816], w_in[..., 2824:3848], w_in[..., 3856:4112], w_in[..., 2816:2824],
         w_in[..., 3848:3856], w_in[..., 4112:4116],
         jnp.zeros((DEPTH, D_MODEL, N_IN - 4116), w_in.dtype)], axis=-1).astype(BF16)
    z4 = jnp.zeros((DEPTH, 4), F32)
    pad = jnp.zeros((DEPTH, 128 - 20), F32)
    gprm = jnp.stack(
        [jnp.concatenate([gdn_dt_bias, z4, mlstm_i_bias, mlstm_f_bias, ssd_dt_bias, pad], axis=-1),
         jnp.concatenate([gdn_a_log, z4, z4, z4, ssd_a_log, pad], axis=-1),
         jnp.concatenate([z4, z4, z4, z4, ssd_d, pad], axis=-1)]
        + [jnp.zeros((DEPTH, 128), F32)] * 5, axis=1)
    row = lambda a: a[:, None, :]
    return dict(
        w_in=w_in_p, w_out=w_out.astype(BF16), conv_w=conv_w, conv_b=row(conv_b), gprm=gprm,
        norm_mix_pre=row(norm_mix_pre), norm_mix_post=row(norm_mix_post),
        norm_ffn_pre=row(norm_ffn_pre), norm_ffn_post=row(norm_ffn_post),
        gdn_norm=row(gdn_norm), mlstm_norm=row(mlstm_norm), ssd_norm=row(ssd_norm),
        ffn_w_up=ffn_w_up.astype(BF16), ffn_w_down=ffn_w_down.astype(BF16),
        ffn_conv_w=ffn_conv_w, ffn_conv_b=row(ffn_conv_b))


def kernel(x_prompt, x_sample, state_conv, state_gdn, state_mlstm_c, state_mlstm_n, state_mlstm_m, state_ssd, state_ffn_conv, norm_mix_pre, norm_mix_post, norm_ffn_pre, norm_ffn_post, w_in, conv_w, conv_b, gdn_a_log, gdn_dt_bias, gdn_norm, mlstm_i_bias, mlstm_f_bias, mlstm_norm, ssd_a_log, ssd_dt_bias, ssd_d, ssd_norm, w_out, ffn_w_up, ffn_conv_w, ffn_conv_b, ffn_w_down):
    prm = _prepare_params(norm_mix_pre, norm_mix_post, norm_ffn_pre, norm_ffn_post, w_in, conv_w, conv_b,
                          gdn_a_log, gdn_dt_bias, gdn_norm, mlstm_i_bias, mlstm_f_bias, mlstm_norm,
                          ssd_a_log, ssd_dt_bias, ssd_d, ssd_norm, w_out, ffn_w_up, ffn_conv_w,
                          ffn_conv_b, ffn_w_down)
    bp, lp, _ = x_prompt.shape
    bs, ls, _ = x_sample.shape

    def state_bufs(nseq):
        return None, jnp.zeros((DEPTH, nseq, 2, D_FF), F32)

    def assemble(bufs, gate):
        conv, gdn, mc, mn, mm, ssd = bufs
        return [conv, gdn, mc, mn, mm[:, :, 0, L_MF:L_MF + ML_HEADS], ssd, gate]

    x = x_prompt.reshape(bp * lp, D_MODEL)
    bufs, gate = state_bufs(bp)
    for layer in range(DEPTH):
        x, *bufs = _mixer_call(x, layer, prm, bufs, slot=False)
        x, gate = _ffn_call(x, layer, prm, gate, slot=False)
    p_out = assemble(bufs, gate)
    y_prompt = x.reshape(bp, lp, D_MODEL)

    x = jnp.pad(x_sample, ((0, 0), (SLOT_FIRST, SLOT - SLOT_FIRST - ls), (0, 0))).reshape(bs * SLOT, D_MODEL)
    mm_in = jnp.pad(state_mlstm_m[:, :, None, :],
                    ((0, 0), (0, 0), (0, SUBLANES - 1), (L_MF, 128 - L_MF - ML_HEADS)))
    bufs, gate = state_bufs(bs)
    for layer in range(DEPTH):
        x, *bufs = _mixer_call(x, layer, prm, bufs, slot=True, tail=state_conv,
                               states=(state_gdn, state_mlstm_c, state_mlstm_n, mm_in, state_ssd))
        x, gate = _ffn_call(x, layer, prm, gate, slot=True, tail=state_ffn_conv)
    s_out = assemble(bufs, gate)
    y_sample = x.reshape(bs, SLOT, D_MODEL)[:, SLOT_FIRST:SLOT_LAST + 1]
    return (y_prompt, y_sample, *p_out, *s_out)
```

```python
import functools

import jax
import jax.numpy as jnp
from jax import lax
from jax.experimental import pallas as pl
from jax.experimental.pallas import tpu as pltpu

F32 = jnp.float32
BF16 = jnp.bfloat16

D_MODEL = 1024
DEPTH = 2
GDN_HEADS, GDN_DK, GDN_DV = 4, 128, 128
ML_HEADS, ML_DQK, ML_DV = 4, 64, 64
SSD_HEADS, SSD_P, SSD_GROUPS, SSD_N = 4, 64, 2, 128
D_FF = 2816
EPS = 1e-6

CONV_DIM = 2304
C_GQ, C_GK, C_GV, C_SX, C_SB, C_SC = 0, 512, 1024, 1536, 1792, 2048
C_GG, C_MQ, C_MK, C_MV, C_MO, C_SZ, C_GATE = 2304, 2816, 3072, 3328, 3584, 3840, 4096
N_IN = 4224
IN_NBLK = 1408
L_GA, L_GB, L_MI, L_MF, L_DT = 0, 4, 8, 12, 16

SUBLANES = 8
SLOT = 8
SLOT_FIRST, SLOT_LAST = 3, 6
FF_BLK = 256
VMEM_LIMIT = 56 * 1024 * 1024

PROMPT_TM, PROMPT_CHUNK, PROMPT_NA, PROMPT_NB = 512, 64, 4, 1
SAMPLE_TM, SAMPLE_NA, SAMPLE_NB = 64, 8, 8
FFN_TM = 1024


def _dot(a, b):
    return jnp.dot(a, b, preferred_element_type=F32)


def _dot_nt(a, b):
    return lax.dot_general(a, b, (((1,), (1,)), ((), ())), preferred_element_type=F32)


def _dot_tn(a, b):
    return lax.dot_general(a, b, (((0,), (0,)), ((), ())), preferred_element_type=F32)


def _sigmoid(x):
    return 1.0 / (1.0 + jnp.exp(-x))


def _silu(x):
    return x * _sigmoid(x)


def _rms(x, w):
    return x * lax.rsqrt(jnp.mean(x * x, axis=-1, keepdims=True) + EPS) * w


def _l2n(x):
    return x * lax.rsqrt(jnp.sum(x * x, axis=-1, keepdims=True) + EPS)


def _gelu_tanh(x):
    return 0.5 * x * (1.0 + jnp.tanh(0.7978845608028654 * (x + 0.044715 * (x * x * x))))


def _col(a, l):
    return a[:, l:l + 1]


def _cumsum_rows(tril_b, x):
    hi = x.astype(BF16)
    r1 = x - hi.astype(F32)
    mid = r1.astype(BF16)
    lo = (r1 - mid.astype(F32)).astype(BF16)
    return _dot(tril_b, hi) + _dot(tril_b, mid) + _dot(tril_b, lo)


def _conv_silu(proj_scr, r0, c, col0, width, cw_ref, cb_ref):
    win = proj_scr[pl.ds(r0, c + SUBLANES), col0:col0 + width]
    acc = cw_ref[0:1, col0:col0 + width] * win
    for j in range(1, 4):
        acc = cw_ref[j:j + 1, col0:col0 + width] * win + pltpu.roll(acc, 1, 0)
    return _silu(acc[SUBLANES:, :] + cb_ref[0:1, col0:col0 + width])


def _gates(proj_scr, r0, c, slot, gprm_ref, tril_b):
    graw = proj_scr[pl.ds(r0 + SUBLANES, c), C_GATE:C_GATE + 128]
    lane = lax.broadcasted_iota(jnp.int32, (c, 128), 1)
    z = graw + gprm_ref[0:1, :]
    soft = jnp.log(1.0 + jnp.exp(-jnp.abs(z)))
    sp = jnp.maximum(z, 0.0) + soft
    log_sig = -(jnp.maximum(-z, 0.0) + soft)
    a_neg = -jnp.exp(gprm_ref[1:2, :])
    is_ga = lane < L_GB
    is_gb = (lane >= L_GB) & (lane < L_MI)
    is_mi = (lane >= L_MI) & (lane < L_MF)
    is_mf = (lane >= L_MF) & (lane < L_DT)
    is_dt = (lane >= L_DT) & (lane < L_DT + SSD_HEADS)
    cum_src = jnp.where(is_ga | is_dt, a_neg * sp, jnp.where(is_mf, log_sig, 0.0))
    elem = jnp.where(is_gb, _sigmoid(graw), jnp.where(is_mi, z, jnp.where(is_dt, sp, 0.0)))
    if slot:
        rr = lax.broadcasted_iota(jnp.int32, (c, 128), 0)
        valid = (rr >= SLOT_FIRST) & (rr <= SLOT_LAST)
        cum_src = jnp.where(valid, cum_src, 0.0)
        elem = jnp.where(valid, elem, jnp.where(is_mi, -jnp.inf, 0.0))
    cum = _cumsum_rows(tril_b, cum_src)
    if c == 128:
        return cum, elem, cum.T, elem.T
    parts = [cum, elem]
    if 2 * c < 128:
        parts.append(jnp.zeros((128 - 2 * c, 128), F32))
    zt = jnp.concatenate(parts, axis=0).T
    return cum, elem, zt[:, 0:c], zt[:, c:2 * c]


def _row0(i, c):
    return i * c if isinstance(i, int) else pl.multiple_of(i * c, c)


def _pass_a(j, *, c, slot, na, proj_scr, cw_ref, cb_ref, gprm_ref,
            cum_s, col_s, gqx_s, gob_s, ml_s, sy_s, sce_s, sdh_s, defer=False):
    chunks = [j * na + a for a in range(na)]
    r0s = [_row0(i, c) for i in chunks]
    conv = functools.partial(_conv_silu, proj_scr, c=c, cw_ref=cw_ref, cb_ref=cb_ref)

    ii = lax.broadcasted_iota(jnp.int32, (c, c), 0)
    jj = lax.broadcasted_iota(jnp.int32, (c, c), 1)
    tril = ii >= jj
    strict = ii > jj
    eye = (ii == jj).astype(F32)
    tril_b = tril.astype(BF16)
    lane = lax.broadcasted_iota(jnp.int32, (c, 128), 1)

    gates = [_gates(proj_scr, r0, c, slot, gprm_ref, tril_b) for r0 in r0s]
    cum = [g[0] for g in gates]
    elem = [g[1] for g in gates]
    cum_t = [g[2] for g in gates]
    elem_t = [g[3] for g in gates]
    for a in range(na):
        cum_s[chunks[a]] = cum[a]

    def row_cum(a, l):
        return cum_t[a][l:l + 1, :]

    def row_elem(a, l):
        return elem_t[a][l:l + 1, :]

    it = [(a, h) for a in range(na) for h in range(GDN_HEADS)]
    n = range(len(it))
    gd, ml, sd = {}, {}, {}

    def g_k():
        k = [_l2n(conv(r0=r0s[a], col0=C_GK + h * GDN_DK, width=GDN_DK)) for a, h in it]
        gam_c = [_col(cum[a], L_GA + h) for a, h in it]
        beta_c = [_col(elem[a], L_GB + h) for a, h in it]
        dmat = [jnp.exp(jnp.where(tril, gam_c[x] - row_cum(a, L_GA + h), -jnp.inf))
                for x, (a, h) in enumerate(it)]
        pk = [-jnp.where(strict, _dot_nt(k[x], k[x]) * dmat[x] * beta_c[x], 0.0) for x in n]
        gd.update(k=k, gam_c=gam_c, beta_c=beta_c, dmat=dmat, pk=pk, t_inv=[eye + pk[x] for x in n])

    def g_neumann():
        pk = [_dot(gd["pk"][x], gd["pk"][x]) for x in n]
        gd.update(pk=pk, t_inv=[gd["t_inv"][x] + _dot(gd["t_inv"][x], pk[x]) for x in n])

    def g_q():
        gd["q"] = [_l2n(conv(r0=r0s[a], col0=C_GQ + h * GDN_DK, width=GDN_DK)) * (GDN_DK ** -0.5)
                   for a, h in it]

    def g_v():
        v = [conv(r0=r0s[a], col0=C_GV + h * GDN_DV, width=GDN_DV) for a, h in it]
        eg = [jnp.exp(gd["gam_c"][x]) for x in n]
        gd.update(eg=eg, rhs=[jnp.concatenate([v[x] * gd["beta_c"][x],
                                               gd["k"][x] * (gd["beta_c"][x] * eg[x])], axis=1) for x in n])

    def g_uw():
        gd["uw"] = [_dot(gd["t_inv"][x], gd["rhs"][x]) for x in n]

    def g_qk():
        gd["qk"] = [_dot_nt(gd["q"][x], gd["k"][x]) * gd["dmat"][x] for x in n]
        gd["kd"] = [gd["k"][x] * jnp.exp(gd["gam_c"][x][c - 1:c, :] - gd["gam_c"][x]) for x in n]

    def g_out():
        quw = [_dot(gd["qk"][x], gd["uw"][x]) for x in n]
        kuw = [_dot_tn(gd["kd"][x], gd["uw"][x]) for x in n]
        for x, (a, h) in enumerate(it):
            idx = chunks[a] * GDN_HEADS + h
            gqx_s[idx, 0:c, :] = (gd["q"][x] * gd["eg"][x] - quw[x][:, GDN_DV:]).astype(gqx_s.dtype)
            gqx_s[idx, c:c + GDN_DK, :] = kuw[x][:, GDN_DV:].astype(gqx_s.dtype)
            gob_s[idx, 0:c, :] = quw[x][:, :GDN_DV]
            gob_s[idx, c:c + GDN_DK, :] = kuw[x][:, :GDN_DV]

    def piece(a, base, h):
        return proj_scr[pl.ds(r0s[a] + SUBLANES, c), base + h * ML_DQK:base + (h + 1) * ML_DQK]

    def m_1():
        mq = [piece(a, C_MQ, h) for a, h in it]
        mk = [piece(a, C_MK, h) * (ML_DQK ** -0.5) for a, h in it]
        b_c = [_col(cum[a], L_MF + h) for a, h in it]
        d = [jnp.where(tril, b_c[x] - row_cum(a, L_MF + h) + row_elem(a, L_MI + h), -jnp.inf)
             for x, (a, h) in enumerate(it)]
        dmax = [jnp.max(d[x], axis=-1, keepdims=True) for x in n]
        dsafe = [jnp.where(dmax[x] == -jnp.inf, 0.0, dmax[x]) for x in n]
        s0 = [_dot_nt(mq[x], mk[x]) * jnp.exp(d[x] - dsafe[x]) for x in n]
        for a in range(na):
            cols = jnp.zeros((c, 128), F32)
            for h in range(ML_HEADS):
                cols = jnp.where(lane == L_MF + h, dmax[a * ML_HEADS + h], cols)
            col_s[chunks[a]] = cols
        ml.update(mk=mk, b_c=b_c, dsafe=dsafe, s0=s0)

    def m_2():
        mv = [piece(a, C_MV, h) for a, h in it]
        i_c = [_col(elem[a], L_MI + h) for a, h in it]
        ones_col = (lax.broadcasted_iota(jnp.int32, (c, 128 - ML_DV), 1) == 0).astype(F32)
        v_aug = [jnp.concatenate([mv[x], ones_col], axis=1) for x in n]
        num0 = [_dot(ml["s0"][x], v_aug[x]) for x in n]
        b_c, dsafe = ml["b_c"], ml["dsafe"]
        kw0 = [ml["mk"][x] * jnp.exp(b_c[x][c - 1:c, :] - b_c[x] + i_c[x] - dsafe[x][c - 1:c, :]) for x in n]
        kv0 = [_dot_tn(kw0[x], v_aug[x]) for x in n]
        for x, (a, h) in enumerate(it):
            idx = chunks[a] * ML_HEADS + h
            ml_s[idx, 0:c, :] = num0[x]
            ml_s[idx, c:c + ML_DQK, :] = kv0[x]

    rep = SSD_HEADS // SSD_GROUPS
    gi = [(a, g) for a in range(na) for g in range(SSD_GROUPS)]
    grp = [a * SSD_GROUPS + h // rep for a, h in it]

    def s_1():
        bg = [conv(r0=r0s[a], col0=C_SB + g * SSD_N, width=SSD_N) for a, g in gi]
        cg = [conv(r0=r0s[a], col0=C_SC + g * SSD_N, width=SSD_N) for a, g in gi]
        sd.update(bg=bg, cg=cg, cb_raw=[_dot_nt(cg[y], bg[y]) for y in range(len(gi))])

    def s_2():
        xg = [conv(r0=r0s[a], col0=C_SX + g * rep * SSD_P, width=rep * SSD_P) for a, g in gi]
        xs = [xg[grp[x]][:, (h % rep) * SSD_P:(h % rep + 1) * SSD_P] for x, (a, h) in enumerate(it)]
        sg_c = [_col(cum[a], L_DT + h) for a, h in it]
        dt_c = [_col(elem[a], L_DT + h) for a, h in it]
        cb = [sd["cb_raw"][grp[x]] * jnp.exp(jnp.where(tril, sg_c[x] - row_cum(a, L_DT + h), -jnp.inf))
              * row_elem(a, L_DT + h) for x, (a, h) in enumerate(it)]
        y0 = [_dot(cb[x], xs[x]) + gprm_ref[2:3, L_DT + h:L_DT + h + 1] * xs[x]
              for x, (a, h) in enumerate(it)]
        dh = [_dot_tn(xs[x] * (jnp.exp(sg_c[x][c - 1:c, :] - sg_c[x]) * dt_c[x]), sd["bg"][grp[x]]) for x in n]
        for x, (a, h) in enumerate(it):
            idx = chunks[a] * SSD_HEADS + h
            sce_s[idx] = (sd["cg"][grp[x]] * jnp.exp(sg_c[x])).astype(sce_s.dtype)
            sdh_s[idx] = dh[x]
        for a in range(na):
            sy_s[chunks[a]] = jnp.concatenate(y0[a * SSD_HEADS:(a + 1) * SSD_HEADS], axis=1)

    chain = [g_k] + [g_neumann] * (c.bit_length() - 2) + [g_uw]
    if c >= 64:
        fill = [m_1, m_2, s_1, s_2, g_q, g_v]
    else:
        fill = [g_q, g_v, m_1, m_2, s_1, s_2]
    order = []
    for pos, link in enumerate(chain):
        order.append(link)
        if pos < len(fill):
            order.append(fill[pos])
    order += fill[len(chain):] + [g_qk, g_out]
    if defer:
        return order
    for stage in order:
        stage()


def _pass_b(j, *, c, slot, nb, proj_scr, mix_scr, gdnn_ref, mln_ref, ssdn_ref,
            cum_s, col_s, gqx_s, gob_s, ml_s, sy_s, sce_s, sdh_s, mlc_s, gdn_o, mm_o, ssd_o):
    hs = range(GDN_HEADS)
    lane_row = lax.broadcasted_iota(jnp.int32, (1, 128), 1)
    is_m = (lane_row >= L_MF) & (lane_row < L_MF + ML_HEADS)

    def load(seq):
        return dict(s=[gdn_o[seq, h] for h in hs], c=[mlc_s[seq * ML_HEADS + h] for h in hs],
                    h=[ssd_o[seq, h] for h in hs], m=mm_o[seq, 0:1, :])

    def store(seq, st):
        for h in hs:
            gdn_o[seq, h] = st["s"][h]
            mlc_s[seq * ML_HEADS + h] = st["c"][h]
            ssd_o[seq, h] = st["h"][h]
        mm_o[seq, 0:1, :] = st["m"]

    def advance(i, st):
        prow = pl.ds(_row0(i, c) + SUBLANES, c)
        cum = cum_s[i]
        dmx = col_s[i]
        last = cum[c - 1:c, :]
        e_last = jnp.exp(last)
        mq = [proj_scr[prow, C_MQ + h * ML_DQK:C_MQ + (h + 1) * ML_DQK] for h in hs]
        r = [_dot(gqx_s[i * GDN_HEADS + h], st["s"][h].astype(gqx_s.dtype))
             for h in hs]
        full = [_dot(mq[h], st["c"][h]) for h in hs]
        yh = [_dot_nt(sce_s[i * SSD_HEADS + h], st["h"][h].astype(sce_s.dtype)) for h in hs]
        inter = cum + st["m"]
        m_t = jnp.maximum(inter, dmx)
        w_intra = jnp.exp(dmx - m_t)
        m_new = jnp.where(is_m, m_t[c - 1:c, :], 0.0)
        w_c_row = jnp.exp(last + st["m"] - m_new)
        w_l_row = w_intra[c - 1:c, :]
        ob = [gob_s[i * GDN_HEADS + h] for h in hs]
        blk = [ml_s[i * ML_HEADS + h] for h in hs]
        new = dict(
            s=[_col(e_last, L_GA + h) * st["s"][h] - r[h][c:, :] + ob[h][c:, :] for h in hs],
            c=[_col(w_c_row, L_MF + h) * st["c"][h] + _col(w_l_row, L_MF + h) * blk[h][c:c + ML_DQK, :]
               for h in hs],
            h=[_col(e_last, L_DT + h) * st["h"][h] + sdh_s[i * SSD_HEADS + h] for h in hs],
            m=m_new)
        return new, dict(i=i, r=r, full=full, yh=yh, ob=ob, blk=blk, inter=inter, m_t=m_t, w_intra=w_intra)

    def outputs(ctx):
        i = ctx["i"]
        r0 = _row0(i, c)
        rows = pl.ds(r0, c)
        prow = pl.ds(r0 + SUBLANES, c)
        w_inter = jnp.exp(ctx["inter"] - ctx["m_t"])
        e_neg_m = jnp.exp(-ctx["m_t"])
        for h in hs:
            gg = proj_scr[prow, C_GG + h * GDN_DV:C_GG + (h + 1) * GDN_DV]
            mix_scr[rows, h * GDN_DV:(h + 1) * GDN_DV] = (
                _rms(ctx["r"][h][:c, :] + ctx["ob"][h][:c, :], gdnn_ref[...]) * _silu(gg))
        h_parts = []
        for h in hs:
            fl = (ctx["full"][h] * _col(w_inter, L_MF + h)
                  + _col(ctx["w_intra"], L_MF + h) * ctx["blk"][h][0:c, :])
            den = jnp.maximum(jnp.abs(fl[:, ML_DV:ML_DV + 1]), _col(e_neg_m, L_MF + h))
            mo = proj_scr[prow, C_MO + h * ML_DV:C_MO + (h + 1) * ML_DV]
            h_parts.append(_rms(_sigmoid(mo) * (fl[:, :ML_DV] / den), mln_ref[...]))
        mix_scr[rows, GDN_HEADS * GDN_DV:GDN_HEADS * GDN_DV + ML_HEADS * ML_DV] = (
            jnp.concatenate(h_parts, axis=1))
        sz = proj_scr[prow, C_SZ:C_SZ + SSD_HEADS * SSD_P]
        y_all = (sy_s[i] + jnp.concatenate(ctx["yh"], axis=1)) * _silu(sz)
        mix_scr[rows, GDN_HEADS * GDN_DV + ML_HEADS * ML_DV:D_MODEL] = _rms(y_all, ssdn_ref[...])

    ctxs = []
    st = None
    for a in range(nb):
        i = j * nb + a
        seq = i if slot else 0
        if slot or a == 0:
            st = load(seq)
        st, ctx = advance(i, st)
        if slot or a == nb - 1:
            store(seq, st)
        ctxs.append(ctx)
    for ctx in ctxs:
        outputs(ctx)


def _ml_state_in(c_mat, n_row):
    eye = (lax.broadcasted_iota(jnp.int32, (ML_DQK, ML_DQK), 0)
           == lax.broadcasted_iota(jnp.int32, (ML_DQK, ML_DQK), 1)).astype(F32)
    n_col = jnp.sum(eye * n_row, axis=-1, keepdims=True)
    first = lax.broadcasted_iota(jnp.int32, (ML_DQK, 128 - ML_DV), 1) == 0
    return jnp.concatenate([c_mat, jnp.where(first, n_col, 0.0)], axis=1)


def _ml_state_out(c_aug):
    eye = (lax.broadcasted_iota(jnp.int32, (ML_DQK, ML_DQK), 0)
           == lax.broadcasted_iota(jnp.int32, (ML_DQK, ML_DQK), 1)).astype(F32)
    n_row = jnp.sum(eye * c_aug[:, ML_DV:ML_DV + 1], axis=0, keepdims=True)
    return c_aug[:, :ML_DV], n_row


def _mixer_kernel(*refs, tm, c, slot, nt, na, nbk, first):
    n_in = (17 if slot else 11) + (0 if first else 6)
    ins, (o_ref, *state_o), (proj_scr, mix_scr, *ab) = refs[:n_in], refs[n_in:n_in + 7], refs[n_in + 7:]
    if slot:
        (x_ref, tail_ref, win_ref, wout_ref, cw_ref, cb_ref, gprm_ref, npre_ref, npost_ref, gdnn_ref,
         mln_ref, ssdn_ref, gdn_i, mc_i, mn_i, mm_i, ssd_i) = ins[:17]
    else:
        (x_ref, win_ref, wout_ref, cw_ref, cb_ref, gprm_ref, npre_ref, npost_ref, gdnn_ref,
         mln_ref, ssdn_ref) = ins[:11]
    if first:
        for ref in state_o:
            ref[1] = jnp.zeros(ref.shape[1:], F32)
        state_o = [ref.at[0] for ref in state_o]
    conv_o, gdn_o, mc_o, mn_o, mm_o, ssd_o = state_o
    names = ("cum_s", "col_s", "gqx_s", "gob_s", "ml_s", "sy_s", "sce_s", "sdh_s", "mlc_s")
    ab = dict(zip(names, ab))
    mlc_s = ab["mlc_s"]
    nseq = gdn_o.shape[0]
    t = pl.program_id(1)

    if slot:
        proj_scr[0:SUBLANES, :] = jnp.zeros((SUBLANES, N_IN), F32)
        gdn_o[...] = gdn_i[...]
        mm_o[...] = mm_i[...]
        ssd_o[...] = ssd_i[...]

        def load_ml(s, carry):
            for h in range(ML_HEADS):
                mlc_s[s * ML_HEADS + h] = _ml_state_in(mc_i[s, h], mn_i[s, h:h + 1, :])
            return carry

        lax.fori_loop(0, nseq, load_ml, 0)
    else:
        @pl.when(t == 0)
        def _():
            proj_scr[0:SUBLANES, :] = jnp.zeros((SUBLANES, N_IN), F32)
            gdn_o[...] = jnp.zeros(gdn_o.shape, F32)
            mlc_s[...] = jnp.zeros(mlc_s.shape, F32)
            mm_o[...] = jnp.zeros(mm_o.shape, F32)
            ssd_o[...] = jnp.zeros(ssd_o.shape, F32)

    x = x_ref[...]
    hn = _rms(x, npre_ref[...]).astype(BF16)

    def in_proj(r_lo, r_hi, nb):
        cols = slice(nb * IN_NBLK, (nb + 1) * IN_NBLK)
        proj_scr[SUBLANES + r_lo:SUBLANES + r_hi, cols] = _dot(hn[r_lo:r_hi, :], win_ref[:, cols])

    def out_proj(r_lo, r_hi):
        out = _dot(mix_scr[r_lo:r_hi, :].astype(BF16), wout_ref[...])
        o_ref[r_lo:r_hi, :] = x[r_lo:r_hi, :] + _rms(out, npost_ref[...])

    def weave(major, minor, minor_first=False):
        major, minor = list(major), list(minor)
        stride = max(1, len(major) // max(1, len(minor)))
        while major or minor:
            if minor and minor_first:
                minor.pop(0)()
            for step in major[:stride]:
                step()
            major = major[stride:]
            if minor and not minor_first:
                minor.pop(0)()

    n_in_blk = N_IN // IN_NBLK
    if slot:
        for nb in range(n_in_blk):
            in_proj(0, tm, nb)
        for s in range(nseq):
            base = SUBLANES + s * SLOT
            proj_scr[base:base + SLOT_FIRST, 0:CONV_DIM] = tail_ref[s]
            conv_o[s] = proj_scr[base + SLOT_LAST - 2:base + SLOT_LAST + 1, 0:CONV_DIM]

    pass_a = functools.partial(_pass_a, c=c, slot=slot, na=na, proj_scr=proj_scr, cw_ref=cw_ref,
                               cb_ref=cb_ref, gprm_ref=gprm_ref,
                               **{k: v for k, v in ab.items() if k != "mlc_s"})
    pass_b = functools.partial(_pass_b, c=c, slot=slot, nb=nbk, proj_scr=proj_scr, mix_scr=mix_scr,
                               gdnn_ref=gdnn_ref, mln_ref=mln_ref, ssdn_ref=ssdn_ref,
                               gdn_o=gdn_o, mm_o=mm_o, ssd_o=ssd_o, **ab)

    n_a, n_b = tm // (c * na), tm // (c * nbk)
    if slot:
        def body_a(j, carry):
            pass_a(j)
            return carry

        def body_b(j, carry):
            pass_b(j)
            return carry

        lax.fori_loop(0, n_a, body_a, 0)
        lax.fori_loop(0, n_b, body_b, 0)
        out_proj(0, tm)
    else:
        assert n_a % 2 == 0 and n_b % n_a == 0
        half, b_per_a = tm // 2, n_b // n_a
        for nb in range(n_in_blk):
            in_proj(0, half, nb)
        weave(pass_a(0, defer=True), [functools.partial(in_proj, half, tm, nb) for nb in range(n_in_blk)],
              minor_first=True)
        for g in range(1, n_a):
            weave(pass_a(g, defer=True), [functools.partial(pass_b, (g - 1) * b_per_a + k) for k in range(b_per_a)])
        weave([functools.partial(pass_b, (n_a - 1) * b_per_a + k) for k in range(b_per_a)],
              [functools.partial(out_proj, 0, half)])
        out_proj(half, tm)

    def store_ml(s, carry):
        for h in range(ML_HEADS):
            c_mat, n_row = _ml_state_out(mlc_s[s * ML_HEADS + h])
            mc_o[s, h] = c_mat
            mn_o[s, h:h + 1, :] = n_row
        return carry

    if slot:
        lax.fori_loop(0, nseq, store_ml, 0)
    else:
        last_rows = proj_scr[tm:tm + SUBLANES, 0:CONV_DIM]
        proj_scr[0:SUBLANES, 0:CONV_DIM] = last_rows

        @pl.when(t == nt - 1)
        def _():
            conv_o[0] = last_rows[SUBLANES - 3:, :]
            store_ml(0, 0)


def _ffn_kernel(*refs, tm, slot, nt):
    if slot:
        (x_ref, ftail_ref, wup_ref, wdn_ref, fw_ref, fb_ref, npre_ref, npost_ref, _,
         o_ref, gate_o, tails_scr, act_scr) = refs
    else:
        (x_ref, wup_ref, wdn_ref, fw_ref, fb_ref, npre_ref, npost_ref, _,
         o_ref, gate_o, tails_scr, act_scr) = refs
    t = pl.program_id(1)

    if slot:
        tails_scr[...] = jnp.zeros(tails_scr.shape, F32)
    else:
        @pl.when(t == 0)
        def _():
            tails_scr[...] = jnp.zeros(tails_scr.shape, F32)

    x = x_ref[...]
    hn = _rms(x, npre_ref[...]).astype(BF16)
    nblk = D_FF // FF_BLK

    def up(blk):
        return (_dot(hn, wup_ref[:, blk * FF_BLK:(blk + 1) * FF_BLK]),
                _dot(hn, wup_ref[:, D_FF + blk * FF_BLK:D_FF + (blk + 1) * FF_BLK]))

    ahead = up(0)
    for blk in range(nblk):
        cols = slice(blk * FF_BLK, (blk + 1) * FF_BLK)
        gate, val = ahead
        if blk + 1 < nblk:
            ahead = up(blk + 1)
        if slot:
            g3 = gate.reshape(tm // SLOT, SLOT, FF_BLK)
            ft = ftail_ref[:, :, cols]
            rr = lax.broadcasted_iota(jnp.int32, g3.shape, 1)
            g3 = jnp.where(rr == SLOT_FIRST - 2, ft[:, 0:1, :], jnp.where(rr == SLOT_FIRST - 1, ft[:, 1:2, :], g3))
            gate_o[:, :, cols] = g3[:, SLOT_LAST - 1:SLOT_LAST + 1, :]
            gate = g3.reshape(tm, FF_BLK)
        full = jnp.concatenate([tails_scr[:, cols], gate], axis=0)
        conv = fb_ref[0:1, cols] + fw_ref[2:3, cols] * gate
        for j in range(2):
            conv = conv + fw_ref[j:j + 1, cols] * pltpu.roll(full, 2 - j, 0)[SUBLANES:, :]
        last_rows = full[tm:tm + SUBLANES, :]
        tails_scr[:, cols] = last_rows
        act_scr[:, cols] = (_gelu_tanh(conv) * val).astype(BF16)
    o_ref[...] = x + _rms(_dot(act_scr[...], wdn_ref[...]), npost_ref[...])
    if not slot:
        @pl.when(t == nt - 1)
        def _():
            gate_o[0] = tails_scr[SUBLANES - 2:, :]


def _const_spec(shape, layer):
    nd = len(shape)
    return pl.BlockSpec((None,) + tuple(shape), lambda b, t: (layer,) + (0,) * nd,
                        pipeline_mode=pl.Buffered(1))


def _state_dims():
    return ((3, CONV_DIM), (GDN_HEADS, GDN_DK, GDN_DV), (ML_HEADS, ML_DQK, ML_DV), (ML_HEADS, ML_DQK),
            (SUBLANES, 128), (SSD_HEADS, SSD_P, SSD_N))


def _mixer_call(x, layer, prm, bufs, *, slot, states=None, tail=None):
    rows = x.shape[0]
    if slot:
        tm, c, nt, na, nbk = SAMPLE_TM, SLOT, 1, SAMPLE_NA, SAMPLE_NB
        nseq = tm // SLOT
    else:
        tm, c, na, nbk = PROMPT_TM, PROMPT_CHUNK, PROMPT_NA, PROMPT_NB
        nt = 2048 // tm
        nseq = 1
    nch = tm // c
    ngrp = rows // (tm * nt)
    nb = ngrp * nseq
    row_spec = lambda w: pl.BlockSpec((tm, w), lambda b, t: (b * nt + t, 0))
    st_in = lambda *dims: pl.BlockSpec((None, nseq) + dims, lambda b, t: (layer, b) + (0,) * len(dims))

    in_specs = [row_spec(D_MODEL)]
    args = [x]
    if slot:
        in_specs.append(st_in(SLOT_FIRST, CONV_DIM))
        args.append(tail)
    in_specs += [
        _const_spec((D_MODEL, N_IN), layer), _const_spec((D_MODEL, D_MODEL), layer),
        _const_spec((4, CONV_DIM), layer), _const_spec((1, CONV_DIM), layer),
        _const_spec((8, 128), layer), _const_spec((1, D_MODEL), layer), _const_spec((1, D_MODEL), layer),
        _const_spec((1, GDN_DV), layer), _const_spec((1, ML_DV), layer),
        _const_spec((1, SSD_HEADS * SSD_P), layer)]
    args += [prm["w_in"], prm["w_out"], prm["conv_w"], prm["conv_b"], prm["gprm"], prm["norm_mix_pre"],
             prm["norm_mix_post"], prm["gdn_norm"], prm["mlstm_norm"], prm["ssd_norm"]]
    if slot:
        in_specs += [st_in(GDN_HEADS, GDN_DK, GDN_DV), st_in(ML_HEADS, ML_DQK, ML_DV),
                     st_in(ML_HEADS, ML_DQK), st_in(SUBLANES, 128), st_in(SSD_HEADS, SSD_P, SSD_N)]
        args += list(states)
    first = bufs is None
    if first:
        aliases = {}
        st_o = lambda *dims: pl.BlockSpec((DEPTH, nseq) + dims, lambda b, t: (0, b) + (0,) * len(dims))
    else:
        aliases = {len(args) + k: 1 + k for k in range(len(bufs))}
        in_specs += [pl.BlockSpec(memory_space=pl.ANY)] * len(bufs)
        args += list(bufs)
        st_o = st_in

    out_specs = [row_spec(D_MODEL)] + [st_o(*dims) for dims in _state_dims()]
    out_shape = [jax.ShapeDtypeStruct((rows, D_MODEL), F32)] + [
        jax.ShapeDtypeStruct((DEPTH, nb) + dims, F32) for dims in _state_dims()]
    lhs_dt = F32 if slot else BF16
    scratch = [
        pltpu.VMEM((tm + SUBLANES, N_IN), F32),
        pltpu.VMEM((tm, D_MODEL), F32),
        pltpu.VMEM((nch, c, 128), F32),
        pltpu.VMEM((nch, c, 128), F32),
        pltpu.VMEM((nch * GDN_HEADS, c + GDN_DK, GDN_DV), lhs_dt),
        pltpu.VMEM((nch * GDN_HEADS, c + GDN_DK, GDN_DV), F32),
        pltpu.VMEM((nch * ML_HEADS, c + ML_DQK, 128), F32),
        pltpu.VMEM((nch, c, SSD_HEADS * SSD_P), F32),
        pltpu.VMEM((nch * SSD_HEADS, c, SSD_N), lhs_dt),
        pltpu.VMEM((nch * SSD_HEADS, SSD_P, SSD_N), F32),
        pltpu.VMEM((nseq * ML_HEADS, ML_DQK, 128), F32),
    ]
    return pl.pallas_call(
        functools.partial(_mixer_kernel, tm=tm, c=c, slot=slot, nt=nt, na=na, nbk=nbk, first=first),
        grid=(ngrp, nt), in_specs=in_specs, out_specs=out_specs, out_shape=out_shape,
        scratch_shapes=scratch, input_output_aliases=aliases,
        compiler_params=pltpu.CompilerParams(dimension_semantics=("arbitrary", "arbitrary"),
                                             vmem_limit_bytes=VMEM_LIMIT),
        name=("mixer_sample" if slot else "mixer_prompt"),
    )(*args)


def _ffn_call(x, layer, prm, buf, *, slot, tail=None):
    rows = x.shape[0]
    tm = FFN_TM
    nt = 1 if slot else 2048 // tm
    ngrp = rows // (tm * nt)
    row_spec = lambda w: pl.BlockSpec((tm, w), lambda b, t: (b * nt + t, 0))
    in_specs = [row_spec(D_MODEL)]
    args = [x]
    nseq = tm // SLOT if slot else 1
    if slot:
        in_specs.append(pl.BlockSpec((None, nseq, 2, D_FF), lambda b, t: (layer, b, 0, 0)))
        args.append(tail)
    in_specs += [_const_spec((D_MODEL, 2 * D_FF), layer), _const_spec((D_FF, D_MODEL), layer),
                 _const_spec((3, D_FF), layer), _const_spec((1, D_FF), layer),
                 _const_spec((1, D_MODEL), layer), _const_spec((1, D_MODEL), layer)]
    args += [prm["ffn_w_up"], prm["ffn_w_down"], prm["ffn_conv_w"], prm["ffn_conv_b"],
             prm["norm_ffn_pre"], prm["norm_ffn_post"]]
    aliases = {len(args): 1}
    in_specs.append(pl.BlockSpec(memory_space=pl.ANY))
    args.append(buf)
    gate_spec = pl.BlockSpec((None, nseq, 2, D_FF), lambda b, t: (layer, b, 0, 0))
    gate_shape = (DEPTH, ngrp * nseq, 2, D_FF)
    return pl.pallas_call(
        functools.partial(_ffn_kernel, tm=tm, slot=slot, nt=nt),
        grid=(ngrp, nt), in_specs=in_specs, out_specs=[row_spec(D_MODEL), gate_spec],
        out_shape=[jax.ShapeDtypeStruct((rows, D_MODEL), F32), jax.ShapeDtypeStruct(gate_shape, F32)],
        scratch_shapes=[pltpu.VMEM((SUBLANES, D_FF), F32), pltpu.VMEM((tm, D_FF), BF16)],
        input_output_aliases=aliases,
        compiler_params=pltpu.CompilerParams(dimension_semantics=("arbitrary", "arbitrary"),
                                             vmem_limit_bytes=VMEM_LIMIT),
        name=("ffn_sample" if slot else "ffn_prompt"),
    )(*args)


def _prepare_params(norm_mix_pre, norm_mix_post, norm_ffn_pre, norm_ffn_post, w_in, conv_w, conv_b,
                    gdn_a_log, gdn_dt_bias, gdn_norm, mlstm_i_bias, mlstm_f_bias, mlstm_norm,
                    ssd_a_log, ssd_dt_bias, ssd_d, ssd_norm, w_out, ffn_w_up, ffn_conv_w, ffn_conv_b,
                    ffn_w_down):
    w_in_p = jnp.concatenate(
        [w_in[..., :2816], w_in[..., 2824:3848], w_in[..., 3856:4112], w_in[..., 2816:2824],
         w_in[..., 3848:3856], w_in[..., 4112:4116],
         jnp.zeros((DEPTH, D_MODEL, N_IN - 4116), w_in.dtype)], axis=-1).astype(BF16)
    z4 = jnp.zeros((DEPTH, 4), F32)
    pad = jnp.zeros((DEPTH, 128 - 20), F32)
    gprm = jnp.stack(
        [jnp.concatenate([gdn_dt_bias, z4, mlstm_i_bias, mlstm_f_bias, ssd_dt_bias, pad], axis=-1),
         jnp.concatenate([gdn_a_log, z4, z4, z4, ssd_a_log, pad], axis=-1),
         jnp.concatenate([z4, z4, z4, z4, ssd_d, pad], axis=-1)]
        + [jnp.zeros((DEPTH, 128), F32)] * 5, axis=1)
    row = lambda a: a[:, None, :]
    return dict(
        w_in=w_in_p, w_out=w_out.astype(BF16), conv_w=conv_w, conv_b=row(conv_b), gprm=gprm,
        norm_mix_pre=row(norm_mix_pre), norm_mix_post=row(norm_mix_post),
        norm_ffn_pre=row(norm_ffn_pre), norm_ffn_post=row(norm_ffn_post),
        gdn_norm=row(gdn_norm), mlstm_norm=row(mlstm_norm), ssd_norm=row(ssd_norm),
        ffn_w_up=ffn_w_up.astype(BF16), ffn_w_down=ffn_w_down.astype(BF16),
        ffn_conv_w=ffn_conv_w, ffn_conv_b=row(ffn_conv_b))


def kernel(x_prompt, x_sample, state_conv, state_gdn, state_mlstm_c, state_mlstm_n, state_mlstm_m, state_ssd, state_ffn_conv, norm_mix_pre, norm_mix_post, norm_ffn_pre, norm_ffn_post, w_in, conv_w, conv_b, gdn_a_log, gdn_dt_bias, gdn_norm, mlstm_i_bias, mlstm_f_bias, mlstm_norm, ssd_a_log, ssd_dt_bias, ssd_d, ssd_norm, w_out, ffn_w_up, ffn_conv_w, ffn_conv_b, ffn_w_down):
    prm = _prepare_params(norm_mix_pre, norm_mix_post, norm_ffn_pre, norm_ffn_post, w_in, conv_w, conv_b,
                          gdn_a_log, gdn_dt_bias, gdn_norm, mlstm_i_bias, mlstm_f_bias, mlstm_norm,
                          ssd_a_log, ssd_dt_bias, ssd_d, ssd_norm, w_out, ffn_w_up, ffn_conv_w,
                          ffn_conv_b, ffn_w_down)
    bp, lp, _ = x_prompt.shape
    bs, ls, _ = x_sample.shape

    def state_bufs(nseq):
        return None, jnp.zeros((DEPTH, nseq, 2, D_FF), F32)

    def assemble(bufs, gate):
        conv, gdn, mc, mn, mm, ssd = bufs
        return [conv, gdn, mc, mn, mm[:, :, 0, L_MF:L_MF + ML_HEADS], ssd, gate]

    x = x_prompt.reshape(bp * lp, D_MODEL)
    bufs, gate = state_bufs(bp)
    for layer in range(DEPTH):
        x, *bufs = _mixer_call(x, layer, prm, bufs, slot=False)
        x, gate = _ffn_call(x, layer, prm, gate, slot=False)
    p_out = assemble(bufs, gate)
    y_prompt = x.reshape(bp, lp, D_MODEL)

    x = jnp.pad(x_sample, ((0, 0), (SLOT_FIRST, SLOT - SLOT_FIRST - ls), (0, 0))).reshape(bs * SLOT, D_MODEL)
    mm_in = jnp.pad(state_mlstm_m[:, :, None, :],
                    ((0, 0), (0, 0), (0, SUBLANES - 1), (L_MF, 128 - L_MF - ML_HEADS)))
    bufs, gate = state_bufs(bs)
    for layer in range(DEPTH):
        x, *bufs = _mixer_call(x, layer, prm, bufs, slot=True, tail=state_conv,
                               states=(state_gdn, state_mlstm_c, state_mlstm_n, mm_in, state_ssd))
        x, gate = _ffn_call(x, layer, prm, gate, slot=True, tail=state_ffn_conv)
    s_out = assemble(bufs, gate)
    y_sample = x.reshape(bs, SLOT, D_MODEL)[:, SLOT_FIRST:SLOT_LAST + 1]
    return (y_prompt, y_sample, *p_out, *s_out)
```

```python
import functools

import jax
import jax.numpy as jnp
from jax import lax
from jax.experimental import pallas as pl
from jax.experimental.pallas import tpu as pltpu

F32 = jnp.float32
BF16 = jnp.bfloat16

D_MODEL = 1024
DEPTH = 2
GDN_HEADS, GDN_DK, GDN_DV = 4, 128, 128
ML_HEADS, ML_DQK, ML_DV = 4, 64, 64
SSD_HEADS, SSD_P, SSD_GROUPS, SSD_N = 4, 64, 2, 128
D_FF = 2816
EPS = 1e-6

CONV_DIM = 2304
C_GQ, C_GK, C_GV, C_SX, C_SB, C_SC = 0, 512, 1024, 1536, 1792, 2048
C_GG, C_MQ, C_MK, C_MV, C_MO, C_SZ, C_GATE = 2304, 2816, 3072, 3328, 3584, 3840, 4096
N_IN = 4224
IN_NBLK = 1408
L_GA, L_GB, L_MI, L_MF, L_DT = 0, 4, 8, 12, 16

SUBLANES = 8
SLOT = 8
SLOT_FIRST, SLOT_LAST = 3, 6
FF_BLK = 256
VMEM_LIMIT = 56 * 1024 * 1024

PROMPT_TM, PROMPT_CHUNK, PROMPT_NA, PROMPT_NB = 512, 64, 4, 1
SAMPLE_TM, SAMPLE_NA, SAMPLE_NB = 64, 8, 8
FFN_TM = 1024


def _dot(a, b):
    return jnp.dot(a, b, preferred_element_type=F32)


def _dot_nt(a, b):
    return lax.dot_general(a, b, (((1,), (1,)), ((), ())), preferred_element_type=F32)


def _dot_tn(a, b):
    return lax.dot_general(a, b, (((0,), (0,)), ((), ())), preferred_element_type=F32)


def _sigmoid(x):
    return 1.0 / (1.0 + jnp.exp(-x))


def _silu(x):
    return x * _sigmoid(x)


def _rms(x, w):
    return x * lax.rsqrt(jnp.mean(x * x, axis=-1, keepdims=True) + EPS) * w


def _l2n(x):
    return x * lax.rsqrt(jnp.sum(x * x, axis=-1, keepdims=True) + EPS)


def _gelu_tanh(x):
    return 0.5 * x * (1.0 + jnp.tanh(0.7978845608028654 * (x + 0.044715 * (x * x * x))))


def _col(a, l):
    return a[:, l:l + 1]


def _cumsum_rows(tril_b, x):
    hi = x.astype(BF16)
    r1 = x - hi.astype(F32)
    mid = r1.astype(BF16)
    lo = (r1 - mid.astype(F32)).astype(BF16)
    return _dot(tril_b, hi) + _dot(tril_b, mid) + _dot(tril_b, lo)


def _conv_silu(proj_scr, r0, c, col0, width, cw_ref, cb_ref):
    win = proj_scr[pl.ds(r0, c + SUBLANES), col0:col0 + width]
    acc = cw_ref[0:1, col0:col0 + width] * win
    for j in range(1, 4):
        acc = cw_ref[j:j + 1, col0:col0 + width] * win + pltpu.roll(acc, 1, 0)
    return _silu(acc[SUBLANES:, :] + cb_ref[0:1, col0:col0 + width])


def _gates(proj_scr, r0, c, slot, gprm_ref, tril_b):
    graw = proj_scr[pl.ds(r0 + SUBLANES, c), C_GATE:C_GATE + 128]
    lane = lax.broadcasted_iota(jnp.int32, (c, 128), 1)
    z = graw + gprm_ref[0:1, :]
    soft = jnp.log(1.0 + jnp.exp(-jnp.abs(z)))
    sp = jnp.maximum(z, 0.0) + soft
    log_sig = -(jnp.maximum(-z, 0.0) + soft)
    a_neg = -jnp.exp(gprm_ref[1:2, :])
    is_ga = lane < L_GB
    is_gb = (lane >= L_GB) & (lane < L_MI)
    is_mi = (lane >= L_MI) & (lane < L_MF)
    is_mf = (lane >= L_MF) & (lane < L_DT)
    is_dt = (lane >= L_DT) & (lane < L_DT + SSD_HEADS)
    cum_src = jnp.where(is_ga | is_dt, a_neg * sp, jnp.where(is_mf, log_sig, 0.0))
    elem = jnp.where(is_gb, _sigmoid(graw), jnp.where(is_mi, z, jnp.where(is_dt, sp, 0.0)))
    if slot:
        rr = lax.broadcasted_iota(jnp.int32, (c, 128), 0)
        valid = (rr >= SLOT_FIRST) & (rr <= SLOT_LAST)
        cum_src = jnp.where(valid, cum_src, 0.0)
        elem = jnp.where(valid, elem, jnp.where(is_mi, -jnp.inf, 0.0))
    cum = _cumsum_rows(tril_b, cum_src)
    if c == 128:
        return cum, elem, cum.T, elem.T
    parts = [cum, elem]
    if 2 * c < 128:
        parts.append(jnp.zeros((128 - 2 * c, 128), F32))
    zt = jnp.concatenate(parts, axis=0).T
    return cum, elem, zt[:, 0:c], zt[:, c:2 * c]


def _row0(i, c):
    return i * c if isinstance(i, int) else pl.multiple_of(i * c, c)


def _pass_a(j, *, c, slot, na, proj_scr, cw_ref, cb_ref, gprm_ref,
            cum_s, col_s, gqx_s, gob_s, ml_s, sy_s, sce_s, sdh_s, defer=False):
    chunks = [j * na + a for a in range(na)]
    r0s = [_row0(i, c) for i in chunks]
    conv = functools.partial(_conv_silu, proj_scr, c=c, cw_ref=cw_ref, cb_ref=cb_ref)

    ii = lax.broadcasted_iota(jnp.int32, (c, c), 0)
    jj = lax.broadcasted_iota(jnp.int32, (c, c), 1)
    tril = ii >= jj
    strict = ii > jj
    eye = (ii == jj).astype(F32)
    tril_b = tril.astype(BF16)
    lane = lax.broadcasted_iota(jnp.int32, (c, 128), 1)

    gates = [_gates(proj_scr, r0, c, slot, gprm_ref, tril_b) for r0 in r0s]
    cum = [g[0] for g in gates]
    elem = [g[1] for g in gates]
    cum_t = [g[2] for g in gates]
    elem_t = [g[3] for g in gates]
    for a in range(na):
        cum_s[chunks[a]] = cum[a]

    def row_cum(a, l):
        return cum_t[a][l:l + 1, :]

    def row_elem(a, l):
        return elem_t[a][l:l + 1, :]

    it = [(a, h) for a in range(na) for h in range(GDN_HEADS)]
    n = range(len(it))
    gd, ml, sd = {}, {}, {}

    def g_k():
        k = [_l2n(conv(r0=r0s[a], col0=C_GK + h * GDN_DK, width=GDN_DK)) for a, h in it]
        gam_c = [_col(cum[a], L_GA + h) for a, h in it]
        beta_c = [_col(elem[a], L_GB + h) for a, h in it]
        dmat = [jnp.exp(jnp.where(tril, gam_c[x] - row_cum(a, L_GA + h), -jnp.inf))
                for x, (a, h) in enumerate(it)]
        pk = [-jnp.where(strict, _dot_nt(k[x], k[x]) * dmat[x] * beta_c[x], 0.0) for x in n]
        gd.update(k=k, gam_c=gam_c, beta_c=beta_c, dmat=dmat, pk=pk, t_inv=[eye + pk[x] for x in n])

    def g_neumann():
        pb = [gd["pk"][x].astype(BF16) for x in n]
        pk = [_dot(pb[x], pb[x]) for x in n]
        gd.update(pk=pk, t_inv=[gd["t_inv"][x] + _dot(gd["t_inv"][x].astype(BF16), pk[x].astype(BF16))
                                for x in n])

    def g_q():
        gd["q"] = [_l2n(conv(r0=r0s[a], col0=C_GQ + h * GDN_DK, width=GDN_DK)) * (GDN_DK ** -0.5)
                   for a, h in it]

    def g_v():
        v = [conv(r0=r0s[a], col0=C_GV + h * GDN_DV, width=GDN_DV) for a, h in it]
        eg = [jnp.exp(gd["gam_c"][x]) for x in n]
        gd.update(eg=eg, rhs=[jnp.concatenate([v[x] * gd["beta_c"][x],
                                               gd["k"][x] * (gd["beta_c"][x] * eg[x])], axis=1) for x in n])

    def g_uw():
        gd["uw"] = [_dot(gd["t_inv"][x], gd["rhs"][x]) for x in n]

    def g_qk():
        gd["qk"] = [_dot_nt(gd["q"][x], gd["k"][x]) * gd["dmat"][x] for x in n]
        gd["kd"] = [gd["k"][x] * jnp.exp(gd["gam_c"][x][c - 1:c, :] - gd["gam_c"][x]) for x in n]

    def g_out():
        quw = [_dot(gd["qk"][x], gd["uw"][x]) for x in n]
        kuw = [_dot_tn(gd["kd"][x], gd["uw"][x]) for x in n]
        for x, (a, h) in enumerate(it):
            idx = chunks[a] * GDN_HEADS + h
            gqx_s[idx, 0:c, :] = (gd["q"][x] * gd["eg"][x] - quw[x][:, GDN_DV:]).astype(gqx_s.dtype)
            gqx_s[idx, c:c + GDN_DK, :] = kuw[x][:, GDN_DV:].astype(gqx_s.dtype)
            gob_s[idx, 0:c, :] = quw[x][:, :GDN_DV]
            gob_s[idx, c:c + GDN_DK, :] = kuw[x][:, :GDN_DV]

    def piece(a, base, h):
        return proj_scr[pl.ds(r0s[a] + SUBLANES, c), base + h * ML_DQK:base + (h + 1) * ML_DQK]

    def m_1():
        mq = [piece(a, C_MQ, h) for a, h in it]
        mk = [piece(a, C_MK, h) * (ML_DQK ** -0.5) for a, h in it]
        b_c = [_col(cum[a], L_MF + h) for a, h in it]
        d = [jnp.where(tril, b_c[x] - row_cum(a, L_MF + h) + row_elem(a, L_MI + h), -jnp.inf)
             for x, (a, h) in enumerate(it)]
        dmax = [jnp.max(d[x], axis=-1, keepdims=True) for x in n]
        dsafe = [jnp.where(dmax[x] == -jnp.inf, 0.0, dmax[x]) for x in n]
        s0 = [_dot_nt(mq[x], mk[x]) * jnp.exp(d[x] - dsafe[x]) for x in n]
        for a in range(na):
            cols = jnp.zeros((c, 128), F32)
            for h in range(ML_HEADS):
                cols = jnp.where(lane == L_MF + h, dmax[a * ML_HEADS + h], cols)
            col_s[chunks[a]] = cols
        ml.update(mk=mk, b_c=b_c, dsafe=dsafe, s0=s0)

    def m_2():
        mv = [piece(a, C_MV, h) for a, h in it]
        i_c = [_col(elem[a], L_MI + h) for a, h in it]
        ones_col = (lax.broadcasted_iota(jnp.int32, (c, 128 - ML_DV), 1) == 0).astype(F32)
        v_aug = [jnp.concatenate([mv[x], ones_col], axis=1) for x in n]
        num0 = [_dot(ml["s0"][x], v_aug[x]) for x in n]
        b_c, dsafe = ml["b_c"], ml["dsafe"]
        kw0 = [ml["mk"][x] * jnp.exp(b_c[x][c - 1:c, :] - b_c[x] + i_c[x] - dsafe[x][c - 1:c, :]) for x in n]
        kv0 = [_dot_tn(kw0[x], v_aug[x]) for x in n]
        for x, (a, h) in enumerate(it):
            idx = chunks[a] * ML_HEADS + h
            ml_s[idx, 0:c, :] = num0[x]
            ml_s[idx, c:c + ML_DQK, :] = kv0[x]

    rep = SSD_HEADS // SSD_GROUPS
    gi = [(a, g) for a in range(na) for g in range(SSD_GROUPS)]
    grp = [a * SSD_GROUPS + h // rep for a, h in it]

    def s_1():
        bg = [conv(r0=r0s[a], col0=C_SB + g * SSD_N, width=SSD_N) for a, g in gi]
        cg = [conv(r0=r0s[a], col0=C_SC + g * SSD_N, width=SSD_N) for a, g in gi]
        sd.update(bg=bg, cg=cg, cb_raw=[_dot_nt(cg[y], bg[y]) for y in range(len(gi))])

    def s_2():
        xg = [conv(r0=r0s[a], col0=C_SX + g * rep * SSD_P, width=rep * SSD_P) for a, g in gi]
        xs = [xg[grp[x]][:, (h % rep) * SSD_P:(h % rep + 1) * SSD_P] for x, (a, h) in enumerate(it)]
        sg_c = [_col(cum[a], L_DT + h) for a, h in it]
        dt_c = [_col(elem[a], L_DT + h) for a, h in it]
        cb = [sd["cb_raw"][grp[x]] * jnp.exp(jnp.where(tril, sg_c[x] - row_cum(a, L_DT + h), -jnp.inf))
              * row_elem(a, L_DT + h) for x, (a, h) in enumerate(it)]
        y0 = [_dot(cb[x], xs[x]) + gprm_ref[2:3, L_DT + h:L_DT + h + 1] * xs[x]
              for x, (a, h) in enumerate(it)]
        dh = [_dot_tn(xs[x] * (jnp.exp(sg_c[x][c - 1:c, :] - sg_c[x]) * dt_c[x]), sd["bg"][grp[x]]) for x in n]
        for x, (a, h) in enumerate(it):
            idx = chunks[a] * SSD_HEADS + h
            sce_s[idx] = (sd["cg"][grp[x]] * jnp.exp(sg_c[x])).astype(sce_s.dtype)
            sdh_s[idx] = dh[x]
        for a in range(na):
            sy_s[chunks[a]] = jnp.concatenate(y0[a * SSD_HEADS:(a + 1) * SSD_HEADS], axis=1)

    chain = [g_k] + [g_neumann] * (c.bit_length() - 2) + [g_uw]
    if c >= 64:
        fill = [m_1, m_2, s_1, s_2, g_q, g_v]
    else:
        fill = [g_q, g_v, m_1, m_2, s_1, s_2]
    order = []
    for pos, link in enumerate(chain):
        order.append(link)
        if pos < len(fill):
            order.append(fill[pos])
    order += fill[len(chain):] + [g_qk, g_out]
    if defer:
        return order
    for stage in order:
        stage()


def _pass_b(j, *, c, slot, nb, proj_scr, mix_scr, gdnn_ref, mln_ref, ssdn_ref,
            cum_s, col_s, gqx_s, gob_s, ml_s, sy_s, sce_s, sdh_s, mlc_s, gdn_o, mm_o, ssd_o):
    hs = range(GDN_HEADS)
    lane_row = lax.broadcasted_iota(jnp.int32, (1, 128), 1)
    is_m = (lane_row >= L_MF) & (lane_row < L_MF + ML_HEADS)

    def load(seq):
        return dict(s=[gdn_o[seq, h] for h in hs], c=[mlc_s[seq * ML_HEADS + h] for h in hs],
                    h=[ssd_o[seq, h] for h in hs], m=mm_o[seq, 0:1, :])

    def store(seq, st):
        for h in hs:
            gdn_o[seq, h] = st["s"][h]
            mlc_s[seq * ML_HEADS + h] = st["c"][h]
            ssd_o[seq, h] = st["h"][h]
        mm_o[seq, 0:1, :] = st["m"]

    def advance(i, st):
        prow = pl.ds(_row0(i, c) + SUBLANES, c)
        cum = cum_s[i]
        dmx = col_s[i]
        last = cum[c - 1:c, :]
        e_last = jnp.exp(last)
        mq = [proj_scr[prow, C_MQ + h * ML_DQK:C_MQ + (h + 1) * ML_DQK] for h in hs]
        r = [_dot(gqx_s[i * GDN_HEADS + h], st["s"][h].astype(gqx_s.dtype))
             for h in hs]
        full = [_dot(mq[h], st["c"][h]) for h in hs]
        yh = [_dot_nt(sce_s[i * SSD_HEADS + h], st["h"][h].astype(sce_s.dtype)) for h in hs]
        inter = cum + st["m"]
        m_t = jnp.maximum(inter, dmx)
        w_intra = jnp.exp(dmx - m_t)
        m_new = jnp.where(is_m, m_t[c - 1:c, :], 0.0)
        w_c_row = jnp.exp(last + st["m"] - m_new)
        w_l_row = w_intra[c - 1:c, :]
        ob = [gob_s[i * GDN_HEADS + h] for h in hs]
        blk = [ml_s[i * ML_HEADS + h] for h in hs]
        new = dict(
            s=[_col(e_last, L_GA + h) * st["s"][h] - r[h][c:, :] + ob[h][c:, :] for h in hs],
            c=[_col(w_c_row, L_MF + h) * st["c"][h] + _col(w_l_row, L_MF + h) * blk[h][c:c + ML_DQK, :]
               for h in hs],
            h=[_col(e_last, L_DT + h) * st["h"][h] + sdh_s[i * SSD_HEADS + h] for h in hs],
            m=m_new)
        return new, dict(i=i, r=r, full=full, yh=yh, ob=ob, blk=blk, inter=inter, m_t=m_t, w_intra=w_intra)

    def outputs(ctx):
        i = ctx["i"]
        r0 = _row0(i, c)
        rows = pl.ds(r0, c)
        prow = pl.ds(r0 + SUBLANES, c)
        w_inter = jnp.exp(ctx["inter"] - ctx["m_t"])
        e_neg_m = jnp.exp(-ctx["m_t"])
        for h in hs:
            gg = proj_scr[prow, C_GG + h * GDN_DV:C_GG + (h + 1) * GDN_DV]
            mix_scr[rows, h * GDN_DV:(h + 1) * GDN_DV] = (
                _rms(ctx["r"][h][:c, :] + ctx["ob"][h][:c, :], gdnn_ref[...]) * _silu(gg))
        h_parts = []
        for h in hs:
            fl = (ctx["full"][h] * _col(w_inter, L_MF + h)
                  + _col(ctx["w_intra"], L_MF + h) * ctx["blk"][h][0:c, :])
            den = jnp.maximum(jnp.abs(fl[:, ML_DV:ML_DV + 1]), _col(e_neg_m, L_MF + h))
            mo = proj_scr[prow, C_MO + h * ML_DV:C_MO + (h + 1) * ML_DV]
            h_parts.append(_rms(_sigmoid(mo) * (fl[:, :ML_DV] / den), mln_ref[...]))
        mix_scr[rows, GDN_HEADS * GDN_DV:GDN_HEADS * GDN_DV + ML_HEADS * ML_DV] = (
            jnp.concatenate(h_parts, axis=1))
        sz = proj_scr[prow, C_SZ:C_SZ + SSD_HEADS * SSD_P]
        y_all = (sy_s[i] + jnp.concatenate(ctx["yh"], axis=1)) * _silu(sz)
        mix_scr[rows, GDN_HEADS * GDN_DV + ML_HEADS * ML_DV:D_MODEL] = _rms(y_all, ssdn_ref[...])

    ctxs = []
    st = None
    for a in range(nb):
        i = j * nb + a
        seq = i if slot else 0
        if slot or a == 0:
            st = load(seq)
        st, ctx = advance(i, st)
        if slot or a == nb - 1:
            store(seq, st)
        ctxs.append(ctx)
    for ctx in ctxs:
        outputs(ctx)


def _ml_state_in(c_mat, n_row):
    eye = (lax.broadcasted_iota(jnp.int32, (ML_DQK, ML_DQK), 0)
           == lax.broadcasted_iota(jnp.int32, (ML_DQK, ML_DQK), 1)).astype(F32)
    n_col = jnp.sum(eye * n_row, axis=-1, keepdims=True)
    first = lax.broadcasted_iota(jnp.int32, (ML_DQK, 128 - ML_DV), 1) == 0
    return jnp.concatenate([c_mat, jnp.where(first, n_col, 0.0)], axis=1)


def _ml_state_out(c_aug):
    eye = (lax.broadcasted_iota(jnp.int32, (ML_DQK, ML_DQK), 0)
           == lax.broadcasted_iota(jnp.int32, (ML_DQK, ML_DQK), 1)).astype(F32)
    n_row = jnp.sum(eye * c_aug[:, ML_DV:ML_DV + 1], axis=0, keepdims=True)
    return c_aug[:, :ML_DV], n_row


def _mixer_kernel(*refs, tm, c, slot, nt, na, nbk, first):
    n_in = (17 if slot else 11) + (0 if first else 6)
    ins, (o_ref, *state_o), (proj_scr, mix_scr, *ab) = refs[:n_in], refs[n_in:n_in + 7], refs[n_in + 7:]
    if slot:
        (x_ref, tail_ref, win_ref, wout_ref, cw_ref, cb_ref, gprm_ref, npre_ref, npost_ref, gdnn_ref,
         mln_ref, ssdn_ref, gdn_i, mc_i, mn_i, mm_i, ssd_i) = ins[:17]
    else:
        (x_ref, win_ref, wout_ref, cw_ref, cb_ref, gprm_ref, npre_ref, npost_ref, gdnn_ref,
         mln_ref, ssdn_ref) = ins[:11]
    if first:
        for ref in state_o:
            ref[1] = jnp.zeros(ref.shape[1:], F32)
        state_o = [ref.at[0] for ref in state_o]
    conv_o, gdn_o, mc_o, mn_o, mm_o, ssd_o = state_o
    names = ("cum_s", "col_s", "gqx_s", "gob_s", "ml_s", "sy_s", "sce_s", "sdh_s", "mlc_s")
    ab = dict(zip(names, ab))
    mlc_s = ab["mlc_s"]
    nseq = gdn_o.shape[0]
    t = pl.program_id(1)

    if slot:
        proj_scr[0:SUBLANES, :] = jnp.zeros((SUBLANES, N_IN), F32)
        gdn_o[...] = gdn_i[...]
        mm_o[...] = mm_i[...]
        ssd_o[...] = ssd_i[...]

        def load_ml(s, carry):
            for h in range(ML_HEADS):
                mlc_s[s * ML_HEADS + h] = _ml_state_in(mc_i[s, h], mn_i[s, h:h + 1, :])
            return carry

        lax.fori_loop(0, nseq, load_ml, 0)
    else:
        @pl.when(t == 0)
        def _():
            proj_scr[0:SUBLANES, :] = jnp.zeros((SUBLANES, N_IN), F32)
            gdn_o[...] = jnp.zeros(gdn_o.shape, F32)
            mlc_s[...] = jnp.zeros(mlc_s.shape, F32)
            mm_o[...] = jnp.zeros(mm_o.shape, F32)
            ssd_o[...] = jnp.zeros(ssd_o.shape, F32)

    x = x_ref[...]
    hn = _rms(x, npre_ref[...]).astype(BF16)

    def in_proj(r_lo, r_hi, nb):
        cols = slice(nb * IN_NBLK, (nb + 1) * IN_NBLK)
        proj_scr[SUBLANES + r_lo:SUBLANES + r_hi, cols] = _dot(hn[r_lo:r_hi, :], win_ref[:, cols])

    def out_proj(r_lo, r_hi):
        out = _dot(mix_scr[r_lo:r_hi, :].astype(BF16), wout_ref[...])
        o_ref[r_lo:r_hi, :] = x[r_lo:r_hi, :] + _rms(out, npost_ref[...])

    def weave(major, minor, minor_first=False):
        major, minor = list(major), list(minor)
        stride = max(1, len(major) // max(1, len(minor)))
        while major or minor:
            if minor and minor_first:
                minor.pop(0)()
            for step in major[:stride]:
                step()
            major = major[stride:]
            if minor and not minor_first:
                minor.pop(0)()

    n_in_blk = N_IN // IN_NBLK
    if slot:
        for nb in range(n_in_blk):
            in_proj(0, tm, nb)
        for s in range(nseq):
            base = SUBLANES + s * SLOT
            proj_scr[base:base + SLOT_FIRST, 0:CONV_DIM] = tail_ref[s]
            conv_o[s] = proj_scr[base + SLOT_LAST - 2:base + SLOT_LAST + 1, 0:CONV_DIM]

    pass_a = functools.partial(_pass_a, c=c, slot=slot, na=na, proj_scr=proj_scr, cw_ref=cw_ref,
                               cb_ref=cb_ref, gprm_ref=gprm_ref,
                               **{k: v for k, v in ab.items() if k != "mlc_s"})
    pass_b = functools.partial(_pass_b, c=c, slot=slot, nb=nbk, proj_scr=proj_scr, mix_scr=mix_scr,
                               gdnn_ref=gdnn_ref, mln_ref=mln_ref, ssdn_ref=ssdn_ref,
                               gdn_o=gdn_o, mm_o=mm_o, ssd_o=ssd_o, **ab)

    n_a, n_b = tm // (c * na), tm // (c * nbk)
    if slot:
        def body_a(j, carry):
            pass_a(j)
            return carry

        def body_b(j, carry):
            pass_b(j)
            return carry

        lax.fori_loop(0, n_a, body_a, 0)
        lax.fori_loop(0, n_b, body_b, 0)
        out_proj(0, tm)
    else:
        assert n_a % 2 == 0 and n_b % n_a == 0
        half, b_per_a = tm // 2, n_b // n_a
        for nb in range(n_in_blk):
            in_proj(0, half, nb)
        weave(pass_a(0, defer=True), [functools.partial(in_proj, half, tm, nb) for nb in range(n_in_blk)],
              minor_first=True)
        for g in range(1, n_a):
            weave(pass_a(g, defer=True), [functools.partial(pass_b, (g - 1) * b_per_a + k) for k in range(b_per_a)])
        weave([functools.partial(pass_b, (n_a - 1) * b_per_a + k) for k in range(b_per_a)],
              [functools.partial(out_proj, 0, half)])
        out_proj(half, tm)

    def store_ml(s, carry):
        for h in range(ML_HEADS):
            c_mat, n_row = _ml_state_out(mlc_s[s * ML_HEADS + h])
            mc_o[s, h] = c_mat
            mn_o[s, h:h + 1, :] = n_row
        return carry

    if slot:
        lax.fori_loop(0, nseq, store_ml, 0)
    else:
        last_rows = proj_scr[tm:tm + SUBLANES, 0:CONV_DIM]
        proj_scr[0:SUBLANES, 0:CONV_DIM] = last_rows

        @pl.when(t == nt - 1)
        def _():
            conv_o[0] = last_rows[SUBLANES - 3:, :]
            store_ml(0, 0)


def _ffn_kernel(*refs, tm, slot, nt):
    if slot:
        (x_ref, ftail_ref, wup_ref, wdn_ref, fw_ref, fb_ref, npre_ref, npost_ref, _,
         o_ref, gate_o, tails_scr, act_scr) = refs
    else:
        (x_ref, wup_ref, wdn_ref, fw_ref, fb_ref, npre_ref, npost_ref, _,
         o_ref, gate_o, tails_scr, act_scr) = refs
    t = pl.program_id(1)

    if slot:
        tails_scr[...] = jnp.zeros(tails_scr.shape, F32)
    else:
        @pl.when(t == 0)
        def _():
            tails_scr[...] = jnp.zeros(tails_scr.shape, F32)

    x = x_ref[...]
    hn = _rms(x, npre_ref[...]).astype(BF16)
    nblk = D_FF // FF_BLK

    def up(blk):
        return (_dot(hn, wup_ref[:, blk * FF_BLK:(blk + 1) * FF_BLK]),
                _dot(hn, wup_ref[:, D_FF + blk * FF_BLK:D_FF + (blk + 1) * FF_BLK]))

    ahead = up(0)
    for blk in range(nblk):
        cols = slice(blk * FF_BLK, (blk + 1) * FF_BLK)
        gate, val = ahead
        if blk + 1 < nblk:
            ahead = up(blk + 1)
        if slot:
            g3 = gate.reshape(tm // SLOT, SLOT, FF_BLK)
            ft = ftail_ref[:, :, cols]
            rr = lax.broadcasted_iota(jnp.int32, g3.shape, 1)
            g3 = jnp.where(rr == SLOT_FIRST - 2, ft[:, 0:1, :], jnp.where(rr == SLOT_FIRST - 1, ft[:, 1:2, :], g3))
            gate_o[:, :, cols] = g3[:, SLOT_LAST - 1:SLOT_LAST + 1, :]
            gate = g3.reshape(tm, FF_BLK)
        full = jnp.concatenate([tails_scr[:, cols], gate], axis=0)
        conv = fb_ref[0:1, cols] + fw_ref[2:3, cols] * gate
        for j in range(2):
            conv = conv + fw_ref[j:j + 1, cols] * pltpu.roll(full, 2 - j, 0)[SUBLANES:, :]
        last_rows = full[tm:tm + SUBLANES, :]
        tails_scr[:, cols] = last_rows
        act_scr[:, cols] = (_gelu_tanh(conv) * val).astype(BF16)
    o_ref[...] = x + _rms(_dot(act_scr[...], wdn_ref[...]), npost_ref[...])
    if not slot:
        @pl.when(t == nt - 1)
        def _():
            gate_o[0] = tails_scr[SUBLANES - 2:, :]


def _const_spec(shape, layer):
    nd = len(shape)
    return pl.BlockSpec((None,) + tuple(shape), lambda b, t: (layer,) + (0,) * nd,
                        pipeline_mode=pl.Buffered(1))


def _state_dims():
    return ((3, CONV_DIM), (GDN_HEADS, GDN_DK, GDN_DV), (ML_HEADS, ML_DQK, ML_DV), (ML_HEADS, ML_DQK),
            (SUBLANES, 128), (SSD_HEADS, SSD_P, SSD_N))


def _mixer_call(x, layer, prm, bufs, *, slot, states=None, tail=None):
    rows = x.shape[0]
    if slot:
        tm, c, nt, na, nbk = SAMPLE_TM, SLOT, 1, SAMPLE_NA, SAMPLE_NB
        nseq = tm // SLOT
    else:
        tm, c, na, nbk = PROMPT_TM, PROMPT_CHUNK, PROMPT_NA, PROMPT_NB
        nt = 2048 // tm
        nseq = 1
    nch = tm // c
    ngrp = rows // (tm * nt)
    nb = ngrp * nseq
    row_spec = lambda w: pl.BlockSpec((tm, w), lambda b, t: (b * nt + t, 0))
    st_in = lambda *dims: pl.BlockSpec((None, nseq) + dims, lambda b, t: (layer, b) + (0,) * len(dims))

    in_specs = [row_spec(D_MODEL)]
    args = [x]
    if slot:
        in_specs.append(st_in(SLOT_FIRST, CONV_DIM))
        args.append(tail)
    in_specs += [
        _const_spec((D_MODEL, N_IN), layer), _const_spec((D_MODEL, D_MODEL), layer),
        _const_spec((4, CONV_DIM), layer), _const_spec((1, CONV_DIM), layer),
        _const_spec((8, 128), layer), _const_spec((1, D_MODEL), layer), _const_spec((1, D_MODEL), layer),
        _const_spec((1, GDN_DV), layer), _const_spec((1, ML_DV), layer),
        _const_spec((1, SSD_HEADS * SSD_P), layer)]
    args += [prm["w_in"], prm["w_out"], prm["conv_w"], prm["conv_b"], prm["gprm"], prm["norm_mix_pre"],
             prm["norm_mix_post"], prm["gdn_norm"], prm["mlstm_norm"], prm["ssd_norm"]]
    if slot:
        in_specs += [st_in(GDN_HEADS, GDN_DK, GDN_DV), st_in(ML_HEADS, ML_DQK, ML_DV),
                     st_in(ML_HEADS, ML_DQK), st_in(SUBLANES, 128), st_in(SSD_HEADS, SSD_P, SSD_N)]
        args += list(states)
    first = bufs is None
    if first:
        aliases = {}
        st_o = lambda *dims: pl.BlockSpec((DEPTH, nseq) + dims, lambda b, t: (0, b) + (0,) * len(dims))
    else:
        aliases = {len(args) + k: 1 + k for k in range(len(bufs))}
        in_specs += [pl.BlockSpec(memory_space=pl.ANY)] * len(bufs)
        args += list(bufs)
        st_o = st_in

    out_specs = [row_spec(D_MODEL)] + [st_o(*dims) for dims in _state_dims()]
    out_shape = [jax.ShapeDtypeStruct((rows, D_MODEL), F32)] + [
        jax.ShapeDtypeStruct((DEPTH, nb) + dims, F32) for dims in _state_dims()]
    lhs_dt = F32 if slot else BF16
    scratch = [
        pltpu.VMEM((tm + SUBLANES, N_IN), F32),
        pltpu.VMEM((tm, D_MODEL), F32),
        pltpu.VMEM((nch, c, 128), F32),
        pltpu.VMEM((nch, c, 128), F32),
        pltpu.VMEM((nch * GDN_HEADS, c + GDN_DK, GDN_DV), lhs_dt),
        pltpu.VMEM((nch * GDN_HEADS, c + GDN_DK, GDN_DV), F32),
        pltpu.VMEM((nch * ML_HEADS, c + ML_DQK, 128), F32),
        pltpu.VMEM((nch, c, SSD_HEADS * SSD_P), F32),
        pltpu.VMEM((nch * SSD_HEADS, c, SSD_N), lhs_dt),
        pltpu.VMEM((nch * SSD_HEADS, SSD_P, SSD_N), F32),
        pltpu.VMEM((nseq * ML_HEADS, ML_DQK, 128), F32),
    ]
    return pl.pallas_call(
        functools.partial(_mixer_kernel, tm=tm, c=c, slot=slot, nt=nt, na=na, nbk=nbk, first=first),
        grid=(ngrp, nt), in_specs=in_specs, out_specs=out_specs, out_shape=out_shape,
        scratch_shapes=scratch, input_output_aliases=aliases,
        compiler_params=pltpu.CompilerParams(dimension_semantics=("arbitrary", "arbitrary"),
                                             vmem_limit_bytes=VMEM_LIMIT),
        name=("mixer_sample" if slot else "mixer_prompt"),
    )(*args)


def _ffn_call(x, layer, prm, buf, *, slot, tail=None):
    rows = x.shape[0]
    tm = FFN_TM
    nt = 1 if slot else 2048 // tm
    ngrp = rows // (tm * nt)
    row_spec = lambda w: pl.BlockSpec((tm, w), lambda b, t: (b * nt + t, 0))
    in_specs = [row_spec(D_MODEL)]
    args = [x]
    nseq = tm // SLOT if slot else 1
    if slot:
        in_specs.append(pl.BlockSpec((None, nseq, 2, D_FF), lambda b, t: (layer, b, 0, 0)))
        args.append(tail)
    in_specs += [_const_spec((D_MODEL, 2 * D_FF), layer), _const_spec((D_FF, D_MODEL), layer),
                 _const_spec((3, D_FF), layer), _const_spec((1, D_FF), layer),
                 _const_spec((1, D_MODEL), layer), _const_spec((1, D_MODEL), layer)]
    args += [prm["ffn_w_up"], prm["ffn_w_down"], prm["ffn_conv_w"], prm["ffn_conv_b"],
             prm["norm_ffn_pre"], prm["norm_ffn_post"]]
    aliases = {len(args): 1}
    in_specs.append(pl.BlockSpec(memory_space=pl.ANY))
    args.append(buf)
    gate_spec = pl.BlockSpec((None, nseq, 2, D_FF), lambda b, t: (layer, b, 0, 0))
    gate_shape = (DEPTH, ngrp * nseq, 2, D_FF)
    return pl.pallas_call(
        functools.partial(_ffn_kernel, tm=tm, slot=slot, nt=nt),
        grid=(ngrp, nt), in_specs=in_specs, out_specs=[row_spec(D_MODEL), gate_spec],
        out_shape=[jax.ShapeDtypeStruct((rows, D_MODEL), F32), jax.ShapeDtypeStruct(gate_shape, F32)],
        scratch_shapes=[pltpu.VMEM((SUBLANES, D_FF), F32), pltpu.VMEM((tm, D_FF), BF16)],
        input_output_aliases=aliases,
        compiler_params=pltpu.CompilerParams(dimension_semantics=("arbitrary", "arbitrary"),
                                             vmem_limit_bytes=VMEM_LIMIT),
        name=("ffn_sample" if slot else "ffn_prompt"),
    )(*args)


def _prepare_params(norm_mix_pre, norm_mix_post, norm_ffn_pre, norm_ffn_post, w_in, conv_w, conv_b,
                    gdn_a_log, gdn_dt_bias, gdn_norm, mlstm_i_bias, mlstm_f_bias, mlstm_norm,
                    ssd_a_log, ssd_dt_bias, ssd_d, ssd_norm, w_out, ffn_w_up, ffn_conv_w, ffn_conv_b,
                    ffn_w_down):
    w_in_p = jnp.concatenate(
        [w_in[..., :2816], w_in[..., 2824:3848], w_in[..., 3856:4112], w_in[..., 2816:2824],
         w_in[..., 3848:3856], w_in[..., 4112:4116],
         jnp.zeros((DEPTH, D_MODEL, N_IN - 4116), w_in.dtype)], axis=-1).astype(BF16)
    z4 = jnp.zeros((DEPTH, 4), F32)
    pad = jnp.zeros((DEPTH, 128 - 20), F32)
    gprm = jnp.stack(
        [jnp.concatenate([gdn_dt_bias, z4, mlstm_i_bias, mlstm_f_bias, ssd_dt_bias, pad], axis=-1),
         jnp.concatenate([gdn_a_log, z4, z4, z4, ssd_a_log, pad], axis=-1),
         jnp.concatenate([z4, z4, z4, z4, ssd_d, pad], axis=-1)]
        + [jnp.zeros((DEPTH, 128), F32)] * 5, axis=1)
    row = lambda a: a[:, None, :]
    return dict(
        w_in=w_in_p, w_out=w_out.astype(BF16), conv_w=conv_w, conv_b=row(conv_b), gprm=gprm,
        norm_mix_pre=row(norm_mix_pre), norm_mix_post=row(norm_mix_post),
        norm_ffn_pre=row(norm_ffn_pre), norm_ffn_post=row(norm_ffn_post),
        gdn_norm=row(gdn_norm), mlstm_norm=row(mlstm_norm), ssd_norm=row(ssd_norm),
        ffn_w_up=ffn_w_up.astype(BF16), ffn_w_down=ffn_w_down.astype(BF16),
        ffn_conv_w=ffn_conv_w, ffn_conv_b=row(ffn_conv_b))


def kernel(x_prompt, x_sample, state_conv, state_gdn, state_mlstm_c, state_mlstm_n, state_mlstm_m, state_ssd, state_ffn_conv, norm_mix_pre, norm_mix_post, norm_ffn_pre, norm_ffn_post, w_in, conv_w, conv_b, gdn_a_log, gdn_dt_bias, gdn_norm, mlstm_i_bias, mlstm_f_bias, mlstm_norm, ssd_a_log, ssd_dt_bias, ssd_d, ssd_norm, w_out, ffn_w_up, ffn_conv_w, ffn_conv_b, ffn_w_down):
    prm = _prepare_params(norm_mix_pre, norm_mix_post, norm_ffn_pre, norm_ffn_post, w_in, conv_w, conv_b,
                          gdn_a_log, gdn_dt_bias, gdn_norm, mlstm_i_bias, mlstm_f_bias, mlstm_norm,
                          ssd_a_log, ssd_dt_bias, ssd_d, ssd_norm, w_out, ffn_w_up, ffn_conv_w,
                          ffn_conv_b, ffn_w_down)
    bp, lp, _ = x_prompt.shape
    bs, ls, _ = x_sample.shape

    def state_bufs(nseq):
        return None, jnp.zeros((DEPTH, nseq, 2, D_FF), F32)

    def assemble(bufs, gate):
        conv, gdn, mc, mn, mm, ssd = bufs
        return [conv, gdn, mc, mn, mm[:, :, 0, L_MF:L_MF + ML_HEADS], ssd, gate]

    x = x_prompt.reshape(bp * lp, D_MODEL)
    bufs, gate = state_bufs(bp)
    for layer in range(DEPTH):
        x, *bufs = _mixer_call(x, layer, prm, bufs, slot=False)
        x, gate = _ffn_call(x, layer, prm, gate, slot=False)
    p_out = assemble(bufs, gate)
    y_prompt = x.reshape(bp, lp, D_MODEL)

    x = jnp.pad(x_sample, ((0, 0), (SLOT_FIRST, SLOT - SLOT_FIRST - ls), (0, 0))).reshape(bs * SLOT, D_MODEL)
    mm_in = jnp.pad(state_mlstm_m[:, :, None, :],
                    ((0, 0), (0, 0), (0, SUBLANES - 1), (L_MF, 128 - L_MF - ML_HEADS)))
    bufs, gate = state_bufs(bs)
    for layer in range(DEPTH):
        x, *bufs = _mixer_call(x, layer, prm, bufs, slot=True, tail=state_conv,
                               states=(state_gdn, state_mlstm_c, state_mlstm_n, mm_in, state_ssd))
        x, gate = _ffn_call(x, layer, prm, gate, slot=True, tail=state_ffn_conv)
    s_out = assemble(bufs, gate)
    y_sample = x.reshape(bs, SLOT, D_MODEL)[:, SLOT_FIRST:SLOT_LAST + 1]
    return (y_prompt, y_sample, *p_out, *s_out)
```
